```python
import math
import jax
import jax.numpy as jnp
from jax import lax
import numpy as np

D_MODEL = 2048
BATCH = 2
SEQ = 4096
DEPTH = 2
DEC_BATCH = 8
DEC_SEQ = 8
PAST_LEN = 16384
PAGE_SIZE = 128

N_ATTN_LAYERS = (DEPTH + 1) // 2
N_SSM_LAYERS = DEPTH // 2
ATTN_WIDTH = D_MODEL // 2
CONV_WIDTH = D_MODEL // 2
HEAD_DIM = 128
N_HEADS = ATTN_WIDTH // HEAD_DIM
DILATED_GROUPS = ((128, 1), (512, 4), (2048, 16))
MAX_WINDOW = 2048
BLK = 128
CONV_K = 3
SSM_WIDTH = D_MODEL
SSM_GROUP = 16
SSM_GROUPS = SSM_WIDTH // SSM_GROUP
SSM_STATE = 64
SSM_CHUNK = 128
ROPE_THETA = 10000.0
RMS_EPS = 1e-6

kernel_name = "dilated_attn_shortconv_s5_hybrid_step"


def rms_norm(x, g):
    xf = x.astype(jnp.float32)
    y = xf * lax.rsqrt(jnp.mean(xf * xf, axis=-1, keepdims=True) + RMS_EPS)
    return (y * g.astype(jnp.float32)).astype(x.dtype)


def rotary(x, pos):
    half = HEAD_DIM // 2
    inv = ROPE_THETA ** (-jnp.arange(half, dtype=jnp.float32) / half)
    ang = pos.astype(jnp.float32)[:, None] * inv[None, :]
    cos = jnp.cos(ang)[None, :, None, :]
    sin = jnp.sin(ang)[None, :, None, :]
    xf = x.astype(jnp.float32)
    x1, x2 = xf[..., :half], xf[..., half:]
    return jnp.concatenate([x1 * cos - x2 * sin, x2 * cos + x1 * sin], axis=-1).astype(x.dtype)


def banded_window_attn(q, k, v, n_back):
    n, L, h, dh = q.shape
    nb = -(-L // BLK)
    pad_end = nb * BLK - L
    qb = jnp.pad(q, ((0, 0), (0, pad_end), (0, 0), (0, 0))).reshape(n, nb, BLK, h, dh)

    def kv_blocks(t):
        tp = jnp.pad(t, ((0, 0), (BLK, pad_end), (0, 0), (0, 0))).reshape(n, nb + 1, BLK, h, dh)
        return jnp.concatenate([tp[:, :-1], tp[:, 1:]], axis=2)

    kb, vb = kv_blocks(k), kv_blocks(v)
    s = jnp.einsum('nbqhd,nbkhd->nbhqk', qb, kb, preferred_element_type=jnp.float32) * (dh ** -0.5)
    qi = jnp.arange(BLK)[:, None]
    kj = jnp.arange(2 * BLK)[None, :]
    dist = BLK + qi - kj
    key_pos = jnp.arange(nb)[:, None, None] * BLK + kj[None] - BLK
    valid = ((dist >= 0) & (dist <= n_back))[None] & (key_pos >= 0)
    s = jnp.where(valid[None, :, None], s, -jnp.inf)
    m = jnp.max(s, axis=-1, keepdims=True)
    p = jnp.exp(s - m)
    l = jnp.sum(p, axis=-1, keepdims=True)
    o = jnp.einsum('nbhqk,nbkhd->nbqhd', p, vb.astype(jnp.float32))
    o = o / jnp.swapaxes(l, 2, 3)
    lse = jnp.swapaxes((m + jnp.log(l))[..., 0], 2, 3)
    o = o.reshape(n, nb * BLK, h, dh)[:, :L]
    lse = lse.reshape(n, nb * BLK, h)[:, :L]
    return o, lse


def merge_dilations(outs, lses):
    w = jax.nn.softmax(jnp.stack(lses, axis=0), axis=0)
    return sum(w[g][..., None] * outs[g] for g in range(len(outs)))


def dilated_attn_prompt(q, k, v):
    b, t, h, dh = q.shape
    outs, lses = [], []
    for window, d in DILATED_GROUPS:
        L = t // d

        def split(x):
            return x.reshape(b, L, d, h, dh).transpose(0, 2, 1, 3, 4).reshape(b * d, L, h, dh)

        o, lse = banded_window_attn(split(q), split(k), split(v), window // d)
        outs.append(o.reshape(b, d, L, h, dh).transpose(0, 2, 1, 3, 4).reshape(b, t, h, dh))
        lses.append(lse.reshape(b, d, L, h).transpose(0, 2, 1, 3).reshape(b, t, h))
    return merge_dilations(outs, lses)


def dilated_attn_sample(q, k_all, v_all, n_buf):
    s_len = q.shape[1]
    outs, lses = [], []
    for window, d in DILATED_GROUPS:
        j = jnp.arange(window // d + 1)
        idx = n_buf + jnp.arange(s_len)[:, None] - d * j[None, :]
        valid = idx >= 0
        idxc = jnp.maximum(idx, 0)
        kg = k_all[:, idxc]
        vg = v_all[:, idxc]
        sc = jnp.einsum('bshd,bskhd->bshk', q, kg, preferred_element_type=jnp.float32) * (HEAD_DIM ** -0.5)
        sc = jnp.where(valid[None, :, None, :], sc, -jnp.inf)
        m = jnp.max(sc, axis=-1, keepdims=True)
        p = jnp.exp(sc - m)
        l = jnp.sum(p, axis=-1, keepdims=True)
        o = jnp.einsum('bshk,bskhd->bshd', p, vg.astype(jnp.float32)) / l
        outs.append(o)
        lses.append((m + jnp.log(l))[..., 0])
    return merge_dilations(outs, lses)


def causal_short_conv(u, buf, w):
    up = jnp.concatenate([buf, u], axis=1)
    t = u.shape[1]
    y = sum(up[:, i:i + t] * w[i] for i in range(CONV_K))
    return y, up[:, -(CONV_K - 1):]


def mixer_ab_layer(x, pos, kv_prev, conv_prev, g, w_in, conv_w, w_out):
    b, t, _ = x.shape
    hn = rms_norm(x, g)
    proj = hn @ w_in
    q, k, v, z_a, gate_b, gate_c, h_in, z_b = jnp.split(proj, 8, axis=-1)
    q = rotary(q.reshape(b, t, N_HEADS, HEAD_DIM), pos)
    k = rotary(k.reshape(b, t, N_HEADS, HEAD_DIM), pos)
    v = v.reshape(b, t, N_HEADS, HEAD_DIM)
    if kv_prev is None:
        o_a = dilated_attn_prompt(q, k, v)
        n_keep = min(MAX_WINDOW, t)
        k_state, v_state = k[:, t - n_keep:], v[:, t - n_keep:]
    else:
        k_buf, v_buf = kv_prev
        o_a = dilated_attn_sample(q, jnp.concatenate([k_buf, k], axis=1),
                                  jnp.concatenate([v_buf, v], axis=1), k_buf.shape[1])
        k_state, v_state = k, v
    o_a = o_a.reshape(b, t, ATTN_WIDTH).astype(x.dtype) * jax.nn.silu(z_a)
    conv_out, conv_state = causal_short_conv(gate_c * h_in, conv_prev, conv_w)
    o_b = gate_b * conv_out * jax.nn.silu(z_b)
    y = jnp.concatenate([o_a, o_b], axis=-1) @ w_out
    return x + y, k_state, v_state, conv_state


def s5_discretise(lam_re, lam_im, log_step, b_re, b_im):
    f32 = jnp.float32
    lr, li = lam_re.astype(f32), lam_im.astype(f32)
    step = jnp.exp(log_step.astype(f32))[:, None]
    mag = jnp.exp(lr * step)
    abar_re, abar_im = mag * jnp.cos(li * step), mag * jnp.sin(li * step)
    nr, ni = abar_re - 1.0, abar_im
    den = lr * lr + li * li
    cr = (nr * lr + ni * li) / den
    ci = (ni * lr - nr * li) / den
    br, bi = b_re.astype(f32), b_im.astype(f32)
    bbar_re = cr[..., None] * br - ci[..., None] * bi
    bbar_im = cr[..., None] * bi + ci[..., None] * br
    return abar_re, abar_im, bbar_re, bbar_im


def _ssm_combine(e1, e2):
    a1r, a1i, b1r, b1i = e1
    a2r, a2i, b2r, b2i = e2
    return (a2r * a1r - a2i * a1i, a2r * a1i + a2i * a1r,
            a2r * b1r - a2i * b1i + b2r, a2r * b1i + a2i * b1r + b2i)


def s5_scan(u, h0_re, h0_im, abar_re, abar_im, bbar_re, bbar_im, c_re, c_im, d_skip):
    b, t, _ = u.shape
    f32 = jnp.float32
    chunk = math.gcd(t, SSM_CHUNK)
    n_chunks = t // chunk
    ug = u.astype(f32).reshape(b, n_chunks, chunk, SSM_GROUPS, SSM_GROUP).transpose(1, 0, 2, 3, 4)
    cr, ci = c_re.astype(f32), c_im.astype(f32)

    def step(carry, u_c):
        hr, hi = carry
        bur = jnp.einsum('bcgk,gpk->bcgp', u_c, bbar_re)
        bui = jnp.einsum('bcgk,gpk->bcgp', u_c, bbar_im)
        bur = bur.at[:, 0].add(abar_re * hr - abar_im * hi)
        bui = bui.at[:, 0].add(abar_re * hi + abar_im * hr)
        ar = jnp.broadcast_to(abar_re, bur.shape)
        ai = jnp.broadcast_to(abar_im, bur.shape)
        _, _, sr, si = lax.associative_scan(_ssm_combine, (ar, ai, bur, bui), axis=1)
        y = jnp.einsum('bcgp,gkp->bcgk', sr, cr) - jnp.einsum('bcgp,gkp->bcgk', si, ci)
        return (sr[:, -1], si[:, -1]), y

    (hr, hi), y = lax.scan(step, (h0_re.astype(f32), h0_im.astype(f32)), ug)
    y = y.transpose(1, 0, 2, 3, 4).reshape(b, t, SSM_WIDTH) + d_skip.astype(f32) * u.astype(f32)
    return y, hr, hi


def mixer_c_layer(x, h0_re, h0_im, g, w_in, lam_re, lam_im, log_step, b_re, b_im,
                  c_re, c_im, d_skip, w_glu, b_glu, w_out):
    hn = rms_norm(x, g)
    u, z = jnp.split(hn @ w_in, 2, axis=-1)
    abar_re, abar_im, bbar_re, bbar_im = s5_discretise(lam_re, lam_im, log_step, b_re, b_im)
    y, hr, hi = s5_scan(u, h0_re, h0_im, abar_re, abar_im, bbar_re, bbar_im, c_re, c_im, d_skip)
    y = jax.nn.gelu(y)
    y = y * jax.nn.sigmoid(y @ w_glu.astype(jnp.float32) + b_glu.astype(jnp.float32))
    y = y.astype(x.dtype) * jax.nn.silu(z)
    return x + y @ w_out, hr, hi


def setup_inputs(seed: int = 0) -> dict:
    key = jax.random.key(seed)
    ks = jax.random.split(key, 25)
    f32 = jnp.float32
    n_buf = min(MAX_WINDOW, PAST_LEN)
    nrm = lambda k, shape, s: jax.random.normal(k, shape, f32) * s
    lam_im_init = jnp.pi * jnp.arange(SSM_STATE, dtype=f32)
    return {
        'x_prompt': nrm(ks[0], (BATCH, SEQ, D_MODEL), 1.0),
        'x_sample': nrm(ks[1], (DEC_BATCH, DEC_SEQ, D_MODEL), 1.0),
        'cache_win_k': nrm(ks[2], (N_ATTN_LAYERS, DEC_BATCH, n_buf, N_HEADS, HEAD_DIM), 1.0),
        'cache_win_v': nrm(ks[3], (N_ATTN_LAYERS, DEC_BATCH, n_buf, N_HEADS, HEAD_DIM), 1.0),
        'state_conv': nrm(ks[4], (N_ATTN_LAYERS, DEC_BATCH, CONV_K - 1, CONV_WIDTH), 1.0),
        'state_ssm_re': nrm(ks[5], (N_SSM_LAYERS, DEC_BATCH, SSM_GROUPS, SSM_STATE), 0.1),
        'state_ssm_im': nrm(ks[6], (N_SSM_LAYERS, DEC_BATCH, SSM_GROUPS, SSM_STATE), 0.1),
        'attn_norm': 1.0 + nrm(ks[7], (N_ATTN_LAYERS, D_MODEL), 0.02),
        'w_in_ab': nrm(ks[8], (N_ATTN_LAYERS, D_MODEL, 4 * ATTN_WIDTH + 4 * CONV_WIDTH), D_MODEL ** -0.5),
        'conv_w': nrm(ks[9], (N_ATTN_LAYERS, CONV_K, CONV_WIDTH), CONV_K ** -0.5),
        'w_out_ab': nrm(ks[10], (N_ATTN_LAYERS, ATTN_WIDTH + CONV_WIDTH, D_MODEL), (ATTN_WIDTH + CONV_WIDTH) ** -0.5),
        'ssm_norm': 1.0 + nrm(ks[11], (N_SSM_LAYERS, D_MODEL), 0.02),
        'w_in_c': nrm(ks[12], (N_SSM_LAYERS, D_MODEL, 2 * SSM_WIDTH), D_MODEL ** -0.5),
        'lam_re': -0.5 + nrm(ks[13], (N_SSM_LAYERS, SSM_GROUPS, SSM_STATE), 0.01),
        'lam_im': lam_im_init + nrm(ks[14], (N_SSM_LAYERS, SSM_GROUPS, SSM_STATE), 0.01),
        'log_step': jax.random.uniform(ks[15], (N_SSM_LAYERS, SSM_GROUPS), f32,
                                       minval=math.log(1e-3), maxval=math.log(1e-1)),
        'b_re': nrm(ks[16], (N_SSM_LAYERS, SSM_GROUPS, SSM_STATE, SSM_GROUP), (2 * SSM_GROUP) ** -0.5),
        'b_im': nrm(ks[17], (N_SSM_LAYERS, SSM_GROUPS, SSM_STATE, SSM_GROUP), (2 * SSM_GROUP) ** -0.5),
        'c_re': nrm(ks[18], (N_SSM_LAYERS, SSM_GROUPS, SSM_GROUP, SSM_STATE), SSM_STATE ** -0.5),
        'c_im': nrm(ks[19], (N_SSM_LAYERS, SSM_GROUPS, SSM_GROUP, SSM_STATE), SSM_STATE ** -0.5),
        'd_skip': nrm(ks[20], (N_SSM_LAYERS, SSM_WIDTH), 1.0),
        'w_glu': nrm(ks[21], (N_SSM_LAYERS, SSM_WIDTH, SSM_WIDTH), SSM_WIDTH ** -0.5),
        'b_glu': nrm(ks[22], (N_SSM_LAYERS, SSM_WIDTH), 0.01),
        'w_out_c': nrm(ks[23], (N_SSM_LAYERS, SSM_WIDTH, D_MODEL), SSM_WIDTH ** -0.5),
        'final_norm': 1.0 + nrm(ks[24], (D_MODEL,), 0.02),
    }


def reference(x_prompt, x_sample, cache_win_k, cache_win_v, state_conv, state_ssm_re, state_ssm_im,
              attn_norm, w_in_ab, conv_w, w_out_ab, ssm_norm, w_in_c, lam_re, lam_im, log_step,
              b_re, b_im, c_re, c_im, d_skip, w_glu, b_glu, w_out_c, final_norm):
    bp, tp, _ = x_prompt.shape
    ts = x_sample.shape[1]
    pos_p = jnp.arange(tp)
    pos_s = PAST_LEN + jnp.arange(ts)
    hp, hs = x_prompt, x_sample
    kp_l, vp_l, cp_l, ks_l, vs_l, cs_l = [], [], [], [], [], []
    srp_l, sip_l, srs_l, sis_l = [], [], [], []
    for layer in range(DEPTH):
        i = layer // 2
        if layer % 2 == 0:
            conv0 = jnp.zeros((bp, CONV_K - 1, CONV_WIDTH), hp.dtype)
            hp, kp, vp, cp = mixer_ab_layer(hp, pos_p, None, conv0, attn_norm[i], w_in_ab[i],
                                            conv_w[i], w_out_ab[i])
            hs, ksn, vsn, csn = mixer_ab_layer(hs, pos_s, (cache_win_k[i], cache_win_v[i]), state_conv[i],
                                               attn_norm[i], w_in_ab[i], conv_w[i], w_out_ab[i])
            kp_l.append(kp); vp_l.append(vp); cp_l.append(cp)
            ks_l.append(ksn); vs_l.append(vsn); cs_l.append(csn)
        else:
            h0 = jnp.zeros((bp, SSM_GROUPS, SSM_STATE), jnp.float32)
            hp, rp, ip = mixer_c_layer(hp, h0, h0, ssm_norm[i], w_in_c[i], lam_re[i], lam_im[i], log_step[i],
                                       b_re[i], b_im[i], c_re[i], c_im[i], d_skip[i], w_glu[i], b_glu[i], w_out_c[i])
            hs, rs, is_ = mixer_c_layer(hs, state_ssm_re[i], state_ssm_im[i], ssm_norm[i], w_in_c[i], lam_re[i],
                                        lam_im[i], log_step[i], b_re[i], b_im[i], c_re[i], c_im[i], d_skip[i],
                                        w_glu[i], b_glu[i], w_out_c[i])
            srp_l.append(rp); sip_l.append(ip); srs_l.append(rs); sis_l.append(is_)
    y_prompt = rms_norm(hp, final_norm)
    y_sample = rms_norm(hs, final_norm)
    return (y_prompt, y_sample,
            jnp.stack(kp_l), jnp.stack(vp_l), jnp.stack(cp_l), jnp.stack(srp_l), jnp.stack(sip_l),
            jnp.stack(ks_l), jnp.stack(vs_l), jnp.stack(cs_l), jnp.stack(srs_l), jnp.stack(sis_l))
```

```python
import functools
import math

import jax
import jax.numpy as jnp
from jax import lax
from jax.experimental import pallas as pl
from jax.experimental.pallas import tpu as pltpu

D_MODEL = 2048
HEAD_DIM = 128
N_HEADS = 8
ATTN_WIDTH = 1024
CONV_WIDTH = 1024
DILATIONS = (1, 4, 16)
N_BACK = 128
ROPE_THETA = 10000.0
RMS_EPS = 1e-6
SSM_GROUP = 16
SSM_GROUPS = 128
SSM_STATE = 64
LANES = 128
VMEM_LIMIT = 56 * 1024 * 1024

F32 = jnp.float32
BF16 = jnp.bfloat16


def _cparams(*sem):
    return pltpu.CompilerParams(dimension_semantics=sem, vmem_limit_bytes=VMEM_LIMIT)


def _rope_table_kernel(inv_ref, cos_ref, sin_ref, *, period, offset):
    rows = cos_ref.shape[0]
    r = lax.broadcasted_iota(jnp.int32, (rows, LANES), 0) + pl.program_id(0) * rows
    pos = (offset + lax.rem(r, period)).astype(F32)
    ang = pos * inv_ref[...]
    lane = lax.broadcasted_iota(jnp.int32, (rows, LANES), 1)
    cos_ref[...] = jnp.cos(ang)
    sin_ref[...] = jnp.where(lane < HEAD_DIM // 2, -1.0, 1.0) * jnp.sin(ang)


def _rope_tables(rows, period, offset):
    half = HEAD_DIM // 2
    inv = ROPE_THETA ** (-jnp.arange(half, dtype=F32) / half)
    inv2 = jnp.concatenate([inv, inv])[None, :]
    tr = min(rows, 256)
    return pl.pallas_call(
        functools.partial(_rope_table_kernel, period=period, offset=offset),
        grid=(rows // tr,),
        in_specs=[pl.BlockSpec((1, LANES), lambda i: (0, 0))],
        out_specs=(pl.BlockSpec((tr, LANES), lambda i: (i, 0)),) * 2,
        out_shape=(jax.ShapeDtypeStruct((rows, LANES), F32),) * 2,
        compiler_params=_cparams("parallel"),
        name="rope_table",
    )(inv2)


def _rms_norm(x, g):
    return x * lax.rsqrt(jnp.mean(x * x, axis=-1, keepdims=True) + RMS_EPS) * g


def _norm_matmul_kernel(x_ref, g_ref, w_ref, *rest, rope_tiles):
    o_ref, hn_ref = rest[-2:]
    n = pl.program_id(1)

    @pl.when(n == 0)
    def _():
        hn_ref[...] = _rms_norm(x_ref[...], g_ref[...]).astype(BF16)

    acc = jnp.dot(hn_ref[...], w_ref[...], preferred_element_type=F32)

    @pl.when(n >= rope_tiles)
    def _():
        o_ref[...] = acc

    if rope_tiles:
        cos_ref, sin_ref = rest[:2]

        @pl.when(n < rope_tiles)
        def _():
            cos = cos_ref[...]
            sin = sin_ref[...]
            for h in range(o_ref.shape[1] // HEAD_DIM):
                xh = acc[:, h * HEAD_DIM:(h + 1) * HEAD_DIM]
                o_ref[:, h * HEAD_DIM:(h + 1) * HEAD_DIM] = xh * cos + pltpu.roll(xh, HEAD_DIM // 2, 1) * sin


def _norm_matmul(x, g, w, rope=None, *, tm, tn, rope_tiles=0, tiles_per_seq=1):
    m, k = x.shape
    n = w.shape[1]
    table = pl.BlockSpec((tm, LANES), lambda i, j: (i % tiles_per_seq, 0))
    return pl.pallas_call(
        functools.partial(_norm_matmul_kernel, rope_tiles=rope_tiles),
        grid=(m // tm, n // tn),
        in_specs=[
            pl.BlockSpec((tm, k), lambda i, j: (i, 0)),
            pl.BlockSpec((1, k), lambda i, j: (0, 0)),
            pl.BlockSpec((k, tn), lambda i, j: (0, j)),
        ] + ([table, table] if rope_tiles else []),
        out_specs=pl.BlockSpec((tm, tn), lambda i, j: (i, j)),
        out_shape=jax.ShapeDtypeStruct((m, n), F32),
        scratch_shapes=[pltpu.VMEM((tm, k), BF16)],
        compiler_params=_cparams("parallel", "arbitrary"),
        name="norm_matmul",
    )(x, g.reshape(1, k), w, *(rope if rope_tiles else ()))


def _attn_prompt_kernel(q_ref, k_ref, v_ref, o_ref, qc_ref, kc_ref, vc_ref, oc_ref, lc_ref,
                        to_ref, tl_ref, lse_ref):
    t = q_ref.shape[1]
    nblk = t // N_BACK
    scale = HEAD_DIM ** -0.5
    kc_ref[0:N_BACK, :] = jnp.zeros((N_BACK, HEAD_DIM), BF16)
    vc_ref[0:N_BACK, :] = jnp.zeros((N_BACK, HEAD_DIM), BF16)
    qi = lax.broadcasted_iota(jnp.int32, (N_BACK, 2 * N_BACK), 0)
    kj = lax.broadcasted_iota(jnp.int32, (N_BACK, 2 * N_BACK), 1)
    dist = N_BACK + qi - kj
    band = (dist >= 0) & (dist <= N_BACK)

    for d in DILATIONS:
        ln = t // d
        blocks_per_class = ln // N_BACK
        for r in range(d):
            rows = pl.ds(r, ln, stride=d) if d > 1 else pl.ds(0, ln)
            qc_ref[r * ln:(r + 1) * ln, :] = q_ref[0, rows, :].astype(BF16)
            kc_ref[N_BACK + r * ln:N_BACK + (r + 1) * ln, :] = k_ref[0, rows, :].astype(BF16)
            vc_ref[N_BACK + r * ln:N_BACK + (r + 1) * ln, :] = v_ref[0, rows, :].astype(BF16)

        def block(b, carry):
            row0 = pl.multiple_of(b * N_BACK, N_BACK)
            j = lax.rem(b, blocks_per_class)
            q = qc_ref[pl.ds(row0, N_BACK), :]
            k2 = kc_ref[pl.ds(row0, 2 * N_BACK), :]
            v2 = vc_ref[pl.ds(row0, 2 * N_BACK), :]
            s = lax.dot_general(q, k2, (((1,), (1,)), ((), ())), preferred_element_type=F32) * scale
            valid = band & ((kj >= N_BACK) | (j > 0))
            s = jnp.where(valid, s, -jnp.inf)
            m = jnp.max(s, axis=-1, keepdims=True)
            p = jnp.exp(s - m)
            l = jnp.sum(p, axis=-1, keepdims=True)
            o = jnp.dot(p.astype(BF16), v2, preferred_element_type=F32)
            oc_ref[pl.ds(row0, N_BACK), :] = o / l
            lc_ref[pl.ds(row0, N_BACK), :] = m + jnp.log(l)
            return carry

        lax.fori_loop(0, nblk, block, 0)

        if d == 1:
            o_ref[0] = oc_ref[...]
            lse_ref[...] = lc_ref[...]
        else:
            for r in range(d):
                to_ref[pl.ds(r, ln, stride=d), :] = oc_ref[r * ln:(r + 1) * ln, :]
                tl_ref[pl.ds(r, ln, stride=d), :] = lc_ref[r * ln:(r + 1) * ln, :]

            def merge(c, carry):
                rows = pl.ds(pl.multiple_of(c * 256, 256), 256)
                la = lse_ref[rows, :]
                lb = tl_ref[rows, :]
                mx = jnp.maximum(la, lb)
                ea = jnp.exp(la - mx)
                eb = jnp.exp(lb - mx)
                tot = ea + eb
                o_ref[0, rows, :] = (o_ref[0, rows, :] * ea + to_ref[rows, :] * eb) / tot
                lse_ref[rows, :] = mx + jnp.log(tot)
                return carry

            lax.fori_loop(0, t // 256, merge, 0)


def _attn_prompt(proj, b, t):
    p3 = proj.reshape(b, t, proj.shape[1])
    blk = lambda off: pl.BlockSpec((1, t, HEAD_DIM), lambda i, h: (i, 0, off + h))
    return pl.pallas_call(
        _attn_prompt_kernel,
        grid=(b, N_HEADS),
        in_specs=[blk(0), blk(N_HEADS), blk(2 * N_HEADS)],
        out_specs=pl.BlockSpec((1, t, HEAD_DIM), lambda i, h: (i, 0, h)),
        out_shape=jax.ShapeDtypeStruct((b, t, ATTN_WIDTH), F32),
        scratch_shapes=[
            pltpu.VMEM((t, HEAD_DIM), BF16),
            pltpu.VMEM((t + N_BACK, HEAD_DIM), BF16),
            pltpu.VMEM((t + N_BACK, HEAD_DIM), BF16),
            pltpu.VMEM((t, HEAD_DIM), F32),
            pltpu.VMEM((t, 1), F32),
            pltpu.VMEM((t, HEAD_DIM), F32),
            pltpu.VMEM((t, 1), F32),
            pltpu.VMEM((t, 1), F32),
        ],
        compiler_params=_cparams("parallel", "parallel"),
        name="attn_prompt",
    )(p3, p3, p3)


def _attn_sample_kernel(q_ref, kn_ref, vn_ref, kc_ref, vc_ref, o_ref):
    s_len = q_ref.shape[1]
    n_buf = kc_ref.shape[1]
    scale = HEAD_DIM ** -0.5
    nt = (((1,), (1,)), ((), ()))
    q = q_ref[0].astype(BF16)
    sc = lax.dot_general(q, kc_ref[0].astype(BF16), nt, preferred_element_type=F32) * scale
    sn = lax.dot_general(q, kn_ref[0].astype(BF16), nt, preferred_element_type=F32) * scale

    def count(dist):
        c = jnp.zeros(dist.shape, F32)
        for d in DILATIONS:
            hit = (dist >= 0) & (dist <= N_BACK * d) & ((dist & (d - 1)) == 0)
            c = c + jnp.where(hit, 1.0, 0.0)
        return c

    cc = count(n_buf + lax.broadcasted_iota(jnp.int32, sc.shape, 0) - lax.broadcasted_iota(jnp.int32, sc.shape, 1))
    cn = count(lax.broadcasted_iota(jnp.int32, sn.shape, 0) - lax.broadcasted_iota(jnp.int32, sn.shape, 1))
    sc = jnp.where(cc > 0, sc, -jnp.inf)
    sn = jnp.where(cn > 0, sn, -jnp.inf)
    m = jnp.maximum(jnp.max(sc, axis=-1, keepdims=True), jnp.max(sn, axis=-1, keepdims=True))
    pc = cc * jnp.exp(sc - m)
    pn = cn * jnp.exp(sn - m)
    l = jnp.sum(pc, axis=-1, keepdims=True) + jnp.sum(pn, axis=-1, keepdims=True)
    o = (jnp.dot(pc.astype(BF16), vc_ref[0].astype(BF16), preferred_element_type=F32)
         + jnp.dot(pn.astype(BF16), vn_ref[0].astype(BF16), preferred_element_type=F32))
    o_ref[0] = o / l
    del s_len


def _attn_sample(proj, cache_k, cache_v, b, s_len):
    n_buf = cache_k.shape[1]
    p3 = proj.reshape(b, s_len, proj.shape[1])
    ck = cache_k.reshape(b, n_buf, ATTN_WIDTH)
    cv = cache_v.reshape(b, n_buf, ATTN_WIDTH)
    new = lambda off: pl.BlockSpec((1, s_len, HEAD_DIM), lambda i, h: (i, 0, off + h))
    old = pl.BlockSpec((1, n_buf, HEAD_DIM), lambda i, h: (i, 0, h))
    return pl.pallas_call(
        _attn_sample_kernel,
        grid=(b, N_HEADS),
        in_specs=[new(0), new(N_HEADS), new(2 * N_HEADS), old, old],
        out_specs=pl.BlockSpec((1, s_len, HEAD_DIM), lambda i, h: (i, 0, h)),
        out_shape=jax.ShapeDtypeStruct((b, s_len, ATTN_WIDTH), F32),
        compiler_params=_cparams("parallel", "parallel"),
        name="attn_sample",
    )(p3, p3, p3, ck, cv)


def _silu(z):
    return z * jax.nn.sigmoid(z)


def _mix_out_kernel(o_ref, za_ref, gb_ref, gc_ref, hi_ref, zb_ref, gch_ref, hih_ref, init_ref, cw_ref,
                    w_ref, x_ref, h_ref, cs_ref, *, tiles_per_seq):
    tm = o_ref.shape[0]
    first = (pl.program_id(0) % tiles_per_seq) == 0
    ch = gc_ref[...] * hi_ref[...]
    halo = gch_ref[...] * hih_ref[...]
    prev1 = jnp.where(first, init_ref[0, 1:2, :], halo[7:8, :])
    prev2 = jnp.where(first, init_ref[0, 0:1, :], halo[6:7, :])
    row = lax.broadcasted_iota(jnp.int32, (tm, 1), 0)
    ch1 = jnp.where(row == 0, prev1, pltpu.roll(ch, 1, 0))
    ch2 = jnp.where(row == 0, prev2, jnp.where(row == 1, prev1, pltpu.roll(ch, 2, 0)))
    conv = ch2 * cw_ref[0:1, :] + ch1 * cw_ref[1:2, :] + ch * cw_ref[2:3, :]
    cs_ref[0] = ch[tm - 2:tm, :]
    o_a = (o_ref[...] * _silu(za_ref[...])).astype(BF16)
    o_b = (gb_ref[...] * conv * _silu(zb_ref[...])).astype(BF16)
    y = jnp.dot(o_a, w_ref[0:ATTN_WIDTH, :], preferred_element_type=F32)
    y = y + jnp.dot(o_b, w_ref[ATTN_WIDTH:, :], preferred_element_type=F32)
    h_ref[...] = x_ref[...] + y


def _mix_out(proj, o_attn, x, conv_init, conv_w, w_out, *, tm, tiles_per_seq):
    m = x.shape[0]
    nseq = conv_init.shape[0]
    col = lambda c: pl.BlockSpec((tm, CONV_WIDTH), lambda i: (i, c))
    halo = lambda c: pl.BlockSpec((8, CONV_WIDTH), lambda i: (jnp.maximum(i * (tm // 8) - 1, 0), c))
    seq = pl.BlockSpec((1, 2, CONV_WIDTH), lambda i: (i // tiles_per_seq, 0, 0))
    return pl.pallas_call(
        functools.partial(_mix_out_kernel, tiles_per_seq=tiles_per_seq),
        grid=(m // tm,),
        in_specs=[
            pl.BlockSpec((tm, ATTN_WIDTH), lambda i: (i, 0)),
            col(3), col(4), col(5), col(6), col(7), halo(5), halo(6), seq,
            pl.BlockSpec((3, CONV_WIDTH), lambda i: (0, 0)),
            pl.BlockSpec((D_MODEL, D_MODEL), lambda i: (0, 0), pipeline_mode=pl.Buffered(1)),
            pl.BlockSpec((tm, D_MODEL), lambda i: (i, 0)),
        ],
        out_specs=(pl.BlockSpec((tm, D_MODEL), lambda i: (i, 0)), seq),
        out_shape=(jax.ShapeDtypeStruct((m, D_MODEL), F32), jax.ShapeDtypeStruct((nseq, 2, CONV_WIDTH), F32)),
        compiler_params=_cparams("arbitrary"),
        name="mix_out",
    )(o_attn, proj, proj, proj, proj, proj, proj, proj, conv_init, conv_w, w_out, x)


N_SCAN = 7
GROUPS_PER_TILE = LANES // SSM_GROUP


def _s5_disc_kernel(lr_ref, li_ref, ls_ref, pr_ref, pi_ref, cr_ref, ci_ref, dr_ref, di_ref, *, chunk):
    lr = lr_ref[...]
    li = li_ref[...]
    step = jnp.exp(ls_ref[...])
    mag = jnp.exp(lr * step)
    ar = mag * jnp.cos(li * step)
    ai = mag * jnp.sin(li * step)
    den = lr * lr + li * li
    nr = ar - 1.0
    cr_ref[...] = (nr * lr + ai * li) / den
    ci_ref[...] = (ai * lr - nr * li) / den
    pr = jnp.ones_like(ar)
    pi = jnp.zeros_like(ar)
    for tau in range(chunk + 1):
        pr_ref[tau] = pr
        pi_ref[tau] = pi
        dr, di = pr, pi
        pr, pi = pr * ar - pi * ai, pr * ai + pi * ar
    for i in range(N_SCAN + 1):
        dr_ref[i] = dr
        di_ref[i] = di
        dr, di = dr * dr - di * di, 2.0 * dr * di


def _s5_prep_kernel(pcr_ref, pci_ref, ccr_ref, cci_ref, br_ref, bi_ref, cr_ref, ci_ref, c2r_ref, c2i_ref,
                    prr_ref, pri_ref, e_ref, tile_ref, wt_ref, qt_ref, *, chunk):
    n = SSM_GROUP * chunk
    dot = lambda a, b: jnp.dot(a, b, precision=lax.Precision.HIGHEST, preferred_element_type=F32)
    ccr, cci = ccr_ref[0], cci_ref[0]
    br, bi = br_ref[0], bi_ref[0]
    bbr = ccr * br - cci * bi
    bbi = ccr * bi + cci * br
    aer = dot(pcr_ref[0], e_ref[...])
    aei = dot(pci_ref[0], e_ref[...])
    btr = dot(bbr, tile_ref[...])
    bti = dot(bbi, tile_ref[...])
    ptr = aer * btr - aei * bti
    pti = aer * bti + aei * btr
    wt_ref[0, n:n + SSM_STATE, :] = ptr.astype(BF16)
    wt_ref[0, n + SSM_STATE:n + 2 * SSM_STATE, :] = pti.astype(BF16)
    taps = dot(cr_ref[0], ptr) - dot(ci_ref[0], pti)
    padded = jnp.concatenate([taps, jnp.zeros_like(taps)], axis=1)
    for t in range(chunk):
        sh = SSM_GROUP * (chunk - 1 - t)
        blk = padded if sh == 0 else pltpu.roll(padded, 2 * n - sh, 1)
        wt_ref[0, SSM_GROUP * t:SSM_GROUP * (t + 1), :] = blk[:, :n].astype(BF16)
    lane = lax.broadcasted_iota(jnp.int32, (SSM_GROUP, 2 * SSM_STATE), 1)
    c2r, c2i = c2r_ref[0], c2i_ref[0]
    for t in range(chunk):
        ar = prr_ref[0, t + 1:t + 2, :]
        ai = pri_ref[0, t + 1:t + 2, :]
        x1 = jnp.where(lane < SSM_STATE, ar, -ai)
        x2 = jnp.where(lane < SSM_STATE, ai, ar)
        qt_ref[0, SSM_GROUP * t:SSM_GROUP * (t + 1), :] = (c2r * x1 - c2i * x2).astype(BF16)


def _s5_operators(lam_re, lam_im, log_step, b_re, b_im, c_re, c_im, chunk):
    g = lam_re.shape[0]
    n = SSM_GROUP * chunk
    dup = lambda a: jnp.concatenate([a, a], axis=-1)
    sds = lambda *s: jax.ShapeDtypeStruct(s, F32)
    pr, pi, cr, ci, dr, di = pl.pallas_call(
        functools.partial(_s5_disc_kernel, chunk=chunk),
        out_shape=(sds(chunk + 1, g, LANES), sds(chunk + 1, g, LANES), sds(g, LANES), sds(g, LANES),
                   sds(N_SCAN + 1, g, LANES), sds(N_SCAN + 1, g, LANES)),
        name="s5_disc",
    )(dup(lam_re), dup(lam_im), log_step[:, None])
    cols = lambda a: a[:, :, :SSM_STATE].transpose(1, 2, 0)
    rows = lambda a: a.transpose(1, 0, 2)
    s_idx = jnp.arange(n) // SSM_GROUP
    e_sel = (jnp.arange(chunk)[:, None] == (chunk - 1 - s_idx)[None, :]).astype(F32)
    tile = (jnp.arange(SSM_GROUP)[:, None] == (jnp.arange(n) % SSM_GROUP)[None, :]).astype(F32)
    per_g = lambda *s: pl.BlockSpec((1,) + s, lambda i: (i,) + (0,) * len(s))
    whole = lambda *s: pl.BlockSpec(s, lambda i: (0,) * len(s))
    wt, qt = pl.pallas_call(
        functools.partial(_s5_prep_kernel, chunk=chunk),
        grid=(g,),
        in_specs=[per_g(SSM_STATE, chunk), per_g(SSM_STATE, chunk), per_g(SSM_STATE, 1), per_g(SSM_STATE, 1),
                  per_g(SSM_STATE, SSM_GROUP), per_g(SSM_STATE, SSM_GROUP),
                  per_g(SSM_GROUP, SSM_STATE), per_g(SSM_GROUP, SSM_STATE),
                  per_g(SSM_GROUP, LANES), per_g(SSM_GROUP, LANES),
                  per_g(chunk + 1, LANES), per_g(chunk + 1, LANES), whole(chunk, n), whole(SSM_GROUP, n)],
        out_specs=(per_g(n + 2 * SSM_STATE, n), per_g(n, 2 * SSM_STATE)),
        out_shape=(jax.ShapeDtypeStruct((g, n + 2 * SSM_STATE, n), BF16),
                   jax.ShapeDtypeStruct((g, n, 2 * SSM_STATE), BF16)),
        compiler_params=_cparams("parallel"),
        name="s5_prep",
    )(cols(pr[:chunk]), cols(pi[:chunk]), cr[:, :SSM_STATE, None], ci[:, :SSM_STATE, None], b_re, b_im,
      c_re, c_im, dup(c_re), dup(c_im), rows(pr), rows(pi), e_sel, tile)
    return wt, qt, cols(dr), cols(di)


def _s5_scan_kernel(u_ref, wt_ref, qt_ref, dr_ref, di_ref, dsk_ref, *rest, chunk, carry):
    if carry:
        y_ref, hf_ref, dall_ref, yall_ref = rest
    else:
        h0_ref, y_ref, hf_ref, dall_ref, yall_ref = rest
    nc = u_ref.shape[0] // chunk
    ncp = dall_ref.shape[2]
    n = SSM_GROUP * chunk
    for s in range(chunk):
        v = u_ref[pl.ds(s, nc, stride=chunk), :]
        if nc < ncp:
            v = jnp.concatenate([v, jnp.zeros((ncp - nc, LANES), F32)], axis=0)
        dall_ref[s] = v.T.astype(BF16)
    lane = lax.broadcasted_iota(jnp.int32, (SSM_STATE, ncp), 1) & (LANES - 1)
    for gp in range(GROUPS_PER_TILE):
        rows = slice(SSM_GROUP * gp, SSM_GROUP * (gp + 1))
        d = dall_ref[:, rows, :].reshape(n, ncp)
        ys = jnp.dot(wt_ref[gp], d, preferred_element_type=F32)
        sr = ys[n:n + SSM_STATE]
        si = ys[n + SSM_STATE:]
        if carry:
            for i in range(N_SCAN):
                sh = 1 << i
                ar = dr_ref[gp, :, i:i + 1]
                ai = di_ref[gp, :, i:i + 1]
                pr = jnp.where(lane >= sh, pltpu.roll(sr, sh, 1), 0.0)
                pi = jnp.where(lane >= sh, pltpu.roll(si, sh, 1), 0.0)
                sr, si = sr + ar * pr - ai * pi, si + ar * pi + ai * pr
            hf_ref[gp, 0:SSM_STATE, :] = sr
            hf_ref[gp, SSM_STATE:, :] = si
            hpr = jnp.where(lane >= 1, pltpu.roll(sr, 1, 1), 0.0)
            hpi = jnp.where(lane >= 1, pltpu.roll(si, 1, 1), 0.0)
        else:
            hpr = h0_ref[gp, 0:SSM_STATE, :]
            hpi = h0_ref[gp, SSM_STATE:, :]
            ar = dr_ref[gp, :, 0:1]
            ai = di_ref[gp, :, 0:1]
            hf_ref[gp, 0:SSM_STATE, :] = sr + ar * hpr - ai * hpi
            hf_ref[gp, SSM_STATE:, :] = si + ar * hpi + ai * hpr
        hp = jnp.concatenate([hpr, hpi], axis=0).astype(BF16)
        y = ys[:n] + jnp.dot(qt_ref[gp], hp, preferred_element_type=F32)
        yall_ref[:, rows, :] = y.reshape(chunk, SSM_GROUP, ncp)
    for t in range(chunk):
        tok = pl.ds(t, nc, stride=chunk)
        y_ref[tok, :] = yall_ref[t].T[:nc] + dsk_ref[...] * u_ref[tok, :]


def _s5_scan(proj, ops, d_skip, h0, *, chunk, chunks_per_seq):
    wt, qt, dr, di = ops
    m = proj.shape[0]
    n = SSM_GROUP * chunk
    carry = h0 is None
    assert chunks_per_seq == (LANES if carry else 1)
    ncp = max(m // chunk, LANES)
    gt = GROUPS_PER_TILE
    tile3 = lambda a, b: pl.BlockSpec((gt, a, b), lambda i: (i, 0, 0))
    lane_tile = pl.BlockSpec((m, LANES), lambda i: (0, i))
    in_specs = [lane_tile, tile3(n + 2 * SSM_STATE, n), tile3(n, 2 * SSM_STATE), tile3(SSM_STATE, N_SCAN + 1),
                tile3(SSM_STATE, N_SCAN + 1), pl.BlockSpec((1, LANES), lambda i: (0, i))]
    args = [proj, wt, qt, dr, di, d_skip.reshape(1, -1)]
    if not carry:
        in_specs.append(tile3(2 * SSM_STATE, ncp))
        args.append(h0)
    return pl.pallas_call(
        functools.partial(_s5_scan_kernel, chunk=chunk, carry=carry),
        grid=(SSM_GROUPS // gt,),
        in_specs=in_specs,
        out_specs=(lane_tile, tile3(2 * SSM_STATE, ncp)),
        out_shape=(jax.ShapeDtypeStruct((m, SSM_GROUPS * SSM_GROUP), F32),
                   jax.ShapeDtypeStruct((SSM_GROUPS, 2 * SSM_STATE, ncp), F32)),
        scratch_shapes=[pltpu.VMEM((chunk, LANES, ncp), BF16), pltpu.VMEM((chunk, LANES, ncp), F32)],
        compiler_params=_cparams("parallel"),
        name="s5_scan",
    )(*args)


def _glu_out_kernel(y_ref, z_ref, x_ref, wg_ref, bg_ref, wo_ref, gf_ref, o_ref):
    g = jax.nn.gelu(y_ref[...], approximate=True)
    gate = jnp.dot(g.astype(BF16), wg_ref[...], preferred_element_type=F32) + bg_ref[...]
    yy = (g * jax.nn.sigmoid(gate)) * _silu(z_ref[...])
    h = x_ref[...] + jnp.dot(yy.astype(BF16), wo_ref[...], preferred_element_type=F32)
    o_ref[...] = _rms_norm(h, gf_ref[...])


def _glu_out(y, proj, x, w_glu, b_glu, w_out, g_final, *, tm):
    m, w = y.shape
    row = lambda c: pl.BlockSpec((tm, w), lambda i: (i, c))
    vec = pl.BlockSpec((1, w), lambda i: (0, 0))
    mat = pl.BlockSpec((w, w), lambda i: (0, 0), pipeline_mode=pl.Buffered(1))
    return pl.pallas_call(
        _glu_out_kernel,
        grid=(m // tm,),
        in_specs=[row(0), row(1), row(0), mat, vec, mat, vec],
        out_specs=row(0),
        out_shape=jax.ShapeDtypeStruct((m, w), F32),
        compiler_params=_cparams("parallel"),
        name="glu_out",
    )(y, proj, x, w_glu, b_glu.reshape(1, w), w_out, g_final.reshape(1, w))


PAST_LEN = 16384
PROMPT_CHUNK = 32


def kernel(x_prompt, x_sample, cache_win_k, cache_win_v, state_conv, state_ssm_re, state_ssm_im, attn_norm, w_in_ab, conv_w, w_out_ab, ssm_norm, w_in_c, lam_re, lam_im, log_step, b_re, b_im, c_re, c_im, d_skip, w_glu, b_glu, w_out_c, final_norm):
    bp, tp, _ = x_prompt.shape
    bs, ts, _ = x_sample.shape
    n_keep = min(2048, tp)
    xp = x_prompt.reshape(bp * tp, D_MODEL)
    xs = x_sample.reshape(bs * ts, D_MODEL)

    w_in0 = w_in_ab[0].astype(BF16)
    w_out0 = w_out_ab[0].astype(BF16)
    proj_p = _norm_matmul(xp, attn_norm[0], w_in0, _rope_tables(tp, tp, 0), tm=512, tn=1024, rope_tiles=2,
                          tiles_per_seq=tp // 512)
    proj_s = _norm_matmul(xs, attn_norm[0], w_in0, _rope_tables(bs * ts, ts, PAST_LEN), tm=bs * ts, tn=1024,
                          rope_tiles=2)
    o_p = _attn_prompt(proj_p, bp, tp).reshape(bp * tp, ATTN_WIDTH)
    o_s = _attn_sample(proj_s, cache_win_k[0], cache_win_v[0], bs, ts).reshape(bs * ts, ATTN_WIDTH)
    h1_p, conv_p = _mix_out(proj_p, o_p, xp, jnp.zeros((bp, 2, CONV_WIDTH), F32), conv_w[0], w_out0,
                            tm=256, tiles_per_seq=tp // 256)
    h1_s, conv_s = _mix_out(proj_s, o_s, xs, state_conv[0], conv_w[0], w_out0, tm=ts, tiles_per_seq=1)
    pp = proj_p.reshape(bp, tp, -1)
    k_p = pp[:, tp - n_keep:, ATTN_WIDTH:2 * ATTN_WIDTH].reshape(1, bp, n_keep, N_HEADS, HEAD_DIM)
    v_p = pp[:, tp - n_keep:, 2 * ATTN_WIDTH:3 * ATTN_WIDTH].reshape(1, bp, n_keep, N_HEADS, HEAD_DIM)
    k_s = proj_s[:, ATTN_WIDTH:2 * ATTN_WIDTH].reshape(1, bs, ts, N_HEADS, HEAD_DIM)
    v_s = proj_s[:, 2 * ATTN_WIDTH:3 * ATTN_WIDTH].reshape(1, bs, ts, N_HEADS, HEAD_DIM)

    w_in1 = w_in_c[0].astype(BF16)
    w_glu1 = w_glu[0].astype(BF16)
    w_out1 = w_out_c[0].astype(BF16)
    ssm = (lam_re[0], lam_im[0], log_step[0], b_re[0], b_im[0], c_re[0], c_im[0])
    proj1_p = _norm_matmul(h1_p, ssm_norm[0], w_in1, tm=512, tn=1024)
    proj1_s = _norm_matmul(h1_s, ssm_norm[0], w_in1, tm=bs * ts, tn=1024)
    y_p, hf_p = _s5_scan(proj1_p, _s5_operators(*ssm, PROMPT_CHUNK), d_skip[0], None,
                         chunk=PROMPT_CHUNK, chunks_per_seq=tp // PROMPT_CHUNK)
    h0 = jnp.concatenate([state_ssm_re[0], state_ssm_im[0]], axis=-1).transpose(1, 2, 0)
    h0 = jnp.pad(h0, ((0, 0), (0, 0), (0, LANES - bs)))
    y_s, hf_s = _s5_scan(proj1_s, _s5_operators(*ssm, ts), d_skip[0], h0, chunk=ts, chunks_per_seq=1)
    out_p = _glu_out(y_p, proj1_p, h1_p, w_glu1, b_glu[0], w_out1, final_norm, tm=256)
    out_s = _glu_out(y_s, proj1_s, h1_s, w_glu1, b_glu[0], w_out1, final_norm, tm=bs * ts)
    last = hf_p[:, :, LANES - 1::LANES].transpose(2, 0, 1)
    fin_s = hf_s[:, :, :bs].transpose(2, 0, 1)
    return (out_p.reshape(bp, tp, D_MODEL), out_s.reshape(bs, ts, D_MODEL),
            k_p, v_p, conv_p[None], last[None, :, :, :SSM_STATE], last[None, :, :, SSM_STATE:],
            k_s, v_s, conv_s[None], fin_s[None, :, :, :SSM_STATE], fin_s[None, :, :, SSM_STATE:])
```

```python
import functools
import math

import jax
import jax.numpy as jnp
from jax import lax
from jax.experimental import pallas as pl
from jax.experimental.pallas import tpu as pltpu

D_MODEL = 2048
HEAD_DIM = 128
N_HEADS = 8
ATTN_WIDTH = 1024
CONV_WIDTH = 1024
DILATIONS = (1, 4, 16)
N_BACK = 128
ROPE_THETA = 10000.0
RMS_EPS = 1e-6
SSM_GROUP = 16
SSM_GROUPS = 128
SSM_STATE = 64
LANES = 128
VMEM_LIMIT = 56 * 1024 * 1024

F32 = jnp.float32
BF16 = jnp.bfloat16


def _cparams(*sem):
    return pltpu.CompilerParams(dimension_semantics=sem, vmem_limit_bytes=VMEM_LIMIT)


def _rope_table_kernel(inv_ref, cos_ref, sin_ref, *, period, offset):
    rows = cos_ref.shape[0]
    r = lax.broadcasted_iota(jnp.int32, (rows, LANES), 0) + pl.program_id(0) * rows
    pos = (offset + lax.rem(r, period)).astype(F32)
    ang = pos * inv_ref[...]
    lane = lax.broadcasted_iota(jnp.int32, (rows, LANES), 1)
    cos_ref[...] = jnp.cos(ang)
    sin_ref[...] = jnp.where(lane < HEAD_DIM // 2, -1.0, 1.0) * jnp.sin(ang)


def _rope_tables(rows, period, offset):
    half = HEAD_DIM // 2
    inv = ROPE_THETA ** (-jnp.arange(half, dtype=F32) / half)
    inv2 = jnp.concatenate([inv, inv])[None, :]
    tr = min(rows, 256)
    return pl.pallas_call(
        functools.partial(_rope_table_kernel, period=period, offset=offset),
        grid=(rows // tr,),
        in_specs=[pl.BlockSpec((1, LANES), lambda i: (0, 0))],
        out_specs=(pl.BlockSpec((tr, LANES), lambda i: (i, 0)),) * 2,
        out_shape=(jax.ShapeDtypeStruct((rows, LANES), F32),) * 2,
        compiler_params=_cparams("parallel"),
        name="rope_table",
    )(inv2)


def _rms_norm(x, g):
    return x * lax.rsqrt(jnp.mean(x * x, axis=-1, keepdims=True) + RMS_EPS) * g


def _norm_matmul_kernel(x_ref, g_ref, w_ref, *rest, rope_tiles):
    o_ref, hn_ref = rest[-2:]
    n = pl.program_id(1)

    @pl.when(n == 0)
    def _():
        hn_ref[...] = _rms_norm(x_ref[...], g_ref[...]).astype(BF16)

    acc = jnp.dot(hn_ref[...], w_ref[...], preferred_element_type=F32)

    @pl.when(n >= rope_tiles)
    def _():
        o_ref[...] = acc

    if rope_tiles:
        cos_ref, sin_ref = rest[:2]

        @pl.when(n < rope_tiles)
        def _():
            cos = cos_ref[...]
            sin = sin_ref[...]
            for h in range(o_ref.shape[1] // HEAD_DIM):
                xh = acc[:, h * HEAD_DIM:(h + 1) * HEAD_DIM]
                o_ref[:, h * HEAD_DIM:(h + 1) * HEAD_DIM] = xh * cos + pltpu.roll(xh, HEAD_DIM // 2, 1) * sin


def _norm_matmul(x, g, w, rope=None, *, tm, tn, rope_tiles=0, tiles_per_seq=1):
    m, k = x.shape
    n = w.shape[1]
    table = pl.BlockSpec((tm, LANES), lambda i, j: (i % tiles_per_seq, 0))
    return pl.pallas_call(
        functools.partial(_norm_matmul_kernel, rope_tiles=rope_tiles),
        grid=(m // tm, n // tn),
        in_specs=[
            pl.BlockSpec((tm, k), lambda i, j: (i, 0)),
            pl.BlockSpec((1, k), lambda i, j: (0, 0)),
            pl.BlockSpec((k, tn), lambda i, j: (0, j)),
        ] + ([table, table] if rope_tiles else []),
        out_specs=pl.BlockSpec((tm, tn), lambda i, j: (i, j)),
        out_shape=jax.ShapeDtypeStruct((m, n), F32),
        scratch_shapes=[pltpu.VMEM((tm, k), BF16)],
        compiler_params=_cparams("parallel", "arbitrary"),
        name="norm_matmul",
    )(x, g.reshape(1, k), w, *(rope if rope_tiles else ()))


def _attn_prompt_kernel(q_ref, k_ref, v_ref, o_ref, ks_ref, vs_ref, qc_ref, kc_ref, vc_ref, oc_ref, lc_ref,
                        to_ref, tl_ref, lse_ref):
    t = q_ref.shape[1]
    n_keep = ks_ref.shape[1]
    ks_ref[0] = k_ref[0, t - n_keep:, :]
    vs_ref[0] = v_ref[0, t - n_keep:, :]
    nblk = t // N_BACK
    scale = HEAD_DIM ** -0.5
    kc_ref[0:N_BACK, :] = jnp.zeros((N_BACK, HEAD_DIM), BF16)
    vc_ref[0:N_BACK, :] = jnp.zeros((N_BACK, HEAD_DIM), BF16)
    qi = lax.broadcasted_iota(jnp.int32, (N_BACK, 2 * N_BACK), 0)
    kj = lax.broadcasted_iota(jnp.int32, (N_BACK, 2 * N_BACK), 1)
    dist = N_BACK + qi - kj
    band = (dist >= 0) & (dist <= N_BACK)

    for d in DILATIONS:
        ln = t // d
        blocks_per_class = ln // N_BACK
        for r in range(d):
            rows = pl.ds(r, ln, stride=d) if d > 1 else pl.ds(0, ln)
            qc_ref[r * ln:(r + 1) * ln, :] = q_ref[0, rows, :].astype(BF16)
            kc_ref[N_BACK + r * ln:N_BACK + (r + 1) * ln, :] = k_ref[0, rows, :].astype(BF16)
            vc_ref[N_BACK + r * ln:N_BACK + (r + 1) * ln, :] = v_ref[0, rows, :].astype(BF16)

        def block(b, carry):
            row0 = pl.multiple_of(b * N_BACK, N_BACK)
            j = lax.rem(b, blocks_per_class)
            q = qc_ref[pl.ds(row0, N_BACK), :]
            k2 = kc_ref[pl.ds(row0, 2 * N_BACK), :]
            v2 = vc_ref[pl.ds(row0, 2 * N_BACK), :]
            s = lax.dot_general(q, k2, (((1,), (1,)), ((), ())), preferred_element_type=F32) * scale
            valid = band & ((kj >= N_BACK) | (j > 0))
            s = jnp.where(valid, s, -jnp.inf)
            m = jnp.max(s, axis=-1, keepdims=True)
            p = jnp.exp(s - m)
            l = jnp.sum(p, axis=-1, keepdims=True)
            o = jnp.dot(p.astype(BF16), v2, preferred_element_type=F32)
            oc_ref[pl.ds(row0, N_BACK), :] = o / l
            lc_ref[pl.ds(row0, N_BACK), :] = jnp.broadcast_to(m + jnp.log(l), (N_BACK, HEAD_DIM))
            return carry

        lax.fori_loop(0, nblk, block, 0, unroll=16)

        if d == 1:
            o_ref[0] = oc_ref[...]
            lse_ref[...] = lc_ref[...]
        else:
            for r in range(d):
                to_ref[pl.ds(r, ln, stride=d), :] = oc_ref[r * ln:(r + 1) * ln, :]
                tl_ref[pl.ds(r, ln, stride=d), :] = lc_ref[r * ln:(r + 1) * ln, :]

            def merge(c, carry):
                rows = pl.ds(pl.multiple_of(c * 64, 64), 64)
                la = lse_ref[rows, :]
                lb = tl_ref[rows, :]
                mx = jnp.maximum(la, lb)
                ea = jnp.exp(la - mx)
                eb = jnp.exp(lb - mx)
                tot = ea + eb
                o_ref[0, rows, :] = (o_ref[0, rows, :] * ea + to_ref[rows, :] * eb) / tot
                lse_ref[rows, :] = mx + jnp.log(tot)
                return carry

            lax.fori_loop(0, t // 64, merge, 0, unroll=2)


def _attn_prompt(proj, b, t, n_keep):
    p3 = proj.reshape(b, t, proj.shape[1])
    blk = lambda off: pl.BlockSpec((1, t, HEAD_DIM), lambda i, h: (i, 0, off + h))
    keep = pl.BlockSpec((1, n_keep, HEAD_DIM), lambda i, h: (i, 0, h))
    return pl.pallas_call(
        _attn_prompt_kernel,
        grid=(b, N_HEADS),
        in_specs=[blk(0), blk(N_HEADS), blk(2 * N_HEADS)],
        out_specs=(pl.BlockSpec((1, t, HEAD_DIM), lambda i, h: (i, 0, h)), keep, keep),
        out_shape=(jax.ShapeDtypeStruct((b, t, ATTN_WIDTH), F32),
                   jax.ShapeDtypeStruct((b, n_keep, ATTN_WIDTH), F32),
                   jax.ShapeDtypeStruct((b, n_keep, ATTN_WIDTH), F32)),
        scratch_shapes=[
            pltpu.VMEM((t, HEAD_DIM), BF16),
            pltpu.VMEM((t + N_BACK, HEAD_DIM), BF16),
            pltpu.VMEM((t + N_BACK, HEAD_DIM), BF16),
            pltpu.VMEM((t, HEAD_DIM), F32),
            pltpu.VMEM((t, HEAD_DIM), F32),
            pltpu.VMEM((t, HEAD_DIM), F32),
            pltpu.VMEM((t, HEAD_DIM), F32),
            pltpu.VMEM((t, HEAD_DIM), F32),
        ],
        compiler_params=_cparams("parallel", "parallel"),
        name="attn_prompt",
    )(p3, p3, p3)


def _attn_sample_kernel(q_ref, kn_ref, vn_ref, kc_ref, vc_ref, o_ref):
    n_buf = kc_ref.shape[1]
    scale = HEAD_DIM ** -0.5
    nt = (((1,), (1,)), ((), ()))
    q = q_ref[0].astype(BF16)
    sc = lax.dot_general(q, kc_ref[0].astype(BF16), nt, preferred_element_type=F32) * scale
    sn = lax.dot_general(q, kn_ref[0].astype(BF16), nt, preferred_element_type=F32) * scale

    def count(dist):
        c = jnp.zeros(dist.shape, F32)
        for d in DILATIONS:
            hit = (dist >= 0) & (dist <= N_BACK * d) & ((dist & (d - 1)) == 0)
            c = c + jnp.where(hit, 1.0, 0.0)
        return c

    cc = count(n_buf + lax.broadcasted_iota(jnp.int32, sc.shape, 0) - lax.broadcasted_iota(jnp.int32, sc.shape, 1))
    cn = count(lax.broadcasted_iota(jnp.int32, sn.shape, 0) - lax.broadcasted_iota(jnp.int32, sn.shape, 1))
    sc = jnp.where(cc > 0, sc, -jnp.inf)
    sn = jnp.where(cn > 0, sn, -jnp.inf)
    m = jnp.maximum(jnp.max(sc, axis=-1, keepdims=True), jnp.max(sn, axis=-1, keepdims=True))
    pc = cc * jnp.exp(sc - m)
    pn = cn * jnp.exp(sn - m)
    l = jnp.sum(pc, axis=-1, keepdims=True) + jnp.sum(pn, axis=-1, keepdims=True)
    o = (jnp.dot(pc.astype(BF16), vc_ref[0].astype(BF16), preferred_element_type=F32)
         + jnp.dot(pn.astype(BF16), vn_ref[0].astype(BF16), preferred_element_type=F32))
    o_ref[0] = o / l


def _attn_sample(proj, cache_k, cache_v, b, s_len):
    n_buf = cache_k.shape[1]
    p3 = proj.reshape(b, s_len, proj.shape[1])
    ck = cache_k.reshape(b, n_buf, ATTN_WIDTH)
    cv = cache_v.reshape(b, n_buf, ATTN_WIDTH)
    new = lambda off: pl.BlockSpec((1, s_len, HEAD_DIM), lambda i, h: (i, 0, off + h))
    old = pl.BlockSpec((1, n_buf, HEAD_DIM), lambda i, h: (i, 0, h))
    return pl.pallas_call(
        _attn_sample_kernel,
        grid=(b, N_HEADS),
        in_specs=[new(0), new(N_HEADS), new(2 * N_HEADS), old, old],
        out_specs=pl.BlockSpec((1, s_len, HEAD_DIM), lambda i, h: (i, 0, h)),
        out_shape=jax.ShapeDtypeStruct((b, s_len, ATTN_WIDTH), F32),
        compiler_params=_cparams("parallel", "parallel"),
        name="attn_sample",
    )(p3, p3, p3, ck, cv)


def _silu(z):
    return z * jax.nn.sigmoid(z)


def _mix_out_kernel(o_ref, za_ref, gb_ref, gc_ref, hi_ref, zb_ref, gch_ref, hih_ref, init_ref, cw_ref,
                    w_ref, x_ref, h_ref, cs_ref, *, tiles_per_seq):
    tm = o_ref.shape[0]
    first = (pl.program_id(0) % tiles_per_seq) == 0
    ch = gc_ref[...] * hi_ref[...]
    halo = gch_ref[...] * hih_ref[...]
    prev1 = jnp.where(first, init_ref[0, 1:2, :], halo[7:8, :])
    prev2 = jnp.where(first, init_ref[0, 0:1, :], halo[6:7, :])
    row = lax.broadcasted_iota(jnp.int32, (tm, 1), 0)
    ch1 = jnp.where(row == 0, prev1, pltpu.roll(ch, 1, 0))
    ch2 = jnp.where(row == 0, prev2, jnp.where(row == 1, prev1, pltpu.roll(ch, 2, 0)))
    conv = ch2 * cw_ref[0:1, :] + ch1 * cw_ref[1:2, :] + ch * cw_ref[2:3, :]
    cs_ref[0] = ch[tm - 2:tm, :]
    o_a = (o_ref[...] * _silu(za_ref[...])).astype(BF16)
    o_b = (gb_ref[...] * conv * _silu(zb_ref[...])).astype(BF16)
    y = jnp.dot(o_a, w_ref[0:ATTN_WIDTH, :], preferred_element_type=F32)
    y = y + jnp.dot(o_b, w_ref[ATTN_WIDTH:, :], preferred_element_type=F32)
    h_ref[...] = x_ref[...] + y


def _mix_out(proj, o_attn, x, conv_init, conv_w, w_out, *, tm, tiles_per_seq):
    m = x.shape[0]
    nseq = conv_init.shape[0]
    col = lambda c: pl.BlockSpec((tm, CONV_WIDTH), lambda i: (i, c))
    halo = lambda c: pl.BlockSpec((8, CONV_WIDTH), lambda i: (jnp.maximum(i * (tm // 8) - 1, 0), c))
    seq = pl.BlockSpec((1, 2, CONV_WIDTH), lambda i: (i // tiles_per_seq, 0, 0))
    return pl.pallas_call(
        functools.partial(_mix_out_kernel, tiles_per_seq=tiles_per_seq),
        grid=(m // tm,),
        in_specs=[
            pl.BlockSpec((tm, ATTN_WIDTH), lambda i: (i, 0)),
            col(3), col(4), col(5), col(6), col(7), halo(5), halo(6), seq,
            pl.BlockSpec((3, CONV_WIDTH), lambda i: (0, 0)),
            pl.BlockSpec((D_MODEL, D_MODEL), lambda i: (0, 0), pipeline_mode=pl.Buffered(1)),
            pl.BlockSpec((tm, D_MODEL), lambda i: (i, 0)),
        ],
        out_specs=(pl.BlockSpec((tm, D_MODEL), lambda i: (i, 0)), seq),
        out_shape=(jax.ShapeDtypeStruct((m, D_MODEL), F32), jax.ShapeDtypeStruct((nseq, 2, CONV_WIDTH), F32)),
        compiler_params=_cparams("arbitrary"),
        name="mix_out",
    )(o_attn, proj, proj, proj, proj, proj, proj, proj, conv_init, conv_w, w_out, x)


N_SCAN = 7
GROUPS_PER_TILE = LANES // SSM_GROUP


def _s5_disc_kernel(lr_ref, li_ref, ls_ref, pr_ref, pi_ref, cr_ref, ci_ref, dr_ref, di_ref, *, chunk):
    lr = lr_ref[...]
    li = li_ref[...]
    step = jnp.exp(ls_ref[...])
    mag = jnp.exp(lr * step)
    ar = mag * jnp.cos(li * step)
    ai = mag * jnp.sin(li * step)
    den = lr * lr + li * li
    nr = ar - 1.0
    cr_ref[...] = (nr * lr + ai * li) / den
    ci_ref[...] = (ai * lr - nr * li) / den
    pr = jnp.ones_like(ar)
    pi = jnp.zeros_like(ar)
    for tau in range(chunk + 1):
        pr_ref[tau] = pr
        pi_ref[tau] = pi
        dr, di = pr, pi
        pr, pi = pr * ar - pi * ai, pr * ai + pi * ar
    for i in range(N_SCAN + 1):
        dr_ref[i] = dr
        di_ref[i] = di
        dr, di = dr * dr - di * di, 2.0 * dr * di


def _s5_prep_kernel(pcr_ref, pci_ref, ccr_ref, cci_ref, br_ref, bi_ref, cr_ref, ci_ref, c2r_ref, c2i_ref,
                    prr_ref, pri_ref, e_ref, tile_ref, wt_ref, qt_ref, *, chunk):
    n = SSM_GROUP * chunk
    dot = lambda a, b: jnp.dot(a, b, precision=lax.Precision.HIGHEST, preferred_element_type=F32)
    lane = lax.broadcasted_iota(jnp.int32, (SSM_GROUP, 2 * SSM_STATE), 1)

    def one_group(g, carry):
        ccr, cci = ccr_ref[g], cci_ref[g]
        br, bi = br_ref[g], bi_ref[g]
        bbr = ccr * br - cci * bi
        bbi = ccr * bi + cci * br
        aer = dot(pcr_ref[g], e_ref[...])
        aei = dot(pci_ref[g], e_ref[...])
        btr = dot(bbr, tile_ref[...])
        bti = dot(bbi, tile_ref[...])
        ptr = aer * btr - aei * bti
        pti = aer * bti + aei * btr
        wt_ref[g, n:n + SSM_STATE, :] = ptr.astype(BF16)
        wt_ref[g, n + SSM_STATE:n + 2 * SSM_STATE, :] = pti.astype(BF16)
        taps = dot(cr_ref[g], ptr) - dot(ci_ref[g], pti)
        padded = jnp.concatenate([taps, jnp.zeros_like(taps)], axis=1)
        for t in range(chunk):
            sh = SSM_GROUP * (chunk - 1 - t)
            blk = padded if sh == 0 else pltpu.roll(padded, 2 * n - sh, 1)
            wt_ref[g, SSM_GROUP * t:SSM_GROUP * (t + 1), :] = blk[:, :n].astype(BF16)
        c2r, c2i = c2r_ref[g], c2i_ref[g]
        for t in range(chunk):
            ar = prr_ref[g, t + 1:t + 2, :]
            ai = pri_ref[g, t + 1:t + 2, :]
            x1 = jnp.where(lane < SSM_STATE, ar, -ai)
            x2 = jnp.where(lane < SSM_STATE, ai, ar)
            qt_ref[g, SSM_GROUP * t:SSM_GROUP * (t + 1), :] = (c2r * x1 - c2i * x2).astype(BF16)
        return carry

    lax.fori_loop(0, wt_ref.shape[0], one_group, 0)


def _s5_operators(lam_re, lam_im, log_step, b_re, b_im, c_re, c_im, chunk):
    g = lam_re.shape[0]
    n = SSM_GROUP * chunk
    dup = lambda a: jnp.concatenate([a, a], axis=-1)
    sds = lambda *s: jax.ShapeDtypeStruct(s, F32)
    pr, pi, cr, ci, dr, di = pl.pallas_call(
        functools.partial(_s5_disc_kernel, chunk=chunk),
        out_shape=(sds(chunk + 1, g, LANES), sds(chunk + 1, g, LANES), sds(g, LANES), sds(g, LANES),
                   sds(N_SCAN + 1, g, LANES), sds(N_SCAN + 1, g, LANES)),
        name="s5_disc",
    )(dup(lam_re), dup(lam_im), log_step[:, None])
    cols = lambda a: a[:, :, :SSM_STATE].transpose(1, 2, 0)
    rows = lambda a: a.transpose(1, 0, 2)
    s_idx = jnp.arange(n) // SSM_GROUP
    e_sel = (jnp.arange(chunk)[:, None] == (chunk - 1 - s_idx)[None, :]).astype(F32)
    tile = (jnp.arange(SSM_GROUP)[:, None] == (jnp.arange(n) % SSM_GROUP)[None, :]).astype(F32)
    per_g = lambda *s: pl.BlockSpec((GROUPS_PER_TILE,) + s, lambda i: (i,) + (0,) * len(s))
    whole = lambda *s: pl.BlockSpec(s, lambda i: (0,) * len(s))
    wt, qt = pl.pallas_call(
        functools.partial(_s5_prep_kernel, chunk=chunk),
        grid=(g // GROUPS_PER_TILE,),
        in_specs=[per_g(SSM_STATE, chunk), per_g(SSM_STATE, chunk), per_g(SSM_STATE, 1), per_g(SSM_STATE, 1),
                  per_g(SSM_STATE, SSM_GROUP), per_g(SSM_STATE, SSM_GROUP),
                  per_g(SSM_GROUP, SSM_STATE), per_g(SSM_GROUP, SSM_STATE),
                  per_g(SSM_GROUP, LANES), per_g(SSM_GROUP, LANES),
                  per_g(chunk + 1, LANES), per_g(chunk + 1, LANES), whole(chunk, n), whole(SSM_GROUP, n)],
        out_specs=(per_g(n + 2 * SSM_STATE, n), per_g(n, 2 * SSM_STATE)),
        out_shape=(jax.ShapeDtypeStruct((g, n + 2 * SSM_STATE, n), BF16),
                   jax.ShapeDtypeStruct((g, n, 2 * SSM_STATE), BF16)),
        compiler_params=_cparams("parallel"),
        name="s5_prep",
    )(cols(pr[:chunk]), cols(pi[:chunk]), cr[:, :SSM_STATE, None], ci[:, :SSM_STATE, None], b_re, b_im,
      c_re, c_im, dup(c_re), dup(c_im), rows(pr), rows(pi), e_sel, tile)
    return wt, qt, cols(dr), cols(di)


def _s5_scan_kernel(u_ref, wt_ref, qt_ref, dr_ref, di_ref, dsk_ref, *rest, chunk, carry):
    if carry:
        y_ref, hf_ref, dall_ref, yall_ref = rest
    else:
        h0_ref, y_ref, hf_ref, dall_ref, yall_ref = rest
    nc = u_ref.shape[0] // chunk
    ncp = dall_ref.shape[2]
    n = SSM_GROUP * chunk
    for s in range(chunk):
        v = u_ref[pl.ds(s, nc, stride=chunk), :]
        if nc < ncp:
            v = jnp.concatenate([v, jnp.zeros((ncp - nc, LANES), F32)], axis=0)
        dall_ref[s] = v.T.astype(BF16)
    lane = lax.broadcasted_iota(jnp.int32, (SSM_STATE, ncp), 1) & (LANES - 1)
    for gp in range(GROUPS_PER_TILE):
        rows = slice(SSM_GROUP * gp, SSM_GROUP * (gp + 1))
        d = dall_ref[:, rows, :].reshape(n, ncp)
        ys = jnp.dot(wt_ref[gp], d, preferred_element_type=F32)
        sr = ys[n:n + SSM_STATE]
        si = ys[n + SSM_STATE:]
        if carry:
            for i in range(N_SCAN):
                sh = 1 << i
                ar = dr_ref[gp, :, i:i + 1]
                ai = di_ref[gp, :, i:i + 1]
                pr = jnp.where(lane >= sh, pltpu.roll(sr, sh, 1), 0.0)
                pi = jnp.where(lane >= sh, pltpu.roll(si, sh, 1), 0.0)
                sr, si = sr + ar * pr - ai * pi, si + ar * pi + ai * pr
            hf_ref[gp, 0:SSM_STATE, :] = sr
            hf_ref[gp, SSM_STATE:, :] = si
            hpr = jnp.where(lane >= 1, pltpu.roll(sr, 1, 1), 0.0)
            hpi = jnp.where(lane >= 1, pltpu.roll(si, 1, 1), 0.0)
        else:
            hpr = h0_ref[gp, 0:SSM_STATE, :]
            hpi = h0_ref[gp, SSM_STATE:, :]
            ar = dr_ref[gp, :, 0:1]
            ai = di_ref[gp, :, 0:1]
            hf_ref[gp, 0:SSM_STATE, :] = sr + ar * hpr - ai * hpi
            hf_ref[gp, SSM_STATE:, :] = si + ar * hpi + ai * hpr
        hp = jnp.concatenate([hpr, hpi], axis=0).astype(BF16)
        y = ys[:n] + jnp.dot(qt_ref[gp], hp, preferred_element_type=F32)
        yall_ref[:, rows, :] = y.reshape(chunk, SSM_GROUP, ncp)
    for t in range(chunk):
        tok = pl.ds(t, nc, stride=chunk)
        y_ref[tok, :] = yall_ref[t].T[:nc] + dsk_ref[...] * u_ref[tok, :]


def _s5_scan(proj, ops, d_skip, h0, *, chunk, chunks_per_seq):
    wt, qt, dr, di = ops
    m = proj.shape[0]
    n = SSM_GROUP * chunk
    carry = h0 is None
    assert chunks_per_seq == (LANES if carry else 1)
    ncp = max(m // chunk, LANES)
    gt = GROUPS_PER_TILE
    tile3 = lambda a, b: pl.BlockSpec((gt, a, b), lambda i: (i, 0, 0))
    lane_tile = pl.BlockSpec((m, LANES), lambda i: (0, i))
    in_specs = [lane_tile, tile3(n + 2 * SSM_STATE, n), tile3(n, 2 * SSM_STATE), tile3(SSM_STATE, N_SCAN + 1),
                tile3(SSM_STATE, N_SCAN + 1), pl.BlockSpec((1, LANES), lambda i: (0, i))]
    args = [proj, wt, qt, dr, di, d_skip.reshape(1, -1)]
    if not carry:
        in_specs.append(tile3(2 * SSM_STATE, ncp))
        args.append(h0)
    return pl.pallas_call(
        functools.partial(_s5_scan_kernel, chunk=chunk, carry=carry),
        grid=(SSM_GROUPS // gt,),
        in_specs=in_specs,
        out_specs=(lane_tile, tile3(2 * SSM_STATE, ncp)),
        out_shape=(jax.ShapeDtypeStruct((m, SSM_GROUPS * SSM_GROUP), F32),
                   jax.ShapeDtypeStruct((SSM_GROUPS, 2 * SSM_STATE, ncp), F32)),
        scratch_shapes=[pltpu.VMEM((chunk, LANES, ncp), BF16), pltpu.VMEM((chunk, LANES, ncp), F32)],
        compiler_params=_cparams("parallel"),
        name="s5_scan",
    )(*args)


def _glu_out_kernel(y_ref, z_ref, x_ref, wg_ref, bg_ref, wo_ref, gf_ref, o_ref):
    g = jax.nn.gelu(y_ref[...], approximate=True)
    gate = jnp.dot(g.astype(BF16), wg_ref[...], preferred_element_type=F32) + bg_ref[...]
    yy = (g * jax.nn.sigmoid(gate)) * _silu(z_ref[...])
    h = x_ref[...] + jnp.dot(yy.astype(BF16), wo_ref[...], preferred_element_type=F32)
    o_ref[...] = _rms_norm(h, gf_ref[...])


def _glu_out(y, proj, x, w_glu, b_glu, w_out, g_final, *, tm):
    m, w = y.shape
    row = lambda c: pl.BlockSpec((tm, w), lambda i: (i, c))
    vec = pl.BlockSpec((1, w), lambda i: (0, 0))
    mat = pl.BlockSpec((w, w), lambda i: (0, 0), pipeline_mode=pl.Buffered(1))
    return pl.pallas_call(
        _glu_out_kernel,
        grid=(m // tm,),
        in_specs=[row(0), row(1), row(0), mat, vec, mat, vec],
        out_specs=row(0),
        out_shape=jax.ShapeDtypeStruct((m, w), F32),
        compiler_params=_cparams("parallel"),
        name="glu_out",
    )(y, proj, x, w_glu, b_glu.reshape(1, w), w_out, g_final.reshape(1, w))


PAST_LEN = 16384
PROMPT_CHUNK = 32


def kernel(x_prompt, x_sample, cache_win_k, cache_win_v, state_conv, state_ssm_re, state_ssm_im, attn_norm, w_in_ab, conv_w, w_out_ab, ssm_norm, w_in_c, lam_re, lam_im, log_step, b_re, b_im, c_re, c_im, d_skip, w_glu, b_glu, w_out_c, final_norm):
    bp, tp, _ = x_prompt.shape
    bs, ts, _ = x_sample.shape
    n_keep = min(2048, tp)
    xp = x_prompt.reshape(bp * tp, D_MODEL)
    xs = x_sample.reshape(bs * ts, D_MODEL)

    w_in0 = w_in_ab[0].astype(BF16)
    w_out0 = w_out_ab[0].astype(BF16)
    proj_p = _norm_matmul(xp, attn_norm[0], w_in0, _rope_tables(tp, tp, 0), tm=512, tn=1024, rope_tiles=2,
                          tiles_per_seq=tp // 512)
    proj_s = _norm_matmul(xs, attn_norm[0], w_in0, _rope_tables(bs * ts, ts, PAST_LEN), tm=bs * ts, tn=1024,
                          rope_tiles=2)
    o_p, k_p, v_p = _attn_prompt(proj_p, bp, tp, n_keep)
    o_p = o_p.reshape(bp * tp, ATTN_WIDTH)
    k_p = k_p.reshape(1, bp, n_keep, N_HEADS, HEAD_DIM)
    v_p = v_p.reshape(1, bp, n_keep, N_HEADS, HEAD_DIM)
    o_s = _attn_sample(proj_s, cache_win_k[0], cache_win_v[0], bs, ts).reshape(bs * ts, ATTN_WIDTH)
    h1_p, conv_p = _mix_out(proj_p, o_p, xp, jnp.zeros((bp, 2, CONV_WIDTH), F32), conv_w[0], w_out0,
                            tm=256, tiles_per_seq=tp // 256)
    h1_s, conv_s = _mix_out(proj_s, o_s, xs, state_conv[0], conv_w[0], w_out0, tm=ts, tiles_per_seq=1)
    k_s = proj_s[:, ATTN_WIDTH:2 * ATTN_WIDTH].reshape(1, bs, ts, N_HEADS, HEAD_DIM)
    v_s = proj_s[:, 2 * ATTN_WIDTH:3 * ATTN_WIDTH].reshape(1, bs, ts, N_HEADS, HEAD_DIM)

    w_in1 = w_in_c[0].astype(BF16)
    w_glu1 = w_glu[0].astype(BF16)
    w_out1 = w_out_c[0].astype(BF16)
    ssm = (lam_re[0], lam_im[0], log_step[0], b_re[0], b_im[0], c_re[0], c_im[0])
    proj1_p = _norm_matmul(h1_p, ssm_norm[0], w_in1, tm=512, tn=1024)
    proj1_s = _norm_matmul(h1_s, ssm_norm[0], w_in1, tm=bs * ts, tn=1024)
    y_p, hf_p = _s5_scan(proj1_p, _s5_operators(*ssm, PROMPT_CHUNK), d_skip[0], None,
                         chunk=PROMPT_CHUNK, chunks_per_seq=tp // PROMPT_CHUNK)
    h0 = jnp.concatenate([state_ssm_re[0], state_ssm_im[0]], axis=-1).transpose(1, 2, 0)
    h0 = jnp.pad(h0, ((0, 0), (0, 0), (0, LANES - bs)))
    y_s, hf_s = _s5_scan(proj1_s, _s5_operators(*ssm, ts), d_skip[0], h0, chunk=ts, chunks_per_seq=1)
    out_p = _glu_out(y_p, proj1_p, h1_p, w_glu1, b_glu[0], w_out1, final_norm, tm=256)
    out_s = _glu_out(y_s, proj1_s, h1_s, w_glu1, b_glu[0], w_out1, final_norm, tm=bs * ts)
    last = hf_p[:, :, LANES - 1::LANES].transpose(2, 0, 1)
    fin_s = hf_s[:, :, :bs].transpose(2, 0, 1)
    return (out_p.reshape(bp, tp, D_MODEL), out_s.reshape(bs, ts, D_MODEL),
            k_p, v_p, conv_p[None], last[None, :, :, :SSM_STATE], last[None, :, :, SSM_STATE:],
            k_s, v_s, conv_s[None], fin_s[None, :, :, :SSM_STATE], fin_s[None, :, :, SSM_STATE:])
```

```python
import functools
import math

import jax
import jax.numpy as jnp
from jax import lax
from jax.experimental import pallas as pl
from jax.experimental.pallas import tpu as pltpu

D_MODEL = 2048
HEAD_DIM = 128
N_HEADS = 8
ATTN_WIDTH = 1024
CONV_WIDTH = 1024
DILATIONS = (1, 4, 16)
N_BACK = 128
ROPE_THETA = 10000.0
RMS_EPS = 1e-6
SSM_GROUP = 16
SSM_GROUPS = 128
SSM_STATE = 64
LANES = 128
VMEM_LIMIT = 56 * 1024 * 1024

F32 = jnp.float32
BF16 = jnp.bfloat16


def _cparams(*sem):
    return pltpu.CompilerParams(dimension_semantics=sem, vmem_limit_bytes=VMEM_LIMIT)


def _rope_table_kernel(inv_ref, cos_ref, sin_ref, *, period, offset):
    rows = cos_ref.shape[0]
    r = lax.broadcasted_iota(jnp.int32, (rows, LANES), 0) + pl.program_id(0) * rows
    pos = (offset + lax.rem(r, period)).astype(F32)
    ang = pos * inv_ref[...]
    lane = lax.broadcasted_iota(jnp.int32, (rows, LANES), 1)
    cos_ref[...] = jnp.cos(ang)
    sin_ref[...] = jnp.where(lane < HEAD_DIM // 2, -1.0, 1.0) * jnp.sin(ang)


def _rope_tables(rows, period, offset):
    half = HEAD_DIM // 2
    inv = ROPE_THETA ** (-jnp.arange(half, dtype=F32) / half)
    inv2 = jnp.concatenate([inv, inv])[None, :]
    tr = min(rows, 256)
    return pl.pallas_call(
        functools.partial(_rope_table_kernel, period=period, offset=offset),
        grid=(rows // tr,),
        in_specs=[pl.BlockSpec((1, LANES), lambda i: (0, 0))],
        out_specs=(pl.BlockSpec((tr, LANES), lambda i: (i, 0)),) * 2,
        out_shape=(jax.ShapeDtypeStruct((rows, LANES), F32),) * 2,
        compiler_params=_cparams("parallel"),
        name="rope_table",
    )(inv2)


def _rms_norm(x, g):
    return x * lax.rsqrt(jnp.mean(x * x, axis=-1, keepdims=True) + RMS_EPS) * g


def _norm_matmul_kernel(x_ref, g_ref, w_ref, *rest, rope_tiles):
    o_ref, hn_ref = rest[-2:]
    n = pl.program_id(1)

    @pl.when(n == 0)
    def _():
        hn_ref[...] = _rms_norm(x_ref[...], g_ref[...]).astype(BF16)

    acc = jnp.dot(hn_ref[...], w_ref[...], preferred_element_type=F32)

    @pl.when(n >= rope_tiles)
    def _():
        o_ref[...] = acc

    if rope_tiles:
        cos_ref, sin_ref = rest[:2]

        @pl.when(n < rope_tiles)
        def _():
            cos = cos_ref[...]
            sin = sin_ref[...]
            for h in range(o_ref.shape[1] // HEAD_DIM):
                xh = acc[:, h * HEAD_DIM:(h + 1) * HEAD_DIM]
                o_ref[:, h * HEAD_DIM:(h + 1) * HEAD_DIM] = xh * cos + pltpu.roll(xh, HEAD_DIM // 2, 1) * sin


def _norm_matmul(x, g, w, rope=None, *, tm, tn, rope_tiles=0, tiles_per_seq=1):
    m, k = x.shape
    n = w.shape[1]
    table = pl.BlockSpec((tm, LANES), lambda i, j: (i % tiles_per_seq, 0))
    return pl.pallas_call(
        functools.partial(_norm_matmul_kernel, rope_tiles=rope_tiles),
        grid=(m // tm, n // tn),
        in_specs=[
            pl.BlockSpec((tm, k), lambda i, j: (i, 0)),
            pl.BlockSpec((1, k), lambda i, j: (0, 0)),
            pl.BlockSpec((k, tn), lambda i, j: (0, j)),
        ] + ([table, table] if rope_tiles else []),
        out_specs=pl.BlockSpec((tm, tn), lambda i, j: (i, j)),
        out_shape=jax.ShapeDtypeStruct((m, n), F32),
        scratch_shapes=[pltpu.VMEM((tm, k), BF16)],
        compiler_params=_cparams("parallel", "arbitrary"),
        name="norm_matmul",
    )(x, g.reshape(1, k), w, *(rope if rope_tiles else ()))


def _attn_prompt_kernel(q_ref, k_ref, v_ref, o_ref, ks_ref, vs_ref, qc_ref, kc_ref, vc_ref, oc_ref, lc_ref,
                        to_ref, tl_ref, lse_ref):
    t = q_ref.shape[1]
    n_keep = ks_ref.shape[1]
    ks_ref[0] = k_ref[0, t - n_keep:, :]
    vs_ref[0] = v_ref[0, t - n_keep:, :]
    nblk = t // N_BACK
    scale = HEAD_DIM ** -0.5
    kc_ref[0:N_BACK, :] = jnp.zeros((N_BACK, HEAD_DIM), BF16)
    vc_ref[0:N_BACK, :] = jnp.zeros((N_BACK, HEAD_DIM), BF16)
    qi = lax.broadcasted_iota(jnp.int32, (N_BACK, 2 * N_BACK), 0)
    kj = lax.broadcasted_iota(jnp.int32, (N_BACK, 2 * N_BACK), 1)
    dist = N_BACK + qi - kj
    band = (dist >= 0) & (dist <= N_BACK)

    for d in DILATIONS:
        ln = t // d
        blocks_per_class = ln // N_BACK
        for r in range(d):
            rows = pl.ds(r, ln, stride=d) if d > 1 else pl.ds(0, ln)
            qc_ref[r * ln:(r + 1) * ln, :] = q_ref[0, rows, :].astype(BF16)
            kc_ref[N_BACK + r * ln:N_BACK + (r + 1) * ln, :] = k_ref[0, rows, :].astype(BF16)
            vc_ref[N_BACK + r * ln:N_BACK + (r + 1) * ln, :] = v_ref[0, rows, :].astype(BF16)

        def block(b, carry):
            row0 = pl.multiple_of(b * N_BACK, N_BACK)
            j = lax.rem(b, blocks_per_class)
            q = qc_ref[pl.ds(row0, N_BACK), :]
            k2 = kc_ref[pl.ds(row0, 2 * N_BACK), :]
            v2 = vc_ref[pl.ds(row0, 2 * N_BACK), :]
            s = lax.dot_general(q, k2, (((1,), (1,)), ((), ())), preferred_element_type=F32) * scale
            valid = band & ((kj >= N_BACK) | (j > 0))
            s = jnp.where(valid, s, -jnp.inf)
            m = jnp.max(s, axis=-1, keepdims=True)
            p = jnp.exp(s - m)
            l = jnp.sum(p, axis=-1, keepdims=True)
            o = jnp.dot(p.astype(BF16), v2, preferred_element_type=F32)
            oc_ref[pl.ds(row0, N_BACK), :] = o / l
            lc_ref[pl.ds(row0, N_BACK), :] = jnp.broadcast_to(m + jnp.log(l), (N_BACK, HEAD_DIM))
            return carry

        lax.fori_loop(0, nblk, block, 0, unroll=16)

        if d == 1:
            o_ref[0] = oc_ref[...]
            lse_ref[...] = lc_ref[...]
        else:
            for r in range(d):
                to_ref[pl.ds(r, ln, stride=d), :] = oc_ref[r * ln:(r + 1) * ln, :]
                tl_ref[pl.ds(r, ln, stride=d), :] = lc_ref[r * ln:(r + 1) * ln, :]

            def merge(c, carry):
                rows = pl.ds(pl.multiple_of(c * 64, 64), 64)
                la = lse_ref[rows, :]
                lb = tl_ref[rows, :]
                mx = jnp.maximum(la, lb)
                ea = jnp.exp(la - mx)
                eb = jnp.exp(lb - mx)
                tot = ea + eb
                o_ref[0, rows, :] = (o_ref[0, rows, :] * ea + to_ref[rows, :] * eb) / tot
                lse_ref[rows, :] = mx + jnp.log(tot)
                return carry

            lax.fori_loop(0, t // 64, merge, 0, unroll=2)


def _attn_prompt(proj, b, t, n_keep):
    p3 = proj.reshape(b, t, proj.shape[1])
    blk = lambda off: pl.BlockSpec((1, t, HEAD_DIM), lambda i, h: (i, 0, off + h))
    keep = pl.BlockSpec((1, n_keep, HEAD_DIM), lambda i, h: (i, 0, h))
    return pl.pallas_call(
        _attn_prompt_kernel,
        grid=(b, N_HEADS),
        in_specs=[blk(0), blk(N_HEADS), blk(2 * N_HEADS)],
        out_specs=(pl.BlockSpec((1, t, HEAD_DIM), lambda i, h: (i, 0, h)), keep, keep),
        out_shape=(jax.ShapeDtypeStruct((b, t, ATTN_WIDTH), F32),
                   jax.ShapeDtypeStruct((b, n_keep, ATTN_WIDTH), F32),
                   jax.ShapeDtypeStruct((b, n_keep, ATTN_WIDTH), F32)),
        scratch_shapes=[
            pltpu.VMEM((t, HEAD_DIM), BF16),
            pltpu.VMEM((t + N_BACK, HEAD_DIM), BF16),
            pltpu.VMEM((t + N_BACK, HEAD_DIM), BF16),
            pltpu.VMEM((t, HEAD_DIM), F32),
            pltpu.VMEM((t, HEAD_DIM), F32),
            pltpu.VMEM((t, HEAD_DIM), F32),
            pltpu.VMEM((t, HEAD_DIM), F32),
            pltpu.VMEM((t, HEAD_DIM), F32),
        ],
        compiler_params=_cparams("parallel", "parallel"),
        name="attn_prompt",
    )(p3, p3, p3)


def _attn_sample_kernel(q_ref, kn_ref, vn_ref, kc_ref, vc_ref, o_ref):
    s_len = q_ref.shape[0]
    n_buf = kc_ref.shape[1] // N_HEADS
    scale = HEAD_DIM ** -0.5
    nt = (((1,), (1,)), ((), ()))

    def count(dist):
        c = jnp.zeros(dist.shape, F32)
        for d in DILATIONS:
            hit = (dist >= 0) & (dist <= N_BACK * d) & ((dist & (d - 1)) == 0)
            c = c + jnp.where(hit, 1.0, 0.0)
        return c

    iota = lambda shape, dim: lax.broadcasted_iota(jnp.int32, shape, dim)
    cc = count(n_buf + iota((s_len, n_buf), 0) - iota((s_len, n_buf), 1))
    cn = count(iota((s_len, s_len), 0) - iota((s_len, s_len), 1))
    for h in range(N_HEADS):
        cols = slice(h * HEAD_DIM, (h + 1) * HEAD_DIM)
        head_rows = pl.ds(h, n_buf, stride=N_HEADS)
        q = q_ref[:, cols].astype(BF16)
        sc = lax.dot_general(q, kc_ref[0, head_rows, :].astype(BF16), nt, preferred_element_type=F32) * scale
        sn = lax.dot_general(q, kn_ref[:, cols].astype(BF16), nt, preferred_element_type=F32) * scale
        sc = jnp.where(cc > 0, sc, -jnp.inf)
        sn = jnp.where(cn > 0, sn, -jnp.inf)
        m = jnp.maximum(jnp.max(sc, axis=-1, keepdims=True), jnp.max(sn, axis=-1, keepdims=True))
        pc = cc * jnp.exp(sc - m)
        pn = cn * jnp.exp(sn - m)
        l = jnp.sum(pc, axis=-1, keepdims=True) + jnp.sum(pn, axis=-1, keepdims=True)
        o = (jnp.dot(pc.astype(BF16), vc_ref[0, head_rows, :].astype(BF16), preferred_element_type=F32)
             + jnp.dot(pn.astype(BF16), vn_ref[:, cols].astype(BF16), preferred_element_type=F32))
        o_ref[:, cols] = o / l


def _attn_sample(proj, cache_k, cache_v, b, s_len):
    n_buf = cache_k.shape[1]
    ck = cache_k.reshape(b, n_buf * N_HEADS, HEAD_DIM)
    cv = cache_v.reshape(b, n_buf * N_HEADS, HEAD_DIM)
    new = lambda c: pl.BlockSpec((s_len, ATTN_WIDTH), lambda i: (i, c))
    old = pl.BlockSpec((1, n_buf * N_HEADS, HEAD_DIM), lambda i: (i, 0, 0))
    return pl.pallas_call(
        _attn_sample_kernel,
        grid=(b,),
        in_specs=[new(0), new(1), new(2), old, old],
        out_specs=new(0),
        out_shape=jax.ShapeDtypeStruct((b * s_len, ATTN_WIDTH), F32),
        compiler_params=_cparams("parallel"),
        name="attn_sample",
    )(proj, proj, proj, ck, cv)


def _silu(z):
    return z * jax.nn.sigmoid(z)


def _mix_out_kernel(o_ref, za_ref, gb_ref, gc_ref, hi_ref, zb_ref, gch_ref, hih_ref, init_ref, cw_ref,
                    w_ref, x_ref, h_ref, cs_ref, *, tiles_per_seq):
    tm = o_ref.shape[0]
    first = (pl.program_id(0) % tiles_per_seq) == 0
    ch = gc_ref[...] * hi_ref[...]
    halo = gch_ref[...] * hih_ref[...]
    prev1 = jnp.where(first, init_ref[0, 1:2, :], halo[7:8, :])
    prev2 = jnp.where(first, init_ref[0, 0:1, :], halo[6:7, :])
    row = lax.broadcasted_iota(jnp.int32, (tm, 1), 0)
    ch1 = jnp.where(row == 0, prev1, pltpu.roll(ch, 1, 0))
    ch2 = jnp.where(row == 0, prev2, jnp.where(row == 1, prev1, pltpu.roll(ch, 2, 0)))
    conv = ch2 * cw_ref[0:1, :] + ch1 * cw_ref[1:2, :] + ch * cw_ref[2:3, :]
    cs_ref[0] = ch[tm - 2:tm, :]
    o_a = (o_ref[...] * _silu(za_ref[...])).astype(BF16)
    o_b = (gb_ref[...] * conv * _silu(zb_ref[...])).astype(BF16)
    y = jnp.dot(o_a, w_ref[0:ATTN_WIDTH, :], preferred_element_type=F32)
    y = y + jnp.dot(o_b, w_ref[ATTN_WIDTH:, :], preferred_element_type=F32)
    h_ref[...] = x_ref[...] + y


def _mix_out(proj, o_attn, x, conv_init, conv_w, w_out, *, tm, tiles_per_seq):
    m = x.shape[0]
    nseq = conv_init.shape[0]
    col = lambda c: pl.BlockSpec((tm, CONV_WIDTH), lambda i: (i, c))
    halo = lambda c: pl.BlockSpec((8, CONV_WIDTH), lambda i: (jnp.maximum(i * (tm // 8) - 1, 0), c))
    seq = pl.BlockSpec((1, 2, CONV_WIDTH), lambda i: (i // tiles_per_seq, 0, 0))
    return pl.pallas_call(
        functools.partial(_mix_out_kernel, tiles_per_seq=tiles_per_seq),
        grid=(m // tm,),
        in_specs=[
            pl.BlockSpec((tm, ATTN_WIDTH), lambda i: (i, 0)),
            col(3), col(4), col(5), col(6), col(7), halo(5), halo(6), seq,
            pl.BlockSpec((3, CONV_WIDTH), lambda i: (0, 0)),
            pl.BlockSpec((D_MODEL, D_MODEL), lambda i: (0, 0), pipeline_mode=pl.Buffered(1)),
            pl.BlockSpec((tm, D_MODEL), lambda i: (i, 0)),
        ],
        out_specs=(pl.BlockSpec((tm, D_MODEL), lambda i: (i, 0)), seq),
        out_shape=(jax.ShapeDtypeStruct((m, D_MODEL), F32), jax.ShapeDtypeStruct((nseq, 2, CONV_WIDTH), F32)),
        compiler_params=_cparams("arbitrary"),
        name="mix_out",
    )(o_attn, proj, proj, proj, proj, proj, proj, proj, conv_init, conv_w, w_out, x)


N_SCAN = 7
GROUPS_PER_TILE = LANES // SSM_GROUP


PROMPT_CHUNK = 32
ROW_C = PROMPT_CHUNK + 1
ROW_DBL = PROMPT_CHUNK + 2
TABLE_ROWS = ROW_DBL + N_SCAN


def _s5_disc_kernel(lr_ref, li_ref, ls_ref, tr_ref, ti_ref):
    lr = lr_ref[...]
    li = li_ref[...]
    step = jnp.exp(ls_ref[...])
    mag = jnp.exp(lr * step)
    ar = mag * jnp.cos(li * step)
    ai = mag * jnp.sin(li * step)
    den = lr * lr + li * li
    nr = ar - 1.0
    tr_ref[ROW_C] = (nr * lr + ai * li) / den
    ti_ref[ROW_C] = (ai * lr - nr * li) / den
    pr = jnp.ones_like(ar)
    pi = jnp.zeros_like(ar)
    for tau in range(PROMPT_CHUNK + 1):
        tr_ref[tau] = pr
        ti_ref[tau] = pi
        dr, di = pr, pi
        pr, pi = pr * ar - pi * ai, pr * ai + pi * ar
    for i in range(N_SCAN):
        tr_ref[ROW_DBL + i] = dr
        ti_ref[ROW_DBL + i] = di
        dr, di = dr * dr - di * di, 2.0 * dr * di


def _s5_tables(lam_re, lam_im, log_step):
    g = lam_re.shape[0]
    dup = lambda a: jnp.concatenate([a, a], axis=-1)
    tr, ti = pl.pallas_call(
        _s5_disc_kernel,
        out_shape=(jax.ShapeDtypeStruct((TABLE_ROWS, g, LANES), F32),) * 2,
        name="s5_disc",
    )(dup(lam_re), dup(lam_im), log_step[:, None])
    return tr.transpose(1, 0, 2), ti.transpose(1, 0, 2)


def _dot_exact(a, b, dims=(((1,), (0,)), ((), ()))):
    return lax.dot_general(a, b, dims, precision=lax.Precision.HIGHEST, preferred_element_type=F32)


NT_DIMS = (((1,), (1,)), ((), ()))


def _s5_prep_kernel(tr_ref, ti_ref, br_ref, bi_ref, cr_ref, ci_ref, c2r_ref, c2i_ref,
                    wt_ref, qt_ref, kr_ref, ki_ref, *, chunk):
    n = SSM_GROUP * chunk
    iota = lambda shape, dim: lax.broadcasted_iota(jnp.int32, shape, dim)
    lane = iota((SSM_GROUP, 2 * SSM_STATE), 1)
    eye = jnp.where(iota((SSM_STATE, LANES), 0) == iota((SSM_STATE, LANES), 1), 1.0, 0.0)
    e_sel = jnp.where(iota((chunk, n), 0) == chunk - 1 - iota((chunk, n), 1) // SSM_GROUP, 1.0, 0.0)
    tile = jnp.where(iota((SSM_GROUP, n), 0) == iota((SSM_GROUP, n), 1) % SSM_GROUP, 1.0, 0.0)

    def one_group(g, carry):
        tab_r, tab_i = tr_ref[g], ti_ref[g]
        col_r = _dot_exact(eye, tab_r, NT_DIMS)
        col_i = _dot_exact(eye, tab_i, NT_DIMS)
        kr_ref[g] = col_r
        ki_ref[g] = col_i
        ccr, cci = col_r[:, ROW_C:ROW_C + 1], col_i[:, ROW_C:ROW_C + 1]
        br, bi = br_ref[g], bi_ref[g]
        bbr = ccr * br - cci * bi
        bbi = ccr * bi + cci * br
        aer = _dot_exact(col_r[:, :chunk], e_sel)
        aei = _dot_exact(col_i[:, :chunk], e_sel)
        btr = _dot_exact(bbr, tile)
        bti = _dot_exact(bbi, tile)
        ptr = aer * btr - aei * bti
        pti = aer * bti + aei * btr
        wt_ref[g, n:n + SSM_STATE, :] = ptr.astype(BF16)
        wt_ref[g, n + SSM_STATE:n + 2 * SSM_STATE, :] = pti.astype(BF16)
        taps = _dot_exact(cr_ref[g], ptr) - _dot_exact(ci_ref[g], pti)
        padded = jnp.concatenate([taps, jnp.zeros_like(taps)], axis=1)
        for t in range(chunk):
            sh = SSM_GROUP * (chunk - 1 - t)
            blk = padded if sh == 0 else pltpu.roll(padded, 2 * n - sh, 1)
            wt_ref[g, SSM_GROUP * t:SSM_GROUP * (t + 1), :] = blk[:, :n].astype(BF16)
        c2r, c2i = c2r_ref[g], c2i_ref[g]
        for t in range(chunk):
            ar = tab_r[t + 1:t + 2, :]
            ai = tab_i[t + 1:t + 2, :]
            x1 = jnp.where(lane < SSM_STATE, ar, -ai)
            x2 = jnp.where(lane < SSM_STATE, ai, ar)
            qt_ref[g, SSM_GROUP * t:SSM_GROUP * (t + 1), :] = (c2r * x1 - c2i * x2).astype(BF16)
        return carry

    lax.fori_loop(0, wt_ref.shape[0], one_group, 0)


def _s5_operators(tables, b_re, b_im, c_re, c_im, chunk):
    g = b_re.shape[0]
    n = SSM_GROUP * chunk
    assert chunk <= PROMPT_CHUNK
    dup = lambda a: jnp.concatenate([a, a], axis=-1)
    per_g = lambda *s: pl.BlockSpec((GROUPS_PER_TILE,) + s, lambda i: (i,) + (0,) * len(s))
    return pl.pallas_call(
        functools.partial(_s5_prep_kernel, chunk=chunk),
        grid=(g // GROUPS_PER_TILE,),
        in_specs=[per_g(TABLE_ROWS, LANES), per_g(TABLE_ROWS, LANES),
                  per_g(SSM_STATE, SSM_GROUP), per_g(SSM_STATE, SSM_GROUP),
                  per_g(SSM_GROUP, SSM_STATE), per_g(SSM_GROUP, SSM_STATE),
                  per_g(SSM_GROUP, LANES), per_g(SSM_GROUP, LANES)],
        out_specs=(per_g(n + 2 * SSM_STATE, n), per_g(n, 2 * SSM_STATE),
                   per_g(SSM_STATE, TABLE_ROWS), per_g(SSM_STATE, TABLE_ROWS)),
        out_shape=(jax.ShapeDtypeStruct((g, n + 2 * SSM_STATE, n), BF16),
                   jax.ShapeDtypeStruct((g, n, 2 * SSM_STATE), BF16),
                   jax.ShapeDtypeStruct((g, SSM_STATE, TABLE_ROWS), F32),
                   jax.ShapeDtypeStruct((g, SSM_STATE, TABLE_ROWS), F32)),
        compiler_params=_cparams("parallel"),
        name="s5_prep",
    )(*tables, b_re, b_im, c_re, c_im, dup(c_re), dup(c_im))


def _s5_scan_kernel(u_ref, wt_ref, qt_ref, kr_ref, ki_ref, dsk_ref, *rest, chunk, carry):
    if carry:
        y_ref, hfr_ref, hfi_ref, dall_ref, yall_ref = rest
    else:
        h0r_ref, h0i_ref, y_ref, hfr_ref, hfi_ref, dall_ref, yall_ref = rest
    nc = u_ref.shape[0] // chunk
    ncp = dall_ref.shape[2]
    nseq = hfr_ref.shape[0]
    n = SSM_GROUP * chunk
    iota = lambda shape, dim: lax.broadcasted_iota(jnp.int32, shape, dim)
    for s in range(chunk):
        v = u_ref[pl.ds(s, nc, stride=chunk), :]
        if nc < ncp:
            v = jnp.concatenate([v, jnp.zeros((ncp - nc, LANES), F32)], axis=0)
        dall_ref[s] = v.T.astype(BF16)
    lane = iota((SSM_STATE, ncp), 1) & (LANES - 1)
    last_lane = iota((8, ncp), 0) * LANES + (LANES - 1) if carry else iota((8, ncp), 0)
    pick = jnp.where(iota((8, ncp), 1) == last_lane, 1.0, 0.0)
    eye = jnp.where(iota((SSM_STATE, SSM_STATE), 0) == iota((SSM_STATE, SSM_STATE), 1), 1.0, 0.0)
    for gp in range(GROUPS_PER_TILE):
        rows = slice(SSM_GROUP * gp, SSM_GROUP * (gp + 1))
        d = dall_ref[:, rows, :].reshape(n, ncp)
        ys = jnp.dot(wt_ref[gp], d, preferred_element_type=F32)
        sr = ys[n:n + SSM_STATE]
        si = ys[n + SSM_STATE:]
        if carry:
            for i in range(N_SCAN):
                sh = 1 << i
                ar = kr_ref[gp, :, ROW_DBL + i:ROW_DBL + i + 1]
                ai = ki_ref[gp, :, ROW_DBL + i:ROW_DBL + i + 1]
                pr = jnp.where(lane >= sh, pltpu.roll(sr, sh, 1), 0.0)
                pi = jnp.where(lane >= sh, pltpu.roll(si, sh, 1), 0.0)
                sr, si = sr + ar * pr - ai * pi, si + ar * pi + ai * pr
            er, ei = sr, si
            hpr = jnp.where(lane >= 1, pltpu.roll(sr, 1, 1), 0.0)
            hpi = jnp.where(lane >= 1, pltpu.roll(si, 1, 1), 0.0)
        else:
            pad = jnp.zeros((ncp - nseq, SSM_STATE), F32)
            hpr = _dot_exact(eye, jnp.concatenate([h0r_ref[:, gp, :], pad], axis=0), NT_DIMS)
            hpi = _dot_exact(eye, jnp.concatenate([h0i_ref[:, gp, :], pad], axis=0), NT_DIMS)
            ar = kr_ref[gp, :, chunk:chunk + 1]
            ai = ki_ref[gp, :, chunk:chunk + 1]
            er = sr + ar * hpr - ai * hpi
            ei = si + ar * hpi + ai * hpr
        hfr_ref[:, gp, :] = _dot_exact(pick, er, NT_DIMS)[:nseq]
        hfi_ref[:, gp, :] = _dot_exact(pick, ei, NT_DIMS)[:nseq]
        hp = jnp.concatenate([hpr, hpi], axis=0).astype(BF16)
        y = ys[:n] + jnp.dot(qt_ref[gp], hp, preferred_element_type=F32)
        yall_ref[:, rows, :] = y.reshape(chunk, SSM_GROUP, ncp)
    for t in range(chunk):
        tok = pl.ds(t, nc, stride=chunk)
        y_ref[tok, :] = yall_ref[t].T[:nc] + dsk_ref[...] * u_ref[tok, :]


def _s5_scan(proj, ops, d_skip, h0, nseq, *, chunk):
    wt, qt, kr, ki = ops
    m = proj.shape[0]
    n = SSM_GROUP * chunk
    carry = h0 is None
    assert m == nseq * chunk * (LANES if carry else 1) and nseq <= 8
    ncp = max(m // chunk, LANES)
    gt = GROUPS_PER_TILE
    tile3 = lambda a, b: pl.BlockSpec((gt, a, b), lambda i: (i, 0, 0))
    lane_tile = pl.BlockSpec((m, LANES), lambda i: (0, i))
    state = pl.BlockSpec((nseq, gt, SSM_STATE), lambda i: (0, i, 0))
    in_specs = [lane_tile, tile3(n + 2 * SSM_STATE, n), tile3(n, 2 * SSM_STATE), tile3(SSM_STATE, TABLE_ROWS),
                tile3(SSM_STATE, TABLE_ROWS), pl.BlockSpec((1, LANES), lambda i: (0, i))]
    args = [proj, wt, qt, kr, ki, d_skip.reshape(1, -1)]
    if not carry:
        in_specs += [state, state]
        args += list(h0)
    sds = jax.ShapeDtypeStruct((nseq, SSM_GROUPS, SSM_STATE), F32)
    return pl.pallas_call(
        functools.partial(_s5_scan_kernel, chunk=chunk, carry=carry),
        grid=(SSM_GROUPS // gt,),
        in_specs=in_specs,
        out_specs=(lane_tile, state, state),
        out_shape=(jax.ShapeDtypeStruct((m, SSM_GROUPS * SSM_GROUP), F32), sds, sds),
        scratch_shapes=[pltpu.VMEM((chunk, LANES, ncp), BF16), pltpu.VMEM((chunk, LANES, ncp), F32)],
        compiler_params=_cparams("parallel"),
        name="s5_scan",
    )(*args)


def _glu_out_kernel(y_ref, z_ref, x_ref, wg_ref, bg_ref, wo_ref, gf_ref, o_ref):
    g = jax.nn.gelu(y_ref[...], approximate=True)
    gate = jnp.dot(g.astype(BF16), wg_ref[...], preferred_element_type=F32) + bg_ref[...]
    yy = (g * jax.nn.sigmoid(gate)) * _silu(z_ref[...])
    h = x_ref[...] + jnp.dot(yy.astype(BF16), wo_ref[...], preferred_element_type=F32)
    o_ref[...] = _rms_norm(h, gf_ref[...])


def _glu_out(y, proj, x, w_glu, b_glu, w_out, g_final, *, tm):
    m, w = y.shape
    row = lambda c: pl.BlockSpec((tm, w), lambda i: (i, c))
    vec = pl.BlockSpec((1, w), lambda i: (0, 0))
    mat = pl.BlockSpec((w, w), lambda i: (0, 0), pipeline_mode=pl.Buffered(1))
    return pl.pallas_call(
        _glu_out_kernel,
        grid=(m // tm,),
        in_specs=[row(0), row(1), row(0), mat, vec, mat, vec],
        out_specs=row(0),
        out_shape=jax.ShapeDtypeStruct((m, w), F32),
        compiler_params=_cparams("parallel"),
        name="glu_out",
    )(y, proj, x, w_glu, b_glu.reshape(1, w), w_out, g_final.reshape(1, w))


PAST_LEN = 16384


def kernel(x_prompt, x_sample, cache_win_k, cache_win_v, state_conv, state_ssm_re, state_ssm_im, attn_norm, w_in_ab, conv_w, w_out_ab, ssm_norm, w_in_c, lam_re, lam_im, log_step, b_re, b_im, c_re, c_im, d_skip, w_glu, b_glu, w_out_c, final_norm):
    bp, tp, _ = x_prompt.shape
    bs, ts, _ = x_sample.shape
    n_keep = min(2048, tp)
    xp = x_prompt.reshape(bp * tp, D_MODEL)
    xs = x_sample.reshape(bs * ts, D_MODEL)

    w_in0 = w_in_ab[0].astype(BF16)
    w_out0 = w_out_ab[0].astype(BF16)
    proj_p = _norm_matmul(xp, attn_norm[0], w_in0, _rope_tables(tp, tp, 0), tm=512, tn=1024, rope_tiles=2,
                          tiles_per_seq=tp // 512)
    proj_s = _norm_matmul(xs, attn_norm[0], w_in0, _rope_tables(bs * ts, ts, PAST_LEN), tm=bs * ts, tn=1024,
                          rope_tiles=2)
    o_p, k_p, v_p = _attn_prompt(proj_p, bp, tp, n_keep)
    o_p = o_p.reshape(bp * tp, ATTN_WIDTH)
    k_p = k_p.reshape(1, bp, n_keep, N_HEADS, HEAD_DIM)
    v_p = v_p.reshape(1, bp, n_keep, N_HEADS, HEAD_DIM)
    o_s = _attn_sample(proj_s, cache_win_k[0], cache_win_v[0], bs, ts)
    h1_p, conv_p = _mix_out(proj_p, o_p, xp, jnp.zeros((bp, 2, CONV_WIDTH), F32), conv_w[0], w_out0,
                            tm=256, tiles_per_seq=tp // 256)
    h1_s, conv_s = _mix_out(proj_s, o_s, xs, state_conv[0], conv_w[0], w_out0, tm=ts, tiles_per_seq=1)
    k_s = proj_s[:, ATTN_WIDTH:2 * ATTN_WIDTH].reshape(1, bs, ts, N_HEADS, HEAD_DIM)
    v_s = proj_s[:, 2 * ATTN_WIDTH:3 * ATTN_WIDTH].reshape(1, bs, ts, N_HEADS, HEAD_DIM)

    w_in1 = w_in_c[0].astype(BF16)
    w_glu1 = w_glu[0].astype(BF16)
    w_out1 = w_out_c[0].astype(BF16)
    tables = _s5_tables(lam_re[0], lam_im[0], log_step[0])
    bc = (b_re[0], b_im[0], c_re[0], c_im[0])
    proj1_p = _norm_matmul(h1_p, ssm_norm[0], w_in1, tm=512, tn=1024)
    proj1_s = _norm_matmul(h1_s, ssm_norm[0], w_in1, tm=bs * ts, tn=1024)
    y_p, hr_p, hi_p = _s5_scan(proj1_p, _s5_operators(tables, *bc, PROMPT_CHUNK), d_skip[0], None, bp,
                               chunk=PROMPT_CHUNK)
    y_s, hr_s, hi_s = _s5_scan(proj1_s, _s5_operators(tables, *bc, ts), d_skip[0],
                               (state_ssm_re[0], state_ssm_im[0]), bs, chunk=ts)
    out_p = _glu_out(y_p, proj1_p, h1_p, w_glu1, b_glu[0], w_out1, final_norm, tm=256)
    out_s = _glu_out(y_s, proj1_s, h1_s, w_glu1, b_glu[0], w_out1, final_norm, tm=bs * ts)
    return (out_p.reshape(bp, tp, D_MODEL), out_s.reshape(bs, ts, D_MODEL),
            k_p, v_p, conv_p[None], hr_p[None], hi_p[None], k_s, v_s, conv_s[None], hr_s[None], hi_s[None])
```

```python
import functools
import math

import jax
import jax.numpy as jnp
from jax import lax
from jax.experimental import pallas as pl
from jax.experimental.pallas import tpu as pltpu

D_MODEL = 2048
HEAD_DIM = 128
N_HEADS = 8
ATTN_WIDTH = 1024
CONV_WIDTH = 1024
DILATIONS = (1, 4, 16)
N_BACK = 128
ROPE_THETA = 10000.0
RMS_EPS = 1e-6
SSM_GROUP = 16
SSM_GROUPS = 128
SSM_STATE = 64
LANES = 128
VMEM_LIMIT = 56 * 1024 * 1024

F32 = jnp.float32
BF16 = jnp.bfloat16


def _cparams(*sem):
    return pltpu.CompilerParams(dimension_semantics=sem, vmem_limit_bytes=VMEM_LIMIT)


def _rope_table_kernel(inv_ref, cos_ref, sin_ref, *, period, offset):
    rows = cos_ref.shape[0]
    r = lax.broadcasted_iota(jnp.int32, (rows, LANES), 0) + pl.program_id(0) * rows
    pos = (offset + lax.rem(r, period)).astype(F32)
    ang = pos * inv_ref[...]
    lane = lax.broadcasted_iota(jnp.int32, (rows, LANES), 1)
    cos_ref[...] = jnp.cos(ang)
    sin_ref[...] = jnp.where(lane < HEAD_DIM // 2, -1.0, 1.0) * jnp.sin(ang)


def _rope_tables(rows, period, offset):
    half = HEAD_DIM // 2
    inv = ROPE_THETA ** (-jnp.arange(half, dtype=F32) / half)
    inv2 = jnp.concatenate([inv, inv])[None, :]
    tr = min(rows, 256)
    return pl.pallas_call(
        functools.partial(_rope_table_kernel, period=period, offset=offset),
        grid=(rows // tr,),
        in_specs=[pl.BlockSpec((1, LANES), lambda i: (0, 0))],
        out_specs=(pl.BlockSpec((tr, LANES), lambda i: (i, 0)),) * 2,
        out_shape=(jax.ShapeDtypeStruct((rows, LANES), F32),) * 2,
        compiler_params=_cparams("parallel"),
        name="rope_table",
    )(inv2)


def _rms_norm(x, g):
    return x * lax.rsqrt(jnp.mean(x * x, axis=-1, keepdims=True) + RMS_EPS) * g


def _norm_matmul_kernel(x_ref, g_ref, w_ref, *rest, rope_tiles):
    o_ref, hn_ref = rest[-2:]
    n = pl.program_id(1)

    @pl.when(n == 0)
    def _():
        hn_ref[...] = _rms_norm(x_ref[...], g_ref[...]).astype(BF16)

    acc = jnp.dot(hn_ref[...], w_ref[...], preferred_element_type=F32)

    @pl.when(n >= rope_tiles)
    def _():
        o_ref[...] = acc

    if rope_tiles:
        cos_ref, sin_ref = rest[:2]

        @pl.when(n < rope_tiles)
        def _():
            cos = cos_ref[...]
            sin = sin_ref[...]
            for h in range(o_ref.shape[1] // HEAD_DIM):
                xh = acc[:, h * HEAD_DIM:(h + 1) * HEAD_DIM]
                o_ref[:, h * HEAD_DIM:(h + 1) * HEAD_DIM] = xh * cos + pltpu.roll(xh, HEAD_DIM // 2, 1) * sin


def _norm_matmul(x, g, w, rope=None, *, tm, tn, rope_tiles=0, tiles_per_seq=1, chunk=None):
    m, k = x.shape
    n = w.shape[1]
    table = pl.BlockSpec((tm, LANES), lambda i, j: (i % tiles_per_seq, 0))
    if chunk is None:
        x_spec = pl.BlockSpec((tm, k), lambda i, j: (i, 0))
        out_spec = pl.BlockSpec((tm, tn), lambda i, j: (i, j))
        out_shape = jax.ShapeDtypeStruct((m, n), F32)
    else:
        assert tm * chunk == m
        x = x.reshape(tm, chunk * k)
        x_spec = pl.BlockSpec((tm, k), lambda i, j: (0, i))
        out_spec = pl.BlockSpec((None, tm, tn), lambda i, j: (i, 0, j))
        out_shape = jax.ShapeDtypeStruct((chunk, tm, n), F32)
    return pl.pallas_call(
        functools.partial(_norm_matmul_kernel, rope_tiles=rope_tiles),
        grid=(m // tm, n // tn),
        in_specs=[x_spec, pl.BlockSpec((1, k), lambda i, j: (0, 0)), pl.BlockSpec((k, tn), lambda i, j: (0, j))]
        + ([table, table] if rope_tiles else []),
        out_specs=out_spec,
        out_shape=out_shape,
        scratch_shapes=[pltpu.VMEM((tm, k), BF16)],
        compiler_params=_cparams("parallel", "arbitrary"),
        name="norm_matmul",
    )(x, g.reshape(1, k), w, *(rope if rope_tiles else ()))


def _attn_prompt_kernel(q_ref, k_ref, v_ref, o_ref, ks_ref, vs_ref, qc_ref, kc_ref, vc_ref, oc_ref, lc_ref,
                        to_ref, tl_ref, lse_ref):
    t = q_ref.shape[1]
    n_keep = ks_ref.shape[1]
    ks_ref[0] = k_ref[0, t - n_keep:, :]
    vs_ref[0] = v_ref[0, t - n_keep:, :]
    nblk = t // N_BACK
    scale = HEAD_DIM ** -0.5
    kc_ref[0:N_BACK, :] = jnp.zeros((N_BACK, HEAD_DIM), BF16)
    vc_ref[0:N_BACK, :] = jnp.zeros((N_BACK, HEAD_DIM), BF16)
    qi = lax.broadcasted_iota(jnp.int32, (N_BACK, 2 * N_BACK), 0)
    kj = lax.broadcasted_iota(jnp.int32, (N_BACK, 2 * N_BACK), 1)
    dist = N_BACK + qi - kj
    band = (dist >= 0) & (dist <= N_BACK)

    for d in DILATIONS:
        ln = t // d
        blocks_per_class = ln // N_BACK
        for r in range(d):
            rows = pl.ds(r, ln, stride=d) if d > 1 else pl.ds(0, ln)
            qc_ref[r * ln:(r + 1) * ln, :] = q_ref[0, rows, :].astype(BF16)
            kc_ref[N_BACK + r * ln:N_BACK + (r + 1) * ln, :] = k_ref[0, rows, :].astype(BF16)
            vc_ref[N_BACK + r * ln:N_BACK + (r + 1) * ln, :] = v_ref[0, rows, :].astype(BF16)

        def block(b, carry):
            row0 = pl.multiple_of(b * N_BACK, N_BACK)
            j = lax.rem(b, blocks_per_class)
            q = qc_ref[pl.ds(row0, N_BACK), :]
            k2 = kc_ref[pl.ds(row0, 2 * N_BACK), :]
            v2 = vc_ref[pl.ds(row0, 2 * N_BACK), :]
            s = lax.dot_general(q, k2, (((1,), (1,)), ((), ())), preferred_element_type=F32) * scale
            valid = band & ((kj >= N_BACK) | (j > 0))
            s = jnp.where(valid, s, -jnp.inf)
            m = jnp.max(s, axis=-1, keepdims=True)
            p = jnp.exp(s - m)
            l = jnp.sum(p, axis=-1, keepdims=True)
            o = jnp.dot(p.astype(BF16), v2, preferred_element_type=F32)
            oc_ref[pl.ds(row0, N_BACK), :] = o / l
            lc_ref[pl.ds(row0, N_BACK), :] = jnp.broadcast_to(m + jnp.log(l), (N_BACK, HEAD_DIM))
            return carry

        lax.fori_loop(0, nblk, block, 0, unroll=16)

        if d == 1:
            o_ref[0] = oc_ref[...]
            lse_ref[...] = lc_ref[...]
        else:
            for r in range(d):
                to_ref[pl.ds(r, ln, stride=d), :] = oc_ref[r * ln:(r + 1) * ln, :]
                tl_ref[pl.ds(r, ln, stride=d), :] = lc_ref[r * ln:(r + 1) * ln, :]

            def merge(c, carry):
                rows = pl.ds(pl.multiple_of(c * 64, 64), 64)
                la = lse_ref[rows, :]
                lb = tl_ref[rows, :]
                mx = jnp.maximum(la, lb)
                ea = jnp.exp(la - mx)
                eb = jnp.exp(lb - mx)
                tot = ea + eb
                o_ref[0, rows, :] = (o_ref[0, rows, :] * ea + to_ref[rows, :] * eb) / tot
                lse_ref[rows, :] = mx + jnp.log(tot)
                return carry

            lax.fori_loop(0, t // 64, merge, 0, unroll=2)


def _attn_prompt(proj, b, t, n_keep):
    p3 = proj.reshape(b, t, proj.shape[1])
    blk = lambda off: pl.BlockSpec((1, t, HEAD_DIM), lambda i, h: (i, 0, off + h))
    keep = pl.BlockSpec((1, n_keep, HEAD_DIM), lambda i, h: (i, 0, h))
    return pl.pallas_call(
        _attn_prompt_kernel,
        grid=(b, N_HEADS),
        in_specs=[blk(0), blk(N_HEADS), blk(2 * N_HEADS)],
        out_specs=(pl.BlockSpec((1, t, HEAD_DIM), lambda i, h: (i, 0, h)), keep, keep),
        out_shape=(jax.ShapeDtypeStruct((b, t, ATTN_WIDTH), F32),
                   jax.ShapeDtypeStruct((b, n_keep, ATTN_WIDTH), F32),
                   jax.ShapeDtypeStruct((b, n_keep, ATTN_WIDTH), F32)),
        scratch_shapes=[
            pltpu.VMEM((t, HEAD_DIM), BF16),
            pltpu.VMEM((t + N_BACK, HEAD_DIM), BF16),
            pltpu.VMEM((t + N_BACK, HEAD_DIM), BF16),
            pltpu.VMEM((t, HEAD_DIM), F32),
            pltpu.VMEM((t, HEAD_DIM), F32),
            pltpu.VMEM((t, HEAD_DIM), F32),
            pltpu.VMEM((t, HEAD_DIM), F32),
            pltpu.VMEM((t, HEAD_DIM), F32),
        ],
        compiler_params=_cparams("parallel", "parallel"),
        name="attn_prompt",
    )(p3, p3, p3)


def _attn_sample_kernel(q_ref, kn_ref, vn_ref, kc_ref, vc_ref, o_ref):
    s_len = q_ref.shape[0]
    n_buf = kc_ref.shape[1] // N_HEADS
    scale = HEAD_DIM ** -0.5
    nt = (((1,), (1,)), ((), ()))

    def count(dist):
        c = jnp.zeros(dist.shape, F32)
        for d in DILATIONS:
            hit = (dist >= 0) & (dist <= N_BACK * d) & ((dist & (d - 1)) == 0)
            c = c + jnp.where(hit, 1.0, 0.0)
        return c

    iota = lambda shape, dim: lax.broadcasted_iota(jnp.int32, shape, dim)
    cc = count(n_buf + iota((s_len, n_buf), 0) - iota((s_len, n_buf), 1))
    cn = count(iota((s_len, s_len), 0) - iota((s_len, s_len), 1))
    for h in range(N_HEADS):
        cols = slice(h * HEAD_DIM, (h + 1) * HEAD_DIM)
        head_rows = pl.ds(h, n_buf, stride=N_HEADS)
        q = q_ref[:, cols].astype(BF16)
        sc = lax.dot_general(q, kc_ref[0, head_rows, :].astype(BF16), nt, preferred_element_type=F32) * scale
        sn = lax.dot_general(q, kn_ref[:, cols].astype(BF16), nt, preferred_element_type=F32) * scale
        sc = jnp.where(cc > 0, sc, -jnp.inf)
        sn = jnp.where(cn > 0, sn, -jnp.inf)
        m = jnp.maximum(jnp.max(sc, axis=-1, keepdims=True), jnp.max(sn, axis=-1, keepdims=True))
        pc = cc * jnp.exp(sc - m)
        pn = cn * jnp.exp(sn - m)
        l = jnp.sum(pc, axis=-1, keepdims=True) + jnp.sum(pn, axis=-1, keepdims=True)
        o = (jnp.dot(pc.astype(BF16), vc_ref[0, head_rows, :].astype(BF16), preferred_element_type=F32)
             + jnp.dot(pn.astype(BF16), vn_ref[:, cols].astype(BF16), preferred_element_type=F32))
        o_ref[:, cols] = o / l


def _attn_sample(proj, cache_k, cache_v, b, s_len):
    n_buf = cache_k.shape[1]
    ck = cache_k.reshape(b, n_buf * N_HEADS, HEAD_DIM)
    cv = cache_v.reshape(b, n_buf * N_HEADS, HEAD_DIM)
    new = lambda c: pl.BlockSpec((s_len, ATTN_WIDTH), lambda i: (i, c))
    old = pl.BlockSpec((1, n_buf * N_HEADS, HEAD_DIM), lambda i: (i, 0, 0))
    return pl.pallas_call(
        _attn_sample_kernel,
        grid=(b,),
        in_specs=[new(0), new(1), new(2), old, old],
        out_specs=new(0),
        out_shape=jax.ShapeDtypeStruct((b * s_len, ATTN_WIDTH), F32),
        compiler_params=_cparams("parallel"),
        name="attn_sample",
    )(proj, proj, proj, ck, cv)


def _silu(z):
    return z * jax.nn.sigmoid(z)


def _mix_out_kernel(o_ref, za_ref, gb_ref, gc_ref, hi_ref, zb_ref, gch_ref, hih_ref, init_ref, cw_ref,
                    w_ref, x_ref, h_ref, cs_ref, *, tiles_per_seq):
    tm = o_ref.shape[0]
    first = (pl.program_id(0) % tiles_per_seq) == 0
    ch = gc_ref[...] * hi_ref[...]
    halo = gch_ref[...] * hih_ref[...]
    prev1 = jnp.where(first, init_ref[0, 1:2, :], halo[7:8, :])
    prev2 = jnp.where(first, init_ref[0, 0:1, :], halo[6:7, :])
    row = lax.broadcasted_iota(jnp.int32, (tm, 1), 0)
    ch1 = jnp.where(row == 0, prev1, pltpu.roll(ch, 1, 0))
    ch2 = jnp.where(row == 0, prev2, jnp.where(row == 1, prev1, pltpu.roll(ch, 2, 0)))
    conv = ch2 * cw_ref[0:1, :] + ch1 * cw_ref[1:2, :] + ch * cw_ref[2:3, :]
    cs_ref[0] = ch[tm - 2:tm, :]
    o_a = (o_ref[...] * _silu(za_ref[...])).astype(BF16)
    o_b = (gb_ref[...] * conv * _silu(zb_ref[...])).astype(BF16)
    y = jnp.dot(o_a, w_ref[0:ATTN_WIDTH, :], preferred_element_type=F32)
    y = y + jnp.dot(o_b, w_ref[ATTN_WIDTH:, :], preferred_element_type=F32)
    h_ref[...] = x_ref[...] + y


def _mix_out(proj, o_attn, x, conv_init, conv_w, w_out, *, tm, tiles_per_seq):
    m = x.shape[0]
    nseq = conv_init.shape[0]
    col = lambda c: pl.BlockSpec((tm, CONV_WIDTH), lambda i: (i, c))
    halo = lambda c: pl.BlockSpec((8, CONV_WIDTH), lambda i: (jnp.maximum(i * (tm // 8) - 1, 0), c))
    seq = pl.BlockSpec((1, 2, CONV_WIDTH), lambda i: (i // tiles_per_seq, 0, 0))
    return pl.pallas_call(
        functools.partial(_mix_out_kernel, tiles_per_seq=tiles_per_seq),
        grid=(m // tm,),
        in_specs=[
            pl.BlockSpec((tm, ATTN_WIDTH), lambda i: (i, 0)),
            col(3), col(4), col(5), col(6), col(7), halo(5), halo(6), seq,
            pl.BlockSpec((3, CONV_WIDTH), lambda i: (0, 0)),
            pl.BlockSpec((D_MODEL, D_MODEL), lambda i: (0, 0), pipeline_mode=pl.Buffered(1)),
            pl.BlockSpec((tm, D_MODEL), lambda i: (i, 0)),
        ],
        out_specs=(pl.BlockSpec((tm, D_MODEL), lambda i: (i, 0)), seq),
        out_shape=(jax.ShapeDtypeStruct((m, D_MODEL), F32), jax.ShapeDtypeStruct((nseq, 2, CONV_WIDTH), F32)),
        compiler_params=_cparams("arbitrary"),
        name="mix_out",
    )(o_attn, proj, proj, proj, proj, proj, proj, proj, conv_init, conv_w, w_out, x)


N_SCAN = 7
GROUPS_PER_TILE = LANES // SSM_GROUP


PROMPT_CHUNK = 32
ROW_C = PROMPT_CHUNK + 1
ROW_DBL = PROMPT_CHUNK + 2
TABLE_ROWS = ROW_DBL + N_SCAN


def _s5_disc_kernel(lr_ref, li_ref, ls_ref, tr_ref, ti_ref):
    lr = lr_ref[...]
    li = li_ref[...]
    step = jnp.exp(ls_ref[...])
    mag = jnp.exp(lr * step)
    ar = mag * jnp.cos(li * step)
    ai = mag * jnp.sin(li * step)
    den = lr * lr + li * li
    nr = ar - 1.0
    tr_ref[ROW_C] = (nr * lr + ai * li) / den
    ti_ref[ROW_C] = (ai * lr - nr * li) / den
    pr = jnp.ones_like(ar)
    pi = jnp.zeros_like(ar)
    for tau in range(PROMPT_CHUNK + 1):
        tr_ref[tau] = pr
        ti_ref[tau] = pi
        dr, di = pr, pi
        pr, pi = pr * ar - pi * ai, pr * ai + pi * ar
    for i in range(N_SCAN):
        tr_ref[ROW_DBL + i] = dr
        ti_ref[ROW_DBL + i] = di
        dr, di = dr * dr - di * di, 2.0 * dr * di


def _s5_tables(lam_re, lam_im, log_step):
    g = lam_re.shape[0]
    dup = lambda a: jnp.concatenate([a, a], axis=-1)
    tr, ti = pl.pallas_call(
        _s5_disc_kernel,
        out_shape=(jax.ShapeDtypeStruct((TABLE_ROWS, g, LANES), F32),) * 2,
        name="s5_disc",
    )(dup(lam_re), dup(lam_im), log_step[:, None])
    return tr.transpose(1, 0, 2), ti.transpose(1, 0, 2)


def _split_bf16(x):
    hi = x.astype(BF16)
    return hi, (x - hi.astype(F32)).astype(BF16)


def _dot_split(a, b, b_is_bf16_exact=False):
    dot = lambda x, y: jnp.dot(x, y, preferred_element_type=F32)
    ah, al = _split_bf16(a)
    if b_is_bf16_exact:
        bh = b.astype(BF16)
        return dot(ah, bh) + dot(al, bh)
    bh, bl = _split_bf16(b)
    return dot(ah, bh) + (dot(ah, bl) + dot(al, bh))


def _pad_rows(x, rows):
    if x.shape[0] == rows:
        return x
    return jnp.concatenate([x, jnp.zeros((rows - x.shape[0], x.shape[1]), x.dtype)], axis=0)


def _s5_prep_kernel(tr_ref, ti_ref, br_ref, bi_ref, cr_ref, ci_ref, c2r_ref, c2i_ref,
                    wt_ref, qt_ref, kr_ref, ki_ref, *, chunk):
    n = SSM_GROUP * chunk
    iota = lambda shape, dim: lax.broadcasted_iota(jnp.int32, shape, dim)
    lane = iota((SSM_GROUP, 2 * SSM_STATE), 1)
    e_sel = jnp.where(iota((chunk, n), 0) == chunk - 1 - iota((chunk, n), 1) // SSM_GROUP, 1.0, 0.0)
    tile = jnp.where(iota((SSM_GROUP, n), 0) == iota((SSM_GROUP, n), 1) % SSM_GROUP, 1.0, 0.0)

    def one_group(g, carry):
        tab_r, tab_i = tr_ref[g], ti_ref[g]
        col_r = _pad_rows(tab_r, LANES).T[:SSM_STATE]
        col_i = _pad_rows(tab_i, LANES).T[:SSM_STATE]
        kr_ref[g] = col_r
        ki_ref[g] = col_i
        ccr, cci = col_r[:, ROW_C:ROW_C + 1], col_i[:, ROW_C:ROW_C + 1]
        br, bi = br_ref[g], bi_ref[g]
        bbr = ccr * br - cci * bi
        bbi = ccr * bi + cci * br
        aer = _dot_split(col_r[:, :chunk], e_sel, True)
        aei = _dot_split(col_i[:, :chunk], e_sel, True)
        btr = _dot_split(bbr, tile, True)
        bti = _dot_split(bbi, tile, True)
        ptr = aer * btr - aei * bti
        pti = aer * bti + aei * btr
        wt_ref[g, n:n + SSM_STATE, :] = ptr.astype(BF16)
        wt_ref[g, n + SSM_STATE:n + 2 * SSM_STATE, :] = pti.astype(BF16)
        taps = _dot_split(cr_ref[g], ptr) - _dot_split(ci_ref[g], pti)
        padded = jnp.concatenate([taps, jnp.zeros_like(taps)], axis=1)
        for t in range(chunk):
            sh = SSM_GROUP * (chunk - 1 - t)
            blk = padded if sh == 0 else pltpu.roll(padded, 2 * n - sh, 1)
            wt_ref[g, SSM_GROUP * t:SSM_GROUP * (t + 1), :] = blk[:, :n].astype(BF16)
        c2r, c2i = c2r_ref[g], c2i_ref[g]
        for t in range(chunk):
            ar = tab_r[t + 1:t + 2, :]
            ai = tab_i[t + 1:t + 2, :]
            x1 = jnp.where(lane < SSM_STATE, ar, -ai)
            x2 = jnp.where(lane < SSM_STATE, ai, ar)
            qt_ref[g, SSM_GROUP * t:SSM_GROUP * (t + 1), :] = (c2r * x1 - c2i * x2).astype(BF16)
        return carry

    lax.fori_loop(0, wt_ref.shape[0], one_group, 0, unroll=4)


def _s5_operators(tables, b_re, b_im, c_re, c_im, chunk):
    g = b_re.shape[0]
    n = SSM_GROUP * chunk
    assert chunk <= PROMPT_CHUNK
    dup = lambda a: jnp.concatenate([a, a], axis=-1)
    per_g = lambda *s: pl.BlockSpec((GROUPS_PER_TILE,) + s, lambda i: (i,) + (0,) * len(s))
    return pl.pallas_call(
        functools.partial(_s5_prep_kernel, chunk=chunk),
        grid=(g // GROUPS_PER_TILE,),
        in_specs=[per_g(TABLE_ROWS, LANES), per_g(TABLE_ROWS, LANES),
                  per_g(SSM_STATE, SSM_GROUP), per_g(SSM_STATE, SSM_GROUP),
                  per_g(SSM_GROUP, SSM_STATE), per_g(SSM_GROUP, SSM_STATE),
                  per_g(SSM_GROUP, LANES), per_g(SSM_GROUP, LANES)],
        out_specs=(per_g(n + 2 * SSM_STATE, n), per_g(n, 2 * SSM_STATE),
                   per_g(SSM_STATE, LANES), per_g(SSM_STATE, LANES)),
        out_shape=(jax.ShapeDtypeStruct((g, n + 2 * SSM_STATE, n), BF16),
                   jax.ShapeDtypeStruct((g, n, 2 * SSM_STATE), BF16),
                   jax.ShapeDtypeStruct((g, SSM_STATE, LANES), F32),
                   jax.ShapeDtypeStruct((g, SSM_STATE, LANES), F32)),
        compiler_params=_cparams("parallel"),
        name="s5_prep",
    )(*tables, b_re, b_im, c_re, c_im, dup(c_re), dup(c_im))


def _s5_scan_kernel(u_ref, wt_ref, qt_ref, kr_ref, ki_ref, dsk_ref, *rest, chunk, carry):
    if carry:
        y_ref, hf_ref, dall_ref, yall_ref = rest
    else:
        h0_ref, y_ref, hf_ref, dall_ref, yall_ref = rest
    nc = u_ref.shape[1]
    ncp = dall_ref.shape[2]
    nseq = ncp // LANES if carry else nc
    n = SSM_GROUP * chunk
    iota = lambda shape, dim: lax.broadcasted_iota(jnp.int32, shape, dim)
    for s in range(chunk):
        dall_ref[s] = _pad_rows(u_ref[s], ncp).T.astype(BF16)
    lane = iota((SSM_STATE, ncp), 1) & (LANES - 1)
    slot = iota((LANES, LANES), 1)
    if not carry:
        h0_all = _pad_rows(jnp.concatenate([h0_ref[:, gp, :] for gp in range(GROUPS_PER_TILE)], axis=0), LANES).T
    acc = jnp.zeros((LANES, LANES), F32)
    for gp in range(GROUPS_PER_TILE):
        rows = slice(SSM_GROUP * gp, SSM_GROUP * (gp + 1))
        d = dall_ref[:, rows, :].reshape(n, ncp)
        ys = jnp.dot(wt_ref[gp], d, preferred_element_type=F32)
        sr = ys[n:n + SSM_STATE]
        si = ys[n + SSM_STATE:]
        if carry:
            for i in range(N_SCAN):
                sh = 1 << i
                ar = kr_ref[gp, :, ROW_DBL + i:ROW_DBL + i + 1]
                ai = ki_ref[gp, :, ROW_DBL + i:ROW_DBL + i + 1]
                pr = jnp.where(lane >= sh, pltpu.roll(sr, sh, 1), 0.0)
                pi = jnp.where(lane >= sh, pltpu.roll(si, sh, 1), 0.0)
                sr, si = sr + ar * pr - ai * pi, si + ar * pi + ai * pr
            hp = jnp.concatenate([jnp.where(lane >= 1, pltpu.roll(sr, 1, 1), 0.0),
                                  jnp.where(lane >= 1, pltpu.roll(si, 1, 1), 0.0)], axis=0)
            for b in range(nseq):
                seq = slice(LANES * b, LANES * (b + 1))
                end = jnp.concatenate([sr[:, seq], si[:, seq]], axis=0)
                acc = jnp.where(slot == 8 * gp + b, pltpu.roll(end, (8 * gp + b + 1) % LANES, 1), acc)
        else:
            hp = h0_all if gp == 0 else pltpu.roll(h0_all, LANES - 8 * gp, 1)
            hp = jnp.where(slot < nseq, hp, 0.0)
            hpr, hpi = hp[:SSM_STATE], hp[SSM_STATE:]
            ar = kr_ref[gp, :, chunk:chunk + 1]
            ai = ki_ref[gp, :, chunk:chunk + 1]
            end = jnp.concatenate([sr + ar * hpr - ai * hpi, si + ar * hpi + ai * hpr], axis=0)
            if gp:
                end = pltpu.roll(end, 8 * gp, 1)
            acc = jnp.where((slot >= 8 * gp) & (slot < 8 * gp + nseq), end, acc)
        y = ys[:n] + jnp.dot(qt_ref[gp], hp.astype(BF16), preferred_element_type=F32)
        yall_ref[:, rows, :] = y.reshape(chunk, SSM_GROUP, ncp)
    hf_ref[...] = acc.T[:8 * GROUPS_PER_TILE].reshape(GROUPS_PER_TILE, 8, LANES)
    for t in range(chunk):
        y_ref[t] = yall_ref[t].T[:nc] + dsk_ref[...] * u_ref[t]


def _s5_scan(proj, ops, d_skip, h0, nseq, *, carry):
    wt, qt, kr, ki = ops
    chunk, nc, _ = proj.shape
    n = SSM_GROUP * chunk
    assert nc == (nseq * LANES if carry else nseq) and (carry or nseq == 8) and nseq <= 8
    ncp = max(nc, LANES)
    gt = GROUPS_PER_TILE
    tile3 = lambda a, b: pl.BlockSpec((gt, a, b), lambda i: (i, 0, 0))
    lane_tile = pl.BlockSpec((chunk, nc, LANES), lambda i: (0, 0, i))
    in_specs = [lane_tile, tile3(n + 2 * SSM_STATE, n), tile3(n, 2 * SSM_STATE), tile3(SSM_STATE, LANES),
                tile3(SSM_STATE, LANES), pl.BlockSpec((1, LANES), lambda i: (0, i))]
    args = [proj, wt, qt, kr, ki, d_skip.reshape(1, -1)]
    if not carry:
        in_specs.append(pl.BlockSpec((nseq, gt, LANES), lambda i: (0, i, 0)))
        args.append(h0)
    return pl.pallas_call(
        functools.partial(_s5_scan_kernel, chunk=chunk, carry=carry),
        grid=(SSM_GROUPS // gt,),
        in_specs=in_specs,
        out_specs=(lane_tile, tile3(8, LANES)),
        out_shape=(jax.ShapeDtypeStruct((chunk, nc, SSM_GROUPS * SSM_GROUP), F32),
                   jax.ShapeDtypeStruct((SSM_GROUPS, 8, LANES), F32)),
        scratch_shapes=[pltpu.VMEM((chunk, LANES, ncp), BF16), pltpu.VMEM((chunk, LANES, ncp), F32)],
        compiler_params=_cparams("parallel"),
        name="s5_scan",
    )(*args)


def _glu_out_kernel(y_ref, z_ref, x_ref, wg_ref, bg_ref, wo_ref, gf_ref, o_ref):
    g = jax.nn.gelu(y_ref[...], approximate=True)
    gate = jnp.dot(g.astype(BF16), wg_ref[...], preferred_element_type=F32) + bg_ref[...]
    yy = (g * jax.nn.sigmoid(gate)) * _silu(z_ref[...])
    h = x_ref[...] + jnp.dot(yy.astype(BF16), wo_ref[...], preferred_element_type=F32)
    o_ref[...] = _rms_norm(h, gf_ref[...])


def _glu_out(y, proj, x, w_glu, b_glu, w_out, g_final, *, tm, chunk=None):
    m, w = x.shape
    vec = pl.BlockSpec((1, w), lambda i: (0, 0))
    mat = pl.BlockSpec((w, w), lambda i: (0, 0), pipeline_mode=pl.Buffered(1))
    if chunk is None:
        row = lambda c: pl.BlockSpec((tm, w), lambda i: (i, c))
        y_spec, z_spec, x_spec = row(0), row(1), row(0)
    else:
        assert tm * chunk == m
        x = x.reshape(tm, chunk * w)
        y_spec = pl.BlockSpec((None, tm, w), lambda i: (i, 0, 0))
        z_spec = pl.BlockSpec((None, tm, w), lambda i: (i, 0, 1))
        x_spec = pl.BlockSpec((tm, w), lambda i: (0, i))
    return pl.pallas_call(
        _glu_out_kernel,
        grid=(m // tm,),
        in_specs=[y_spec, z_spec, x_spec, mat, vec, mat, vec],
        out_specs=x_spec,
        out_shape=jax.ShapeDtypeStruct(x.shape, F32),
        compiler_params=_cparams("parallel"),
        name="glu_out",
    )(y, proj, x, w_glu, b_glu.reshape(1, w), w_out, g_final.reshape(1, w)).reshape(m, w)


PAST_LEN = 16384


def kernel(x_prompt, x_sample, cache_win_k, cache_win_v, state_conv, state_ssm_re, state_ssm_im, attn_norm, w_in_ab, conv_w, w_out_ab, ssm_norm, w_in_c, lam_re, lam_im, log_step, b_re, b_im, c_re, c_im, d_skip, w_glu, b_glu, w_out_c, final_norm):
    bp, tp, _ = x_prompt.shape
    bs, ts, _ = x_sample.shape
    n_keep = min(2048, tp)
    xp = x_prompt.reshape(bp * tp, D_MODEL)
    xs = x_sample.reshape(bs * ts, D_MODEL)

    w_in0 = w_in_ab[0].astype(BF16)
    w_out0 = w_out_ab[0].astype(BF16)
    proj_p = _norm_matmul(xp, attn_norm[0], w_in0, _rope_tables(tp, tp, 0), tm=512, tn=1024, rope_tiles=2,
                          tiles_per_seq=tp // 512)
    proj_s = _norm_matmul(xs, attn_norm[0], w_in0, _rope_tables(bs * ts, ts, PAST_LEN), tm=bs * ts, tn=1024,
                          rope_tiles=2)
    o_p, k_p, v_p = _attn_prompt(proj_p, bp, tp, n_keep)
    o_p = o_p.reshape(bp * tp, ATTN_WIDTH)
    k_p = k_p.reshape(1, bp, n_keep, N_HEADS, HEAD_DIM)
    v_p = v_p.reshape(1, bp, n_keep, N_HEADS, HEAD_DIM)
    o_s = _attn_sample(proj_s, cache_win_k[0], cache_win_v[0], bs, ts)
    h1_p, conv_p = _mix_out(proj_p, o_p, xp, jnp.zeros((bp, 2, CONV_WIDTH), F32), conv_w[0], w_out0,
                            tm=256, tiles_per_seq=tp // 256)
    h1_s, conv_s = _mix_out(proj_s, o_s, xs, state_conv[0], conv_w[0], w_out0, tm=ts, tiles_per_seq=1)
    k_s = proj_s[:, ATTN_WIDTH:2 * ATTN_WIDTH].reshape(1, bs, ts, N_HEADS, HEAD_DIM)
    v_s = proj_s[:, 2 * ATTN_WIDTH:3 * ATTN_WIDTH].reshape(1, bs, ts, N_HEADS, HEAD_DIM)

    w_in1 = w_in_c[0].astype(BF16)
    w_glu1 = w_glu[0].astype(BF16)
    w_out1 = w_out_c[0].astype(BF16)
    tables = _s5_tables(lam_re[0], lam_im[0], log_step[0])
    bc = (b_re[0], b_im[0], c_re[0], c_im[0])
    nc_p = bp * tp // PROMPT_CHUNK
    proj1_p = _norm_matmul(h1_p, ssm_norm[0], w_in1, tm=nc_p, tn=2048, chunk=PROMPT_CHUNK)
    y_p, hf_p = _s5_scan(proj1_p, _s5_operators(tables, *bc, PROMPT_CHUNK), d_skip[0], None, bp, carry=True)
    out_p = _glu_out(y_p, proj1_p, h1_p, w_glu1, b_glu[0], w_out1, final_norm, tm=nc_p, chunk=PROMPT_CHUNK)
    proj1_s = _norm_matmul(h1_s, ssm_norm[0], w_in1, tm=bs * ts, tn=1024)
    h0 = jnp.concatenate([state_ssm_re[0], state_ssm_im[0]], axis=-1)
    y_s, hf_s = _s5_scan(proj1_s.reshape(bs, ts, -1).transpose(1, 0, 2), _s5_operators(tables, *bc, ts), d_skip[0],
                         h0, bs, carry=False)
    y_s = y_s.transpose(1, 0, 2).reshape(bs * ts, D_MODEL)
    out_s = _glu_out(y_s, proj1_s, h1_s, w_glu1, b_glu[0], w_out1, final_norm, tm=bs * ts)
    hf_p = hf_p[:, :bp].transpose(1, 0, 2)[None]
    hf_s = hf_s[:, :bs].transpose(1, 0, 2)[None]
    return (out_p.reshape(bp, tp, D_MODEL), out_s.reshape(bs, ts, D_MODEL),
            k_p, v_p, conv_p[None], hf_p[..., :SSM_STATE], hf_p[..., SSM_STATE:],
            k_s, v_s, conv_s[None], hf_s[..., :SSM_STATE], hf_s[..., SSM_STATE:])
```

```python
import functools
import math

import jax
import jax.numpy as jnp
from jax import lax
from jax.experimental import pallas as pl
from jax.experimental.pallas import tpu as pltpu

D_MODEL = 2048
HEAD_DIM = 128
N_HEADS = 8
ATTN_WIDTH = 1024
CONV_WIDTH = 1024
DILATIONS = (1, 4, 16)
N_BACK = 128
ROPE_THETA = 10000.0
RMS_EPS = 1e-6
SSM_GROUP = 16
SSM_GROUPS = 128
SSM_STATE = 64
LANES = 128
VMEM_LIMIT = 56 * 1024 * 1024

F32 = jnp.float32
BF16 = jnp.bfloat16


def _cparams(*sem):
    return pltpu.CompilerParams(dimension_semantics=sem, vmem_limit_bytes=VMEM_LIMIT)


def _rope_table_kernel(inv_ref, cos_ref, sin_ref, *, period, offset):
    rows = cos_ref.shape[0]
    r = lax.broadcasted_iota(jnp.int32, (rows, LANES), 0) + pl.program_id(0) * rows
    pos = (offset + lax.rem(r, period)).astype(F32)
    ang = pos * inv_ref[...]
    lane = lax.broadcasted_iota(jnp.int32, (rows, LANES), 1)
    cos_ref[...] = jnp.cos(ang)
    sin_ref[...] = jnp.where(lane < HEAD_DIM // 2, -1.0, 1.0) * jnp.sin(ang)


def _rope_tables(rows, period, offset):
    half = HEAD_DIM // 2
    inv = ROPE_THETA ** (-jnp.arange(half, dtype=F32) / half)
    inv2 = jnp.concatenate([inv, inv])[None, :]
    tr = min(rows, 256)
    return pl.pallas_call(
        functools.partial(_rope_table_kernel, period=period, offset=offset),
        grid=(rows // tr,),
        in_specs=[pl.BlockSpec((1, LANES), lambda i: (0, 0))],
        out_specs=(pl.BlockSpec((tr, LANES), lambda i: (i, 0)),) * 2,
        out_shape=(jax.ShapeDtypeStruct((rows, LANES), F32),) * 2,
        compiler_params=_cparams("parallel"),
        name="rope_table",
    )(inv2)


def _rms_norm(x, g):
    return x * lax.rsqrt(jnp.mean(x * x, axis=-1, keepdims=True) + RMS_EPS) * g


def _norm_matmul_kernel(x_ref, g_ref, w_ref, *rest, rope_tiles):
    o_ref, hn_ref = rest[-2:]
    n = pl.program_id(1)

    @pl.when(n == 0)
    def _():
        hn_ref[...] = _rms_norm(x_ref[...], g_ref[...]).astype(BF16)

    acc = jnp.dot(hn_ref[...], w_ref[...], preferred_element_type=F32)

    @pl.when(n >= rope_tiles)
    def _():
        o_ref[...] = acc

    if rope_tiles:
        cos_ref, sin_ref = rest[:2]

        @pl.when(n < rope_tiles)
        def _():
            cos = cos_ref[...]
            sin = sin_ref[...]
            for h in range(o_ref.shape[1] // HEAD_DIM):
                xh = acc[:, h * HEAD_DIM:(h + 1) * HEAD_DIM]
                o_ref[:, h * HEAD_DIM:(h + 1) * HEAD_DIM] = xh * cos + pltpu.roll(xh, HEAD_DIM // 2, 1) * sin


def _norm_matmul(x, g, w, rope=None, *, tm, tn, rope_tiles=0, tiles_per_seq=1):
    m, k = x.shape
    n = w.shape[1]
    table = pl.BlockSpec((tm, LANES), lambda i, j: (i % tiles_per_seq, 0))
    return pl.pallas_call(
        functools.partial(_norm_matmul_kernel, rope_tiles=rope_tiles),
        grid=(m // tm, n // tn),
        in_specs=[pl.BlockSpec((tm, k), lambda i, j: (i, 0)), pl.BlockSpec((1, k), lambda i, j: (0, 0)),
                  pl.BlockSpec((k, tn), lambda i, j: (0, j))] + ([table, table] if rope_tiles else []),
        out_specs=pl.BlockSpec((tm, tn), lambda i, j: (i, j)),
        out_shape=jax.ShapeDtypeStruct((m, n), F32),
        scratch_shapes=[pltpu.VMEM((tm, k), BF16)],
        compiler_params=_cparams("parallel", "arbitrary"),
        name="norm_matmul",
    )(x, g.reshape(1, k), w, *(rope if rope_tiles else ()))


POS_TILE = 8


def _norm_matmul_pm_kernel(x_ref, g_ref, w_ref, o_ref, hn_ref, slab_ref):
    ct, _, k = x_ref.shape

    @pl.when(pl.program_id(1) == 0)
    def _():
        hn = _rms_norm(x_ref[...], g_ref[...])
        for j in range(k // LANES):
            slab_ref[j] = hn[:, :, j * LANES:(j + 1) * LANES].reshape(ct * POS_TILE, LANES)
        for s in range(POS_TILE):
            for j in range(k // LANES):
                hn_ref[s * ct:(s + 1) * ct, j * LANES:(j + 1) * LANES] = (
                    slab_ref[j, pl.ds(s, ct, stride=POS_TILE), :].astype(BF16))

    acc = jnp.dot(hn_ref[...], w_ref[...], preferred_element_type=F32)
    o_ref[...] = acc.reshape(o_ref.shape)


def _norm_matmul_pm(x, g, w, *, chunk, ct, tn):
    m, k = x.shape
    n = w.shape[1]
    nc = m // chunk
    sg = chunk // POS_TILE
    return pl.pallas_call(
        _norm_matmul_pm_kernel,
        grid=((nc // ct) * sg, n // tn),
        in_specs=[pl.BlockSpec((ct, POS_TILE, k), lambda i, j: (i // sg, i % sg, 0)),
                  pl.BlockSpec((1, k), lambda i, j: (0, 0)), pl.BlockSpec((k, tn), lambda i, j: (0, j))],
        out_specs=pl.BlockSpec((POS_TILE, ct, tn), lambda i, j: (i % sg, i // sg, j)),
        out_shape=jax.ShapeDtypeStruct((chunk, nc, n), F32),
        scratch_shapes=[pltpu.VMEM((ct * POS_TILE, k), BF16), pltpu.VMEM((k // LANES, ct * POS_TILE, LANES), F32)],
        compiler_params=_cparams("parallel", "arbitrary"),
        name="norm_matmul_pm",
    )(x.reshape(nc, chunk, k), g.reshape(1, k), w)


def _attn_prompt_kernel(q_ref, k_ref, v_ref, o_ref, ks_ref, vs_ref, qc_ref, kc_ref, vc_ref, oc_ref, lc_ref,
                        to_ref, tl_ref, lse_ref):
    t = q_ref.shape[1]
    n_keep = ks_ref.shape[1]
    ks_ref[0] = k_ref[0, t - n_keep:, :]
    vs_ref[0] = v_ref[0, t - n_keep:, :]
    nblk = t // N_BACK
    scale = HEAD_DIM ** -0.5
    kc_ref[0:N_BACK, :] = jnp.zeros((N_BACK, HEAD_DIM), BF16)
    vc_ref[0:N_BACK, :] = jnp.zeros((N_BACK, HEAD_DIM), BF16)
    qi = lax.broadcasted_iota(jnp.int32, (N_BACK, 2 * N_BACK), 0)
    kj = lax.broadcasted_iota(jnp.int32, (N_BACK, 2 * N_BACK), 1)
    dist = N_BACK + qi - kj
    band = (dist >= 0) & (dist <= N_BACK)

    for d in DILATIONS:
        ln = t // d
        blocks_per_class = ln // N_BACK
        for r in range(d):
            rows = pl.ds(r, ln, stride=d) if d > 1 else pl.ds(0, ln)
            qc_ref[r * ln:(r + 1) * ln, :] = q_ref[0, rows, :].astype(BF16)
            kc_ref[N_BACK + r * ln:N_BACK + (r + 1) * ln, :] = k_ref[0, rows, :].astype(BF16)
            vc_ref[N_BACK + r * ln:N_BACK + (r + 1) * ln, :] = v_ref[0, rows, :].astype(BF16)

        def block(b, carry):
            row0 = pl.multiple_of(b * N_BACK, N_BACK)
            j = lax.rem(b, blocks_per_class)
            q = qc_ref[pl.ds(row0, N_BACK), :]
            k2 = kc_ref[pl.ds(row0, 2 * N_BACK), :]
            v2 = vc_ref[pl.ds(row0, 2 * N_BACK), :]
            s = lax.dot_general(q, k2, (((1,), (1,)), ((), ())), preferred_element_type=F32) * scale
            valid = band & ((kj >= N_BACK) | (j > 0))
            s = jnp.where(valid, s, -jnp.inf)
            m = jnp.max(s, axis=-1, keepdims=True)
            p = jnp.exp(s - m)
            l = jnp.sum(p, axis=-1, keepdims=True)
            o = jnp.dot(p.astype(BF16), v2, preferred_element_type=F32)
            oc_ref[pl.ds(row0, N_BACK), :] = o / l
            lc_ref[pl.ds(row0, N_BACK), :] = jnp.broadcast_to(m + jnp.log(l), (N_BACK, HEAD_DIM))
            return carry

        lax.fori_loop(0, nblk, block, 0, unroll=16)

        if d == 1:
            o_ref[0] = oc_ref[...]
            lse_ref[...] = lc_ref[...]
        else:
            for r in range(d):
                to_ref[pl.ds(r, ln, stride=d), :] = oc_ref[r * ln:(r + 1) * ln, :]
                tl_ref[pl.ds(r, ln, stride=d), :] = lc_ref[r * ln:(r + 1) * ln, :]

            def merge(c, carry):
                rows = pl.ds(pl.multiple_of(c * 64, 64), 64)
                la = lse_ref[rows, :]
                lb = tl_ref[rows, :]
                mx = jnp.maximum(la, lb)
                ea = jnp.exp(la - mx)
                eb = jnp.exp(lb - mx)
                tot = ea + eb
                o_ref[0, rows, :] = (o_ref[0, rows, :] * ea + to_ref[rows, :] * eb) / tot
                lse_ref[rows, :] = mx + jnp.log(tot)
                return carry

            lax.fori_loop(0, t // 64, merge, 0, unroll=2)


def _attn_prompt(proj, b, t, n_keep):
    p3 = proj.reshape(b, t, proj.shape[1])
    blk = lambda off: pl.BlockSpec((1, t, HEAD_DIM), lambda i, h: (i, 0, off + h))
    keep = pl.BlockSpec((1, n_keep, HEAD_DIM), lambda i, h: (i, 0, h))
    return pl.pallas_call(
        _attn_prompt_kernel,
        grid=(b, N_HEADS),
        in_specs=[blk(0), blk(N_HEADS), blk(2 * N_HEADS)],
        out_specs=(pl.BlockSpec((1, t, HEAD_DIM), lambda i, h: (i, 0, h)), keep, keep),
        out_shape=(jax.ShapeDtypeStruct((b, t, ATTN_WIDTH), F32),
                   jax.ShapeDtypeStruct((b, n_keep, ATTN_WIDTH), F32),
                   jax.ShapeDtypeStruct((b, n_keep, ATTN_WIDTH), F32)),
        scratch_shapes=[
            pltpu.VMEM((t, HEAD_DIM), BF16),
            pltpu.VMEM((t + N_BACK, HEAD_DIM), BF16),
            pltpu.VMEM((t + N_BACK, HEAD_DIM), BF16),
            pltpu.VMEM((t, HEAD_DIM), F32),
            pltpu.VMEM((t, HEAD_DIM), F32),
            pltpu.VMEM((t, HEAD_DIM), F32),
            pltpu.VMEM((t, HEAD_DIM), F32),
            pltpu.VMEM((t, HEAD_DIM), F32),
        ],
        compiler_params=_cparams("parallel", "parallel"),
        name="attn_prompt",
    )(p3, p3, p3)


def _attn_sample_kernel(q_ref, kn_ref, vn_ref, kc_ref, vc_ref, o_ref):
    s_len = q_ref.shape[0]
    n_buf = kc_ref.shape[1] // N_HEADS
    scale = HEAD_DIM ** -0.5
    nt = (((1,), (1,)), ((), ()))

    def count(dist):
        c = jnp.zeros(dist.shape, F32)
        for d in DILATIONS:
            hit = (dist >= 0) & (dist <= N_BACK * d) & ((dist & (d - 1)) == 0)
            c = c + jnp.where(hit, 1.0, 0.0)
        return c

    iota = lambda shape, dim: lax.broadcasted_iota(jnp.int32, shape, dim)
    cc = count(n_buf + iota((s_len, n_buf), 0) - iota((s_len, n_buf), 1))
    cn = count(iota((s_len, s_len), 0) - iota((s_len, s_len), 1))
    for h in range(N_HEADS):
        cols = slice(h * HEAD_DIM, (h + 1) * HEAD_DIM)
        head_rows = pl.ds(h, n_buf, stride=N_HEADS)
        q = q_ref[:, cols].astype(BF16)
        sc = lax.dot_general(q, kc_ref[0, head_rows, :].astype(BF16), nt, preferred_element_type=F32) * scale
        sn = lax.dot_general(q, kn_ref[:, cols].astype(BF16), nt, preferred_element_type=F32) * scale
        sc = jnp.where(cc > 0, sc, -jnp.inf)
        sn = jnp.where(cn > 0, sn, -jnp.inf)
        m = jnp.maximum(jnp.max(sc, axis=-1, keepdims=True), jnp.max(sn, axis=-1, keepdims=True))
        pc = cc * jnp.exp(sc - m)
        pn = cn * jnp.exp(sn - m)
        l = jnp.sum(pc, axis=-1, keepdims=True) + jnp.sum(pn, axis=-1, keepdims=True)
        o = (jnp.dot(pc.astype(BF16), vc_ref[0, head_rows, :].astype(BF16), preferred_element_type=F32)
             + jnp.dot(pn.astype(BF16), vn_ref[:, cols].astype(BF16), preferred_element_type=F32))
        o_ref[:, cols] = o / l


def _attn_sample(proj, cache_k, cache_v, b, s_len):
    n_buf = cache_k.shape[1]
    ck = cache_k.reshape(b, n_buf * N_HEADS, HEAD_DIM)
    cv = cache_v.reshape(b, n_buf * N_HEADS, HEAD_DIM)
    new = lambda c: pl.BlockSpec((s_len, ATTN_WIDTH), lambda i: (i, c))
    old = pl.BlockSpec((1, n_buf * N_HEADS, HEAD_DIM), lambda i: (i, 0, 0))
    return pl.pallas_call(
        _attn_sample_kernel,
        grid=(b,),
        in_specs=[new(0), new(1), new(2), old, old],
        out_specs=new(0),
        out_shape=jax.ShapeDtypeStruct((b * s_len, ATTN_WIDTH), F32),
        compiler_params=_cparams("parallel"),
        name="attn_sample",
    )(proj, proj, proj, ck, cv)


def _silu(z):
    return z * jax.nn.sigmoid(z)


def _mix_out_kernel(o_ref, za_ref, gb_ref, gc_ref, hi_ref, zb_ref, gch_ref, hih_ref, init_ref, cw_ref,
                    w_ref, x_ref, h_ref, cs_ref, *, tiles_per_seq):
    tm = o_ref.shape[0]
    first = (pl.program_id(0) % tiles_per_seq) == 0
    ch = gc_ref[...] * hi_ref[...]
    halo = gch_ref[...] * hih_ref[...]
    prev1 = jnp.where(first, init_ref[0, 1:2, :], halo[7:8, :])
    prev2 = jnp.where(first, init_ref[0, 0:1, :], halo[6:7, :])
    row = lax.broadcasted_iota(jnp.int32, (tm, 1), 0)
    ch1 = jnp.where(row == 0, prev1, pltpu.roll(ch, 1, 0))
    ch2 = jnp.where(row == 0, prev2, jnp.where(row == 1, prev1, pltpu.roll(ch, 2, 0)))
    conv = ch2 * cw_ref[0:1, :] + ch1 * cw_ref[1:2, :] + ch * cw_ref[2:3, :]
    cs_ref[0] = ch[tm - 2:tm, :]
    o_a = (o_ref[...] * _silu(za_ref[...])).astype(BF16)
    o_b = (gb_ref[...] * conv * _silu(zb_ref[...])).astype(BF16)
    y = jnp.dot(o_a, w_ref[0:ATTN_WIDTH, :], preferred_element_type=F32)
    y = y + jnp.dot(o_b, w_ref[ATTN_WIDTH:, :], preferred_element_type=F32)
    h_ref[...] = x_ref[...] + y


def _mix_out(proj, o_attn, x, conv_init, conv_w, w_out, *, tm, tiles_per_seq):
    m = x.shape[0]
    nseq = conv_init.shape[0]
    col = lambda c: pl.BlockSpec((tm, CONV_WIDTH), lambda i: (i, c))
    halo = lambda c: pl.BlockSpec((8, CONV_WIDTH), lambda i: (jnp.maximum(i * (tm // 8) - 1, 0), c))
    seq = pl.BlockSpec((1, 2, CONV_WIDTH), lambda i: (i // tiles_per_seq, 0, 0))
    return pl.pallas_call(
        functools.partial(_mix_out_kernel, tiles_per_seq=tiles_per_seq),
        grid=(m // tm,),
        in_specs=[
            pl.BlockSpec((tm, ATTN_WIDTH), lambda i: (i, 0)),
            col(3), col(4), col(5), col(6), col(7), halo(5), halo(6), seq,
            pl.BlockSpec((3, CONV_WIDTH), lambda i: (0, 0)),
            pl.BlockSpec((D_MODEL, D_MODEL), lambda i: (0, 0), pipeline_mode=pl.Buffered(1)),
            pl.BlockSpec((tm, D_MODEL), lambda i: (i, 0)),
        ],
        out_specs=(pl.BlockSpec((tm, D_MODEL), lambda i: (i, 0)), seq),
        out_shape=(jax.ShapeDtypeStruct((m, D_MODEL), F32), jax.ShapeDtypeStruct((nseq, 2, CONV_WIDTH), F32)),
        compiler_params=_cparams("arbitrary"),
        name="mix_out",
    )(o_attn, proj, proj, proj, proj, proj, proj, proj, conv_init, conv_w, w_out, x)


N_SCAN = 7
GROUPS_PER_TILE = LANES // SSM_GROUP


PROMPT_CHUNK = 32
ROW_C = PROMPT_CHUNK + 1
ROW_DBL = PROMPT_CHUNK + 2
TABLE_ROWS = ROW_DBL + N_SCAN


def _s5_disc_kernel(lr_ref, li_ref, ls_ref, tr_ref, ti_ref):
    lr = lr_ref[...]
    li = li_ref[...]
    step = jnp.exp(ls_ref[...])
    mag = jnp.exp(lr * step)
    ar = mag * jnp.cos(li * step)
    ai = mag * jnp.sin(li * step)
    den = lr * lr + li * li
    nr = ar - 1.0
    tr_ref[ROW_C] = (nr * lr + ai * li) / den
    ti_ref[ROW_C] = (ai * lr - nr * li) / den
    pr = jnp.ones_like(ar)
    pi = jnp.zeros_like(ar)
    for tau in range(PROMPT_CHUNK + 1):
        tr_ref[tau] = pr
        ti_ref[tau] = pi
        dr, di = pr, pi
        pr, pi = pr * ar - pi * ai, pr * ai + pi * ar
    for i in range(N_SCAN):
        tr_ref[ROW_DBL + i] = dr
        ti_ref[ROW_DBL + i] = di
        dr, di = dr * dr - di * di, 2.0 * dr * di


def _s5_tables(lam_re, lam_im, log_step):
    g = lam_re.shape[0]
    dup = lambda a: jnp.concatenate([a, a], axis=-1)
    tr, ti = pl.pallas_call(
        _s5_disc_kernel,
        out_shape=(jax.ShapeDtypeStruct((TABLE_ROWS, g, LANES), F32),) * 2,
        name="s5_disc",
    )(dup(lam_re), dup(lam_im), log_step[:, None])
    return tr.transpose(1, 0, 2), ti.transpose(1, 0, 2)


def _split_bf16(x):
    hi = x.astype(BF16)
    return hi, (x - hi.astype(F32)).astype(BF16)


def _dot_split(a, b, b_is_bf16_exact=False):
    dot = lambda x, y: jnp.dot(x, y, preferred_element_type=F32)
    ah, al = _split_bf16(a)
    if b_is_bf16_exact:
        bh = b.astype(BF16)
        return dot(ah, bh) + dot(al, bh)
    bh, bl = _split_bf16(b)
    return dot(ah, bh) + (dot(ah, bl) + dot(al, bh))


def _pad_rows(x, rows):
    if x.shape[0] == rows:
        return x
    return jnp.concatenate([x, jnp.zeros((rows - x.shape[0], x.shape[1]), x.dtype)], axis=0)


def _s5_prep_kernel(tr_ref, ti_ref, br_ref, bi_ref, cr_ref, ci_ref, c2r_ref, c2i_ref,
                    wt_ref, qt_ref, kr_ref, ki_ref, *, chunk):
    n = SSM_GROUP * chunk
    iota = lambda shape, dim: lax.broadcasted_iota(jnp.int32, shape, dim)
    lane = iota((SSM_GROUP, 2 * SSM_STATE), 1)
    e_sel = jnp.where(iota((chunk, n), 0) == chunk - 1 - iota((chunk, n), 1) // SSM_GROUP, 1.0, 0.0)
    tile = jnp.where(iota((SSM_GROUP, n), 0) == iota((SSM_GROUP, n), 1) % SSM_GROUP, 1.0, 0.0)

    def one_group(g, carry):
        tab_r, tab_i = tr_ref[g], ti_ref[g]
        col_r = _pad_rows(tab_r, LANES).T[:SSM_STATE]
        col_i = _pad_rows(tab_i, LANES).T[:SSM_STATE]
        kr_ref[g] = col_r
        ki_ref[g] = col_i
        ccr, cci = col_r[:, ROW_C:ROW_C + 1], col_i[:, ROW_C:ROW_C + 1]
        br, bi = br_ref[g], bi_ref[g]
        bbr = ccr * br - cci * bi
        bbi = ccr * bi + cci * br
        aer = _dot_split(col_r[:, :chunk], e_sel, True)
        aei = _dot_split(col_i[:, :chunk], e_sel, True)
        btr = _dot_split(bbr, tile, True)
        bti = _dot_split(bbi, tile, True)
        ptr = aer * btr - aei * bti
        pti = aer * bti + aei * btr
        wt_ref[g, n:n + SSM_STATE, :] = ptr.astype(BF16)
        wt_ref[g, n + SSM_STATE:n + 2 * SSM_STATE, :] = pti.astype(BF16)
        taps = _dot_split(cr_ref[g], ptr) - _dot_split(ci_ref[g], pti)
        padded = jnp.concatenate([taps, jnp.zeros_like(taps)], axis=1)
        for t in range(chunk):
            sh = SSM_GROUP * (chunk - 1 - t)
            blk = padded if sh == 0 else pltpu.roll(padded, 2 * n - sh, 1)
            wt_ref[g, SSM_GROUP * t:SSM_GROUP * (t + 1), :] = blk[:, :n].astype(BF16)
        c2r, c2i = c2r_ref[g], c2i_ref[g]
        for t in range(chunk):
            ar = tab_r[t + 1:t + 2, :]
            ai = tab_i[t + 1:t + 2, :]
            x1 = jnp.where(lane < SSM_STATE, ar, -ai)
            x2 = jnp.where(lane < SSM_STATE, ai, ar)
            qt_ref[g, SSM_GROUP * t:SSM_GROUP * (t + 1), :] = (c2r * x1 - c2i * x2).astype(BF16)
        return carry

    lax.fori_loop(0, wt_ref.shape[0], one_group, 0, unroll=4)


def _s5_operators(tables, b_re, b_im, c_re, c_im, chunk):
    g = b_re.shape[0]
    n = SSM_GROUP * chunk
    assert chunk <= PROMPT_CHUNK
    dup = lambda a: jnp.concatenate([a, a], axis=-1)
    per_g = lambda *s: pl.BlockSpec((GROUPS_PER_TILE,) + s, lambda i: (i,) + (0,) * len(s))
    return pl.pallas_call(
        functools.partial(_s5_prep_kernel, chunk=chunk),
        grid=(g // GROUPS_PER_TILE,),
        in_specs=[per_g(TABLE_ROWS, LANES), per_g(TABLE_ROWS, LANES),
                  per_g(SSM_STATE, SSM_GROUP), per_g(SSM_STATE, SSM_GROUP),
                  per_g(SSM_GROUP, SSM_STATE), per_g(SSM_GROUP, SSM_STATE),
                  per_g(SSM_GROUP, LANES), per_g(SSM_GROUP, LANES)],
        out_specs=(per_g(n + 2 * SSM_STATE, n), per_g(n, 2 * SSM_STATE),
                   per_g(SSM_STATE, LANES), per_g(SSM_STATE, LANES)),
        out_shape=(jax.ShapeDtypeStruct((g, n + 2 * SSM_STATE, n), BF16),
                   jax.ShapeDtypeStruct((g, n, 2 * SSM_STATE), BF16),
                   jax.ShapeDtypeStruct((g, SSM_STATE, LANES), F32),
                   jax.ShapeDtypeStruct((g, SSM_STATE, LANES), F32)),
        compiler_params=_cparams("parallel"),
        name="s5_prep",
    )(*tables, b_re, b_im, c_re, c_im, dup(c_re), dup(c_im))


def _s5_scan_kernel(u_ref, wt_ref, qt_ref, kr_ref, ki_ref, dsk_ref, *rest, chunk, carry):
    if carry:
        y_ref, hf_ref, dall_ref, yall_ref = rest
    else:
        h0_ref, y_ref, hf_ref, dall_ref, yall_ref = rest
    nc = u_ref.shape[1]
    ncp = dall_ref.shape[2]
    nseq = ncp // LANES if carry else nc
    n = SSM_GROUP * chunk
    iota = lambda shape, dim: lax.broadcasted_iota(jnp.int32, shape, dim)
    for s in range(chunk):
        dall_ref[s] = _pad_rows(u_ref[s], ncp).T.astype(BF16)
    lane = iota((SSM_STATE, ncp), 1) & (LANES - 1)
    slot = iota((LANES, LANES), 1)
    if not carry:
        h0_all = _pad_rows(jnp.concatenate([h0_ref[:, gp, :] for gp in range(GROUPS_PER_TILE)], axis=0), LANES).T
    acc = jnp.zeros((LANES, LANES), F32)
    for gp in range(GROUPS_PER_TILE):
        rows = slice(SSM_GROUP * gp, SSM_GROUP * (gp + 1))
        d = dall_ref[:, rows, :].reshape(n, ncp)
        ys = jnp.dot(wt_ref[gp], d, preferred_element_type=F32)
        sr = ys[n:n + SSM_STATE]
        si = ys[n + SSM_STATE:]
        if carry:
            for i in range(N_SCAN):
                sh = 1 << i
                ar = kr_ref[gp, :, ROW_DBL + i:ROW_DBL + i + 1]
                ai = ki_ref[gp, :, ROW_DBL + i:ROW_DBL + i + 1]
                pr = jnp.where(lane >= sh, pltpu.roll(sr, sh, 1), 0.0)
                pi = jnp.where(lane >= sh, pltpu.roll(si, sh, 1), 0.0)
                sr, si = sr + ar * pr - ai * pi, si + ar * pi + ai * pr
            hp = jnp.concatenate([jnp.where(lane >= 1, pltpu.roll(sr, 1, 1), 0.0),
                                  jnp.where(lane >= 1, pltpu.roll(si, 1, 1), 0.0)], axis=0)
            for b in range(nseq):
                seq = slice(LANES * b, LANES * (b + 1))
                end = jnp.concatenate([sr[:, seq], si[:, seq]], axis=0)
                acc = jnp.where(slot == 8 * gp + b, pltpu.roll(end, (8 * gp + b + 1) % LANES, 1), acc)
        else:
            hp = h0_all if gp == 0 else pltpu.roll(h0_all, LANES - 8 * gp, 1)
            hp = jnp.where(slot < nseq, hp, 0.0)
            hpr, hpi = hp[:SSM_STATE], hp[SSM_STATE:]
            ar = kr_ref[gp, :, chunk:chunk + 1]
            ai = ki_ref[gp, :, chunk:chunk + 1]
            end = jnp.concatenate([sr + ar * hpr - ai * hpi, si + ar * hpi + ai * hpr], axis=0)
            if gp:
                end = pltpu.roll(end, 8 * gp, 1)
            acc = jnp.where((slot >= 8 * gp) & (slot < 8 * gp + nseq), end, acc)
        y = ys[:n] + jnp.dot(qt_ref[gp], hp.astype(BF16), preferred_element_type=F32)
        yall_ref[:, rows, :] = y.reshape(chunk, SSM_GROUP, ncp)
    hf_ref[...] = acc.T[:8 * GROUPS_PER_TILE].reshape(GROUPS_PER_TILE, 8, LANES)
    for t in range(chunk):
        y_ref[t] = yall_ref[t].T[:nc] + dsk_ref[...] * u_ref[t]


def _s5_scan(proj, ops, d_skip, h0, nseq, *, carry):
    wt, qt, kr, ki = ops
    chunk, nc, _ = proj.shape
    n = SSM_GROUP * chunk
    assert nc == (nseq * LANES if carry else nseq) and (carry or nseq == 8) and nseq <= 8
    ncp = max(nc, LANES)
    gt = GROUPS_PER_TILE
    tile3 = lambda a, b: pl.BlockSpec((gt, a, b), lambda i: (i, 0, 0))
    lane_tile = pl.BlockSpec((chunk, nc, LANES), lambda i: (0, 0, i))
    in_specs = [lane_tile, tile3(n + 2 * SSM_STATE, n), tile3(n, 2 * SSM_STATE), tile3(SSM_STATE, LANES),
                tile3(SSM_STATE, LANES), pl.BlockSpec((1, LANES), lambda i: (0, i))]
    args = [proj, wt, qt, kr, ki, d_skip.reshape(1, -1)]
    if not carry:
        in_specs.append(pl.BlockSpec((nseq, gt, LANES), lambda i: (0, i, 0)))
        args.append(h0)
    return pl.pallas_call(
        functools.partial(_s5_scan_kernel, chunk=chunk, carry=carry),
        grid=(SSM_GROUPS // gt,),
        in_specs=in_specs,
        out_specs=(lane_tile, tile3(8, LANES)),
        out_shape=(jax.ShapeDtypeStruct((chunk, nc, SSM_GROUPS * SSM_GROUP), F32),
                   jax.ShapeDtypeStruct((SSM_GROUPS, 8, LANES), F32)),
        scratch_shapes=[pltpu.VMEM((chunk, LANES, ncp), BF16), pltpu.VMEM((chunk, LANES, ncp), F32)],
        compiler_params=_cparams("parallel"),
        name="s5_scan",
    )(*args)


def _gated_out(y, z, wg_ref, bg_ref, wo_ref):
    g = jax.nn.gelu(y, approximate=True)
    gate = jnp.dot(g.astype(BF16), wg_ref[...], preferred_element_type=F32) + bg_ref[...]
    yy = (g * jax.nn.sigmoid(gate)) * _silu(z)
    return jnp.dot(yy.astype(BF16), wo_ref[...], preferred_element_type=F32)


def _glu_out_kernel(y_ref, z_ref, x_ref, wg_ref, bg_ref, wo_ref, gf_ref, o_ref):
    h = x_ref[...] + _gated_out(y_ref[...], z_ref[...], wg_ref, bg_ref, wo_ref)
    o_ref[...] = _rms_norm(h, gf_ref[...])


def _glu_out(y, proj, x, w_glu, b_glu, w_out, g_final, *, tm):
    m, w = x.shape
    row = lambda c: pl.BlockSpec((tm, w), lambda i: (i, c))
    vec = pl.BlockSpec((1, w), lambda i: (0, 0))
    mat = pl.BlockSpec((w, w), lambda i: (0, 0), pipeline_mode=pl.Buffered(1))
    return pl.pallas_call(
        _glu_out_kernel,
        grid=(m // tm,),
        in_specs=[row(0), row(1), row(0), mat, vec, mat, vec],
        out_specs=row(0),
        out_shape=jax.ShapeDtypeStruct((m, w), F32),
        compiler_params=_cparams("parallel"),
        name="glu_out",
    )(y, proj, x, w_glu, b_glu.reshape(1, w), w_out, g_final.reshape(1, w))


def _glu_out_pm_kernel(y_ref, z_ref, x_ref, wg_ref, bg_ref, wo_ref, gf_ref, o_ref, slab_ref):
    p, ct, w = y_ref.shape
    r = _gated_out(y_ref[...].reshape(p * ct, w), z_ref[...].reshape(p * ct, w), wg_ref, bg_ref, wo_ref)
    for s in range(p):
        for j in range(w // LANES):
            slab_ref[j, pl.ds(s, ct, stride=p), :] = r[s * ct:(s + 1) * ct, j * LANES:(j + 1) * LANES]
    r_tok = jnp.concatenate([slab_ref[j].reshape(ct, p, LANES) for j in range(w // LANES)], axis=2)
    o_ref[...] = _rms_norm(x_ref[...] + r_tok, gf_ref[...])


def _glu_out_pm(y, proj, x, w_glu, b_glu, w_out, g_final, *, ct):
    chunk, nc, w = y.shape
    m = chunk * nc
    sg = chunk // POS_TILE
    pm = lambda c: pl.BlockSpec((POS_TILE, ct, w), lambda i: (i % sg, i // sg, c))
    tok = pl.BlockSpec((ct, POS_TILE, w), lambda i: (i // sg, i % sg, 0))
    vec = pl.BlockSpec((1, w), lambda i: (0, 0))
    mat = pl.BlockSpec((w, w), lambda i: (0, 0), pipeline_mode=pl.Buffered(1))
    return pl.pallas_call(
        _glu_out_pm_kernel,
        grid=((nc // ct) * sg,),
        in_specs=[pm(0), pm(1), tok, mat, vec, mat, vec],
        out_specs=tok,
        out_shape=jax.ShapeDtypeStruct((nc, chunk, w), F32),
        scratch_shapes=[pltpu.VMEM((w // LANES, ct * POS_TILE, LANES), F32)],
        compiler_params=_cparams("parallel"),
        name="glu_out_pm",
    )(y, proj, x.reshape(nc, chunk, w), w_glu, b_glu.reshape(1, w), w_out, g_final.reshape(1, w)).reshape(m, w)


PAST_LEN = 16384


def kernel(x_prompt, x_sample, cache_win_k, cache_win_v, state_conv, state_ssm_re, state_ssm_im, attn_norm, w_in_ab, conv_w, w_out_ab, ssm_norm, w_in_c, lam_re, lam_im, log_step, b_re, b_im, c_re, c_im, d_skip, w_glu, b_glu, w_out_c, final_norm):
    bp, tp, _ = x_prompt.shape
    bs, ts, _ = x_sample.shape
    n_keep = min(2048, tp)
    xp = x_prompt.reshape(bp * tp, D_MODEL)
    xs = x_sample.reshape(bs * ts, D_MODEL)

    w_in0 = w_in_ab[0].astype(BF16)
    w_out0 = w_out_ab[0].astype(BF16)
    proj_p = _norm_matmul(xp, attn_norm[0], w_in0, _rope_tables(tp, tp, 0), tm=512, tn=1024, rope_tiles=2,
                          tiles_per_seq=tp // 512)
    proj_s = _norm_matmul(xs, attn_norm[0], w_in0, _rope_tables(bs * ts, ts, PAST_LEN), tm=bs * ts, tn=1024,
                          rope_tiles=2)
    o_p, k_p, v_p = _attn_prompt(proj_p, bp, tp, n_keep)
    o_p = o_p.reshape(bp * tp, ATTN_WIDTH)
    k_p = k_p.reshape(1, bp, n_keep, N_HEADS, HEAD_DIM)
    v_p = v_p.reshape(1, bp, n_keep, N_HEADS, HEAD_DIM)
    o_s = _attn_sample(proj_s, cache_win_k[0], cache_win_v[0], bs, ts)
    h1_p, conv_p = _mix_out(proj_p, o_p, xp, jnp.zeros((bp, 2, CONV_WIDTH), F32), conv_w[0], w_out0,
                            tm=256, tiles_per_seq=tp // 256)
    h1_s, conv_s = _mix_out(proj_s, o_s, xs, state_conv[0], conv_w[0], w_out0, tm=ts, tiles_per_seq=1)
    k_s = proj_s[:, ATTN_WIDTH:2 * ATTN_WIDTH].reshape(1, bs, ts, N_HEADS, HEAD_DIM)
    v_s = proj_s[:, 2 * ATTN_WIDTH:3 * ATTN_WIDTH].reshape(1, bs, ts, N_HEADS, HEAD_DIM)

    w_in1 = w_in_c[0].astype(BF16)
    w_glu1 = w_glu[0].astype(BF16)
    w_out1 = w_out_c[0].astype(BF16)
    tables = _s5_tables(lam_re[0], lam_im[0], log_step[0])
    bc = (b_re[0], b_im[0], c_re[0], c_im[0])
    proj1_p = _norm_matmul_pm(h1_p, ssm_norm[0], w_in1, chunk=PROMPT_CHUNK, ct=64, tn=1024)
    y_p, hf_p = _s5_scan(proj1_p, _s5_operators(tables, *bc, PROMPT_CHUNK), d_skip[0], None, bp, carry=True)
    out_p = _glu_out_pm(y_p, proj1_p, h1_p, w_glu1, b_glu[0], w_out1, final_norm, ct=32)
    proj1_s = _norm_matmul(h1_s, ssm_norm[0], w_in1, tm=bs * ts, tn=1024)
    h0 = jnp.concatenate([state_ssm_re[0], state_ssm_im[0]], axis=-1)
    y_s, hf_s = _s5_scan(proj1_s.reshape(bs, ts, -1).transpose(1, 0, 2), _s5_operators(tables, *bc, ts), d_skip[0],
                         h0, bs, carry=False)
    y_s = y_s.transpose(1, 0, 2).reshape(bs * ts, D_MODEL)
    out_s = _glu_out(y_s, proj1_s, h1_s, w_glu1, b_glu[0], w_out1, final_norm, tm=bs * ts)
    hf_p = hf_p[:, :bp].transpose(1, 0, 2)[None]
    hf_s = hf_s[:, :bs].transpose(1, 0, 2)[None]
    return (out_p.reshape(bp, tp, D_MODEL), out_s.reshape(bs, ts, D_MODEL),
            k_p, v_p, conv_p[None], hf_p[..., :SSM_STATE], hf_p[..., SSM_STATE:],
            k_s, v_s, conv_s[None], hf_s[..., :SSM_STATE], hf_s[..., SSM_STATE:])
```

```python
import functools
import math

import jax
import jax.numpy as jnp
from jax import lax
from jax.experimental import pallas as pl
from jax.experimental.pallas import tpu as pltpu

D_MODEL = 2048
HEAD_DIM = 128
N_HEADS = 8
ATTN_WIDTH = 1024
CONV_WIDTH = 1024
DILATIONS = (1, 4, 16)
N_BACK = 128
ROPE_THETA = 10000.0
RMS_EPS = 1e-6
SSM_GROUP = 16
SSM_GROUPS = 128
SSM_STATE = 64
LANES = 128
VMEM_LIMIT = 56 * 1024 * 1024

F32 = jnp.float32
BF16 = jnp.bfloat16


def _cparams(*sem):
    return pltpu.CompilerParams(dimension_semantics=sem, vmem_limit_bytes=VMEM_LIMIT)


def _rope_table_kernel(inv_ref, cos_ref, sin_ref, *, period, offset):
    rows = cos_ref.shape[0]
    r = lax.broadcasted_iota(jnp.int32, (rows, LANES), 0) + pl.program_id(0) * rows
    pos = (offset + lax.rem(r, period)).astype(F32)
    ang = pos * inv_ref[...]
    lane = lax.broadcasted_iota(jnp.int32, (rows, LANES), 1)
    cos_ref[...] = jnp.cos(ang)
    sin_ref[...] = jnp.where(lane < HEAD_DIM // 2, -1.0, 1.0) * jnp.sin(ang)


def _rope_tables(rows, period, offset):
    half = HEAD_DIM // 2
    inv = ROPE_THETA ** (-jnp.arange(half, dtype=F32) / half)
    inv2 = jnp.concatenate([inv, inv])[None, :]
    tr = min(rows, 256)
    return pl.pallas_call(
        functools.partial(_rope_table_kernel, period=period, offset=offset),
        grid=(rows // tr,),
        in_specs=[pl.BlockSpec((1, LANES), lambda i: (0, 0))],
        out_specs=(pl.BlockSpec((tr, LANES), lambda i: (i, 0)),) * 2,
        out_shape=(jax.ShapeDtypeStruct((rows, LANES), F32),) * 2,
        compiler_params=_cparams("parallel"),
        name="rope_table",
    )(inv2)


def _rms_norm(x, g):
    return x * lax.rsqrt(jnp.mean(x * x, axis=-1, keepdims=True) + RMS_EPS) * g


def _norm_matmul_kernel(x_ref, g_ref, w_ref, *rest, rope_tiles):
    o_ref, hn_ref = rest[-2:]
    n = pl.program_id(1)

    @pl.when(n == 0)
    def _():
        hn_ref[...] = _rms_norm(x_ref[...], g_ref[...]).astype(BF16)

    acc = jnp.dot(hn_ref[...], w_ref[...], preferred_element_type=F32)

    @pl.when(n >= rope_tiles)
    def _():
        o_ref[...] = acc

    if rope_tiles:
        cos_ref, sin_ref = rest[:2]

        @pl.when(n < rope_tiles)
        def _():
            cos = cos_ref[...]
            sin = sin_ref[...]
            for h in range(o_ref.shape[1] // HEAD_DIM):
                xh = acc[:, h * HEAD_DIM:(h + 1) * HEAD_DIM]
                o_ref[:, h * HEAD_DIM:(h + 1) * HEAD_DIM] = xh * cos + pltpu.roll(xh, HEAD_DIM // 2, 1) * sin


def _norm_matmul(x, g, w, rope=None, *, tm, tn, rope_tiles=0, tiles_per_seq=1):
    m, k = x.shape
    n = w.shape[1]
    table = pl.BlockSpec((tm, LANES), lambda i, j: (i % tiles_per_seq, 0))
    return pl.pallas_call(
        functools.partial(_norm_matmul_kernel, rope_tiles=rope_tiles),
        grid=(m // tm, n // tn),
        in_specs=[pl.BlockSpec((tm, k), lambda i, j: (i, 0)), pl.BlockSpec((1, k), lambda i, j: (0, 0)),
                  pl.BlockSpec((k, tn), lambda i, j: (0, j))] + ([table, table] if rope_tiles else []),
        out_specs=pl.BlockSpec((tm, tn), lambda i, j: (i, j)),
        out_shape=jax.ShapeDtypeStruct((m, n), F32),
        scratch_shapes=[pltpu.VMEM((tm, k), BF16)],
        compiler_params=_cparams("parallel", "arbitrary"),
        name="norm_matmul",
    )(x, g.reshape(1, k), w, *(rope if rope_tiles else ()))


POS_TILE = 8


def _norm_matmul_pm_kernel(x_ref, g_ref, w_ref, o_ref, hn_ref, slab_ref):
    ct, _, k = x_ref.shape

    @pl.when(pl.program_id(1) == 0)
    def _():
        hn = _rms_norm(x_ref[...], g_ref[...])
        for j in range(k // LANES):
            slab_ref[j] = hn[:, :, j * LANES:(j + 1) * LANES].reshape(ct * POS_TILE, LANES)
        for s in range(POS_TILE):
            for j in range(k // LANES):
                hn_ref[s * ct:(s + 1) * ct, j * LANES:(j + 1) * LANES] = (
                    slab_ref[j, pl.ds(s, ct, stride=POS_TILE), :].astype(BF16))

    acc = jnp.dot(hn_ref[...], w_ref[...], preferred_element_type=F32)
    o_ref[...] = acc.reshape(o_ref.shape)


def _norm_matmul_pm(x, g, w, *, chunk, ct, tn):
    m, k = x.shape
    n = w.shape[1]
    nc = m // chunk
    sg = chunk // POS_TILE
    return pl.pallas_call(
        _norm_matmul_pm_kernel,
        grid=((nc // ct) * sg, n // tn),
        in_specs=[pl.BlockSpec((ct, POS_TILE, k), lambda i, j: (i // sg, i % sg, 0)),
                  pl.BlockSpec((1, k), lambda i, j: (0, 0)), pl.BlockSpec((k, tn), lambda i, j: (0, j))],
        out_specs=pl.BlockSpec((POS_TILE, ct, tn), lambda i, j: (i % sg, i // sg, j)),
        out_shape=jax.ShapeDtypeStruct((chunk, nc, n), F32),
        scratch_shapes=[pltpu.VMEM((ct * POS_TILE, k), BF16), pltpu.VMEM((k // LANES, ct * POS_TILE, LANES), F32)],
        compiler_params=_cparams("parallel", "arbitrary"),
        name="norm_matmul_pm",
    )(x.reshape(nc, chunk, k), g.reshape(1, k), w)


def _attn_prompt_kernel(q_ref, k_ref, v_ref, o_ref, ks_ref, vs_ref, qc_ref, kc_ref, vc_ref, oc_ref, lc_ref,
                        to_ref, tl_ref, lse_ref):
    t = q_ref.shape[1]
    n_keep = ks_ref.shape[1]
    ks_ref[0] = k_ref[0, t - n_keep:, :]
    vs_ref[0] = v_ref[0, t - n_keep:, :]
    nblk = t // N_BACK
    scale = HEAD_DIM ** -0.5
    kc_ref[0:N_BACK, :] = jnp.zeros((N_BACK, HEAD_DIM), BF16)
    vc_ref[0:N_BACK, :] = jnp.zeros((N_BACK, HEAD_DIM), BF16)
    qi = lax.broadcasted_iota(jnp.int32, (N_BACK, 2 * N_BACK), 0)
    kj = lax.broadcasted_iota(jnp.int32, (N_BACK, 2 * N_BACK), 1)
    dist = N_BACK + qi - kj
    band = (dist >= 0) & (dist <= N_BACK)

    for d in DILATIONS:
        ln = t // d
        blocks_per_class = ln // N_BACK
        for r in range(d):
            rows = pl.ds(r, ln, stride=d) if d > 1 else pl.ds(0, ln)
            qc_ref[r * ln:(r + 1) * ln, :] = q_ref[0, rows, :].astype(BF16)
            kc_ref[N_BACK + r * ln:N_BACK + (r + 1) * ln, :] = k_ref[0, rows, :].astype(BF16)
            vc_ref[N_BACK + r * ln:N_BACK + (r + 1) * ln, :] = v_ref[0, rows, :].astype(BF16)

        def block(b, carry):
            row0 = pl.multiple_of(b * N_BACK, N_BACK)
            j = lax.rem(b, blocks_per_class)
            q = qc_ref[pl.ds(row0, N_BACK), :]
            k2 = kc_ref[pl.ds(row0, 2 * N_BACK), :]
            v2 = vc_ref[pl.ds(row0, 2 * N_BACK), :]
            s = lax.dot_general(q, k2, (((1,), (1,)), ((), ())), preferred_element_type=F32) * scale
            valid = band & ((kj >= N_BACK) | (j > 0))
            s = jnp.where(valid, s, -jnp.inf)
            m = jnp.max(s, axis=-1, keepdims=True)
            p = jnp.exp(s - m)
            l = jnp.sum(p, axis=-1, keepdims=True)
            o = jnp.dot(p.astype(BF16), v2, preferred_element_type=F32)
            oc_ref[pl.ds(row0, N_BACK), :] = o / l
            lc_ref[pl.ds(row0, N_BACK), :] = jnp.broadcast_to(m + jnp.log(l), (N_BACK, HEAD_DIM))
            return carry

        lax.fori_loop(0, nblk, block, 0, unroll=16)

        if d == 1:
            o_ref[0] = oc_ref[...]
            lse_ref[...] = lc_ref[...]
        else:
            for r in range(d):
                to_ref[pl.ds(r, ln, stride=d), :] = oc_ref[r * ln:(r + 1) * ln, :]
                tl_ref[pl.ds(r, ln, stride=d), :] = lc_ref[r * ln:(r + 1) * ln, :]

            def merge(c, carry):
                rows = pl.ds(pl.multiple_of(c * 64, 64), 64)
                la = lse_ref[rows, :]
                lb = tl_ref[rows, :]
                mx = jnp.maximum(la, lb)
                ea = jnp.exp(la - mx)
                eb = jnp.exp(lb - mx)
                tot = ea + eb
                o_ref[0, rows, :] = (o_ref[0, rows, :] * ea + to_ref[rows, :] * eb) / tot
                lse_ref[rows, :] = mx + jnp.log(tot)
                return carry

            lax.fori_loop(0, t // 64, merge, 0, unroll=2)


def _attn_prompt(proj, b, t, n_keep):
    p3 = proj.reshape(b, t, proj.shape[1])
    blk = lambda off: pl.BlockSpec((1, t, HEAD_DIM), lambda i, h: (i, 0, off + h))
    keep = pl.BlockSpec((1, n_keep, HEAD_DIM), lambda i, h: (i, 0, h))
    return pl.pallas_call(
        _attn_prompt_kernel,
        grid=(b, N_HEADS),
        in_specs=[blk(0), blk(N_HEADS), blk(2 * N_HEADS)],
        out_specs=(pl.BlockSpec((1, t, HEAD_DIM), lambda i, h: (i, 0, h)), keep, keep),
        out_shape=(jax.ShapeDtypeStruct((b, t, ATTN_WIDTH), F32),
                   jax.ShapeDtypeStruct((b, n_keep, ATTN_WIDTH), F32),
                   jax.ShapeDtypeStruct((b, n_keep, ATTN_WIDTH), F32)),
        scratch_shapes=[
            pltpu.VMEM((t, HEAD_DIM), BF16),
            pltpu.VMEM((t + N_BACK, HEAD_DIM), BF16),
            pltpu.VMEM((t + N_BACK, HEAD_DIM), BF16),
            pltpu.VMEM((t, HEAD_DIM), F32),
            pltpu.VMEM((t, HEAD_DIM), F32),
            pltpu.VMEM((t, HEAD_DIM), F32),
            pltpu.VMEM((t, HEAD_DIM), F32),
            pltpu.VMEM((t, HEAD_DIM), F32),
        ],
        compiler_params=_cparams("parallel", "parallel"),
        name="attn_prompt",
    )(p3, p3, p3)


def _attn_sample_kernel(q_ref, kn_ref, vn_ref, kc_ref, vc_ref, o_ref):
    s_len = q_ref.shape[0]
    n_buf = kc_ref.shape[1] // N_HEADS
    scale = HEAD_DIM ** -0.5
    nt = (((1,), (1,)), ((), ()))

    def count(dist):
        c = jnp.zeros(dist.shape, F32)
        for d in DILATIONS:
            hit = (dist >= 0) & (dist <= N_BACK * d) & ((dist & (d - 1)) == 0)
            c = c + jnp.where(hit, 1.0, 0.0)
        return c

    iota = lambda shape, dim: lax.broadcasted_iota(jnp.int32, shape, dim)
    cc = count(n_buf + iota((s_len, n_buf), 0) - iota((s_len, n_buf), 1))
    cn = count(iota((s_len, s_len), 0) - iota((s_len, s_len), 1))
    for h in range(N_HEADS):
        cols = slice(h * HEAD_DIM, (h + 1) * HEAD_DIM)
        head_rows = pl.ds(h, n_buf, stride=N_HEADS)
        q = q_ref[:, cols].astype(BF16)
        sc = lax.dot_general(q, kc_ref[0, head_rows, :].astype(BF16), nt, preferred_element_type=F32) * scale
        sn = lax.dot_general(q, kn_ref[:, cols].astype(BF16), nt, preferred_element_type=F32) * scale
        sc = jnp.where(cc > 0, sc, -jnp.inf)
        sn = jnp.where(cn > 0, sn, -jnp.inf)
        m = jnp.maximum(jnp.max(sc, axis=-1, keepdims=True), jnp.max(sn, axis=-1, keepdims=True))
        pc = cc * jnp.exp(sc - m)
        pn = cn * jnp.exp(sn - m)
        l = jnp.sum(pc, axis=-1, keepdims=True) + jnp.sum(pn, axis=-1, keepdims=True)
        o = (jnp.dot(pc.astype(BF16), vc_ref[0, head_rows, :].astype(BF16), preferred_element_type=F32)
             + jnp.dot(pn.astype(BF16), vn_ref[:, cols].astype(BF16), preferred_element_type=F32))
        o_ref[:, cols] = o / l


def _attn_sample(proj, cache_k, cache_v, b, s_len):
    n_buf = cache_k.shape[1]
    ck = cache_k.reshape(b, n_buf * N_HEADS, HEAD_DIM)
    cv = cache_v.reshape(b, n_buf * N_HEADS, HEAD_DIM)
    new = lambda c: pl.BlockSpec((s_len, ATTN_WIDTH), lambda i: (i, c))
    old = pl.BlockSpec((1, n_buf * N_HEADS, HEAD_DIM), lambda i: (i, 0, 0))
    return pl.pallas_call(
        _attn_sample_kernel,
        grid=(b,),
        in_specs=[new(0), new(1), new(2), old, old],
        out_specs=new(0),
        out_shape=jax.ShapeDtypeStruct((b * s_len, ATTN_WIDTH), F32),
        compiler_params=_cparams("parallel"),
        name="attn_sample",
    )(proj, proj, proj, ck, cv)


def _silu(z):
    return z * jax.nn.sigmoid(z)


def _mix_out_kernel(o_ref, za_ref, gb_ref, gc_ref, hi_ref, zb_ref, gch_ref, hih_ref, init_ref, cw_ref,
                    w_ref, x_ref, h_ref, cs_ref, *, tiles_per_seq):
    tm = o_ref.shape[0]
    first = (pl.program_id(0) % tiles_per_seq) == 0
    ch = gc_ref[...] * hi_ref[...]
    halo = gch_ref[...] * hih_ref[...]
    prev1 = jnp.where(first, init_ref[0, 1:2, :], halo[7:8, :])
    prev2 = jnp.where(first, init_ref[0, 0:1, :], halo[6:7, :])
    row = lax.broadcasted_iota(jnp.int32, (tm, 1), 0)
    ch1 = jnp.where(row == 0, prev1, pltpu.roll(ch, 1, 0))
    ch2 = jnp.where(row == 0, prev2, jnp.where(row == 1, prev1, pltpu.roll(ch, 2, 0)))
    conv = ch2 * cw_ref[0:1, :] + ch1 * cw_ref[1:2, :] + ch * cw_ref[2:3, :]
    cs_ref[0] = ch[tm - 2:tm, :]
    o_a = (o_ref[...] * _silu(za_ref[...])).astype(BF16)
    o_b = (gb_ref[...] * conv * _silu(zb_ref[...])).astype(BF16)
    y = jnp.dot(o_a, w_ref[0:ATTN_WIDTH, :], preferred_element_type=F32)
    y = y + jnp.dot(o_b, w_ref[ATTN_WIDTH:, :], preferred_element_type=F32)
    h_ref[...] = x_ref[...] + y


def _mix_out(proj, o_attn, x, conv_init, conv_w, w_out, *, tm, tiles_per_seq):
    m = x.shape[0]
    nseq = conv_init.shape[0]
    col = lambda c: pl.BlockSpec((tm, CONV_WIDTH), lambda i: (i, c))
    halo = lambda c: pl.BlockSpec((8, CONV_WIDTH), lambda i: (jnp.maximum(i * (tm // 8) - 1, 0), c))
    seq = pl.BlockSpec((1, 2, CONV_WIDTH), lambda i: (i // tiles_per_seq, 0, 0))
    return pl.pallas_call(
        functools.partial(_mix_out_kernel, tiles_per_seq=tiles_per_seq),
        grid=(m // tm,),
        in_specs=[
            pl.BlockSpec((tm, ATTN_WIDTH), lambda i: (i, 0)),
            col(3), col(4), col(5), col(6), col(7), halo(5), halo(6), seq,
            pl.BlockSpec((3, CONV_WIDTH), lambda i: (0, 0)),
            pl.BlockSpec((D_MODEL, D_MODEL), lambda i: (0, 0), pipeline_mode=pl.Buffered(1)),
            pl.BlockSpec((tm, D_MODEL), lambda i: (i, 0)),
        ],
        out_specs=(pl.BlockSpec((tm, D_MODEL), lambda i: (i, 0)), seq),
        out_shape=(jax.ShapeDtypeStruct((m, D_MODEL), F32), jax.ShapeDtypeStruct((nseq, 2, CONV_WIDTH), F32)),
        compiler_params=_cparams("arbitrary"),
        name="mix_out",
    )(o_attn, proj, proj, proj, proj, proj, proj, proj, conv_init, conv_w, w_out, x)


N_SCAN = 7
GROUPS_PER_TILE = LANES // SSM_GROUP


PROMPT_CHUNK = 32
ROW_C = PROMPT_CHUNK + 1
ROW_DBL = PROMPT_CHUNK + 2
TABLE_ROWS = ROW_DBL + N_SCAN


def _s5_disc_kernel(lr_ref, li_ref, ls_ref, tr_ref, ti_ref):
    lr = lr_ref[...]
    li = li_ref[...]
    step = jnp.exp(ls_ref[...])
    mag = jnp.exp(lr * step)
    ar = mag * jnp.cos(li * step)
    ai = mag * jnp.sin(li * step)
    den = lr * lr + li * li
    nr = ar - 1.0
    tr_ref[ROW_C] = (nr * lr + ai * li) / den
    ti_ref[ROW_C] = (ai * lr - nr * li) / den
    pr = jnp.ones_like(ar)
    pi = jnp.zeros_like(ar)
    for tau in range(PROMPT_CHUNK + 1):
        tr_ref[tau] = pr
        ti_ref[tau] = pi
        dr, di = pr, pi
        pr, pi = pr * ar - pi * ai, pr * ai + pi * ar
    for i in range(N_SCAN):
        tr_ref[ROW_DBL + i] = dr
        ti_ref[ROW_DBL + i] = di
        dr, di = dr * dr - di * di, 2.0 * dr * di


def _s5_tables(lam_re, lam_im, log_step):
    g = lam_re.shape[0]
    dup = lambda a: jnp.concatenate([a, a], axis=-1)
    tr, ti = pl.pallas_call(
        _s5_disc_kernel,
        out_shape=(jax.ShapeDtypeStruct((TABLE_ROWS, g, LANES), F32),) * 2,
        name="s5_disc",
    )(dup(lam_re), dup(lam_im), log_step[:, None])
    return tr.transpose(1, 0, 2), ti.transpose(1, 0, 2)


def _split_bf16(x):
    hi = x.astype(BF16)
    return hi, (x - hi.astype(F32)).astype(BF16)


def _dot_split(a, b, b_is_bf16_exact=False):
    dot = lambda x, y: jnp.dot(x, y, preferred_element_type=F32)
    ah, al = _split_bf16(a)
    if b_is_bf16_exact:
        bh = b.astype(BF16)
        return dot(ah, bh) + dot(al, bh)
    bh, bl = _split_bf16(b)
    return dot(ah, bh) + (dot(ah, bl) + dot(al, bh))


def _pad_rows(x, rows):
    if x.shape[0] == rows:
        return x
    return jnp.concatenate([x, jnp.zeros((rows - x.shape[0], x.shape[1]), x.dtype)], axis=0)


def _s5_prep_kernel(tr_ref, ti_ref, br_ref, bi_ref, cr_ref, ci_ref, c2r_ref, c2i_ref,
                    wt_ref, qt_ref, kr_ref, ki_ref, *, chunk):
    n = SSM_GROUP * chunk
    iota = lambda shape, dim: lax.broadcasted_iota(jnp.int32, shape, dim)
    lane = iota((SSM_GROUP, 2 * SSM_STATE), 1)
    e_sel = jnp.where(iota((chunk, n), 0) == chunk - 1 - iota((chunk, n), 1) // SSM_GROUP, 1.0, 0.0)
    tile = jnp.where(iota((SSM_GROUP, n), 0) == iota((SSM_GROUP, n), 1) % SSM_GROUP, 1.0, 0.0)

    def one_group(g, carry):
        tab_r, tab_i = tr_ref[g], ti_ref[g]
        col_r = _pad_rows(tab_r, LANES).T[:SSM_STATE]
        col_i = _pad_rows(tab_i, LANES).T[:SSM_STATE]
        kr_ref[g] = col_r
        ki_ref[g] = col_i
        ccr, cci = col_r[:, ROW_C:ROW_C + 1], col_i[:, ROW_C:ROW_C + 1]
        br, bi = br_ref[g], bi_ref[g]
        bbr = ccr * br - cci * bi
        bbi = ccr * bi + cci * br
        aer = _dot_split(col_r[:, :chunk], e_sel, True)
        aei = _dot_split(col_i[:, :chunk], e_sel, True)
        btr = _dot_split(bbr, tile, True)
        bti = _dot_split(bbi, tile, True)
        ptr = aer * btr - aei * bti
        pti = aer * bti + aei * btr
        wt_ref[g, n:n + SSM_STATE, :] = ptr.astype(BF16)
        wt_ref[g, n + SSM_STATE:n + 2 * SSM_STATE, :] = pti.astype(BF16)
        taps = _dot_split(cr_ref[g], ptr) - _dot_split(ci_ref[g], pti)
        padded = jnp.concatenate([taps, jnp.zeros_like(taps)], axis=1)
        for t in range(chunk):
            sh = SSM_GROUP * (chunk - 1 - t)
            blk = padded if sh == 0 else pltpu.roll(padded, 2 * n - sh, 1)
            wt_ref[g, SSM_GROUP * t:SSM_GROUP * (t + 1), :] = blk[:, :n].astype(BF16)
        c2r, c2i = c2r_ref[g], c2i_ref[g]
        for t in range(chunk):
            ar = tab_r[t + 1:t + 2, :]
            ai = tab_i[t + 1:t + 2, :]
            x1 = jnp.where(lane < SSM_STATE, ar, -ai)
            x2 = jnp.where(lane < SSM_STATE, ai, ar)
            qt_ref[g, SSM_GROUP * t:SSM_GROUP * (t + 1), :] = (c2r * x1 - c2i * x2).astype(BF16)
        return carry

    lax.fori_loop(0, wt_ref.shape[0], one_group, 0, unroll=4)


def _s5_operators(tables, b_re, b_im, c_re, c_im, chunk):
    g = b_re.shape[0]
    n = SSM_GROUP * chunk
    assert chunk <= PROMPT_CHUNK
    dup = lambda a: jnp.concatenate([a, a], axis=-1)
    per_g = lambda *s: pl.BlockSpec((GROUPS_PER_TILE,) + s, lambda i: (i,) + (0,) * len(s))
    return pl.pallas_call(
        functools.partial(_s5_prep_kernel, chunk=chunk),
        grid=(g // GROUPS_PER_TILE,),
        in_specs=[per_g(TABLE_ROWS, LANES), per_g(TABLE_ROWS, LANES),
                  per_g(SSM_STATE, SSM_GROUP), per_g(SSM_STATE, SSM_GROUP),
                  per_g(SSM_GROUP, SSM_STATE), per_g(SSM_GROUP, SSM_STATE),
                  per_g(SSM_GROUP, LANES), per_g(SSM_GROUP, LANES)],
        out_specs=(per_g(n + 2 * SSM_STATE, n), per_g(n, 2 * SSM_STATE),
                   per_g(SSM_STATE, LANES), per_g(SSM_STATE, LANES)),
        out_shape=(jax.ShapeDtypeStruct((g, n + 2 * SSM_STATE, n), BF16),
                   jax.ShapeDtypeStruct((g, n, 2 * SSM_STATE), BF16),
                   jax.ShapeDtypeStruct((g, SSM_STATE, LANES), F32),
                   jax.ShapeDtypeStruct((g, SSM_STATE, LANES), F32)),
        compiler_params=_cparams("parallel"),
        name="s5_prep",
    )(*tables, b_re, b_im, c_re, c_im, dup(c_re), dup(c_im))


def _s5_scan_kernel(u_ref, wt_ref, ws_ref, qt_ref, kr_ref, ki_ref, dsk_ref, *rest, chunk, carry):
    if carry:
        y_ref, hf_ref, dall_ref, yall_ref = rest
    else:
        h0_ref, y_ref, hf_ref, dall_ref, yall_ref = rest
    nc = u_ref.shape[1]
    ncp = dall_ref.shape[2]
    nseq = ncp // LANES if carry else nc
    n = SSM_GROUP * chunk
    iota = lambda shape, dim: lax.broadcasted_iota(jnp.int32, shape, dim)
    for s in range(chunk):
        dall_ref[s] = _pad_rows(u_ref[s], ncp).T.astype(BF16)
    lane = iota((SSM_STATE, ncp), 1) & (LANES - 1)
    slot = iota((LANES, LANES), 1)
    if not carry:
        h0_all = _pad_rows(jnp.concatenate([h0_ref[:, gp, :] for gp in range(GROUPS_PER_TILE)], axis=0), LANES).T
    acc = jnp.zeros((LANES, LANES), F32)
    for gp in range(GROUPS_PER_TILE):
        rows = slice(SSM_GROUP * gp, SSM_GROUP * (gp + 1))
        d = dall_ref[:, rows, :].reshape(n, ncp)
        ys = jnp.dot(wt_ref[gp], d, preferred_element_type=F32)
        st = jnp.dot(ws_ref[gp], d, preferred_element_type=F32)
        sr = st[:SSM_STATE]
        si = st[SSM_STATE:]
        if carry:
            for i in range(N_SCAN):
                sh = 1 << i
                ar = kr_ref[gp, :, ROW_DBL + i:ROW_DBL + i + 1]
                ai = ki_ref[gp, :, ROW_DBL + i:ROW_DBL + i + 1]
                pr = jnp.where(lane >= sh, pltpu.roll(sr, sh, 1), 0.0)
                pi = jnp.where(lane >= sh, pltpu.roll(si, sh, 1), 0.0)
                sr, si = sr + ar * pr - ai * pi, si + ar * pi + ai * pr
            hp = jnp.concatenate([jnp.where(lane >= 1, pltpu.roll(sr, 1, 1), 0.0),
                                  jnp.where(lane >= 1, pltpu.roll(si, 1, 1), 0.0)], axis=0)
            for b in range(nseq):
                seq = slice(LANES * b, LANES * (b + 1))
                end = jnp.concatenate([sr[:, seq], si[:, seq]], axis=0)
                acc = jnp.where(slot == 8 * gp + b, pltpu.roll(end, (8 * gp + b + 1) % LANES, 1), acc)
        else:
            hp = h0_all if gp == 0 else pltpu.roll(h0_all, LANES - 8 * gp, 1)
            hp = jnp.where(slot < nseq, hp, 0.0)
            hpr, hpi = hp[:SSM_STATE], hp[SSM_STATE:]
            ar = kr_ref[gp, :, chunk:chunk + 1]
            ai = ki_ref[gp, :, chunk:chunk + 1]
            end = jnp.concatenate([sr + ar * hpr - ai * hpi, si + ar * hpi + ai * hpr], axis=0)
            if gp:
                end = pltpu.roll(end, 8 * gp, 1)
            acc = jnp.where((slot >= 8 * gp) & (slot < 8 * gp + nseq), end, acc)
        y = ys + jnp.dot(qt_ref[gp], hp.astype(BF16), preferred_element_type=F32)
        yall_ref[:, rows, :] = y.reshape(chunk, SSM_GROUP, ncp)
    hf_ref[...] = acc.T[:8 * GROUPS_PER_TILE].reshape(GROUPS_PER_TILE, 8, LANES)
    for t in range(chunk):
        y_ref[t] = yall_ref[t].T[:nc] + dsk_ref[...] * u_ref[t]


def _s5_scan(proj, ops, d_skip, h0, nseq, *, carry):
    wt, qt, kr, ki = ops
    chunk, nc, _ = proj.shape
    n = SSM_GROUP * chunk
    n_full = SSM_GROUP * PROMPT_CHUNK
    assert nc == (nseq * LANES if carry else nseq) and (carry or nseq == 8) and nseq <= 8
    assert n_full % n == 0 and n_full % (2 * SSM_STATE) == 0
    ncp = max(nc, LANES)
    gt = GROUPS_PER_TILE
    tile3 = lambda a, b: pl.BlockSpec((gt, a, b), lambda i: (i, 0, 0))
    lane_tile = pl.BlockSpec((chunk, nc, LANES), lambda i: (0, 0, i))
    end_rows = pl.BlockSpec((gt, 2 * SSM_STATE, n), lambda i: (i, n_full // (2 * SSM_STATE), n_full // n - 1))
    in_specs = [lane_tile, tile3(n, n), end_rows, tile3(n, 2 * SSM_STATE), tile3(SSM_STATE, LANES),
                tile3(SSM_STATE, LANES), pl.BlockSpec((1, LANES), lambda i: (0, i))]
    args = [proj, wt, wt, qt, kr, ki, d_skip.reshape(1, -1)]
    if not carry:
        in_specs.append(pl.BlockSpec((nseq, gt, LANES), lambda i: (0, i, 0)))
        args.append(h0)
    return pl.pallas_call(
        functools.partial(_s5_scan_kernel, chunk=chunk, carry=carry),
        grid=(SSM_GROUPS // gt,),
        in_specs=in_specs,
        out_specs=(lane_tile, tile3(8, LANES)),
        out_shape=(jax.ShapeDtypeStruct((chunk, nc, SSM_GROUPS * SSM_GROUP), F32),
                   jax.ShapeDtypeStruct((SSM_GROUPS, 8, LANES), F32)),
        scratch_shapes=[pltpu.VMEM((chunk, LANES, ncp), BF16), pltpu.VMEM((chunk, LANES, ncp), F32)],
        compiler_params=_cparams("parallel"),
        name="s5_scan",
    )(*args)


def _gated_out(y, z, wg_ref, bg_ref, wo_ref):
    g = jax.nn.gelu(y, approximate=True)
    gate = jnp.dot(g.astype(BF16), wg_ref[...], preferred_element_type=F32) + bg_ref[...]
    yy = (g * jax.nn.sigmoid(gate)) * _silu(z)
    return jnp.dot(yy.astype(BF16), wo_ref[...], preferred_element_type=F32)


def _glu_out_kernel(y_ref, z_ref, x_ref, wg_ref, bg_ref, wo_ref, gf_ref, o_ref):
    h = x_ref[...] + _gated_out(y_ref[...], z_ref[...], wg_ref, bg_ref, wo_ref)
    o_ref[...] = _rms_norm(h, gf_ref[...])


def _glu_out(y, proj, x, w_glu, b_glu, w_out, g_final, *, tm):
    m, w = x.shape
    row = lambda c: pl.BlockSpec((tm, w), lambda i: (i, c))
    vec = pl.BlockSpec((1, w), lambda i: (0, 0))
    mat = pl.BlockSpec((w, w), lambda i: (0, 0), pipeline_mode=pl.Buffered(1))
    return pl.pallas_call(
        _glu_out_kernel,
        grid=(m // tm,),
        in_specs=[row(0), row(1), row(0), mat, vec, mat, vec],
        out_specs=row(0),
        out_shape=jax.ShapeDtypeStruct((m, w), F32),
        compiler_params=_cparams("parallel"),
        name="glu_out",
    )(y, proj, x, w_glu, b_glu.reshape(1, w), w_out, g_final.reshape(1, w))


def _glu_out_pm_kernel(y_ref, z_ref, x_ref, wg_ref, bg_ref, wo_ref, gf_ref, o_ref, slab_ref):
    p, ct, w = y_ref.shape
    r = _gated_out(y_ref[...].reshape(p * ct, w), z_ref[...].reshape(p * ct, w), wg_ref, bg_ref, wo_ref)
    for s in range(p):
        for j in range(w // LANES):
            slab_ref[j, pl.ds(s, ct, stride=p), :] = r[s * ct:(s + 1) * ct, j * LANES:(j + 1) * LANES]
    r_tok = jnp.concatenate([slab_ref[j].reshape(ct, p, LANES) for j in range(w // LANES)], axis=2)
    o_ref[...] = _rms_norm(x_ref[...] + r_tok, gf_ref[...])


def _glu_out_pm(y, proj, x, w_glu, b_glu, w_out, g_final, *, ct):
    chunk, nc, w = y.shape
    m = chunk * nc
    sg = chunk // POS_TILE
    pm = lambda c: pl.BlockSpec((POS_TILE, ct, w), lambda i: (i % sg, i // sg, c))
    tok = pl.BlockSpec((ct, POS_TILE, w), lambda i: (i // sg, i % sg, 0))
    vec = pl.BlockSpec((1, w), lambda i: (0, 0))
    mat = pl.BlockSpec((w, w), lambda i: (0, 0), pipeline_mode=pl.Buffered(1))
    return pl.pallas_call(
        _glu_out_pm_kernel,
        grid=((nc // ct) * sg,),
        in_specs=[pm(0), pm(1), tok, mat, vec, mat, vec],
        out_specs=tok,
        out_shape=jax.ShapeDtypeStruct((nc, chunk, w), F32),
        scratch_shapes=[pltpu.VMEM((w // LANES, ct * POS_TILE, LANES), F32)],
        compiler_params=_cparams("parallel"),
        name="glu_out_pm",
    )(y, proj, x.reshape(nc, chunk, w), w_glu, b_glu.reshape(1, w), w_out, g_final.reshape(1, w)).reshape(m, w)


PAST_LEN = 16384


def kernel(x_prompt, x_sample, cache_win_k, cache_win_v, state_conv, state_ssm_re, state_ssm_im, attn_norm, w_in_ab, conv_w, w_out_ab, ssm_norm, w_in_c, lam_re, lam_im, log_step, b_re, b_im, c_re, c_im, d_skip, w_glu, b_glu, w_out_c, final_norm):
    bp, tp, _ = x_prompt.shape
    bs, ts, _ = x_sample.shape
    n_keep = min(2048, tp)
    xp = x_prompt.reshape(bp * tp, D_MODEL)
    xs = x_sample.reshape(bs * ts, D_MODEL)

    w_in0 = w_in_ab[0].astype(BF16)
    w_out0 = w_out_ab[0].astype(BF16)
    proj_p = _norm_matmul(xp, attn_norm[0], w_in0, _rope_tables(tp, tp, 0), tm=1024, tn=1024, rope_tiles=2,
                          tiles_per_seq=tp // 1024)
    proj_s = _norm_matmul(xs, attn_norm[0], w_in0, _rope_tables(bs * ts, ts, PAST_LEN), tm=bs * ts, tn=1024,
                          rope_tiles=2)
    o_p, k_p, v_p = _attn_prompt(proj_p, bp, tp, n_keep)
    o_p = o_p.reshape(bp * tp, ATTN_WIDTH)
    k_p = k_p.reshape(1, bp, n_keep, N_HEADS, HEAD_DIM)
    v_p = v_p.reshape(1, bp, n_keep, N_HEADS, HEAD_DIM)
    o_s = _attn_sample(proj_s, cache_win_k[0], cache_win_v[0], bs, ts)
    h1_p, conv_p = _mix_out(proj_p, o_p, xp, jnp.zeros((bp, 2, CONV_WIDTH), F32), conv_w[0], w_out0,
                            tm=256, tiles_per_seq=tp // 256)
    h1_s, conv_s = _mix_out(proj_s, o_s, xs, state_conv[0], conv_w[0], w_out0, tm=ts, tiles_per_seq=1)
    k_s = proj_s[:, ATTN_WIDTH:2 * ATTN_WIDTH].reshape(1, bs, ts, N_HEADS, HEAD_DIM)
    v_s = proj_s[:, 2 * ATTN_WIDTH:3 * ATTN_WIDTH].reshape(1, bs, ts, N_HEADS, HEAD_DIM)

    w_in1 = w_in_c[0].astype(BF16)
    w_glu1 = w_glu[0].astype(BF16)
    w_out1 = w_out_c[0].astype(BF16)
    tables = _s5_tables(lam_re[0], lam_im[0], log_step[0])
    bc = (b_re[0], b_im[0], c_re[0], c_im[0])
    proj1_p = _norm_matmul_pm(h1_p, ssm_norm[0], w_in1, chunk=PROMPT_CHUNK, ct=64, tn=1024)
    ops = _s5_operators(tables, *bc, PROMPT_CHUNK)
    y_p, hf_p = _s5_scan(proj1_p, ops, d_skip[0], None, bp, carry=True)
    out_p = _glu_out_pm(y_p, proj1_p, h1_p, w_glu1, b_glu[0], w_out1, final_norm, ct=32)
    proj1_s = _norm_matmul(h1_s, ssm_norm[0], w_in1, tm=bs * ts, tn=1024)
    h0 = jnp.concatenate([state_ssm_re[0], state_ssm_im[0]], axis=-1)
    y_s, hf_s = _s5_scan(proj1_s.reshape(bs, ts, -1).transpose(1, 0, 2), ops, d_skip[0], h0, bs, carry=False)
    y_s = y_s.transpose(1, 0, 2).reshape(bs * ts, D_MODEL)
    out_s = _glu_out(y_s, proj1_s, h1_s, w_glu1, b_glu[0], w_out1, final_norm, tm=bs * ts)
    hf_p = hf_p[:, :bp].transpose(1, 0, 2)[None]
    hf_s = hf_s[:, :bs].transpose(1, 0, 2)[None]
    return (out_p.reshape(bp, tp, D_MODEL), out_s.reshape(bs, ts, D_MODEL),
            k_p, v_p, conv_p[None], hf_p[..., :SSM_STATE], hf_p[..., SSM_STATE:],
            k_s, v_s, conv_s[None], hf_s[..., :SSM_STATE], hf_s[..., SSM_STATE:])
```

```python
import functools
import math

import jax
import jax.numpy as jnp
from jax import lax
from jax.experimental import pallas as pl
from jax.experimental.pallas import tpu as pltpu

D_MODEL = 2048
HEAD_DIM = 128
N_HEADS = 8
ATTN_WIDTH = 1024
CONV_WIDTH = 1024
DILATIONS = (1, 4, 16)
N_BACK = 128
ROPE_THETA = 10000.0
RMS_EPS = 1e-6
SSM_GROUP = 16
SSM_GROUPS = 128
SSM_STATE = 64
LANES = 128
VMEM_LIMIT = 56 * 1024 * 1024

F32 = jnp.float32
BF16 = jnp.bfloat16


def _cparams(*sem):
    return pltpu.CompilerParams(dimension_semantics=sem, vmem_limit_bytes=VMEM_LIMIT)


def _rope_table_kernel(inv_ref, cos_ref, sin_ref, *, period, offset):
    rows = cos_ref.shape[0]
    r = lax.broadcasted_iota(jnp.int32, (rows, LANES), 0) + pl.program_id(0) * rows
    pos = (offset + lax.rem(r, period)).astype(F32)
    ang = pos * inv_ref[...]
    lane = lax.broadcasted_iota(jnp.int32, (rows, LANES), 1)
    cos_ref[...] = jnp.cos(ang)
    sin_ref[...] = jnp.where(lane < HEAD_DIM // 2, -1.0, 1.0) * jnp.sin(ang)


def _rope_tables(rows, period, offset):
    half = HEAD_DIM // 2
    inv = ROPE_THETA ** (-jnp.arange(half, dtype=F32) / half)
    inv2 = jnp.concatenate([inv, inv])[None, :]
    tr = min(rows, 256)
    return pl.pallas_call(
        functools.partial(_rope_table_kernel, period=period, offset=offset),
        grid=(rows // tr,),
        in_specs=[pl.BlockSpec((1, LANES), lambda i: (0, 0))],
        out_specs=(pl.BlockSpec((tr, LANES), lambda i: (i, 0)),) * 2,
        out_shape=(jax.ShapeDtypeStruct((rows, LANES), F32),) * 2,
        compiler_params=_cparams("parallel"),
        name="rope_table",
    )(inv2)


def _rms_norm(x, g):
    return x * lax.rsqrt(jnp.mean(x * x, axis=-1, keepdims=True) + RMS_EPS) * g


def _norm_matmul_kernel(x_ref, g_ref, w_ref, *rest, rope_tiles):
    o_ref, hn_ref = rest[-2:]
    n = pl.program_id(1)

    @pl.when(n == 0)
    def _():
        hn_ref[...] = _rms_norm(x_ref[...], g_ref[...]).astype(BF16)

    acc = jnp.dot(hn_ref[...], w_ref[...], preferred_element_type=F32)

    @pl.when(n >= rope_tiles)
    def _():
        o_ref[...] = acc

    if rope_tiles:
        cos_ref, sin_ref = rest[:2]

        @pl.when(n < rope_tiles)
        def _():
            cos = cos_ref[...]
            sin = sin_ref[...]
            for h in range(o_ref.shape[1] // HEAD_DIM):
                xh = acc[:, h * HEAD_DIM:(h + 1) * HEAD_DIM]
                o_ref[:, h * HEAD_DIM:(h + 1) * HEAD_DIM] = xh * cos + pltpu.roll(xh, HEAD_DIM // 2, 1) * sin


def _norm_matmul(x, g, w, rope=None, *, tm, tn, rope_tiles=0, tiles_per_seq=1):
    m, k = x.shape
    n = w.shape[1]
    table = pl.BlockSpec((tm, LANES), lambda i, j: (i % tiles_per_seq, 0))
    return pl.pallas_call(
        functools.partial(_norm_matmul_kernel, rope_tiles=rope_tiles),
        grid=(m // tm, n // tn),
        in_specs=[pl.BlockSpec((tm, k), lambda i, j: (i, 0)), pl.BlockSpec((1, k), lambda i, j: (0, 0)),
                  pl.BlockSpec((k, tn), lambda i, j: (0, j))] + ([table, table] if rope_tiles else []),
        out_specs=pl.BlockSpec((tm, tn), lambda i, j: (i, j)),
        out_shape=jax.ShapeDtypeStruct((m, n), F32),
        scratch_shapes=[pltpu.VMEM((tm, k), BF16)],
        compiler_params=_cparams("parallel", "arbitrary"),
        name="norm_matmul",
    )(x, g.reshape(1, k), w, *(rope if rope_tiles else ()))


POS_TILE = 8


def _norm_matmul_pm_kernel(x_ref, g_ref, w_ref, o_ref, hn_ref, slab_ref):
    ct, _, k = x_ref.shape

    @pl.when(pl.program_id(1) == 0)
    def _():
        hn = _rms_norm(x_ref[...], g_ref[...])
        for j in range(k // LANES):
            slab_ref[j] = hn[:, :, j * LANES:(j + 1) * LANES].reshape(ct * POS_TILE, LANES)
        for s in range(POS_TILE):
            for j in range(k // LANES):
                hn_ref[s * ct:(s + 1) * ct, j * LANES:(j + 1) * LANES] = (
                    slab_ref[j, pl.ds(s, ct, stride=POS_TILE), :].astype(BF16))

    acc = jnp.dot(hn_ref[...], w_ref[...], preferred_element_type=F32)
    o_ref[...] = acc.reshape(o_ref.shape)


def _norm_matmul_pm(x, g, w, *, chunk, ct, tn):
    m, k = x.shape
    n = w.shape[1]
    nc = m // chunk
    sg = chunk // POS_TILE
    return pl.pallas_call(
        _norm_matmul_pm_kernel,
        grid=((nc // ct) * sg, n // tn),
        in_specs=[pl.BlockSpec((ct, POS_TILE, k), lambda i, j: (i // sg, i % sg, 0)),
                  pl.BlockSpec((1, k), lambda i, j: (0, 0)), pl.BlockSpec((k, tn), lambda i, j: (0, j))],
        out_specs=pl.BlockSpec((POS_TILE, ct, tn), lambda i, j: (i % sg, i // sg, j)),
        out_shape=jax.ShapeDtypeStruct((chunk, nc, n), F32),
        scratch_shapes=[pltpu.VMEM((ct * POS_TILE, k), BF16), pltpu.VMEM((k // LANES, ct * POS_TILE, LANES), F32)],
        compiler_params=_cparams("parallel", "arbitrary"),
        name="norm_matmul_pm",
    )(x.reshape(nc, chunk, k), g.reshape(1, k), w)


def _attn_prompt_kernel(q_ref, k_ref, v_ref, o_ref, ks_ref, vs_ref, qc_ref, kc_ref, vc_ref, oc_ref, lc_ref,
                        to_ref, tl_ref, lse_ref):
    t = q_ref.shape[1]
    n_keep = ks_ref.shape[1]
    ks_ref[0] = k_ref[0, t - n_keep:, :]
    vs_ref[0] = v_ref[0, t - n_keep:, :]
    nblk = t // N_BACK
    scale = HEAD_DIM ** -0.5
    kc_ref[0:N_BACK, :] = jnp.zeros((N_BACK, HEAD_DIM), BF16)
    vc_ref[0:N_BACK, :] = jnp.zeros((N_BACK, HEAD_DIM), BF16)
    qi = lax.broadcasted_iota(jnp.int32, (N_BACK, 2 * N_BACK), 0)
    kj = lax.broadcasted_iota(jnp.int32, (N_BACK, 2 * N_BACK), 1)
    dist = N_BACK + qi - kj
    band = (dist >= 0) & (dist <= N_BACK)

    for d in DILATIONS:
        ln = t // d
        blocks_per_class = ln // N_BACK
        for r in range(d):
            rows = pl.ds(r, ln, stride=d) if d > 1 else pl.ds(0, ln)
            qc_ref[r * ln:(r + 1) * ln, :] = q_ref[0, rows, :].astype(BF16)
            kc_ref[N_BACK + r * ln:N_BACK + (r + 1) * ln, :] = k_ref[0, rows, :].astype(BF16)
            vc_ref[N_BACK + r * ln:N_BACK + (r + 1) * ln, :] = v_ref[0, rows, :].astype(BF16)

        def block(b, carry):
            row0 = pl.multiple_of(b * N_BACK, N_BACK)
            j = lax.rem(b, blocks_per_class)
            q = qc_ref[pl.ds(row0, N_BACK), :]
            k2 = kc_ref[pl.ds(row0, 2 * N_BACK), :]
            v2 = vc_ref[pl.ds(row0, 2 * N_BACK), :]
            s = lax.dot_general(q, k2, (((1,), (1,)), ((), ())), preferred_element_type=F32) * scale
            valid = band & ((kj >= N_BACK) | (j > 0))
            s = jnp.where(valid, s, -jnp.inf)
            m = jnp.max(s, axis=-1, keepdims=True)
            p = jnp.exp(s - m)
            l = jnp.sum(p, axis=-1, keepdims=True)
            o = jnp.dot(p.astype(BF16), v2, preferred_element_type=F32)
            oc_ref[pl.ds(row0, N_BACK), :] = o / l
            lc_ref[pl.ds(row0, N_BACK), :] = jnp.broadcast_to(m + jnp.log(l), (N_BACK, HEAD_DIM))
            return carry

        lax.fori_loop(0, nblk, block, 0, unroll=16)

        if d == 1:
            o_ref[0] = oc_ref[...]
            lse_ref[...] = lc_ref[...]
        else:
            for r in range(d):
                to_ref[pl.ds(r, ln, stride=d), :] = oc_ref[r * ln:(r + 1) * ln, :]
                tl_ref[pl.ds(r, ln, stride=d), :] = lc_ref[r * ln:(r + 1) * ln, :]

            def merge(c, carry):
                rows = pl.ds(pl.multiple_of(c * 64, 64), 64)
                la = lse_ref[rows, :]
                lb = tl_ref[rows, :]
                mx = jnp.maximum(la, lb)
                ea = jnp.exp(la - mx)
                eb = jnp.exp(lb - mx)
                tot = ea + eb
                o_ref[0, rows, :] = (o_ref[0, rows, :] * ea + to_ref[rows, :] * eb) / tot
                lse_ref[rows, :] = mx + jnp.log(tot)
                return carry

            lax.fori_loop(0, t // 64, merge, 0, unroll=2)


def _attn_prompt(proj, b, t, n_keep):
    p3 = proj.reshape(b, t, proj.shape[1])
    blk = lambda off: pl.BlockSpec((1, t, HEAD_DIM), lambda i, h: (i, 0, off + h))
    keep = pl.BlockSpec((1, n_keep, HEAD_DIM), lambda i, h: (i, 0, h))
    return pl.pallas_call(
        _attn_prompt_kernel,
        grid=(b, N_HEADS),
        in_specs=[blk(0), blk(N_HEADS), blk(2 * N_HEADS)],
        out_specs=(pl.BlockSpec((1, t, HEAD_DIM), lambda i, h: (i, 0, h)), keep, keep),
        out_shape=(jax.ShapeDtypeStruct((b, t, ATTN_WIDTH), F32),
                   jax.ShapeDtypeStruct((b, n_keep, ATTN_WIDTH), F32),
                   jax.ShapeDtypeStruct((b, n_keep, ATTN_WIDTH), F32)),
        scratch_shapes=[
            pltpu.VMEM((t, HEAD_DIM), BF16),
            pltpu.VMEM((t + N_BACK, HEAD_DIM), BF16),
            pltpu.VMEM((t + N_BACK, HEAD_DIM), BF16),
            pltpu.VMEM((t, HEAD_DIM), F32),
            pltpu.VMEM((t, HEAD_DIM), F32),
            pltpu.VMEM((t, HEAD_DIM), F32),
            pltpu.VMEM((t, HEAD_DIM), F32),
            pltpu.VMEM((t, HEAD_DIM), F32),
        ],
        compiler_params=_cparams("parallel", "parallel"),
        name="attn_prompt",
    )(p3, p3, p3)


def _attn_sample_kernel(q_ref, kn_ref, vn_ref, kc_ref, vc_ref, o_ref):
    s_len = q_ref.shape[0]
    n_buf = kc_ref.shape[1] // N_HEADS
    scale = HEAD_DIM ** -0.5
    nt = (((1,), (1,)), ((), ()))

    def count(dist):
        c = jnp.zeros(dist.shape, F32)
        for d in DILATIONS:
            hit = (dist >= 0) & (dist <= N_BACK * d) & ((dist & (d - 1)) == 0)
            c = c + jnp.where(hit, 1.0, 0.0)
        return c

    iota = lambda shape, dim: lax.broadcasted_iota(jnp.int32, shape, dim)
    cc = count(n_buf + iota((s_len, n_buf), 0) - iota((s_len, n_buf), 1))
    cn = count(iota((s_len, s_len), 0) - iota((s_len, s_len), 1))
    for h in range(N_HEADS):
        cols = slice(h * HEAD_DIM, (h + 1) * HEAD_DIM)
        head_rows = pl.ds(h, n_buf, stride=N_HEADS)
        q = q_ref[:, cols].astype(BF16)
        sc = lax.dot_general(q, kc_ref[0, head_rows, :].astype(BF16), nt, preferred_element_type=F32) * scale
        sn = lax.dot_general(q, kn_ref[:, cols].astype(BF16), nt, preferred_element_type=F32) * scale
        sc = jnp.where(cc > 0, sc, -jnp.inf)
        sn = jnp.where(cn > 0, sn, -jnp.inf)
        m = jnp.maximum(jnp.max(sc, axis=-1, keepdims=True), jnp.max(sn, axis=-1, keepdims=True))
        pc = cc * jnp.exp(sc - m)
        pn = cn * jnp.exp(sn - m)
        l = jnp.sum(pc, axis=-1, keepdims=True) + jnp.sum(pn, axis=-1, keepdims=True)
        o = (jnp.dot(pc.astype(BF16), vc_ref[0, head_rows, :].astype(BF16), preferred_element_type=F32)
             + jnp.dot(pn.astype(BF16), vn_ref[:, cols].astype(BF16), preferred_element_type=F32))
        o_ref[:, cols] = o / l


def _attn_sample(proj, cache_k, cache_v, b, s_len):
    n_buf = cache_k.shape[1]
    ck = cache_k.reshape(b, n_buf * N_HEADS, HEAD_DIM)
    cv = cache_v.reshape(b, n_buf * N_HEADS, HEAD_DIM)
    new = lambda c: pl.BlockSpec((s_len, ATTN_WIDTH), lambda i: (i, c))
    old = pl.BlockSpec((1, n_buf * N_HEADS, HEAD_DIM), lambda i: (i, 0, 0))
    return pl.pallas_call(
        _attn_sample_kernel,
        grid=(b,),
        in_specs=[new(0), new(1), new(2), old, old],
        out_specs=new(0),
        out_shape=jax.ShapeDtypeStruct((b * s_len, ATTN_WIDTH), F32),
        compiler_params=_cparams("parallel"),
        name="attn_sample",
    )(proj, proj, proj, ck, cv)


def _silu(z):
    return z * jax.nn.sigmoid(z)


def _mix_out_kernel(x_ref, o_ref, g_ref, wg_ref, init_ref, cw_ref, w_ref, h_ref, cs_ref, tail_ref, *,
                    tiles_per_seq):
    tm = x_ref.shape[0]
    cw = CONV_WIDTH

    @pl.when(pl.program_id(0) % tiles_per_seq == 0)
    def _():
        tail_ref[...] = init_ref[0]

    x = x_ref[...]
    hn = _rms_norm(x, g_ref[...]).astype(BF16)
    gate = lambda c: jnp.dot(hn, wg_ref[:, c * cw:(c + 1) * cw], preferred_element_type=F32)
    ch = gate(2) * gate(3)
    prev1 = tail_ref[1:2, :]
    prev2 = tail_ref[0:1, :]
    row = lax.broadcasted_iota(jnp.int32, (tm, 1), 0)
    ch1 = jnp.where(row == 0, prev1, pltpu.roll(ch, 1, 0))
    ch2 = jnp.where(row == 0, prev2, jnp.where(row == 1, prev1, pltpu.roll(ch, 2, 0)))
    conv = ch2 * cw_ref[0:1, :] + ch1 * cw_ref[1:2, :] + ch * cw_ref[2:3, :]
    tail_ref[...] = ch[tm - 2:tm, :]
    cs_ref[0] = ch[tm - 2:tm, :]
    o_b = (gate(1) * conv * _silu(gate(4))).astype(BF16)
    y = jnp.dot(o_b, w_ref[ATTN_WIDTH:, :], preferred_element_type=F32)
    o_a = (o_ref[...] * _silu(gate(0))).astype(BF16)
    y = y + jnp.dot(o_a, w_ref[0:ATTN_WIDTH, :], preferred_element_type=F32)
    h_ref[...] = x + y


def _mix_out(x, o_attn, g, w_gates, conv_init, conv_w, w_out, *, tm, tiles_per_seq):
    m = x.shape[0]
    nseq = conv_init.shape[0]
    seq = pl.BlockSpec((1, 2, CONV_WIDTH), lambda i: (i // tiles_per_seq, 0, 0))
    whole = lambda a: pl.BlockSpec(a.shape, lambda i: (0,) * a.ndim, pipeline_mode=pl.Buffered(1))
    return pl.pallas_call(
        functools.partial(_mix_out_kernel, tiles_per_seq=tiles_per_seq),
        grid=(m // tm,),
        in_specs=[
            pl.BlockSpec((tm, D_MODEL), lambda i: (i, 0)),
            pl.BlockSpec((tm, ATTN_WIDTH), lambda i: (i, 0)),
            pl.BlockSpec((1, D_MODEL), lambda i: (0, 0)),
            whole(w_gates), seq, pl.BlockSpec((3, CONV_WIDTH), lambda i: (0, 0)), whole(w_out),
        ],
        out_specs=(pl.BlockSpec((tm, D_MODEL), lambda i: (i, 0)), seq),
        out_shape=(jax.ShapeDtypeStruct((m, D_MODEL), F32), jax.ShapeDtypeStruct((nseq, 2, CONV_WIDTH), F32)),
        scratch_shapes=[pltpu.VMEM((2, CONV_WIDTH), F32)],
        compiler_params=_cparams("arbitrary"),
        name="mix_out",
    )(x, o_attn, g.reshape(1, -1), w_gates, conv_init, conv_w, w_out)


N_SCAN = 7
GROUPS_PER_TILE = LANES // SSM_GROUP


PROMPT_CHUNK = 32
ROW_C = PROMPT_CHUNK + 1
ROW_DBL = PROMPT_CHUNK + 2
TABLE_ROWS = ROW_DBL + N_SCAN


def _s5_disc_kernel(lr_ref, li_ref, ls_ref, tr_ref, ti_ref):
    lr = lr_ref[...]
    li = li_ref[...]
    step = jnp.exp(ls_ref[...])
    mag = jnp.exp(lr * step)
    ar = mag * jnp.cos(li * step)
    ai = mag * jnp.sin(li * step)
    den = lr * lr + li * li
    nr = ar - 1.0
    tr_ref[ROW_C] = (nr * lr + ai * li) / den
    ti_ref[ROW_C] = (ai * lr - nr * li) / den
    pr = jnp.ones_like(ar)
    pi = jnp.zeros_like(ar)
    for tau in range(PROMPT_CHUNK + 1):
        tr_ref[tau] = pr
        ti_ref[tau] = pi
        dr, di = pr, pi
        pr, pi = pr * ar - pi * ai, pr * ai + pi * ar
    for i in range(N_SCAN):
        tr_ref[ROW_DBL + i] = dr
        ti_ref[ROW_DBL + i] = di
        dr, di = dr * dr - di * di, 2.0 * dr * di


def _s5_tables(lam_re, lam_im, log_step):
    g = lam_re.shape[0]
    dup = lambda a: jnp.concatenate([a, a], axis=-1)
    tr, ti = pl.pallas_call(
        _s5_disc_kernel,
        out_shape=(jax.ShapeDtypeStruct((TABLE_ROWS, g, LANES), F32),) * 2,
        name="s5_disc",
    )(dup(lam_re), dup(lam_im), log_step[:, None])
    return tr.transpose(1, 0, 2), ti.transpose(1, 0, 2)


def _split_bf16(x):
    hi = x.astype(BF16)
    return hi, (x - hi.astype(F32)).astype(BF16)


def _dot_split(a, b, b_is_bf16_exact=False):
    dot = lambda x, y: jnp.dot(x, y, preferred_element_type=F32)
    ah, al = _split_bf16(a)
    if b_is_bf16_exact:
        bh = b.astype(BF16)
        return dot(ah, bh) + dot(al, bh)
    bh, bl = _split_bf16(b)
    return dot(ah, bh) + (dot(ah, bl) + dot(al, bh))


def _pad_rows(x, rows):
    if x.shape[0] == rows:
        return x
    return jnp.concatenate([x, jnp.zeros((rows - x.shape[0], x.shape[1]), x.dtype)], axis=0)


def _s5_prep_kernel(tr_ref, ti_ref, br_ref, bi_ref, cr_ref, ci_ref, c2r_ref, c2i_ref,
                    wt_ref, qt_ref, kr_ref, ki_ref, *, chunk):
    n = SSM_GROUP * chunk
    iota = lambda shape, dim: lax.broadcasted_iota(jnp.int32, shape, dim)
    lane = iota((SSM_GROUP, 2 * SSM_STATE), 1)
    e_sel = jnp.where(iota((chunk, n), 0) == chunk - 1 - iota((chunk, n), 1) // SSM_GROUP, 1.0, 0.0)
    tile = jnp.where(iota((SSM_GROUP, n), 0) == iota((SSM_GROUP, n), 1) % SSM_GROUP, 1.0, 0.0)

    def one_group(g, carry):
        tab_r, tab_i = tr_ref[g], ti_ref[g]
        col_r = _pad_rows(tab_r, LANES).T[:SSM_STATE]
        col_i = _pad_rows(tab_i, LANES).T[:SSM_STATE]
        kr_ref[g] = col_r
        ki_ref[g] = col_i
        ccr, cci = col_r[:, ROW_C:ROW_C + 1], col_i[:, ROW_C:ROW_C + 1]
        br, bi = br_ref[g], bi_ref[g]
        bbr = ccr * br - cci * bi
        bbi = ccr * bi + cci * br
        aer = _dot_split(col_r[:, :chunk], e_sel, True)
        aei = _dot_split(col_i[:, :chunk], e_sel, True)
        btr = _dot_split(bbr, tile, True)
        bti = _dot_split(bbi, tile, True)
        ptr = aer * btr - aei * bti
        pti = aer * bti + aei * btr
        wt_ref[g, n:n + SSM_STATE, :] = ptr.astype(BF16)
        wt_ref[g, n + SSM_STATE:n + 2 * SSM_STATE, :] = pti.astype(BF16)
        taps = _dot_split(cr_ref[g], ptr) - _dot_split(ci_ref[g], pti)
        padded = jnp.concatenate([taps, jnp.zeros_like(taps)], axis=1)
        for t in range(chunk):
            sh = SSM_GROUP * (chunk - 1 - t)
            blk = padded if sh == 0 else pltpu.roll(padded, 2 * n - sh, 1)
            wt_ref[g, SSM_GROUP * t:SSM_GROUP * (t + 1), :] = blk[:, :n].astype(BF16)
        c2r, c2i = c2r_ref[g], c2i_ref[g]
        for t in range(chunk):
            ar = tab_r[t + 1:t + 2, :]
            ai = tab_i[t + 1:t + 2, :]
            x1 = jnp.where(lane < SSM_STATE, ar, -ai)
            x2 = jnp.where(lane < SSM_STATE, ai, ar)
            qt_ref[g, SSM_GROUP * t:SSM_GROUP * (t + 1), :] = (c2r * x1 - c2i * x2).astype(BF16)
        return carry

    lax.fori_loop(0, wt_ref.shape[0], one_group, 0, unroll=4)


def _s5_operators(tables, b_re, b_im, c_re, c_im, chunk):
    g = b_re.shape[0]
    n = SSM_GROUP * chunk
    assert chunk <= PROMPT_CHUNK
    dup = lambda a: jnp.concatenate([a, a], axis=-1)
    per_g = lambda *s: pl.BlockSpec((GROUPS_PER_TILE,) + s, lambda i: (i,) + (0,) * len(s))
    return pl.pallas_call(
        functools.partial(_s5_prep_kernel, chunk=chunk),
        grid=(g // GROUPS_PER_TILE,),
        in_specs=[per_g(TABLE_ROWS, LANES), per_g(TABLE_ROWS, LANES),
                  per_g(SSM_STATE, SSM_GROUP), per_g(SSM_STATE, SSM_GROUP),
                  per_g(SSM_GROUP, SSM_STATE), per_g(SSM_GROUP, SSM_STATE),
                  per_g(SSM_GROUP, LANES), per_g(SSM_GROUP, LANES)],
        out_specs=(per_g(n + 2 * SSM_STATE, n), per_g(n, 2 * SSM_STATE),
                   per_g(SSM_STATE, LANES), per_g(SSM_STATE, LANES)),
        out_shape=(jax.ShapeDtypeStruct((g, n + 2 * SSM_STATE, n), BF16),
                   jax.ShapeDtypeStruct((g, n, 2 * SSM_STATE), BF16),
                   jax.ShapeDtypeStruct((g, SSM_STATE, LANES), F32),
                   jax.ShapeDtypeStruct((g, SSM_STATE, LANES), F32)),
        compiler_params=_cparams("parallel"),
        name="s5_prep",
    )(*tables, b_re, b_im, c_re, c_im, dup(c_re), dup(c_im))


def _s5_scan_kernel(u_ref, wt_ref, *rest, chunk, carry, split_ops):
    ws_ref = rest[0] if split_ops else wt_ref
    qt_ref, kr_ref, ki_ref, dsk_ref = rest[split_ops:split_ops + 4]
    rest = rest[split_ops + 4:]
    _s5_scan_body(u_ref, wt_ref, ws_ref, qt_ref, kr_ref, ki_ref, dsk_ref, *rest, chunk=chunk, carry=carry)


def _s5_scan_body(u_ref, wt_ref, ws_ref, qt_ref, kr_ref, ki_ref, dsk_ref, *rest, chunk, carry):
    if carry:
        y_ref, hf_ref, dall_ref, yall_ref = rest
    else:
        h0_ref, y_ref, hf_ref, dall_ref, yall_ref = rest
    nc = u_ref.shape[1]
    ncp = dall_ref.shape[2]
    nseq = ncp // LANES if carry else nc
    n = SSM_GROUP * chunk
    iota = lambda shape, dim: lax.broadcasted_iota(jnp.int32, shape, dim)
    for s in range(chunk):
        dall_ref[s] = _pad_rows(u_ref[s], ncp).T.astype(BF16)
    lane = iota((SSM_STATE, ncp), 1) & (LANES - 1)
    slot = iota((LANES, LANES), 1)
    if not carry:
        h0_all = _pad_rows(jnp.concatenate([h0_ref[:, gp, :] for gp in range(GROUPS_PER_TILE)], axis=0), LANES).T
    acc = jnp.zeros((LANES, LANES), F32)
    for gp in range(GROUPS_PER_TILE):
        rows = slice(SSM_GROUP * gp, SSM_GROUP * (gp + 1))
        d = dall_ref[:, rows, :].reshape(n, ncp)
        ys = jnp.dot(wt_ref[gp], d, preferred_element_type=F32)
        if ws_ref is wt_ref:
            st = ys[n:]
            ys = ys[:n]
        else:
            st = jnp.dot(ws_ref[gp], d, preferred_element_type=F32)
        sr = st[:SSM_STATE]
        si = st[SSM_STATE:]
        if carry:
            for i in range(N_SCAN):
                sh = 1 << i
                ar = kr_ref[gp, :, ROW_DBL + i:ROW_DBL + i + 1]
                ai = ki_ref[gp, :, ROW_DBL + i:ROW_DBL + i + 1]
                pr = jnp.where(lane >= sh, pltpu.roll(sr, sh, 1), 0.0)
                pi = jnp.where(lane >= sh, pltpu.roll(si, sh, 1), 0.0)
                sr, si = sr + ar * pr - ai * pi, si + ar * pi + ai * pr
            hp = jnp.concatenate([jnp.where(lane >= 1, pltpu.roll(sr, 1, 1), 0.0),
                                  jnp.where(lane >= 1, pltpu.roll(si, 1, 1), 0.0)], axis=0)
            for b in range(nseq):
                seq = slice(LANES * b, LANES * (b + 1))
                end = jnp.concatenate([sr[:, seq], si[:, seq]], axis=0)
                acc = jnp.where(slot == 8 * gp + b, pltpu.roll(end, (8 * gp + b + 1) % LANES, 1), acc)
        else:
            hp = h0_all if gp == 0 else pltpu.roll(h0_all, LANES - 8 * gp, 1)
            hp = jnp.where(slot < nseq, hp, 0.0)
            hpr, hpi = hp[:SSM_STATE], hp[SSM_STATE:]
            ar = kr_ref[gp, :, chunk:chunk + 1]
            ai = ki_ref[gp, :, chunk:chunk + 1]
            end = jnp.concatenate([sr + ar * hpr - ai * hpi, si + ar * hpi + ai * hpr], axis=0)
            if gp:
                end = pltpu.roll(end, 8 * gp, 1)
            acc = jnp.where((slot >= 8 * gp) & (slot < 8 * gp + nseq), end, acc)
        y = ys + jnp.dot(qt_ref[gp], hp.astype(BF16), preferred_element_type=F32)
        yall_ref[:, rows, :] = y.reshape(chunk, SSM_GROUP, ncp)
    hf_ref[...] = acc.T[:8 * GROUPS_PER_TILE].reshape(GROUPS_PER_TILE, 8, LANES)
    for t in range(chunk):
        y_ref[t] = yall_ref[t].T[:nc] + dsk_ref[...] * u_ref[t]


def _s5_scan(proj, ops, d_skip, h0, nseq, *, carry):
    wt, qt, kr, ki = ops
    chunk, nc, _ = proj.shape
    n = SSM_GROUP * chunk
    n_full = SSM_GROUP * PROMPT_CHUNK
    assert nc == (nseq * LANES if carry else nseq) and (carry or nseq == 8) and nseq <= 8
    assert n_full % n == 0 and n_full % (2 * SSM_STATE) == 0
    ncp = max(nc, LANES)
    gt = GROUPS_PER_TILE
    tile3 = lambda a, b: pl.BlockSpec((gt, a, b), lambda i: (i, 0, 0))
    lane_tile = pl.BlockSpec((chunk, nc, LANES), lambda i: (0, 0, i))
    split_ops = n != n_full
    if split_ops:
        end_rows = pl.BlockSpec((gt, 2 * SSM_STATE, n), lambda i: (i, n_full // (2 * SSM_STATE), n_full // n - 1))
        op_specs, op_args = [tile3(n, n), end_rows], [wt, wt]
    else:
        op_specs, op_args = [tile3(n + 2 * SSM_STATE, n)], [wt]
    in_specs = [lane_tile] + op_specs + [tile3(n, 2 * SSM_STATE), tile3(SSM_STATE, LANES),
                                         tile3(SSM_STATE, LANES), pl.BlockSpec((1, LANES), lambda i: (0, i))]
    args = [proj] + op_args + [qt, kr, ki, d_skip.reshape(1, -1)]
    if not carry:
        in_specs.append(pl.BlockSpec((nseq, gt, LANES), lambda i: (0, i, 0)))
        args.append(h0)
    return pl.pallas_call(
        functools.partial(_s5_scan_kernel, chunk=chunk, carry=carry, split_ops=split_ops),
        grid=(SSM_GROUPS // gt,),
        in_specs=in_specs,
        out_specs=(lane_tile, tile3(8, LANES)),
        out_shape=(jax.ShapeDtypeStruct((chunk, nc, SSM_GROUPS * SSM_GROUP), F32),
                   jax.ShapeDtypeStruct((SSM_GROUPS, 8, LANES), F32)),
        scratch_shapes=[pltpu.VMEM((chunk, LANES, ncp), BF16), pltpu.VMEM((chunk, LANES, ncp), F32)],
        compiler_params=_cparams("parallel"),
        name="s5_scan",
    )(*args)


def _ssm_out(y, x, gn_ref, wz_ref, wg_ref, bg_ref, wo_ref, gf_ref):
    z = jnp.dot(_rms_norm(x, gn_ref[...]).astype(BF16), wz_ref[...], preferred_element_type=F32)
    g = jax.nn.gelu(y, approximate=True)
    gate = jnp.dot(g.astype(BF16), wg_ref[...], preferred_element_type=F32) + bg_ref[...]
    yy = (g * jax.nn.sigmoid(gate)) * _silu(z)
    return _rms_norm(x + jnp.dot(yy.astype(BF16), wo_ref[...], preferred_element_type=F32), gf_ref[...])


def _glu_out_kernel(y_ref, x_ref, *rest):
    o_ref = rest[-1]
    o_ref[...] = _ssm_out(y_ref[...], x_ref[...], *rest[:-1])


def _glu_out_pm_kernel(y_ref, x_ref, *rest):
    o_ref, slab_ref = rest[-2:]
    p, ct, w = y_ref.shape
    for s in range(p):
        for j in range(w // LANES):
            slab_ref[j, pl.ds(s, ct, stride=p), :] = y_ref[s, :, j * LANES:(j + 1) * LANES]
    y = jnp.concatenate([slab_ref[j] for j in range(w // LANES)], axis=1)
    o_ref[...] = _ssm_out(y, x_ref[...].reshape(ct * p, w), *rest[:-2]).reshape(ct, p, w)


def _glu_out(y, x, g_norm, w_z, w_glu, b_glu, w_out, g_final, *, tm=None, ct=None):
    m, w = x.shape
    vec = pl.BlockSpec((1, w), lambda i: (0, 0))
    mat = pl.BlockSpec((w, w), lambda i: (0, 0), pipeline_mode=pl.Buffered(1))
    params = (g_norm.reshape(1, w), w_z, w_glu, b_glu.reshape(1, w), w_out, g_final.reshape(1, w))
    param_specs = [vec, mat, mat, vec, mat, vec]
    if ct is None:
        row = pl.BlockSpec((tm, w), lambda i: (i, 0))
        return pl.pallas_call(
            _glu_out_kernel,
            grid=(m // tm,),
            in_specs=[row, row] + param_specs,
            out_specs=row,
            out_shape=jax.ShapeDtypeStruct((m, w), F32),
            compiler_params=_cparams("parallel"),
            name="glu_out",
        )(y, x, *params)
    chunk, nc, _ = y.shape
    sg = chunk // POS_TILE
    tok = pl.BlockSpec((ct, POS_TILE, w), lambda i: (i // sg, i % sg, 0))
    return pl.pallas_call(
        _glu_out_pm_kernel,
        grid=((nc // ct) * sg,),
        in_specs=[pl.BlockSpec((POS_TILE, ct, w), lambda i: (i % sg, i // sg, 0)), tok] + param_specs,
        out_specs=tok,
        out_shape=jax.ShapeDtypeStruct((nc, chunk, w), F32),
        scratch_shapes=[pltpu.VMEM((w // LANES, ct * POS_TILE, LANES), F32)],
        compiler_params=_cparams("parallel"),
        name="glu_out_pm",
    )(y, x.reshape(nc, chunk, w), *params).reshape(m, w)


PAST_LEN = 16384


def kernel(x_prompt, x_sample, cache_win_k, cache_win_v, state_conv, state_ssm_re, state_ssm_im, attn_norm, w_in_ab, conv_w, w_out_ab, ssm_norm, w_in_c, lam_re, lam_im, log_step, b_re, b_im, c_re, c_im, d_skip, w_glu, b_glu, w_out_c, final_norm):
    bp, tp, _ = x_prompt.shape
    bs, ts, _ = x_sample.shape
    n_keep = min(2048, tp)
    xp = x_prompt.reshape(bp * tp, D_MODEL)
    xs = x_sample.reshape(bs * ts, D_MODEL)

    w_qkv = w_in_ab[0, :, :3 * ATTN_WIDTH].astype(BF16)
    w_gates = w_in_ab[0, :, 3 * ATTN_WIDTH:].astype(BF16)
    w_out0 = w_out_ab[0].astype(BF16)
    proj_p = _norm_matmul(xp, attn_norm[0], w_qkv, _rope_tables(tp, tp, 0), tm=1024, tn=1024, rope_tiles=2,
                          tiles_per_seq=tp // 1024)
    proj_s = _norm_matmul(xs, attn_norm[0], w_qkv, _rope_tables(bs * ts, ts, PAST_LEN), tm=bs * ts, tn=1024,
                          rope_tiles=2)
    o_p, k_p, v_p = _attn_prompt(proj_p, bp, tp, n_keep)
    o_p = o_p.reshape(bp * tp, ATTN_WIDTH)
    k_p = k_p.reshape(1, bp, n_keep, N_HEADS, HEAD_DIM)
    v_p = v_p.reshape(1, bp, n_keep, N_HEADS, HEAD_DIM)
    o_s = _attn_sample(proj_s, cache_win_k[0], cache_win_v[0], bs, ts)
    h1_p, conv_p = _mix_out(xp, o_p, attn_norm[0], w_gates, jnp.zeros((bp, 2, CONV_WIDTH), F32), conv_w[0],
                            w_out0, tm=256, tiles_per_seq=tp // 256)
    h1_s, conv_s = _mix_out(xs, o_s, attn_norm[0], w_gates, state_conv[0], conv_w[0], w_out0,
                            tm=ts, tiles_per_seq=1)
    k_s = proj_s[:, ATTN_WIDTH:2 * ATTN_WIDTH].reshape(1, bs, ts, N_HEADS, HEAD_DIM)
    v_s = proj_s[:, 2 * ATTN_WIDTH:3 * ATTN_WIDTH].reshape(1, bs, ts, N_HEADS, HEAD_DIM)

    w_u = w_in_c[0, :, :D_MODEL].astype(BF16)
    out_params = (ssm_norm[0], w_in_c[0, :, D_MODEL:].astype(BF16), w_glu[0].astype(BF16), b_glu[0],
                  w_out_c[0].astype(BF16), final_norm)
    tables = _s5_tables(lam_re[0], lam_im[0], log_step[0])
    ops = _s5_operators(tables, b_re[0], b_im[0], c_re[0], c_im[0], PROMPT_CHUNK)
    u_p = _norm_matmul_pm(h1_p, ssm_norm[0], w_u, chunk=PROMPT_CHUNK, ct=64, tn=1024)
    y_p, hf_p = _s5_scan(u_p, ops, d_skip[0], None, bp, carry=True)
    out_p = _glu_out(y_p, h1_p, *out_params, ct=32)
    u_s = _norm_matmul(h1_s, ssm_norm[0], w_u, tm=bs * ts, tn=1024)
    h0 = jnp.concatenate([state_ssm_re[0], state_ssm_im[0]], axis=-1)
    y_s, hf_s = _s5_scan(u_s.reshape(bs, ts, -1).transpose(1, 0, 2), ops, d_skip[0], h0, bs, carry=False)
    out_s = _glu_out(y_s.transpose(1, 0, 2).reshape(bs * ts, D_MODEL), h1_s, *out_params, tm=bs * ts)
    hf_p = hf_p[:, :bp].transpose(1, 0, 2)[None]
    hf_s = hf_s[:, :bs].transpose(1, 0, 2)[None]
    return (out_p.reshape(bp, tp, D_MODEL), out_s.reshape(bs, ts, D_MODEL),
            k_p, v_p, conv_p[None], hf_p[..., :SSM_STATE], hf_p[..., SSM_STATE:],
            k_s, v_s, conv_s[None], hf_s[..., :SSM_STATE], hf_s[..., SSM_STATE:])
```

```python
import functools
import math

import jax
import jax.numpy as jnp
from jax import lax
from jax.experimental import pallas as pl
from jax.experimental.pallas import tpu as pltpu

D_MODEL = 2048
HEAD_DIM = 128
N_HEADS = 8
ATTN_WIDTH = 1024
CONV_WIDTH = 1024
DILATIONS = (1, 4, 16)
N_BACK = 128
ROPE_THETA = 10000.0
RMS_EPS = 1e-6
SSM_GROUP = 16
SSM_GROUPS = 128
SSM_STATE = 64
LANES = 128
VMEM_LIMIT = 56 * 1024 * 1024

F32 = jnp.float32
BF16 = jnp.bfloat16


def _cparams(*sem):
    return pltpu.CompilerParams(dimension_semantics=sem, vmem_limit_bytes=VMEM_LIMIT)


def _rope_table_kernel(inv_ref, cos_ref, sin_ref, *, period, offset):
    rows = cos_ref.shape[0]
    r = lax.broadcasted_iota(jnp.int32, (rows, LANES), 0) + pl.program_id(0) * rows
    pos = (offset + lax.rem(r, period)).astype(F32)
    ang = pos * inv_ref[...]
    lane = lax.broadcasted_iota(jnp.int32, (rows, LANES), 1)
    cos_ref[...] = jnp.cos(ang)
    sin_ref[...] = jnp.where(lane < HEAD_DIM // 2, -1.0, 1.0) * jnp.sin(ang)


def _rope_tables(rows, period, offset):
    half = HEAD_DIM // 2
    inv = ROPE_THETA ** (-jnp.arange(half, dtype=F32) / half)
    inv2 = jnp.concatenate([inv, inv])[None, :]
    tr = min(rows, 256)
    return pl.pallas_call(
        functools.partial(_rope_table_kernel, period=period, offset=offset),
        grid=(rows // tr,),
        in_specs=[pl.BlockSpec((1, LANES), lambda i: (0, 0))],
        out_specs=(pl.BlockSpec((tr, LANES), lambda i: (i, 0)),) * 2,
        out_shape=(jax.ShapeDtypeStruct((rows, LANES), F32),) * 2,
        compiler_params=_cparams("parallel"),
        name="rope_table",
    )(inv2)


def _rms_norm(x, g):
    return x * lax.rsqrt(jnp.mean(x * x, axis=-1, keepdims=True) + RMS_EPS) * g


POS_TILE = 8
COL_SLAB = 256


def _cast_blocks(cast_in, cast_out):
    for src, dst in zip(cast_in, cast_out):
        dst[...] = src[...].astype(BF16)


def _norm_proj_kernel(*refs, rope_cols, pos_major, n_cast):
    x_ref, g_ref, w_ref = refs[:3]
    pos = 3
    if rope_cols:
        cos_ref, sin_ref = refs[3:5]
        pos = 5
    cast_in, o_ref, cast_out = refs[pos:pos + n_cast], refs[pos + n_cast], refs[pos + n_cast + 1:pos + 2 * n_cast + 1]
    hn0, hn1 = refs[pos + 2 * n_cast + 1:pos + 2 * n_cast + 3]
    k = pl.program_id(0)

    @pl.when(k == 0)
    def _():
        hn1[...] = jnp.zeros(hn1.shape, BF16)

    def normalise(dst):
        hn = _rms_norm(x_ref[...], g_ref[...])
        if not pos_major:
            dst[...] = hn.astype(BF16)
            return
        slab_ref = refs[-1]
        ct, _, width = x_ref.shape
        for j in range(width // LANES):
            slab_ref[j] = hn[:, :, j * LANES:(j + 1) * LANES].reshape(ct * POS_TILE, LANES)
        for s in range(POS_TILE):
            for j in range(width // LANES):
                dst[s * ct:(s + 1) * ct, j * LANES:(j + 1) * LANES] = (
                    slab_ref[j, pl.ds(s, ct, stride=POS_TILE), :].astype(BF16))

    def project(src):
        hn = src[...]
        for c in range(w_ref.shape[1] // COL_SLAB):
            cols = slice(c * COL_SLAB, (c + 1) * COL_SLAB)
            acc = jnp.dot(hn, w_ref[:, cols], preferred_element_type=F32)
            if c * COL_SLAB < rope_cols:
                for h in range(COL_SLAB // HEAD_DIM):
                    xh = acc[:, h * HEAD_DIM:(h + 1) * HEAD_DIM]
                    lanes = slice(c * COL_SLAB + h * HEAD_DIM, c * COL_SLAB + (h + 1) * HEAD_DIM)
                    o_ref[:, lanes] = xh * cos_ref[...] + pltpu.roll(xh, HEAD_DIM // 2, 1) * sin_ref[...]
            elif pos_major:
                o_ref[:, :, cols] = acc.reshape(POS_TILE, o_ref.shape[1], COL_SLAB)
            else:
                o_ref[:, cols] = acc

    def step(cur, prev):
        normalise(cur)
        _cast_blocks(cast_in, cast_out)
        project(prev)

    pl.when(k % 2 == 0)(lambda: step(hn0, hn1))
    pl.when(k % 2 == 1)(lambda: step(hn1, hn0))


def _cast_specs(casts, n_blocks, index):
    ins, outs, shapes = [], [], []
    for arr, width, col in casts:
        rows = arr.shape[0] // n_blocks
        ins.append(pl.BlockSpec((rows, width), lambda *k, col=col: (index(*k), col)))
        outs.append(pl.BlockSpec((rows, width), lambda *k: (index(*k), 0)))
        shapes.append(jax.ShapeDtypeStruct((arr.shape[0], width), BF16))
    return ins, outs, shapes


def _norm_proj(x, g, w, *, tm=None, chunk=None, ct=None, rope=None, rope_cols=0, tiles_per_seq=1, casts=()):
    m, kd = x.shape
    n = w.shape[1]
    pos_major = chunk is not None
    if pos_major:
        nc, sg = m // chunk, chunk // POS_TILE
        t, rows = (nc // ct) * sg, ct * POS_TILE
    else:
        t, rows = m // tm, tm
    cur = lambda k: jnp.minimum(k, t - 1)
    prv = lambda k: jnp.maximum(k - 1, 0)
    if pos_major:
        x = x.reshape(nc, chunk, kd)
        x_spec = pl.BlockSpec((ct, POS_TILE, kd), lambda k: (cur(k) // sg, cur(k) % sg, 0))
        out_spec = pl.BlockSpec((POS_TILE, ct, n), lambda k: (prv(k) % sg, prv(k) // sg, 0))
        out_shape = jax.ShapeDtypeStruct((chunk, nc, n), F32)
    else:
        x_spec = pl.BlockSpec((tm, kd), lambda k: (cur(k), 0))
        out_spec = pl.BlockSpec((tm, n), lambda k: (prv(k), 0))
        out_shape = jax.ShapeDtypeStruct((m, n), F32)
    table = pl.BlockSpec((rows, LANES), lambda k: (prv(k) % tiles_per_seq, 0))
    cast_in, cast_out, cast_shapes = _cast_specs(casts, t, cur)
    res = pl.pallas_call(
        functools.partial(_norm_proj_kernel, rope_cols=rope_cols, pos_major=pos_major, n_cast=len(casts)),
        grid=(t + 1,),
        in_specs=[x_spec, pl.BlockSpec((1, kd), lambda k: (0, 0)),
                  pl.BlockSpec((kd, n), lambda k: (0, 0), pipeline_mode=pl.Buffered(1))]
        + ([table, table] if rope_cols else []) + cast_in,
        out_specs=[out_spec] + cast_out,
        out_shape=[out_shape] + cast_shapes,
        scratch_shapes=[pltpu.VMEM((rows, kd), BF16)] * 2
        + ([pltpu.VMEM((kd // LANES, rows, LANES), F32)] if pos_major else []),
        compiler_params=_cparams("arbitrary"),
        name="norm_proj",
    )(x, g.reshape(1, kd), w, *(rope if rope_cols else ()), *[c[0] for c in casts])
    return res if casts else res[0]


def _attn_prompt_kernel(q_ref, k_ref, v_ref, *refs, n_cast):
    cast_in, (o_ref, ks_ref, vs_ref) = refs[:n_cast], refs[n_cast:n_cast + 3]
    cast_out = refs[n_cast + 3:2 * n_cast + 3]
    qc_ref, kc_ref, vc_ref, oc_ref, lc_ref, to_ref, tl_ref, lse_ref = refs[2 * n_cast + 3:]
    _cast_blocks(cast_in, cast_out)
    t = q_ref.shape[1]
    n_keep = ks_ref.shape[1]
    ks_ref[0] = k_ref[0, t - n_keep:, :]
    vs_ref[0] = v_ref[0, t - n_keep:, :]
    nblk = t // N_BACK
    scale = HEAD_DIM ** -0.5
    kc_ref[0:N_BACK, :] = jnp.zeros((N_BACK, HEAD_DIM), BF16)
    vc_ref[0:N_BACK, :] = jnp.zeros((N_BACK, HEAD_DIM), BF16)
    qi = lax.broadcasted_iota(jnp.int32, (N_BACK, 2 * N_BACK), 0)
    kj = lax.broadcasted_iota(jnp.int32, (N_BACK, 2 * N_BACK), 1)
    dist = N_BACK + qi - kj
    band = (dist >= 0) & (dist <= N_BACK)

    for d in DILATIONS:
        ln = t // d
        blocks_per_class = ln // N_BACK
        for r in range(d):
            rows = pl.ds(r, ln, stride=d) if d > 1 else pl.ds(0, ln)
            qc_ref[r * ln:(r + 1) * ln, :] = q_ref[0, rows, :].astype(BF16)
            kc_ref[N_BACK + r * ln:N_BACK + (r + 1) * ln, :] = k_ref[0, rows, :].astype(BF16)
            vc_ref[N_BACK + r * ln:N_BACK + (r + 1) * ln, :] = v_ref[0, rows, :].astype(BF16)

        def block(b, carry):
            row0 = pl.multiple_of(b * N_BACK, N_BACK)
            j = lax.rem(b, blocks_per_class)
            q = qc_ref[pl.ds(row0, N_BACK), :]
            k2 = kc_ref[pl.ds(row0, 2 * N_BACK), :]
            v2 = vc_ref[pl.ds(row0, 2 * N_BACK), :]
            s = lax.dot_general(q, k2, (((1,), (1,)), ((), ())), preferred_element_type=F32) * scale
            valid = band & ((kj >= N_BACK) | (j > 0))
            s = jnp.where(valid, s, -jnp.inf)
            m = jnp.max(s, axis=-1, keepdims=True)
            p = jnp.exp(s - m)
            l = jnp.sum(p, axis=-1, keepdims=True)
            o = jnp.dot(p.astype(BF16), v2, preferred_element_type=F32)
            oc_ref[pl.ds(row0, N_BACK), :] = o / l
            lc_ref[pl.ds(row0, N_BACK), :] = jnp.broadcast_to(m + jnp.log(l), (N_BACK, HEAD_DIM))
            return carry

        lax.fori_loop(0, nblk, block, 0, unroll=16)

        if d == 1:
            o_ref[0] = oc_ref[...]
            lse_ref[...] = lc_ref[...]
        else:
            for r in range(d):
                to_ref[pl.ds(r, ln, stride=d), :] = oc_ref[r * ln:(r + 1) * ln, :]
                tl_ref[pl.ds(r, ln, stride=d), :] = lc_ref[r * ln:(r + 1) * ln, :]

            def merge(c, carry):
                rows = pl.ds(pl.multiple_of(c * 64, 64), 64)
                la = lse_ref[rows, :]
                lb = tl_ref[rows, :]
                mx = jnp.maximum(la, lb)
                ea = jnp.exp(la - mx)
                eb = jnp.exp(lb - mx)
                tot = ea + eb
                o_ref[0, rows, :] = (o_ref[0, rows, :] * ea + to_ref[rows, :] * eb) / tot
                lse_ref[rows, :] = mx + jnp.log(tot)
                return carry

            lax.fori_loop(0, t // 64, merge, 0, unroll=2)


def _attn_prompt(proj, b, t, n_keep, casts=()):
    p3 = proj.reshape(b, t, proj.shape[1])
    blk = lambda off: pl.BlockSpec((1, t, HEAD_DIM), lambda i, h: (i, 0, off + h))
    keep = pl.BlockSpec((1, n_keep, HEAD_DIM), lambda i, h: (i, 0, h))
    cast_in, cast_out, cast_shapes = _cast_specs(casts, b * N_HEADS, lambda i, h: i * N_HEADS + h)
    return pl.pallas_call(
        functools.partial(_attn_prompt_kernel, n_cast=len(casts)),
        grid=(b, N_HEADS),
        in_specs=[blk(0), blk(N_HEADS), blk(2 * N_HEADS)] + cast_in,
        out_specs=[pl.BlockSpec((1, t, HEAD_DIM), lambda i, h: (i, 0, h)), keep, keep] + cast_out,
        out_shape=[jax.ShapeDtypeStruct((b, t, ATTN_WIDTH), F32),
                   jax.ShapeDtypeStruct((b, n_keep, ATTN_WIDTH), F32),
                   jax.ShapeDtypeStruct((b, n_keep, ATTN_WIDTH), F32)] + cast_shapes,
        scratch_shapes=[
            pltpu.VMEM((t, HEAD_DIM), BF16),
            pltpu.VMEM((t + N_BACK, HEAD_DIM), BF16),
            pltpu.VMEM((t + N_BACK, HEAD_DIM), BF16),
            pltpu.VMEM((t, HEAD_DIM), F32),
            pltpu.VMEM((t, HEAD_DIM), F32),
            pltpu.VMEM((t, HEAD_DIM), F32),
            pltpu.VMEM((t, HEAD_DIM), F32),
            pltpu.VMEM((t, HEAD_DIM), F32),
        ],
        compiler_params=_cparams("parallel", "parallel"),
        name="attn_prompt",
    )(p3, p3, p3, *[c[0] for c in casts])


def _attn_sample_kernel(q_ref, kn_ref, vn_ref, kc_ref, vc_ref, o_ref):
    s_len = q_ref.shape[0]
    n_buf = kc_ref.shape[1] // N_HEADS
    scale = HEAD_DIM ** -0.5
    nt = (((1,), (1,)), ((), ()))

    def count(dist):
        c = jnp.zeros(dist.shape, F32)
        for d in DILATIONS:
            hit = (dist >= 0) & (dist <= N_BACK * d) & ((dist & (d - 1)) == 0)
            c = c + jnp.where(hit, 1.0, 0.0)
        return c

    iota = lambda shape, dim: lax.broadcasted_iota(jnp.int32, shape, dim)
    cc = count(n_buf + iota((s_len, n_buf), 0) - iota((s_len, n_buf), 1))
    cn = count(iota((s_len, s_len), 0) - iota((s_len, s_len), 1))
    for h in range(N_HEADS):
        cols = slice(h * HEAD_DIM, (h + 1) * HEAD_DIM)
        head_rows = pl.ds(h, n_buf, stride=N_HEADS)
        q = q_ref[:, cols].astype(BF16)
        sc = lax.dot_general(q, kc_ref[0, head_rows, :].astype(BF16), nt, preferred_element_type=F32) * scale
        sn = lax.dot_general(q, kn_ref[:, cols].astype(BF16), nt, preferred_element_type=F32) * scale
        sc = jnp.where(cc > 0, sc, -jnp.inf)
        sn = jnp.where(cn > 0, sn, -jnp.inf)
        m = jnp.maximum(jnp.max(sc, axis=-1, keepdims=True), jnp.max(sn, axis=-1, keepdims=True))
        pc = cc * jnp.exp(sc - m)
        pn = cn * jnp.exp(sn - m)
        l = jnp.sum(pc, axis=-1, keepdims=True) + jnp.sum(pn, axis=-1, keepdims=True)
        o = (jnp.dot(pc.astype(BF16), vc_ref[0, head_rows, :].astype(BF16), preferred_element_type=F32)
             + jnp.dot(pn.astype(BF16), vn_ref[:, cols].astype(BF16), preferred_element_type=F32))
        o_ref[:, cols] = o / l


def _attn_sample(proj, cache_k, cache_v, b, s_len):
    n_buf = cache_k.shape[1]
    ck = cache_k.reshape(b, n_buf * N_HEADS, HEAD_DIM)
    cv = cache_v.reshape(b, n_buf * N_HEADS, HEAD_DIM)
    new = lambda c: pl.BlockSpec((s_len, ATTN_WIDTH), lambda i: (i, c))
    old = pl.BlockSpec((1, n_buf * N_HEADS, HEAD_DIM), lambda i: (i, 0, 0))
    return pl.pallas_call(
        _attn_sample_kernel,
        grid=(b,),
        in_specs=[new(0), new(1), new(2), old, old],
        out_specs=new(0),
        out_shape=jax.ShapeDtypeStruct((b * s_len, ATTN_WIDTH), F32),
        compiler_params=_cparams("parallel"),
        name="attn_sample",
    )(proj, proj, proj, ck, cv)


def _silu(z):
    return z * jax.nn.sigmoid(z)


def _mix_out_kernel(x_ref, o_ref, g_ref, wza_ref, wgb_ref, wgc_ref, whi_ref, wzb_ref, p2_ref, p1_ref, cw_ref,
                    w_ref, h_ref, cs_ref, tail_ref, *, tiles_per_seq):
    tm = x_ref.shape[0]
    k = p1_ref.shape[0]
    ln = tm // k
    x = x_ref[...]
    hn = _rms_norm(x, g_ref[...]).astype(BF16)
    gate = lambda w: jnp.dot(hn, w[...], preferred_element_type=F32)
    ch = gate(wgc_ref) * gate(whi_ref)
    if k == 1:
        @pl.when(pl.program_id(0) % tiles_per_seq == 0)
        def _():
            tail_ref[0:1, :] = p2_ref[0]
            tail_ref[1:2, :] = p1_ref[0]

        prev2, prev1 = tail_ref[0:1, :], tail_ref[1:2, :]
        tail_ref[...] = ch[tm - 2:tm, :]
        cs_ref[0] = ch[tm - 2:tm, :]
    else:
        per_row = lambda p: jnp.broadcast_to(p[...], (k, ln, CONV_WIDTH)).reshape(tm, CONV_WIDTH)
        prev2, prev1 = per_row(p2_ref), per_row(p1_ref)
        cs_ref[...] = ch.reshape(k, ln, CONV_WIDTH)[:, ln - 2:, :]
    pos = lax.rem(lax.broadcasted_iota(jnp.int32, (tm, 1), 0), ln)
    ch1 = jnp.where(pos == 0, prev1, pltpu.roll(ch, 1, 0))
    ch2 = jnp.where(pos == 0, prev2, jnp.where(pos == 1, prev1, pltpu.roll(ch, 2, 0)))
    conv = ch2 * cw_ref[0:1, :] + ch1 * cw_ref[1:2, :] + ch * cw_ref[2:3, :]
    o_b = (gate(wgb_ref) * conv * _silu(gate(wzb_ref))).astype(BF16)
    y = jnp.dot(o_b, w_ref[ATTN_WIDTH:, :], preferred_element_type=F32)
    o_a = (o_ref[...] * _silu(gate(wza_ref))).astype(BF16)
    y = y + jnp.dot(o_a, w_ref[0:ATTN_WIDTH, :], preferred_element_type=F32)
    h_ref[...] = x + y


def _mix_out(x, o_attn, g, w_gates, conv_init, conv_w, w_out, *, tm, tiles_per_seq=1, seqs_per_tile=1):
    m = x.shape[0]
    nseq = conv_init.shape[0]
    assert tiles_per_seq == 1 or seqs_per_tile == 1
    seq_of = lambda i: i // tiles_per_seq
    before = pl.BlockSpec((seqs_per_tile, 1, CONV_WIDTH), lambda i: (seq_of(i), 0, 0))
    after = pl.BlockSpec((seqs_per_tile, 2, CONV_WIDTH), lambda i: (seq_of(i), 0, 0))
    whole = lambda a: pl.BlockSpec(a.shape, lambda i: (0,) * a.ndim, pipeline_mode=pl.Buffered(1))
    return pl.pallas_call(
        functools.partial(_mix_out_kernel, tiles_per_seq=tiles_per_seq),
        grid=(m // tm,),
        in_specs=[pl.BlockSpec((tm, D_MODEL), lambda i: (i, 0)), pl.BlockSpec((tm, ATTN_WIDTH), lambda i: (i, 0)),
                  pl.BlockSpec((1, D_MODEL), lambda i: (0, 0))] + [whole(w) for w in w_gates]
        + [before, before, pl.BlockSpec((3, CONV_WIDTH), lambda i: (0, 0)), whole(w_out)],
        out_specs=(pl.BlockSpec((tm, D_MODEL), lambda i: (i, 0)), after),
        out_shape=(jax.ShapeDtypeStruct((m, D_MODEL), F32), jax.ShapeDtypeStruct((nseq, 2, CONV_WIDTH), F32)),
        scratch_shapes=[pltpu.VMEM((2, CONV_WIDTH), F32)],
        compiler_params=_cparams("arbitrary"),
        name="mix_out",
    )(x, o_attn, g.reshape(1, -1), *w_gates, conv_init[:, 0:1], conv_init[:, 1:2], conv_w, w_out)


N_SCAN = 7
GROUPS_PER_TILE = LANES // SSM_GROUP


PROMPT_CHUNK = 32
ROW_C = PROMPT_CHUNK + 1
ROW_DBL = PROMPT_CHUNK + 2
TABLE_ROWS = ROW_DBL + N_SCAN


def _s5_disc_kernel(lr_ref, li_ref, ls_ref, tr_ref, ti_ref):
    lr = lr_ref[...]
    li = li_ref[...]
    step = jnp.exp(ls_ref[...])
    mag = jnp.exp(lr * step)
    ar = mag * jnp.cos(li * step)
    ai = mag * jnp.sin(li * step)
    den = lr * lr + li * li
    nr = ar - 1.0
    tr_ref[ROW_C] = (nr * lr + ai * li) / den
    ti_ref[ROW_C] = (ai * lr - nr * li) / den
    pr = jnp.ones_like(ar)
    pi = jnp.zeros_like(ar)
    for tau in range(PROMPT_CHUNK + 1):
        tr_ref[tau] = pr
        ti_ref[tau] = pi
        dr, di = pr, pi
        pr, pi = pr * ar - pi * ai, pr * ai + pi * ar
    for i in range(N_SCAN):
        tr_ref[ROW_DBL + i] = dr
        ti_ref[ROW_DBL + i] = di
        dr, di = dr * dr - di * di, 2.0 * dr * di


def _s5_tables(lam_re, lam_im, log_step):
    g = lam_re.shape[0]
    dup = lambda a: jnp.concatenate([a, a], axis=-1)
    tr, ti = pl.pallas_call(
        _s5_disc_kernel,
        out_shape=(jax.ShapeDtypeStruct((TABLE_ROWS, g, LANES), F32),) * 2,
        name="s5_disc",
    )(dup(lam_re), dup(lam_im), log_step[:, None])
    return tr.transpose(1, 0, 2), ti.transpose(1, 0, 2)


def _split_bf16(x):
    hi = x.astype(BF16)
    return hi, (x - hi.astype(F32)).astype(BF16)


def _dot_split(a, b, b_is_bf16_exact=False):
    dot = lambda x, y: jnp.dot(x, y, preferred_element_type=F32)
    ah, al = _split_bf16(a)
    if b_is_bf16_exact:
        bh = b.astype(BF16)
        return dot(ah, bh) + dot(al, bh)
    bh, bl = _split_bf16(b)
    return dot(ah, bh) + (dot(ah, bl) + dot(al, bh))


def _pad_rows(x, rows):
    if x.shape[0] == rows:
        return x
    return jnp.concatenate([x, jnp.zeros((rows - x.shape[0], x.shape[1]), x.dtype)], axis=0)


def _s5_prep_kernel(tr_ref, ti_ref, br_ref, bi_ref, cr_ref, ci_ref, c2r_ref, c2i_ref,
                    wt_ref, qt_ref, kr_ref, ki_ref, *, chunk):
    n = SSM_GROUP * chunk
    iota = lambda shape, dim: lax.broadcasted_iota(jnp.int32, shape, dim)
    lane = iota((SSM_GROUP, 2 * SSM_STATE), 1)
    e_sel = jnp.where(iota((chunk, n), 0) == chunk - 1 - iota((chunk, n), 1) // SSM_GROUP, 1.0, 0.0)
    tile = jnp.where(iota((SSM_GROUP, n), 0) == iota((SSM_GROUP, n), 1) % SSM_GROUP, 1.0, 0.0)

    def one_group(g, carry):
        tab_r, tab_i = tr_ref[g], ti_ref[g]
        col_r = _pad_rows(tab_r, LANES).T[:SSM_STATE]
        col_i = _pad_rows(tab_i, LANES).T[:SSM_STATE]
        kr_ref[g] = col_r
        ki_ref[g] = col_i
        ccr, cci = col_r[:, ROW_C:ROW_C + 1], col_i[:, ROW_C:ROW_C + 1]
        br, bi = br_ref[g], bi_ref[g]
        bbr = ccr * br - cci * bi
        bbi = ccr * bi + cci * br
        aer = _dot_split(col_r[:, :chunk], e_sel, True)
        aei = _dot_split(col_i[:, :chunk], e_sel, True)
        btr = _dot_split(bbr, tile, True)
        bti = _dot_split(bbi, tile, True)
        ptr = aer * btr - aei * bti
        pti = aer * bti + aei * btr
        wt_ref[g, n:n + SSM_STATE, :] = ptr.astype(BF16)
        wt_ref[g, n + SSM_STATE:n + 2 * SSM_STATE, :] = pti.astype(BF16)
        taps = _dot_split(cr_ref[g], ptr) - _dot_split(ci_ref[g], pti)
        padded = jnp.concatenate([taps, jnp.zeros_like(taps)], axis=1)
        for t in range(chunk):
            sh = SSM_GROUP * (chunk - 1 - t)
            blk = padded if sh == 0 else pltpu.roll(padded, 2 * n - sh, 1)
            wt_ref[g, SSM_GROUP * t:SSM_GROUP * (t + 1), :] = blk[:, :n].astype(BF16)
        c2r, c2i = c2r_ref[g], c2i_ref[g]
        for t in range(chunk):
            ar = tab_r[t + 1:t + 2, :]
            ai = tab_i[t + 1:t + 2, :]
            x1 = jnp.where(lane < SSM_STATE, ar, -ai)
            x2 = jnp.where(lane < SSM_STATE, ai, ar)
            qt_ref[g, SSM_GROUP * t:SSM_GROUP * (t + 1), :] = (c2r * x1 - c2i * x2).astype(BF16)
        return carry

    lax.fori_loop(0, wt_ref.shape[0], one_group, 0, unroll=4)


def _s5_operators(tables, b_re, b_im, c_re, c_im, chunk):
    g = b_re.shape[0]
    n = SSM_GROUP * chunk
    assert chunk <= PROMPT_CHUNK
    dup = lambda a: jnp.concatenate([a, a], axis=-1)
    per_g = lambda *s: pl.BlockSpec((GROUPS_PER_TILE,) + s, lambda i: (i,) + (0,) * len(s))
    return pl.pallas_call(
        functools.partial(_s5_prep_kernel, chunk=chunk),
        grid=(g // GROUPS_PER_TILE,),
        in_specs=[per_g(TABLE_ROWS, LANES), per_g(TABLE_ROWS, LANES),
                  per_g(SSM_STATE, SSM_GROUP), per_g(SSM_STATE, SSM_GROUP),
                  per_g(SSM_GROUP, SSM_STATE), per_g(SSM_GROUP, SSM_STATE),
                  per_g(SSM_GROUP, LANES), per_g(SSM_GROUP, LANES)],
        out_specs=(per_g(n + 2 * SSM_STATE, n), per_g(n, 2 * SSM_STATE),
                   per_g(SSM_STATE, LANES), per_g(SSM_STATE, LANES)),
        out_shape=(jax.ShapeDtypeStruct((g, n + 2 * SSM_STATE, n), BF16),
                   jax.ShapeDtypeStruct((g, n, 2 * SSM_STATE), BF16),
                   jax.ShapeDtypeStruct((g, SSM_STATE, LANES), F32),
                   jax.ShapeDtypeStruct((g, SSM_STATE, LANES), F32)),
        compiler_params=_cparams("parallel"),
        name="s5_prep",
    )(*tables, b_re, b_im, c_re, c_im, dup(c_re), dup(c_im))


def _s5_scan_kernel(u_ref, wt_ref, *rest, chunk, carry, split_ops):
    ws_ref = rest[0] if split_ops else wt_ref
    qt_ref, kr_ref, ki_ref, dsk_ref = rest[split_ops:split_ops + 4]
    rest = rest[split_ops + 4:]
    _s5_scan_body(u_ref, wt_ref, ws_ref, qt_ref, kr_ref, ki_ref, dsk_ref, *rest, chunk=chunk, carry=carry)


def _s5_scan_body(u_ref, wt_ref, ws_ref, qt_ref, kr_ref, ki_ref, dsk_ref, *rest, chunk, carry):
    if carry:
        y_ref, hf_ref, dall_ref, yall_ref = rest
    else:
        h0_ref, y_ref, hf_ref, dall_ref, yall_ref = rest
    nc = u_ref.shape[1]
    ncp = dall_ref.shape[2]
    nseq = ncp // LANES if carry else nc
    n = SSM_GROUP * chunk
    iota = lambda shape, dim: lax.broadcasted_iota(jnp.int32, shape, dim)
    for s in range(chunk):
        dall_ref[s] = _pad_rows(u_ref[s], ncp).T.astype(BF16)
    lane = iota((SSM_STATE, ncp), 1) & (LANES - 1)
    slot = iota((LANES, LANES), 1)
    if not carry:
        h0_all = _pad_rows(jnp.concatenate([h0_ref[:, gp, :] for gp in range(GROUPS_PER_TILE)], axis=0), LANES).T
    acc = jnp.zeros((LANES, LANES), F32)
    for gp in range(GROUPS_PER_TILE):
        rows = slice(SSM_GROUP * gp, SSM_GROUP * (gp + 1))
        d = dall_ref[:, rows, :].reshape(n, ncp)
        ys = jnp.dot(wt_ref[gp], d, preferred_element_type=F32)
        if ws_ref is wt_ref:
            st = ys[n:]
            ys = ys[:n]
        else:
            st = jnp.dot(ws_ref[gp], d, preferred_element_type=F32)
        sr = st[:SSM_STATE]
        si = st[SSM_STATE:]
        if carry:
            for i in range(N_SCAN):
                sh = 1 << i
                ar = kr_ref[gp, :, ROW_DBL + i:ROW_DBL + i + 1]
                ai = ki_ref[gp, :, ROW_DBL + i:ROW_DBL + i + 1]
                pr = jnp.where(lane >= sh, pltpu.roll(sr, sh, 1), 0.0)
                pi = jnp.where(lane >= sh, pltpu.roll(si, sh, 1), 0.0)
                sr, si = sr + ar * pr - ai * pi, si + ar * pi + ai * pr
            hp = jnp.concatenate([jnp.where(lane >= 1, pltpu.roll(sr, 1, 1), 0.0),
                                  jnp.where(lane >= 1, pltpu.roll(si, 1, 1), 0.0)], axis=0)
            for b in range(nseq):
                seq = slice(LANES * b, LANES * (b + 1))
                end = jnp.concatenate([sr[:, seq], si[:, seq]], axis=0)
                acc = jnp.where(slot == 8 * gp + b, pltpu.roll(end, (8 * gp + b + 1) % LANES, 1), acc)
        else:
            hp = h0_all if gp == 0 else pltpu.roll(h0_all, LANES - 8 * gp, 1)
            hp = jnp.where(slot < nseq, hp, 0.0)
            hpr, hpi = hp[:SSM_STATE], hp[SSM_STATE:]
            ar = kr_ref[gp, :, chunk:chunk + 1]
            ai = ki_ref[gp, :, chunk:chunk + 1]
            end = jnp.concatenate([sr + ar * hpr - ai * hpi, si + ar * hpi + ai * hpr], axis=0)
            if gp:
                end = pltpu.roll(end, 8 * gp, 1)
            acc = jnp.where((slot >= 8 * gp) & (slot < 8 * gp + nseq), end, acc)
        y = ys + jnp.dot(qt_ref[gp], hp.astype(BF16), preferred_element_type=F32)
        yall_ref[:, rows, :] = y.reshape(chunk, SSM_GROUP, ncp)
    hf_ref[...] = acc.T[:8 * GROUPS_PER_TILE].reshape(GROUPS_PER_TILE, 8, LANES)
    for t in range(chunk):
        y_ref[t] = yall_ref[t].T[:nc] + dsk_ref[...] * u_ref[t]


def _s5_scan(proj, ops, d_skip, h0, nseq, *, carry):
    wt, qt, kr, ki = ops
    chunk, nc, _ = proj.shape
    n = SSM_GROUP * chunk
    n_full = SSM_GROUP * PROMPT_CHUNK
    assert nc == (nseq * LANES if carry else nseq) and (carry or nseq == 8) and nseq <= 8
    assert n_full % n == 0 and n_full % (2 * SSM_STATE) == 0
    ncp = max(nc, LANES)
    gt = GROUPS_PER_TILE
    tile3 = lambda a, b: pl.BlockSpec((gt, a, b), lambda i: (i, 0, 0))
    lane_tile = pl.BlockSpec((chunk, nc, LANES), lambda i: (0, 0, i))
    split_ops = n != n_full
    if split_ops:
        end_rows = pl.BlockSpec((gt, 2 * SSM_STATE, n), lambda i: (i, n_full // (2 * SSM_STATE), n_full // n - 1))
        op_specs, op_args = [tile3(n, n), end_rows], [wt, wt]
    else:
        op_specs, op_args = [tile3(n + 2 * SSM_STATE, n)], [wt]
    in_specs = [lane_tile] + op_specs + [tile3(n, 2 * SSM_STATE), tile3(SSM_STATE, LANES),
                                         tile3(SSM_STATE, LANES), pl.BlockSpec((1, LANES), lambda i: (0, i))]
    args = [proj] + op_args + [qt, kr, ki, d_skip.reshape(1, -1)]
    if not carry:
        in_specs.append(pl.BlockSpec((nseq, gt, LANES), lambda i: (0, i, 0)))
        args.append(h0)
    return pl.pallas_call(
        functools.partial(_s5_scan_kernel, chunk=chunk, carry=carry, split_ops=split_ops),
        grid=(SSM_GROUPS // gt,),
        in_specs=in_specs,
        out_specs=(lane_tile, tile3(8, LANES)),
        out_shape=(jax.ShapeDtypeStruct((chunk, nc, SSM_GROUPS * SSM_GROUP), F32),
                   jax.ShapeDtypeStruct((SSM_GROUPS, 8, LANES), F32)),
        scratch_shapes=[pltpu.VMEM((chunk, LANES, ncp), BF16), pltpu.VMEM((chunk, LANES, ncp), F32)],
        compiler_params=_cparams("parallel"),
        name="s5_scan",
    )(*args)


def _ssm_out(y, x, gn_ref, wz_ref, wg_ref, bg_ref, wo_ref, gf_ref):
    z = jnp.dot(_rms_norm(x, gn_ref[...]).astype(BF16), wz_ref[...], preferred_element_type=F32)
    g = jax.nn.gelu(y, approximate=True)
    gate = jnp.dot(g.astype(BF16), wg_ref[...], preferred_element_type=F32) + bg_ref[...]
    yy = (g * jax.nn.sigmoid(gate)) * _silu(z)
    return _rms_norm(x + jnp.dot(yy.astype(BF16), wo_ref[...], preferred_element_type=F32), gf_ref[...])


def _glu_out_kernel(y_ref, x_ref, *rest):
    o_ref = rest[-1]
    o_ref[...] = _ssm_out(y_ref[...], x_ref[...], *rest[:-1])


def _glu_out_pm_kernel(y_ref, x_ref, *rest):
    o_ref, slab_ref = rest[-2:]
    p, ct, w = y_ref.shape
    for s in range(p):
        for j in range(w // LANES):
            slab_ref[j, pl.ds(s, ct, stride=p), :] = y_ref[s, :, j * LANES:(j + 1) * LANES]
    y = jnp.concatenate([slab_ref[j] for j in range(w // LANES)], axis=1)
    o_ref[...] = _ssm_out(y, x_ref[...].reshape(ct * p, w), *rest[:-2]).reshape(ct, p, w)


def _glu_out(y, x, g_norm, w_z, w_glu, b_glu, w_out, g_final, *, tm=None, ct=None):
    m, w = x.shape
    vec = pl.BlockSpec((1, w), lambda i: (0, 0))
    mat = pl.BlockSpec((w, w), lambda i: (0, 0), pipeline_mode=pl.Buffered(1))
    params = (g_norm.reshape(1, w), w_z, w_glu, b_glu.reshape(1, w), w_out, g_final.reshape(1, w))
    param_specs = [vec, mat, mat, vec, mat, vec]
    if ct is None:
        row = pl.BlockSpec((tm, w), lambda i: (i, 0))
        return pl.pallas_call(
            _glu_out_kernel,
            grid=(m // tm,),
            in_specs=[row, row] + param_specs,
            out_specs=row,
            out_shape=jax.ShapeDtypeStruct((m, w), F32),
            compiler_params=_cparams("parallel"),
            name="glu_out",
        )(y, x, *params)
    chunk, nc, _ = y.shape
    sg = chunk // POS_TILE
    tok = pl.BlockSpec((ct, POS_TILE, w), lambda i: (i // sg, i % sg, 0))
    return pl.pallas_call(
        _glu_out_pm_kernel,
        grid=((nc // ct) * sg,),
        in_specs=[pl.BlockSpec((POS_TILE, ct, w), lambda i: (i % sg, i // sg, 0)), tok] + param_specs,
        out_specs=tok,
        out_shape=jax.ShapeDtypeStruct((nc, chunk, w), F32),
        scratch_shapes=[pltpu.VMEM((w // LANES, ct * POS_TILE, LANES), F32)],
        compiler_params=_cparams("parallel"),
        name="glu_out_pm",
    )(y, x.reshape(nc, chunk, w), *params).reshape(m, w)


PAST_LEN = 16384


def kernel(x_prompt, x_sample, cache_win_k, cache_win_v, state_conv, state_ssm_re, state_ssm_im, attn_norm, w_in_ab, conv_w, w_out_ab, ssm_norm, w_in_c, lam_re, lam_im, log_step, b_re, b_im, c_re, c_im, d_skip, w_glu, b_glu, w_out_c, final_norm):
    bp, tp, _ = x_prompt.shape
    bs, ts, _ = x_sample.shape
    n_keep = min(2048, tp)
    xp = x_prompt.reshape(bp * tp, D_MODEL)
    xs = x_sample.reshape(bs * ts, D_MODEL)

    qkv_cols = 3 * ATTN_WIDTH
    w_qkv = w_in_ab[0, :, :qkv_cols].astype(BF16)
    gate_casts = [(w_in_ab[0], CONV_WIDTH, qkv_cols // CONV_WIDTH + c) for c in range(5)]
    proj_p, *w_gates, w_out0 = _norm_proj(
        xp, attn_norm[0], w_qkv, tm=512, rope=_rope_tables(tp, tp, 0), rope_cols=2 * ATTN_WIDTH,
        tiles_per_seq=tp // 512, casts=gate_casts + [(w_out_ab[0], D_MODEL, 0)])
    proj_s = _norm_proj(xs, attn_norm[0], w_qkv, tm=bs * ts, rope=_rope_tables(bs * ts, ts, PAST_LEN),
                        rope_cols=2 * ATTN_WIDTH)
    o_p, k_p, v_p, w_u, w_z, w_glu1, w_out1 = _attn_prompt(
        proj_p, bp, tp, n_keep,
        casts=[(w_in_c[0], D_MODEL, 0), (w_in_c[0], D_MODEL, 1), (w_glu[0], D_MODEL, 0), (w_out_c[0], D_MODEL, 0)])
    o_p = o_p.reshape(bp * tp, ATTN_WIDTH)
    k_p = k_p.reshape(1, bp, n_keep, N_HEADS, HEAD_DIM)
    v_p = v_p.reshape(1, bp, n_keep, N_HEADS, HEAD_DIM)
    o_s = _attn_sample(proj_s, cache_win_k[0], cache_win_v[0], bs, ts)
    h1_p, conv_p = _mix_out(xp, o_p, attn_norm[0], w_gates, jnp.zeros((bp, 2, CONV_WIDTH), F32), conv_w[0],
                            w_out0, tm=256, tiles_per_seq=tp // 256)
    h1_s, conv_s = _mix_out(xs, o_s, attn_norm[0], w_gates, state_conv[0], conv_w[0], w_out0,
                            tm=bs * ts, seqs_per_tile=bs)
    k_s = proj_s[:, ATTN_WIDTH:2 * ATTN_WIDTH].reshape(1, bs, ts, N_HEADS, HEAD_DIM)
    v_s = proj_s[:, 2 * ATTN_WIDTH:3 * ATTN_WIDTH].reshape(1, bs, ts, N_HEADS, HEAD_DIM)

    out_params = (ssm_norm[0], w_z, w_glu1, b_glu[0], w_out1, final_norm)
    tables = _s5_tables(lam_re[0], lam_im[0], log_step[0])
    ops = _s5_operators(tables, b_re[0], b_im[0], c_re[0], c_im[0], PROMPT_CHUNK)
    u_p = _norm_proj(h1_p, ssm_norm[0], w_u, chunk=PROMPT_CHUNK, ct=64)
    y_p, hf_p = _s5_scan(u_p, ops, d_skip[0], None, bp, carry=True)
    out_p = _glu_out(y_p, h1_p, *out_params, ct=32)
    u_s = _norm_proj(h1_s, ssm_norm[0], w_u, tm=bs * ts)
    h0 = jnp.concatenate([state_ssm_re[0], state_ssm_im[0]], axis=-1)
    y_s, hf_s = _s5_scan(u_s.reshape(bs, ts, -1).transpose(1, 0, 2), ops, d_skip[0], h0, bs, carry=False)
    out_s = _glu_out(y_s.transpose(1, 0, 2).reshape(bs * ts, D_MODEL), h1_s, *out_params, tm=bs * ts)
    hf_p = hf_p[:, :bp].transpose(1, 0, 2)[None]
    hf_s = hf_s[:, :bs].transpose(1, 0, 2)[None]
    return (out_p.reshape(bp, tp, D_MODEL), out_s.reshape(bs, ts, D_MODEL),
            k_p, v_p, conv_p[None], hf_p[..., :SSM_STATE], hf_p[..., SSM_STATE:],
            k_s, v_s, conv_s[None], hf_s[..., :SSM_STATE], hf_s[..., SSM_STATE:])
```

```python
import functools
import math

import jax
import jax.numpy as jnp
from jax import lax
from jax.experimental import pallas as pl
from jax.experimental.pallas import tpu as pltpu

D_MODEL = 2048
HEAD_DIM = 128
N_HEADS = 8
ATTN_WIDTH = 1024
CONV_WIDTH = 1024
DILATIONS = (1, 4, 16)
N_BACK = 128
ROPE_THETA = 10000.0
RMS_EPS = 1e-6
SSM_GROUP = 16
SSM_GROUPS = 128
SSM_STATE = 64
LANES = 128
VMEM_LIMIT = 56 * 1024 * 1024

F32 = jnp.float32
BF16 = jnp.bfloat16


def _cparams(*sem):
    return pltpu.CompilerParams(dimension_semantics=sem, vmem_limit_bytes=VMEM_LIMIT)


def _rope_table_kernel(inv_ref, cos_ref, sin_ref, *, period, offset):
    rows = cos_ref.shape[0]
    r = lax.broadcasted_iota(jnp.int32, (rows, LANES), 0) + pl.program_id(0) * rows
    pos = (offset + lax.rem(r, period)).astype(F32)
    ang = pos * inv_ref[...]
    lane = lax.broadcasted_iota(jnp.int32, (rows, LANES), 1)
    cos_ref[...] = jnp.cos(ang)
    sin_ref[...] = jnp.where(lane < HEAD_DIM // 2, -1.0, 1.0) * jnp.sin(ang)


def _rope_tables(rows, period, offset):
    half = HEAD_DIM // 2
    inv = ROPE_THETA ** (-jnp.arange(half, dtype=F32) / half)
    inv2 = jnp.concatenate([inv, inv])[None, :]
    tr = min(rows, 256)
    return pl.pallas_call(
        functools.partial(_rope_table_kernel, period=period, offset=offset),
        grid=(rows // tr,),
        in_specs=[pl.BlockSpec((1, LANES), lambda i: (0, 0))],
        out_specs=(pl.BlockSpec((tr, LANES), lambda i: (i, 0)),) * 2,
        out_shape=(jax.ShapeDtypeStruct((rows, LANES), F32),) * 2,
        compiler_params=_cparams("parallel"),
        name="rope_table",
    )(inv2)


def _rms_norm(x, g):
    return x * lax.rsqrt(jnp.mean(x * x, axis=-1, keepdims=True) + RMS_EPS) * g


POS_TILE = 8
COL_SLAB = 256


def _cast_blocks(cast_in, cast_out):
    for src, dst in zip(cast_in, cast_out):
        dst[...] = src[...].astype(BF16)


def _norm_proj_kernel(*refs, rope_cols, pos_major, n_cast):
    x_ref, g_ref, w_ref = refs[:3]
    pos = 3
    if rope_cols:
        cos_ref, sin_ref = refs[3:5]
        pos = 5
    cast_in, o_ref, cast_out = refs[pos:pos + n_cast], refs[pos + n_cast], refs[pos + n_cast + 1:pos + 2 * n_cast + 1]
    hn0, hn1 = refs[pos + 2 * n_cast + 1:pos + 2 * n_cast + 3]
    k = pl.program_id(0)

    @pl.when(k == 0)
    def _():
        hn1[...] = jnp.zeros(hn1.shape, BF16)

    def normalise(dst):
        hn = _rms_norm(x_ref[...], g_ref[...])
        if not pos_major:
            dst[...] = hn.astype(BF16)
            return
        slab_ref = refs[-1]
        ct, _, width = x_ref.shape
        for j in range(width // LANES):
            slab_ref[j] = hn[:, :, j * LANES:(j + 1) * LANES].reshape(ct * POS_TILE, LANES)
        for s in range(POS_TILE):
            for j in range(width // LANES):
                dst[s * ct:(s + 1) * ct, j * LANES:(j + 1) * LANES] = (
                    slab_ref[j, pl.ds(s, ct, stride=POS_TILE), :].astype(BF16))

    def project(src):
        hn = src[...]
        for c in range(w_ref.shape[1] // COL_SLAB):
            cols = slice(c * COL_SLAB, (c + 1) * COL_SLAB)
            acc = jnp.dot(hn, w_ref[:, cols], preferred_element_type=F32)
            if c * COL_SLAB < rope_cols:
                for h in range(COL_SLAB // HEAD_DIM):
                    xh = acc[:, h * HEAD_DIM:(h + 1) * HEAD_DIM]
                    lanes = slice(c * COL_SLAB + h * HEAD_DIM, c * COL_SLAB + (h + 1) * HEAD_DIM)
                    o_ref[:, lanes] = xh * cos_ref[...] + pltpu.roll(xh, HEAD_DIM // 2, 1) * sin_ref[...]
            elif pos_major:
                o_ref[:, :, cols] = acc.reshape(POS_TILE, o_ref.shape[1], COL_SLAB)
            else:
                o_ref[:, cols] = acc

    def step(cur, prev):
        normalise(cur)
        _cast_blocks(cast_in, cast_out)
        project(prev)

    pl.when(k % 2 == 0)(lambda: step(hn0, hn1))
    pl.when(k % 2 == 1)(lambda: step(hn1, hn0))


def _cast_specs(casts, n_blocks, index):
    ins, outs, shapes = [], [], []
    for arr, width, col in casts:
        rows = arr.shape[0] // n_blocks
        ins.append(pl.BlockSpec((rows, width), lambda *k, col=col: (index(*k), col)))
        outs.append(pl.BlockSpec((rows, width), lambda *k: (index(*k), 0)))
        shapes.append(jax.ShapeDtypeStruct((arr.shape[0], width), BF16))
    return ins, outs, shapes


def _norm_proj(x, g, w, *, tm=None, chunk=None, ct=None, rope=None, rope_cols=0, tiles_per_seq=1, casts=()):
    m, kd = x.shape
    n = w.shape[1]
    pos_major = chunk is not None
    if pos_major:
        nc, sg = m // chunk, chunk // POS_TILE
        t, rows = (nc // ct) * sg, ct * POS_TILE
    else:
        t, rows = m // tm, tm
    cur = lambda k: jnp.minimum(k, t - 1)
    prv = lambda k: jnp.maximum(k - 1, 0)
    if pos_major:
        x = x.reshape(nc, chunk, kd)
        x_spec = pl.BlockSpec((ct, POS_TILE, kd), lambda k: (cur(k) // sg, cur(k) % sg, 0))
        out_spec = pl.BlockSpec((POS_TILE, ct, n), lambda k: (prv(k) % sg, prv(k) // sg, 0))
        out_shape = jax.ShapeDtypeStruct((chunk, nc, n), F32)
    else:
        x_spec = pl.BlockSpec((tm, kd), lambda k: (cur(k), 0))
        out_spec = pl.BlockSpec((tm, n), lambda k: (prv(k), 0))
        out_shape = jax.ShapeDtypeStruct((m, n), F32)
    table = pl.BlockSpec((rows, LANES), lambda k: (prv(k) % tiles_per_seq, 0))
    cast_in, cast_out, cast_shapes = _cast_specs(casts, t, cur)
    res = pl.pallas_call(
        functools.partial(_norm_proj_kernel, rope_cols=rope_cols, pos_major=pos_major, n_cast=len(casts)),
        grid=(t + 1,),
        in_specs=[x_spec, pl.BlockSpec((1, kd), lambda k: (0, 0)),
                  pl.BlockSpec((kd, n), lambda k: (0, 0), pipeline_mode=pl.Buffered(1))]
        + ([table, table] if rope_cols else []) + cast_in,
        out_specs=[out_spec] + cast_out,
        out_shape=[out_shape] + cast_shapes,
        scratch_shapes=[pltpu.VMEM((rows, kd), BF16)] * 2
        + ([pltpu.VMEM((kd // LANES, rows, LANES), F32)] if pos_major else []),
        compiler_params=_cparams("arbitrary"),
        name="norm_proj",
    )(x, g.reshape(1, kd), w, *(rope if rope_cols else ()), *[c[0] for c in casts])
    return res if casts else res[0]


def _attn_prompt_kernel(q_ref, k_ref, v_ref, *refs, n_cast):
    cast_in, (o_ref, ks_ref, vs_ref) = refs[:n_cast], refs[n_cast:n_cast + 3]
    cast_out = refs[n_cast + 3:2 * n_cast + 3]
    qn_ref, kn_ref, vn_ref, q16_ref, k16_ref, v16_ref, st_ref, acc_ref, mm_ref, ll_ref = refs[2 * n_cast + 3:]
    _cast_blocks(cast_in, cast_out)
    t = q_ref.shape[1]
    n_keep = ks_ref.shape[1]
    ks_ref[0] = k_ref[0, t - n_keep:, :]
    vs_ref[0] = v_ref[0, t - n_keep:, :]
    l4, l16 = t // 4, t // 16
    nblk = t // N_BACK
    piece = N_BACK // 4
    scale = HEAD_DIM ** -0.5
    nt = (((1,), (1,)), ((), ()))
    pad = jnp.zeros((N_BACK, HEAD_DIM), BF16)
    for ref in (kn_ref, vn_ref, k16_ref, v16_ref):
        ref[0:N_BACK, :] = pad

    qn_ref[...] = q_ref[0].astype(BF16)
    kn_ref[N_BACK:, :] = k_ref[0].astype(BF16)
    vn_ref[N_BACK:, :] = v_ref[0].astype(BF16)
    for src, dst, off in ((q_ref, q16_ref, 0), (k_ref, k16_ref, N_BACK), (v_ref, v16_ref, N_BACK)):
        for r4 in range(4):
            st_ref[r4 * l4:(r4 + 1) * l4, :] = src[0, pl.ds(r4, l4, stride=4), :]
        for r4 in range(4):
            for a in range(4):
                r16 = r4 + 4 * a
                dst[off + r16 * l16:off + (r16 + 1) * l16, :] = (
                    st_ref[pl.ds(r4 * l4 + a, l16, stride=4), :].astype(BF16))

    iota = lambda dim: lax.broadcasted_iota(jnp.int32, (N_BACK, 2 * N_BACK), dim)
    qi, kj = iota(0), iota(1)
    dist = N_BACK + qi - kj
    band = (dist >= 0) & (dist <= N_BACK)
    has_prev = kj >= N_BACK
    dist4 = 4 * (qi % piece - kj % (2 * piece) + piece) + (qi // piece - kj // (2 * piece))
    band4 = (dist4 >= 0) & (dist4 <= N_BACK)
    has_prev4 = kj % (2 * piece) >= piece

    def scores(q, k2, valid):
        s = lax.dot_general(q, k2, nt, preferred_element_type=F32) * scale
        return jnp.where(valid, s, -jnp.inf)

    def fresh(s, v2):
        m = jnp.max(s, axis=-1, keepdims=True)
        p = jnp.exp(s - m)
        wide = lambda c: jnp.broadcast_to(c, (N_BACK, HEAD_DIM))
        return wide(m), wide(jnp.sum(p, axis=-1, keepdims=True)), jnp.dot(p.astype(BF16), v2,
                                                                         preferred_element_type=F32)

    def merged(s, v2, m_old, l_old, a_old):
        m_new = jnp.maximum(m_old, jnp.max(s, axis=-1, keepdims=True))
        alpha = jnp.exp(m_old - m_new)
        p = jnp.exp(s - jnp.concatenate([m_new, m_new], axis=1))
        l_new = alpha * l_old + jnp.sum(p, axis=-1, keepdims=True)
        return m_new, l_new, alpha * a_old + jnp.dot(p.astype(BF16), v2, preferred_element_type=F32)

    def block16(b, carry):
        row0 = pl.multiple_of(b * N_BACK, N_BACK)
        rows = pl.ds(row0, N_BACK)
        later = lax.rem(b, l16 // N_BACK) > 0
        s = scores(q16_ref[rows, :], k16_ref[pl.ds(row0, 2 * N_BACK), :], band & (has_prev | later))
        mm_ref[rows, :], ll_ref[rows, :], acc_ref[rows, :] = fresh(s, v16_ref[pl.ds(row0, 2 * N_BACK), :])
        return carry

    lax.fori_loop(0, nblk, block16, 0, unroll=16)

    n_j = l16 // piece

    def block4(b, carry):
        rho, j = b // n_j, lax.rem(b, n_j)
        base = pl.multiple_of(rho * l16 + j * piece, piece)
        q_rows = [pl.ds(base + a * 4 * l16, piece) for a in range(4)]
        k_rows = [pl.ds(base + a * 4 * l16 + N_BACK - piece, 2 * piece) for a in range(4)]
        gather = lambda ref, rows: jnp.concatenate([ref[r, :] for r in rows], axis=0)
        s = scores(gather(q16_ref, q_rows), gather(k16_ref, k_rows), band4 & (has_prev4 | (j > 0)))
        m, l, acc = merged(s, gather(v16_ref, k_rows), gather(mm_ref, q_rows), gather(ll_ref, q_rows),
                           gather(acc_ref, q_rows))
        for a, r in enumerate(q_rows):
            part = slice(a * piece, (a + 1) * piece)
            mm_ref[r, :], ll_ref[r, :], acc_ref[r, :] = m[part], l[part], acc[part]
        return carry

    lax.fori_loop(0, nblk, block4, 0, unroll=16)

    for src, dst in ((acc_ref, o_ref.at[0]), (mm_ref, acc_ref), (ll_ref, mm_ref)):
        for r4 in range(4):
            for a in range(4):
                r16 = r4 + 4 * a
                st_ref[pl.ds(r4 * l4 + a, l16, stride=4), :] = src[r16 * l16:(r16 + 1) * l16, :]
        for r4 in range(4):
            dst[pl.ds(r4, l4, stride=4), :] = st_ref[r4 * l4:(r4 + 1) * l4, :]

    def block1(b, carry):
        row0 = pl.multiple_of(b * N_BACK, N_BACK)
        rows = pl.ds(row0, N_BACK)
        s = scores(qn_ref[rows, :], kn_ref[pl.ds(row0, 2 * N_BACK), :], band & (has_prev | (b > 0)))
        _, l, acc = merged(s, vn_ref[pl.ds(row0, 2 * N_BACK), :], acc_ref[rows, :], mm_ref[rows, :],
                           o_ref[0, rows, :])
        o_ref[0, rows, :] = acc / l
        return carry

    lax.fori_loop(0, nblk, block1, 0, unroll=16)


def _attn_prompt(proj, b, t, n_keep, casts=()):
    assert t % (16 * N_BACK) == 0
    p3 = proj.reshape(b, t, proj.shape[1])
    blk = lambda off: pl.BlockSpec((1, t, HEAD_DIM), lambda i, h: (i, 0, off + h))
    keep = pl.BlockSpec((1, n_keep, HEAD_DIM), lambda i, h: (i, 0, h))
    cast_in, cast_out, cast_shapes = _cast_specs(casts, b * N_HEADS, lambda i, h: i * N_HEADS + h)
    return pl.pallas_call(
        functools.partial(_attn_prompt_kernel, n_cast=len(casts)),
        grid=(b, N_HEADS),
        in_specs=[blk(0), blk(N_HEADS), blk(2 * N_HEADS)] + cast_in,
        out_specs=[pl.BlockSpec((1, t, HEAD_DIM), lambda i, h: (i, 0, h)), keep, keep] + cast_out,
        out_shape=[jax.ShapeDtypeStruct((b, t, ATTN_WIDTH), F32),
                   jax.ShapeDtypeStruct((b, n_keep, ATTN_WIDTH), F32),
                   jax.ShapeDtypeStruct((b, n_keep, ATTN_WIDTH), F32)] + cast_shapes,
        scratch_shapes=[
            pltpu.VMEM((t, HEAD_DIM), BF16),
            pltpu.VMEM((t + N_BACK, HEAD_DIM), BF16),
            pltpu.VMEM((t + N_BACK, HEAD_DIM), BF16),
            pltpu.VMEM((t, HEAD_DIM), BF16),
            pltpu.VMEM((t + N_BACK, HEAD_DIM), BF16),
            pltpu.VMEM((t + N_BACK, HEAD_DIM), BF16),
            pltpu.VMEM((t, HEAD_DIM), F32),
            pltpu.VMEM((t, HEAD_DIM), F32),
            pltpu.VMEM((t, HEAD_DIM), F32),
            pltpu.VMEM((t, HEAD_DIM), F32),
        ],
        compiler_params=_cparams("parallel", "parallel"),
        name="attn_prompt",
    )(p3, p3, p3, *[c[0] for c in casts])


def _attn_sample_kernel(q_ref, kn_ref, vn_ref, kc_ref, vc_ref, o_ref):
    s_len = q_ref.shape[0]
    n_buf = kc_ref.shape[1] // N_HEADS
    scale = HEAD_DIM ** -0.5
    nt = (((1,), (1,)), ((), ()))

    def count(dist):
        c = jnp.zeros(dist.shape, F32)
        for d in DILATIONS:
            hit = (dist >= 0) & (dist <= N_BACK * d) & ((dist & (d - 1)) == 0)
            c = c + jnp.where(hit, 1.0, 0.0)
        return c

    iota = lambda shape, dim: lax.broadcasted_iota(jnp.int32, shape, dim)
    cc = count(n_buf + iota((s_len, n_buf), 0) - iota((s_len, n_buf), 1))
    cn = count(iota((s_len, s_len), 0) - iota((s_len, s_len), 1))
    for h in range(N_HEADS):
        cols = slice(h * HEAD_DIM, (h + 1) * HEAD_DIM)
        head_rows = pl.ds(h, n_buf, stride=N_HEADS)
        q = q_ref[:, cols].astype(BF16)
        sc = lax.dot_general(q, kc_ref[0, head_rows, :].astype(BF16), nt, preferred_element_type=F32) * scale
        sn = lax.dot_general(q, kn_ref[:, cols].astype(BF16), nt, preferred_element_type=F32) * scale
        sc = jnp.where(cc > 0, sc, -jnp.inf)
        sn = jnp.where(cn > 0, sn, -jnp.inf)
        m = jnp.maximum(jnp.max(sc, axis=-1, keepdims=True), jnp.max(sn, axis=-1, keepdims=True))
        pc = cc * jnp.exp(sc - m)
        pn = cn * jnp.exp(sn - m)
        l = jnp.sum(pc, axis=-1, keepdims=True) + jnp.sum(pn, axis=-1, keepdims=True)
        o = (jnp.dot(pc.astype(BF16), vc_ref[0, head_rows, :].astype(BF16), preferred_element_type=F32)
             + jnp.dot(pn.astype(BF16), vn_ref[:, cols].astype(BF16), preferred_element_type=F32))
        o_ref[:, cols] = o / l


def _attn_sample(proj, cache_k, cache_v, b, s_len):
    n_buf = cache_k.shape[1]
    ck = cache_k.reshape(b, n_buf * N_HEADS, HEAD_DIM)
    cv = cache_v.reshape(b, n_buf * N_HEADS, HEAD_DIM)
    new = lambda c: pl.BlockSpec((s_len, ATTN_WIDTH), lambda i: (i, c))
    old = pl.BlockSpec((1, n_buf * N_HEADS, HEAD_DIM), lambda i: (i, 0, 0))
    return pl.pallas_call(
        _attn_sample_kernel,
        grid=(b,),
        in_specs=[new(0), new(1), new(2), old, old],
        out_specs=new(0),
        out_shape=jax.ShapeDtypeStruct((b * s_len, ATTN_WIDTH), F32),
        compiler_params=_cparams("parallel"),
        name="attn_sample",
    )(proj, proj, proj, ck, cv)


def _silu(z):
    return z * jax.nn.sigmoid(z)


def _mix_out_kernel(x_ref, o_ref, g_ref, wza_ref, wgb_ref, wgc_ref, whi_ref, wzb_ref, p2_ref, p1_ref, cw_ref,
                    w_ref, h_ref, cs_ref, tail_ref, *, tiles_per_seq):
    tm = x_ref.shape[0]
    k = p1_ref.shape[0]
    ln = tm // k
    x = x_ref[...]
    hn = _rms_norm(x, g_ref[...]).astype(BF16)
    gate = lambda w: jnp.dot(hn, w[...], preferred_element_type=F32)
    ch = gate(wgc_ref) * gate(whi_ref)
    if k == 1:
        @pl.when(pl.program_id(0) % tiles_per_seq == 0)
        def _():
            tail_ref[0:1, :] = p2_ref[0]
            tail_ref[1:2, :] = p1_ref[0]

        prev2, prev1 = tail_ref[0:1, :], tail_ref[1:2, :]
        tail_ref[...] = ch[tm - 2:tm, :]
        cs_ref[0] = ch[tm - 2:tm, :]
    else:
        per_row = lambda p: jnp.broadcast_to(p[...], (k, ln, CONV_WIDTH)).reshape(tm, CONV_WIDTH)
        prev2, prev1 = per_row(p2_ref), per_row(p1_ref)
        cs_ref[...] = ch.reshape(k, ln, CONV_WIDTH)[:, ln - 2:, :]
    pos = lax.rem(lax.broadcasted_iota(jnp.int32, (tm, 1), 0), ln)
    ch1 = jnp.where(pos == 0, prev1, pltpu.roll(ch, 1, 0))
    ch2 = jnp.where(pos == 0, prev2, jnp.where(pos == 1, prev1, pltpu.roll(ch, 2, 0)))
    conv = ch2 * cw_ref[0:1, :] + ch1 * cw_ref[1:2, :] + ch * cw_ref[2:3, :]
    o_b = (gate(wgb_ref) * conv * _silu(gate(wzb_ref))).astype(BF16)
    y = jnp.dot(o_b, w_ref[ATTN_WIDTH:, :], preferred_element_type=F32)
    o_a = (o_ref[...] * _silu(gate(wza_ref))).astype(BF16)
    y = y + jnp.dot(o_a, w_ref[0:ATTN_WIDTH, :], preferred_element_type=F32)
    h_ref[...] = x + y


def _mix_out(x, o_attn, g, w_gates, conv_init, conv_w, w_out, *, tm, tiles_per_seq=1, seqs_per_tile=1):
    m = x.shape[0]
    nseq = conv_init.shape[0]
    assert tiles_per_seq == 1 or seqs_per_tile == 1
    seq_of = lambda i: i // tiles_per_seq
    before = pl.BlockSpec((seqs_per_tile, 1, CONV_WIDTH), lambda i: (seq_of(i), 0, 0))
    after = pl.BlockSpec((seqs_per_tile, 2, CONV_WIDTH), lambda i: (seq_of(i), 0, 0))
    whole = lambda a: pl.BlockSpec(a.shape, lambda i: (0,) * a.ndim, pipeline_mode=pl.Buffered(1))
    return pl.pallas_call(
        functools.partial(_mix_out_kernel, tiles_per_seq=tiles_per_seq),
        grid=(m // tm,),
        in_specs=[pl.BlockSpec((tm, D_MODEL), lambda i: (i, 0)), pl.BlockSpec((tm, ATTN_WIDTH), lambda i: (i, 0)),
                  pl.BlockSpec((1, D_MODEL), lambda i: (0, 0))] + [whole(w) for w in w_gates]
        + [before, before, pl.BlockSpec((3, CONV_WIDTH), lambda i: (0, 0)), whole(w_out)],
        out_specs=(pl.BlockSpec((tm, D_MODEL), lambda i: (i, 0)), after),
        out_shape=(jax.ShapeDtypeStruct((m, D_MODEL), F32), jax.ShapeDtypeStruct((nseq, 2, CONV_WIDTH), F32)),
        scratch_shapes=[pltpu.VMEM((2, CONV_WIDTH), F32)],
        compiler_params=_cparams("arbitrary"),
        name="mix_out",
    )(x, o_attn, g.reshape(1, -1), *w_gates, conv_init[:, 0:1], conv_init[:, 1:2], conv_w, w_out)


N_SCAN = 7
GROUPS_PER_TILE = LANES // SSM_GROUP


PROMPT_CHUNK = 32
ROW_C = PROMPT_CHUNK + 1
ROW_DBL = PROMPT_CHUNK + 2
TABLE_ROWS = ROW_DBL + N_SCAN


def _s5_disc_kernel(lr_ref, li_ref, ls_ref, tr_ref, ti_ref):
    lr = lr_ref[...]
    li = li_ref[...]
    step = jnp.exp(ls_ref[...])
    mag = jnp.exp(lr * step)
    ar = mag * jnp.cos(li * step)
    ai = mag * jnp.sin(li * step)
    den = lr * lr + li * li
    nr = ar - 1.0
    tr_ref[ROW_C] = (nr * lr + ai * li) / den
    ti_ref[ROW_C] = (ai * lr - nr * li) / den
    pr = jnp.ones_like(ar)
    pi = jnp.zeros_like(ar)
    for tau in range(PROMPT_CHUNK + 1):
        tr_ref[tau] = pr
        ti_ref[tau] = pi
        dr, di = pr, pi
        pr, pi = pr * ar - pi * ai, pr * ai + pi * ar
    for i in range(N_SCAN):
        tr_ref[ROW_DBL + i] = dr
        ti_ref[ROW_DBL + i] = di
        dr, di = dr * dr - di * di, 2.0 * dr * di


def _s5_tables(lam_re, lam_im, log_step):
    g = lam_re.shape[0]
    dup = lambda a: jnp.concatenate([a, a], axis=-1)
    tr, ti = pl.pallas_call(
        _s5_disc_kernel,
        out_shape=(jax.ShapeDtypeStruct((TABLE_ROWS, g, LANES), F32),) * 2,
        name="s5_disc",
    )(dup(lam_re), dup(lam_im), log_step[:, None])
    return tr.transpose(1, 0, 2), ti.transpose(1, 0, 2)


def _split_bf16(x):
    hi = x.astype(BF16)
    return hi, (x - hi.astype(F32)).astype(BF16)


def _dot_split(a, b, b_is_bf16_exact=False):
    dot = lambda x, y: jnp.dot(x, y, preferred_element_type=F32)
    ah, al = _split_bf16(a)
    if b_is_bf16_exact:
        bh = b.astype(BF16)
        return dot(ah, bh) + dot(al, bh)
    bh, bl = _split_bf16(b)
    return dot(ah, bh) + (dot(ah, bl) + dot(al, bh))


def _pad_rows(x, rows):
    if x.shape[0] == rows:
        return x
    return jnp.concatenate([x, jnp.zeros((rows - x.shape[0], x.shape[1]), x.dtype)], axis=0)


def _s5_prep_kernel(tr_ref, ti_ref, br_ref, bi_ref, cr_ref, ci_ref, c2r_ref, c2i_ref,
                    wt_ref, qt_ref, kr_ref, ki_ref, *, chunk):
    n = SSM_GROUP * chunk
    iota = lambda shape, dim: lax.broadcasted_iota(jnp.int32, shape, dim)
    lane = iota((SSM_GROUP, 2 * SSM_STATE), 1)
    e_sel = jnp.where(iota((chunk, n), 0) == chunk - 1 - iota((chunk, n), 1) // SSM_GROUP, 1.0, 0.0)
    tile = jnp.where(iota((SSM_GROUP, n), 0) == iota((SSM_GROUP, n), 1) % SSM_GROUP, 1.0, 0.0)

    def one_group(g, carry):
        tab_r, tab_i = tr_ref[g], ti_ref[g]
        col_r = _pad_rows(tab_r, LANES).T[:SSM_STATE]
        col_i = _pad_rows(tab_i, LANES).T[:SSM_STATE]
        kr_ref[g] = col_r
        ki_ref[g] = col_i
        ccr, cci = col_r[:, ROW_C:ROW_C + 1], col_i[:, ROW_C:ROW_C + 1]
        br, bi = br_ref[g], bi_ref[g]
        bbr = ccr * br - cci * bi
        bbi = ccr * bi + cci * br
        aer = _dot_split(col_r[:, :chunk], e_sel, True)
        aei = _dot_split(col_i[:, :chunk], e_sel, True)
        btr = _dot_split(bbr, tile, True)
        bti = _dot_split(bbi, tile, True)
        ptr = aer * btr - aei * bti
        pti = aer * bti + aei * btr
        wt_ref[g, n:n + SSM_STATE, :] = ptr.astype(BF16)
        wt_ref[g, n + SSM_STATE:n + 2 * SSM_STATE, :] = pti.astype(BF16)
        taps = _dot_split(cr_ref[g], ptr) - _dot_split(ci_ref[g], pti)
        padded = jnp.concatenate([taps, jnp.zeros_like(taps)], axis=1)
        for t in range(chunk):
            sh = SSM_GROUP * (chunk - 1 - t)
            blk = padded if sh == 0 else pltpu.roll(padded, 2 * n - sh, 1)
            wt_ref[g, SSM_GROUP * t:SSM_GROUP * (t + 1), :] = blk[:, :n].astype(BF16)
        c2r, c2i = c2r_ref[g], c2i_ref[g]
        for t in range(chunk):
            ar = tab_r[t + 1:t + 2, :]
            ai = tab_i[t + 1:t + 2, :]
            x1 = jnp.where(lane < SSM_STATE, ar, -ai)
            x2 = jnp.where(lane < SSM_STATE, ai, ar)
            qt_ref[g, SSM_GROUP * t:SSM_GROUP * (t + 1), :] = (c2r * x1 - c2i * x2).astype(BF16)
        return carry

    lax.fori_loop(0, wt_ref.shape[0], one_group, 0, unroll=4)


def _s5_operators(tables, b_re, b_im, c_re, c_im, chunk):
    g = b_re.shape[0]
    n = SSM_GROUP * chunk
    assert chunk <= PROMPT_CHUNK
    dup = lambda a: jnp.concatenate([a, a], axis=-1)
    per_g = lambda *s: pl.BlockSpec((GROUPS_PER_TILE,) + s, lambda i: (i,) + (0,) * len(s))
    return pl.pallas_call(
        functools.partial(_s5_prep_kernel, chunk=chunk),
        grid=(g // GROUPS_PER_TILE,),
        in_specs=[per_g(TABLE_ROWS, LANES), per_g(TABLE_ROWS, LANES),
                  per_g(SSM_STATE, SSM_GROUP), per_g(SSM_STATE, SSM_GROUP),
                  per_g(SSM_GROUP, SSM_STATE), per_g(SSM_GROUP, SSM_STATE),
                  per_g(SSM_GROUP, LANES), per_g(SSM_GROUP, LANES)],
        out_specs=(per_g(n + 2 * SSM_STATE, n), per_g(n, 2 * SSM_STATE),
                   per_g(SSM_STATE, LANES), per_g(SSM_STATE, LANES)),
        out_shape=(jax.ShapeDtypeStruct((g, n + 2 * SSM_STATE, n), BF16),
                   jax.ShapeDtypeStruct((g, n, 2 * SSM_STATE), BF16),
                   jax.ShapeDtypeStruct((g, SSM_STATE, LANES), F32),
                   jax.ShapeDtypeStruct((g, SSM_STATE, LANES), F32)),
        compiler_params=_cparams("parallel"),
        name="s5_prep",
    )(*tables, b_re, b_im, c_re, c_im, dup(c_re), dup(c_im))


def _s5_scan_kernel(u_ref, wt_ref, *rest, chunk, carry, split_ops):
    ws_ref = rest[0] if split_ops else wt_ref
    qt_ref, kr_ref, ki_ref, dsk_ref = rest[split_ops:split_ops + 4]
    rest = rest[split_ops + 4:]
    _s5_scan_body(u_ref, wt_ref, ws_ref, qt_ref, kr_ref, ki_ref, dsk_ref, *rest, chunk=chunk, carry=carry)


def _s5_scan_body(u_ref, wt_ref, ws_ref, qt_ref, kr_ref, ki_ref, dsk_ref, *rest, chunk, carry):
    if carry:
        y_ref, hf_ref, dall_ref, yall_ref = rest
    else:
        h0_ref, y_ref, hf_ref, dall_ref, yall_ref = rest
    nc = u_ref.shape[1]
    ncp = dall_ref.shape[2]
    nseq = ncp // LANES if carry else nc
    n = SSM_GROUP * chunk
    iota = lambda shape, dim: lax.broadcasted_iota(jnp.int32, shape, dim)
    for s in range(chunk):
        dall_ref[s] = _pad_rows(u_ref[s], ncp).T.astype(BF16)
    lane = iota((SSM_STATE, ncp), 1) & (LANES - 1)
    slot = iota((LANES, LANES), 1)
    if not carry:
        h0_all = _pad_rows(jnp.concatenate([h0_ref[:, gp, :] for gp in range(GROUPS_PER_TILE)], axis=0), LANES).T
    acc = jnp.zeros((LANES, LANES), F32)
    for gp in range(GROUPS_PER_TILE):
        rows = slice(SSM_GROUP * gp, SSM_GROUP * (gp + 1))
        d = dall_ref[:, rows, :].reshape(n, ncp)
        ys = jnp.dot(wt_ref[gp], d, preferred_element_type=F32)
        if ws_ref is wt_ref:
            st = ys[n:]
            ys = ys[:n]
        else:
            st = jnp.dot(ws_ref[gp], d, preferred_element_type=F32)
        sr = st[:SSM_STATE]
        si = st[SSM_STATE:]
        if carry:
            for i in range(N_SCAN):
                sh = 1 << i
                ar = kr_ref[gp, :, ROW_DBL + i:ROW_DBL + i + 1]
                ai = ki_ref[gp, :, ROW_DBL + i:ROW_DBL + i + 1]
                pr = jnp.where(lane >= sh, pltpu.roll(sr, sh, 1), 0.0)
                pi = jnp.where(lane >= sh, pltpu.roll(si, sh, 1), 0.0)
                sr, si = sr + ar * pr - ai * pi, si + ar * pi + ai * pr
            hp = jnp.concatenate([jnp.where(lane >= 1, pltpu.roll(sr, 1, 1), 0.0),
                                  jnp.where(lane >= 1, pltpu.roll(si, 1, 1), 0.0)], axis=0)
            for b in range(nseq):
                seq = slice(LANES * b, LANES * (b + 1))
                end = jnp.concatenate([sr[:, seq], si[:, seq]], axis=0)
                acc = jnp.where(slot == 8 * gp + b, pltpu.roll(end, (8 * gp + b + 1) % LANES, 1), acc)
        else:
            hp = h0_all if gp == 0 else pltpu.roll(h0_all, LANES - 8 * gp, 1)
            hp = jnp.where(slot < nseq, hp, 0.0)
            hpr, hpi = hp[:SSM_STATE], hp[SSM_STATE:]
            ar = kr_ref[gp, :, chunk:chunk + 1]
            ai = ki_ref[gp, :, chunk:chunk + 1]
            end = jnp.concatenate([sr + ar * hpr - ai * hpi, si + ar * hpi + ai * hpr], axis=0)
            if gp:
                end = pltpu.roll(end, 8 * gp, 1)
            acc = jnp.where((slot >= 8 * gp) & (slot < 8 * gp + nseq), end, acc)
        y = ys + jnp.dot(qt_ref[gp], hp.astype(BF16), preferred_element_type=F32)
        yall_ref[:, rows, :] = y.reshape(chunk, SSM_GROUP, ncp)
    hf_ref[...] = acc.T[:8 * GROUPS_PER_TILE].reshape(GROUPS_PER_TILE, 8, LANES)
    for t in range(chunk):
        y_ref[t] = yall_ref[t].T[:nc] + dsk_ref[...] * u_ref[t]


def _s5_scan(proj, ops, d_skip, h0, nseq, *, carry):
    wt, qt, kr, ki = ops
    chunk, nc, _ = proj.shape
    n = SSM_GROUP * chunk
    n_full = SSM_GROUP * PROMPT_CHUNK
    assert nc == (nseq * LANES if carry else nseq) and (carry or nseq == 8) and nseq <= 8
    assert n_full % n == 0 and n_full % (2 * SSM_STATE) == 0
    ncp = max(nc, LANES)
    gt = GROUPS_PER_TILE
    tile3 = lambda a, b: pl.BlockSpec((gt, a, b), lambda i: (i, 0, 0))
    lane_tile = pl.BlockSpec((chunk, nc, LANES), lambda i: (0, 0, i))
    split_ops = n != n_full
    if split_ops:
        end_rows = pl.BlockSpec((gt, 2 * SSM_STATE, n), lambda i: (i, n_full // (2 * SSM_STATE), n_full // n - 1))
        op_specs, op_args = [tile3(n, n), end_rows], [wt, wt]
    else:
        op_specs, op_args = [tile3(n + 2 * SSM_STATE, n)], [wt]
    in_specs = [lane_tile] + op_specs + [tile3(n, 2 * SSM_STATE), tile3(SSM_STATE, LANES),
                                         tile3(SSM_STATE, LANES), pl.BlockSpec((1, LANES), lambda i: (0, i))]
    args = [proj] + op_args + [qt, kr, ki, d_skip.reshape(1, -1)]
    if not carry:
        in_specs.append(pl.BlockSpec((nseq, gt, LANES), lambda i: (0, i, 0)))
        args.append(h0)
    return pl.pallas_call(
        functools.partial(_s5_scan_kernel, chunk=chunk, carry=carry, split_ops=split_ops),
        grid=(SSM_GROUPS // gt,),
        in_specs=in_specs,
        out_specs=(lane_tile, tile3(8, LANES)),
        out_shape=(jax.ShapeDtypeStruct((chunk, nc, SSM_GROUPS * SSM_GROUP), F32),
                   jax.ShapeDtypeStruct((SSM_GROUPS, 8, LANES), F32)),
        scratch_shapes=[pltpu.VMEM((chunk, LANES, ncp), BF16), pltpu.VMEM((chunk, LANES, ncp), F32)],
        compiler_params=_cparams("parallel"),
        name="s5_scan",
    )(*args)


def _ssm_out(y, x, gn_ref, wz_ref, wg_ref, bg_ref, wo_ref, gf_ref):
    z = jnp.dot(_rms_norm(x, gn_ref[...]).astype(BF16), wz_ref[...], preferred_element_type=F32)
    g = jax.nn.gelu(y, approximate=True)
    gate = jnp.dot(g.astype(BF16), wg_ref[...], preferred_element_type=F32) + bg_ref[...]
    yy = (g * jax.nn.sigmoid(gate)) * _silu(z)
    return _rms_norm(x + jnp.dot(yy.astype(BF16), wo_ref[...], preferred_element_type=F32), gf_ref[...])


def _glu_out_kernel(y_ref, x_ref, *rest):
    o_ref = rest[-1]
    o_ref[...] = _ssm_out(y_ref[...], x_ref[...], *rest[:-1])


def _glu_out_pm_kernel(y_ref, x_ref, *rest):
    o_ref, slab_ref = rest[-2:]
    p, ct, w = y_ref.shape
    for s in range(p):
        for j in range(w // LANES):
            slab_ref[j, pl.ds(s, ct, stride=p), :] = y_ref[s, :, j * LANES:(j + 1) * LANES]
    y = jnp.concatenate([slab_ref[j] for j in range(w // LANES)], axis=1)
    o_ref[...] = _ssm_out(y, x_ref[...].reshape(ct * p, w), *rest[:-2]).reshape(ct, p, w)


def _glu_out(y, x, g_norm, w_z, w_glu, b_glu, w_out, g_final, *, tm=None, ct=None):
    m, w = x.shape
    vec = pl.BlockSpec((1, w), lambda i: (0, 0))
    mat = pl.BlockSpec((w, w), lambda i: (0, 0), pipeline_mode=pl.Buffered(1))
    params = (g_norm.reshape(1, w), w_z, w_glu, b_glu.reshape(1, w), w_out, g_final.reshape(1, w))
    param_specs = [vec, mat, mat, vec, mat, vec]
    if ct is None:
        row = pl.BlockSpec((tm, w), lambda i: (i, 0))
        return pl.pallas_call(
            _glu_out_kernel,
            grid=(m // tm,),
            in_specs=[row, row] + param_specs,
            out_specs=row,
            out_shape=jax.ShapeDtypeStruct((m, w), F32),
            compiler_params=_cparams("parallel"),
            name="glu_out",
        )(y, x, *params)
    chunk, nc, _ = y.shape
    sg = chunk // POS_TILE
    tok = pl.BlockSpec((ct, POS_TILE, w), lambda i: (i // sg, i % sg, 0))
    return pl.pallas_call(
        _glu_out_pm_kernel,
        grid=((nc // ct) * sg,),
        in_specs=[pl.BlockSpec((POS_TILE, ct, w), lambda i: (i % sg, i // sg, 0)), tok] + param_specs,
        out_specs=tok,
        out_shape=jax.ShapeDtypeStruct((nc, chunk, w), F32),
        scratch_shapes=[pltpu.VMEM((w // LANES, ct * POS_TILE, LANES), F32)],
        compiler_params=_cparams("parallel"),
        name="glu_out_pm",
    )(y, x.reshape(nc, chunk, w), *params).reshape(m, w)


PAST_LEN = 16384


def kernel(x_prompt, x_sample, cache_win_k, cache_win_v, state_conv, state_ssm_re, state_ssm_im, attn_norm, w_in_ab, conv_w, w_out_ab, ssm_norm, w_in_c, lam_re, lam_im, log_step, b_re, b_im, c_re, c_im, d_skip, w_glu, b_glu, w_out_c, final_norm):
    bp, tp, _ = x_prompt.shape
    bs, ts, _ = x_sample.shape
    n_keep = min(2048, tp)
    xp = x_prompt.reshape(bp * tp, D_MODEL)
    xs = x_sample.reshape(bs * ts, D_MODEL)

    qkv_cols = 3 * ATTN_WIDTH
    w_qkv = w_in_ab[0, :, :qkv_cols].astype(BF16)
    gate_casts = [(w_in_ab[0], CONV_WIDTH, qkv_cols // CONV_WIDTH + c) for c in range(5)]
    proj_p, *w_gates, w_out0 = _norm_proj(
        xp, attn_norm[0], w_qkv, tm=512, rope=_rope_tables(tp, tp, 0), rope_cols=2 * ATTN_WIDTH,
        tiles_per_seq=tp // 512, casts=gate_casts + [(w_out_ab[0], D_MODEL, 0)])
    proj_s = _norm_proj(xs, attn_norm[0], w_qkv, tm=bs * ts, rope=_rope_tables(bs * ts, ts, PAST_LEN),
                        rope_cols=2 * ATTN_WIDTH)
    o_p, k_p, v_p, w_u, w_z, w_glu1, w_out1 = _attn_prompt(
        proj_p, bp, tp, n_keep,
        casts=[(w_in_c[0], D_MODEL, 0), (w_in_c[0], D_MODEL, 1), (w_glu[0], D_MODEL, 0), (w_out_c[0], D_MODEL, 0)])
    o_p = o_p.reshape(bp * tp, ATTN_WIDTH)
    k_p = k_p.reshape(1, bp, n_keep, N_HEADS, HEAD_DIM)
    v_p = v_p.reshape(1, bp, n_keep, N_HEADS, HEAD_DIM)
    o_s = _attn_sample(proj_s, cache_win_k[0], cache_win_v[0], bs, ts)
    h1_p, conv_p = _mix_out(xp, o_p, attn_norm[0], w_gates, jnp.zeros((bp, 2, CONV_WIDTH), F32), conv_w[0],
                            w_out0, tm=256, tiles_per_seq=tp // 256)
    h1_s, conv_s = _mix_out(xs, o_s, attn_norm[0], w_gates, state_conv[0], conv_w[0], w_out0,
                            tm=bs * ts, seqs_per_tile=bs)
    k_s = proj_s[:, ATTN_WIDTH:2 * ATTN_WIDTH].reshape(1, bs, ts, N_HEADS, HEAD_DIM)
    v_s = proj_s[:, 2 * ATTN_WIDTH:3 * ATTN_WIDTH].reshape(1, bs, ts, N_HEADS, HEAD_DIM)

    out_params = (ssm_norm[0], w_z, w_glu1, b_glu[0], w_out1, final_norm)
    tables = _s5_tables(lam_re[0], lam_im[0], log_step[0])
    ops = _s5_operators(tables, b_re[0], b_im[0], c_re[0], c_im[0], PROMPT_CHUNK)
    u_p = _norm_proj(h1_p, ssm_norm[0], w_u, chunk=PROMPT_CHUNK, ct=64)
    y_p, hf_p = _s5_scan(u_p, ops, d_skip[0], None, bp, carry=True)
    out_p = _glu_out(y_p, h1_p, *out_params, ct=32)
    u_s = _norm_proj(h1_s, ssm_norm[0], w_u, tm=bs * ts)
    h0 = jnp.concatenate([state_ssm_re[0], state_ssm_im[0]], axis=-1)
    y_s, hf_s = _s5_scan(u_s.reshape(bs, ts, -1).transpose(1, 0, 2), ops, d_skip[0], h0, bs, carry=False)
    out_s = _glu_out(y_s.transpose(1, 0, 2).reshape(bs * ts, D_MODEL), h1_s, *out_params, tm=bs * ts)
    hf_p = hf_p[:, :bp].transpose(1, 0, 2)[None]
    hf_s = hf_s[:, :bs].transpose(1, 0, 2)[None]
    return (out_p.reshape(bp, tp, D_MODEL), out_s.reshape(bs, ts, D_MODEL),
            k_p, v_p, conv_p[None], hf_p[..., :SSM_STATE], hf_p[..., SSM_STATE:],
            k_s, v_s, conv_s[None], hf_s[..., :SSM_STATE], hf_s[..., SSM_STATE:])
```

```python
import functools
import math

import jax
import jax.numpy as jnp
from jax import lax
from jax.experimental import pallas as pl
from jax.experimental.pallas import tpu as pltpu

D_MODEL = 2048
HEAD_DIM = 128
N_HEADS = 8
ATTN_WIDTH = 1024
CONV_WIDTH = 1024
DILATIONS = (1, 4, 16)
N_BACK = 128
ROPE_THETA = 10000.0
RMS_EPS = 1e-6
SSM_GROUP = 16
SSM_GROUPS = 128
SSM_STATE = 64
LANES = 128
VMEM_LIMIT = 56 * 1024 * 1024

F32 = jnp.float32
BF16 = jnp.bfloat16


def _cparams(*sem):
    return pltpu.CompilerParams(dimension_semantics=sem, vmem_limit_bytes=VMEM_LIMIT)


def _rope_table_kernel(inv_ref, cos_ref, sin_ref, *, period, offset):
    rows = cos_ref.shape[0]
    r = lax.broadcasted_iota(jnp.int32, (rows, LANES), 0) + pl.program_id(0) * rows
    pos = (offset + lax.rem(r, period)).astype(F32)
    ang = pos * inv_ref[...]
    lane = lax.broadcasted_iota(jnp.int32, (rows, LANES), 1)
    cos_ref[...] = jnp.cos(ang)
    sin_ref[...] = jnp.where(lane < HEAD_DIM // 2, -1.0, 1.0) * jnp.sin(ang)


def _rope_tables(rows, period, offset):
    half = HEAD_DIM // 2
    inv = ROPE_THETA ** (-jnp.arange(half, dtype=F32) / half)
    inv2 = jnp.concatenate([inv, inv])[None, :]
    tr = min(rows, 256)
    return pl.pallas_call(
        functools.partial(_rope_table_kernel, period=period, offset=offset),
        grid=(rows // tr,),
        in_specs=[pl.BlockSpec((1, LANES), lambda i: (0, 0))],
        out_specs=(pl.BlockSpec((tr, LANES), lambda i: (i, 0)),) * 2,
        out_shape=(jax.ShapeDtypeStruct((rows, LANES), F32),) * 2,
        compiler_params=_cparams("parallel"),
        name="rope_table",
    )(inv2)


def _rms_norm(x, g):
    return x * lax.rsqrt(jnp.mean(x * x, axis=-1, keepdims=True) + RMS_EPS) * g


POS_TILE = 8
COL_SLAB = 256


def _cast_blocks(cast_in, cast_out):
    for src, dst in zip(cast_in, cast_out):
        dst[...] = src[...].astype(BF16)


def _norm_proj_kernel(*refs, rope_cols, n_cast):
    x_ref, g_ref, w_ref = refs[:3]
    pos = 3
    if rope_cols:
        cos_ref, sin_ref = refs[3:5]
        pos = 5
    cast_in, o_ref, cast_out = refs[pos:pos + n_cast], refs[pos + n_cast], refs[pos + n_cast + 1:pos + 2 * n_cast + 1]
    hn0, hn1 = refs[pos + 2 * n_cast + 1:]
    k = pl.program_id(0)

    def normalise(dst):
        x = x_ref[...]
        dst[...] = _rms_norm(x.reshape(dst.shape), g_ref[...]).astype(BF16)
        _cast_blocks(cast_in, cast_out)

    def project(src):
        hn = src[...]
        for c in range(w_ref.shape[1] // COL_SLAB):
            cols = slice(c * COL_SLAB, (c + 1) * COL_SLAB)
            acc = jnp.dot(hn, w_ref[:, cols], preferred_element_type=F32)
            if c * COL_SLAB < rope_cols:
                for h in range(COL_SLAB // HEAD_DIM):
                    xh = acc[:, h * HEAD_DIM:(h + 1) * HEAD_DIM]
                    lanes = slice(c * COL_SLAB + h * HEAD_DIM, c * COL_SLAB + (h + 1) * HEAD_DIM)
                    o_ref[:, lanes] = xh * cos_ref[...] + pltpu.roll(xh, HEAD_DIM // 2, 1) * sin_ref[...]
            else:
                o_ref[:, cols] = acc

    def step(cur, prev):
        normalise(cur)
        project(prev)

    pl.when(k == 0)(lambda: normalise(hn0))
    pl.when((k > 0) & (k % 2 == 0))(lambda: step(hn0, hn1))
    pl.when(k % 2 == 1)(lambda: step(hn1, hn0))


def _cast_specs(casts, n_blocks, index):
    ins, outs, shapes = [], [], []
    for arr, width, col in casts:
        rows = arr.shape[0] // n_blocks
        ins.append(pl.BlockSpec((rows, width), lambda *k, col=col: (index(*k), col)))
        outs.append(pl.BlockSpec((rows, width), lambda *k: (index(*k), 0)))
        shapes.append(jax.ShapeDtypeStruct((arr.shape[0], width), BF16))
    return ins, outs, shapes


def _norm_proj(x, g, w, *, tm=None, chunk=None, ct=None, rope=None, rope_cols=0, tiles_per_seq=1, casts=()):
    m, kd = x.shape
    n = w.shape[1]
    cur = lambda k: jnp.minimum(k, t - 1)
    prv = lambda k: jnp.maximum(k - 1, 0)
    if chunk is None:
        t, rows = m // tm, tm
        x_spec = pl.BlockSpec((tm, kd), lambda k: (cur(k), 0))
        out_spec = pl.BlockSpec((tm, n), lambda k: (prv(k), 0))
        out_shape = jax.ShapeDtypeStruct((m, n), F32)
    else:
        nc, sg = m // chunk, chunk // POS_TILE
        t, rows = (nc // ct) * sg, ct * POS_TILE
        x = x.reshape(nc, chunk, kd)
        x_spec = pl.BlockSpec((ct, POS_TILE, kd), lambda k: (cur(k) // sg, cur(k) % sg, 0))
        out_spec = pl.BlockSpec((None, rows, n), lambda k: (prv(k) % sg, prv(k) // sg, 0))
        out_shape = jax.ShapeDtypeStruct((sg, nc * POS_TILE, n), F32)
    table = pl.BlockSpec((rows, LANES), lambda k: (prv(k) % tiles_per_seq, 0))
    cast_in, cast_out, cast_shapes = _cast_specs(casts, t, cur)
    res = pl.pallas_call(
        functools.partial(_norm_proj_kernel, rope_cols=rope_cols, n_cast=len(casts)),
        grid=(t + 1,),
        in_specs=[x_spec, pl.BlockSpec((1, kd), lambda k: (0, 0)),
                  pl.BlockSpec((kd, n), lambda k: (0, 0), pipeline_mode=pl.Buffered(1))]
        + ([table, table] if rope_cols else []) + cast_in,
        out_specs=[out_spec] + cast_out,
        out_shape=[out_shape] + cast_shapes,
        scratch_shapes=[pltpu.VMEM((rows, kd), BF16)] * 2,
        compiler_params=_cparams("arbitrary"),
        name="norm_proj",
    )(x, g.reshape(1, kd), w, *(rope if rope_cols else ()), *[c[0] for c in casts])
    return res if casts else res[0]


def _attn_prompt_kernel(q_ref, k_ref, v_ref, *refs, n_cast):
    cast_in, (o_ref, ks_ref, vs_ref) = refs[:n_cast], refs[n_cast:n_cast + 3]
    cast_out = refs[n_cast + 3:2 * n_cast + 3]
    qn_ref, kn_ref, vn_ref, q16_ref, k16_ref, v16_ref, st_ref, acc_ref, mm_ref, ll_ref = refs[2 * n_cast + 3:]
    _cast_blocks(cast_in, cast_out)
    t = q_ref.shape[1]
    n_keep = ks_ref.shape[1]
    ks_ref[0] = k_ref[0, t - n_keep:, :]
    vs_ref[0] = v_ref[0, t - n_keep:, :]
    l4, l16 = t // 4, t // 16
    nblk = t // N_BACK
    piece = N_BACK // 4
    scale = HEAD_DIM ** -0.5
    nt = (((1,), (1,)), ((), ()))
    pad = jnp.zeros((N_BACK, HEAD_DIM), BF16)
    for ref in (kn_ref, vn_ref, k16_ref, v16_ref):
        ref[0:N_BACK, :] = pad

    qn_ref[...] = q_ref[0].astype(BF16)
    kn_ref[N_BACK:, :] = k_ref[0].astype(BF16)
    vn_ref[N_BACK:, :] = v_ref[0].astype(BF16)
    for src, dst, off in ((q_ref, q16_ref, 0), (k_ref, k16_ref, N_BACK), (v_ref, v16_ref, N_BACK)):
        for r4 in range(4):
            st_ref[r4 * l4:(r4 + 1) * l4, :] = src[0, pl.ds(r4, l4, stride=4), :]
        for r4 in range(4):
            for a in range(4):
                r16 = r4 + 4 * a
                dst[off + r16 * l16:off + (r16 + 1) * l16, :] = (
                    st_ref[pl.ds(r4 * l4 + a, l16, stride=4), :].astype(BF16))

    iota = lambda dim: lax.broadcasted_iota(jnp.int32, (N_BACK, 2 * N_BACK), dim)
    qi, kj = iota(0), iota(1)
    dist = N_BACK + qi - kj
    band = (dist >= 0) & (dist <= N_BACK)
    has_prev = kj >= N_BACK
    dist4 = 4 * (qi % piece - kj % (2 * piece) + piece) + (qi // piece - kj // (2 * piece))
    band4 = (dist4 >= 0) & (dist4 <= N_BACK)
    has_prev4 = kj % (2 * piece) >= piece

    def scores(q, k2, valid):
        s = lax.dot_general(q, k2, nt, preferred_element_type=F32) * scale
        return jnp.where(valid, s, -jnp.inf)

    def fresh(s, v2):
        m = jnp.max(s, axis=-1, keepdims=True)
        p = jnp.exp(s - m)
        wide = lambda c: jnp.broadcast_to(c, (N_BACK, HEAD_DIM))
        return wide(m), wide(jnp.sum(p, axis=-1, keepdims=True)), jnp.dot(p.astype(BF16), v2,
                                                                         preferred_element_type=F32)

    def merged(s, v2, m_old, l_old, a_old):
        m_new = jnp.maximum(m_old, jnp.max(s, axis=-1, keepdims=True))
        alpha = jnp.exp(m_old - m_new)
        p = jnp.exp(s - jnp.concatenate([m_new, m_new], axis=1))
        l_new = alpha * l_old + jnp.sum(p, axis=-1, keepdims=True)
        return m_new, l_new, alpha * a_old + jnp.dot(p.astype(BF16), v2, preferred_element_type=F32)

    def block16(b, carry):
        row0 = pl.multiple_of(b * N_BACK, N_BACK)
        rows = pl.ds(row0, N_BACK)
        later = lax.rem(b, l16 // N_BACK) > 0
        s = scores(q16_ref[rows, :], k16_ref[pl.ds(row0, 2 * N_BACK), :], band & (has_prev | later))
        mm_ref[rows, :], ll_ref[rows, :], acc_ref[rows, :] = fresh(s, v16_ref[pl.ds(row0, 2 * N_BACK), :])
        return carry

    lax.fori_loop(0, nblk, block16, 0, unroll=16)

    n_j = l16 // piece

    def block4(b, carry):
        rho, j = b // n_j, lax.rem(b, n_j)
        base = pl.multiple_of(rho * l16 + j * piece, piece)
        q_rows = [pl.ds(base + a * 4 * l16, piece) for a in range(4)]
        k_rows = [pl.ds(base + a * 4 * l16 + N_BACK - piece, 2 * piece) for a in range(4)]
        gather = lambda ref, rows: jnp.concatenate([ref[r, :] for r in rows], axis=0)
        s = scores(gather(q16_ref, q_rows), gather(k16_ref, k_rows), band4 & (has_prev4 | (j > 0)))
        m, l, acc = merged(s, gather(v16_ref, k_rows), gather(mm_ref, q_rows), gather(ll_ref, q_rows),
                           gather(acc_ref, q_rows))
        for a, r in enumerate(q_rows):
            part = slice(a * piece, (a + 1) * piece)
            mm_ref[r, :], ll_ref[r, :], acc_ref[r, :] = m[part], l[part], acc[part]
        return carry

    lax.fori_loop(0, nblk, block4, 0, unroll=16)

    for src, dst in ((acc_ref, o_ref.at[0]), (mm_ref, acc_ref), (ll_ref, mm_ref)):
        for r4 in range(4):
            for a in range(4):
                r16 = r4 + 4 * a
                st_ref[pl.ds(r4 * l4 + a, l16, stride=4), :] = src[r16 * l16:(r16 + 1) * l16, :]
        for r4 in range(4):
            dst[pl.ds(r4, l4, stride=4), :] = st_ref[r4 * l4:(r4 + 1) * l4, :]

    def block1(b, carry):
        row0 = pl.multiple_of(b * N_BACK, N_BACK)
        rows = pl.ds(row0, N_BACK)
        s = scores(qn_ref[rows, :], kn_ref[pl.ds(row0, 2 * N_BACK), :], band & (has_prev | (b > 0)))
        _, l, acc = merged(s, vn_ref[pl.ds(row0, 2 * N_BACK), :], acc_ref[rows, :], mm_ref[rows, :],
                           o_ref[0, rows, :])
        o_ref[0, rows, :] = acc / l
        return carry

    lax.fori_loop(0, nblk, block1, 0, unroll=16)


def _attn_prompt(proj, b, t, n_keep, casts=()):
    assert t % (16 * N_BACK) == 0
    p3 = proj.reshape(b, t, proj.shape[1])
    blk = lambda off: pl.BlockSpec((1, t, HEAD_DIM), lambda i, h: (i, 0, off + h))
    keep = pl.BlockSpec((1, n_keep, HEAD_DIM), lambda i, h: (i, 0, h))
    cast_in, cast_out, cast_shapes = _cast_specs(casts, b * N_HEADS, lambda i, h: i * N_HEADS + h)
    return pl.pallas_call(
        functools.partial(_attn_prompt_kernel, n_cast=len(casts)),
        grid=(b, N_HEADS),
        in_specs=[blk(0), blk(N_HEADS), blk(2 * N_HEADS)] + cast_in,
        out_specs=[pl.BlockSpec((1, t, HEAD_DIM), lambda i, h: (i, 0, h)), keep, keep] + cast_out,
        out_shape=[jax.ShapeDtypeStruct((b, t, ATTN_WIDTH), F32),
                   jax.ShapeDtypeStruct((b, n_keep, ATTN_WIDTH), F32),
                   jax.ShapeDtypeStruct((b, n_keep, ATTN_WIDTH), F32)] + cast_shapes,
        scratch_shapes=[
            pltpu.VMEM((t, HEAD_DIM), BF16),
            pltpu.VMEM((t + N_BACK, HEAD_DIM), BF16),
            pltpu.VMEM((t + N_BACK, HEAD_DIM), BF16),
            pltpu.VMEM((t, HEAD_DIM), BF16),
            pltpu.VMEM((t + N_BACK, HEAD_DIM), BF16),
            pltpu.VMEM((t + N_BACK, HEAD_DIM), BF16),
            pltpu.VMEM((t, HEAD_DIM), F32),
            pltpu.VMEM((t, HEAD_DIM), F32),
            pltpu.VMEM((t, HEAD_DIM), F32),
            pltpu.VMEM((t, HEAD_DIM), F32),
        ],
        compiler_params=_cparams("parallel", "parallel"),
        name="attn_prompt",
    )(p3, p3, p3, *[c[0] for c in casts])


def _attn_sample_kernel(q_ref, kn_ref, vn_ref, kc_ref, vc_ref, o_ref):
    s_len = q_ref.shape[0]
    n_buf = kc_ref.shape[1] // N_HEADS
    scale = HEAD_DIM ** -0.5
    nt = (((1,), (1,)), ((), ()))

    def count(dist):
        c = jnp.zeros(dist.shape, F32)
        for d in DILATIONS:
            hit = (dist >= 0) & (dist <= N_BACK * d) & ((dist & (d - 1)) == 0)
            c = c + jnp.where(hit, 1.0, 0.0)
        return c

    iota = lambda shape, dim: lax.broadcasted_iota(jnp.int32, shape, dim)
    cc = count(n_buf + iota((s_len, n_buf), 0) - iota((s_len, n_buf), 1))
    cn = count(iota((s_len, s_len), 0) - iota((s_len, s_len), 1))
    for h in range(N_HEADS):
        cols = slice(h * HEAD_DIM, (h + 1) * HEAD_DIM)
        head_rows = pl.ds(h, n_buf, stride=N_HEADS)
        q = q_ref[:, cols].astype(BF16)
        sc = lax.dot_general(q, kc_ref[0, head_rows, :].astype(BF16), nt, preferred_element_type=F32) * scale
        sn = lax.dot_general(q, kn_ref[:, cols].astype(BF16), nt, preferred_element_type=F32) * scale
        sc = jnp.where(cc > 0, sc, -jnp.inf)
        sn = jnp.where(cn > 0, sn, -jnp.inf)
        m = jnp.maximum(jnp.max(sc, axis=-1, keepdims=True), jnp.max(sn, axis=-1, keepdims=True))
        pc = cc * jnp.exp(sc - m)
        pn = cn * jnp.exp(sn - m)
        l = jnp.sum(pc, axis=-1, keepdims=True) + jnp.sum(pn, axis=-1, keepdims=True)
        o = (jnp.dot(pc.astype(BF16), vc_ref[0, head_rows, :].astype(BF16), preferred_element_type=F32)
             + jnp.dot(pn.astype(BF16), vn_ref[:, cols].astype(BF16), preferred_element_type=F32))
        o_ref[:, cols] = o / l


def _attn_sample(proj, cache_k, cache_v, b, s_len):
    n_buf = cache_k.shape[1]
    ck = cache_k.reshape(b, n_buf * N_HEADS, HEAD_DIM)
    cv = cache_v.reshape(b, n_buf * N_HEADS, HEAD_DIM)
    new = lambda c: pl.BlockSpec((s_len, ATTN_WIDTH), lambda i: (i, c))
    old = pl.BlockSpec((1, n_buf * N_HEADS, HEAD_DIM), lambda i: (i, 0, 0))
    return pl.pallas_call(
        _attn_sample_kernel,
        grid=(b,),
        in_specs=[new(0), new(1), new(2), old, old],
        out_specs=new(0),
        out_shape=jax.ShapeDtypeStruct((b * s_len, ATTN_WIDTH), F32),
        compiler_params=_cparams("parallel"),
        name="attn_sample",
    )(proj, proj, proj, ck, cv)


def _silu(z):
    return z * jax.nn.sigmoid(z)


def _mix_out_kernel(x_ref, o_ref, g_ref, wza_ref, wgb_ref, wgc_ref, whi_ref, wzb_ref, p2_ref, p1_ref, cw_ref,
                    w_ref, h_ref, cs_ref, tail_ref, *, tiles_per_seq):
    tm = x_ref.shape[0]
    k = p1_ref.shape[0]
    ln = tm // k
    x = x_ref[...]
    hn = _rms_norm(x, g_ref[...]).astype(BF16)
    gate = lambda w: jnp.dot(hn, w[...], preferred_element_type=F32)
    ch = gate(wgc_ref) * gate(whi_ref)
    if k == 1:
        @pl.when(pl.program_id(0) % tiles_per_seq == 0)
        def _():
            tail_ref[0:1, :] = p2_ref[0]
            tail_ref[1:2, :] = p1_ref[0]

        prev2, prev1 = tail_ref[0:1, :], tail_ref[1:2, :]
        tail_ref[...] = ch[tm - 2:tm, :]
        cs_ref[0] = ch[tm - 2:tm, :]
    else:
        per_row = lambda p: jnp.broadcast_to(p[...], (k, ln, CONV_WIDTH)).reshape(tm, CONV_WIDTH)
        prev2, prev1 = per_row(p2_ref), per_row(p1_ref)
        cs_ref[...] = ch.reshape(k, ln, CONV_WIDTH)[:, ln - 2:, :]
    pos = lax.rem(lax.broadcasted_iota(jnp.int32, (tm, 1), 0), ln)
    ch1 = jnp.where(pos == 0, prev1, pltpu.roll(ch, 1, 0))
    ch2 = jnp.where(pos == 0, prev2, jnp.where(pos == 1, prev1, pltpu.roll(ch, 2, 0)))
    conv = ch2 * cw_ref[0:1, :] + ch1 * cw_ref[1:2, :] + ch * cw_ref[2:3, :]
    o_b = (gate(wgb_ref) * conv * _silu(gate(wzb_ref))).astype(BF16)
    y = jnp.dot(o_b, w_ref[ATTN_WIDTH:, :], preferred_element_type=F32)
    o_a = (o_ref[...] * _silu(gate(wza_ref))).astype(BF16)
    y = y + jnp.dot(o_a, w_ref[0:ATTN_WIDTH, :], preferred_element_type=F32)
    h_ref[...] = x + y


def _mix_out(x, o_attn, g, w_gates, conv_init, conv_w, w_out, *, tm, tiles_per_seq=1, seqs_per_tile=1):
    m = x.shape[0]
    nseq = conv_init.shape[0]
    assert tiles_per_seq == 1 or seqs_per_tile == 1
    seq_of = lambda i: i // tiles_per_seq
    before = pl.BlockSpec((seqs_per_tile, 1, CONV_WIDTH), lambda i: (seq_of(i), 0, 0))
    after = pl.BlockSpec((seqs_per_tile, 2, CONV_WIDTH), lambda i: (seq_of(i), 0, 0))
    whole = lambda a: pl.BlockSpec(a.shape, lambda i: (0,) * a.ndim, pipeline_mode=pl.Buffered(1))
    return pl.pallas_call(
        functools.partial(_mix_out_kernel, tiles_per_seq=tiles_per_seq),
        grid=(m // tm,),
        in_specs=[pl.BlockSpec((tm, D_MODEL), lambda i: (i, 0)), pl.BlockSpec((tm, ATTN_WIDTH), lambda i: (i, 0)),
                  pl.BlockSpec((1, D_MODEL), lambda i: (0, 0))] + [whole(w) for w in w_gates]
        + [before, before, pl.BlockSpec((3, CONV_WIDTH), lambda i: (0, 0)), whole(w_out)],
        out_specs=(pl.BlockSpec((tm, D_MODEL), lambda i: (i, 0)), after),
        out_shape=(jax.ShapeDtypeStruct((m, D_MODEL), F32), jax.ShapeDtypeStruct((nseq, 2, CONV_WIDTH), F32)),
        scratch_shapes=[pltpu.VMEM((2, CONV_WIDTH), F32)],
        compiler_params=_cparams("arbitrary"),
        name="mix_out",
    )(x, o_attn, g.reshape(1, -1), *w_gates, conv_init[:, 0:1], conv_init[:, 1:2], conv_w, w_out)


N_SCAN = 7
GROUPS_PER_TILE = LANES // SSM_GROUP


PROMPT_CHUNK = 32
ROW_C = PROMPT_CHUNK + 1
ROW_DBL = PROMPT_CHUNK + 2
TABLE_ROWS = ROW_DBL + N_SCAN


def _s5_disc_kernel(lr_ref, li_ref, ls_ref, tr_ref, ti_ref):
    lr = lr_ref[...]
    li = li_ref[...]
    step = jnp.exp(ls_ref[...])
    mag = jnp.exp(lr * step)
    ar = mag * jnp.cos(li * step)
    ai = mag * jnp.sin(li * step)
    den = lr * lr + li * li
    nr = ar - 1.0
    tr_ref[ROW_C] = (nr * lr + ai * li) / den
    ti_ref[ROW_C] = (ai * lr - nr * li) / den
    pr = jnp.ones_like(ar)
    pi = jnp.zeros_like(ar)
    for tau in range(PROMPT_CHUNK + 1):
        tr_ref[tau] = pr
        ti_ref[tau] = pi
        dr, di = pr, pi
        pr, pi = pr * ar - pi * ai, pr * ai + pi * ar
    for i in range(N_SCAN):
        tr_ref[ROW_DBL + i] = dr
        ti_ref[ROW_DBL + i] = di
        dr, di = dr * dr - di * di, 2.0 * dr * di


def _s5_tables(lam_re, lam_im, log_step):
    g = lam_re.shape[0]
    dup = lambda a: jnp.concatenate([a, a], axis=-1)
    tr, ti = pl.pallas_call(
        _s5_disc_kernel,
        out_shape=(jax.ShapeDtypeStruct((TABLE_ROWS, g, LANES), F32),) * 2,
        name="s5_disc",
    )(dup(lam_re), dup(lam_im), log_step[:, None])
    return tr.transpose(1, 0, 2), ti.transpose(1, 0, 2)


def _split_bf16(x):
    hi = x.astype(BF16)
    return hi, (x - hi.astype(F32)).astype(BF16)


def _dot_split(a, b, b_is_bf16_exact=False):
    dot = lambda x, y: jnp.dot(x, y, preferred_element_type=F32)
    ah, al = _split_bf16(a)
    if b_is_bf16_exact:
        bh = b.astype(BF16)
        return dot(ah, bh) + dot(al, bh)
    bh, bl = _split_bf16(b)
    return dot(ah, bh) + (dot(ah, bl) + dot(al, bh))


def _pad_rows(x, rows):
    if x.shape[0] == rows:
        return x
    return jnp.concatenate([x, jnp.zeros((rows - x.shape[0], x.shape[1]), x.dtype)], axis=0)


def _s5_prep_kernel(tr_ref, ti_ref, br_ref, bi_ref, cr_ref, ci_ref, c2r_ref, c2i_ref,
                    wt_ref, qt_ref, kr_ref, ki_ref, *, chunk):
    n = SSM_GROUP * chunk
    iota = lambda shape, dim: lax.broadcasted_iota(jnp.int32, shape, dim)
    lane = iota((SSM_GROUP, 2 * SSM_STATE), 1)
    e_sel = jnp.where(iota((chunk, n), 0) == chunk - 1 - iota((chunk, n), 1) // SSM_GROUP, 1.0, 0.0)
    tile = jnp.where(iota((SSM_GROUP, n), 0) == iota((SSM_GROUP, n), 1) % SSM_GROUP, 1.0, 0.0)

    def one_group(g, carry):
        tab_r, tab_i = tr_ref[g], ti_ref[g]
        col_r = _pad_rows(tab_r, LANES).T[:SSM_STATE]
        col_i = _pad_rows(tab_i, LANES).T[:SSM_STATE]
        kr_ref[g] = col_r
        ki_ref[g] = col_i
        ccr, cci = col_r[:, ROW_C:ROW_C + 1], col_i[:, ROW_C:ROW_C + 1]
        br, bi = br_ref[g], bi_ref[g]
        bbr = ccr * br - cci * bi
        bbi = ccr * bi + cci * br
        aer = _dot_split(col_r[:, :chunk], e_sel, True)
        aei = _dot_split(col_i[:, :chunk], e_sel, True)
        btr = _dot_split(bbr, tile, True)
        bti = _dot_split(bbi, tile, True)
        ptr = aer * btr - aei * bti
        pti = aer * bti + aei * btr
        wt_ref[g, n:n + SSM_STATE, :] = ptr.astype(BF16)
        wt_ref[g, n + SSM_STATE:n + 2 * SSM_STATE, :] = pti.astype(BF16)
        taps = _dot_split(cr_ref[g], ptr) - _dot_split(ci_ref[g], pti)
        padded = jnp.concatenate([taps, jnp.zeros_like(taps)], axis=1)
        for t in range(chunk):
            sh = SSM_GROUP * (chunk - 1 - t)
            blk = padded if sh == 0 else pltpu.roll(padded, 2 * n - sh, 1)
            wt_ref[g, SSM_GROUP * t:SSM_GROUP * (t + 1), :] = blk[:, :n].astype(BF16)
        c2r, c2i = c2r_ref[g], c2i_ref[g]
        for t in range(chunk):
            ar = tab_r[t + 1:t + 2, :]
            ai = tab_i[t + 1:t + 2, :]
            x1 = jnp.where(lane < SSM_STATE, ar, -ai)
            x2 = jnp.where(lane < SSM_STATE, ai, ar)
            qt_ref[g, SSM_GROUP * t:SSM_GROUP * (t + 1), :] = (c2r * x1 - c2i * x2).astype(BF16)
        return carry

    lax.fori_loop(0, wt_ref.shape[0], one_group, 0, unroll=4)


def _s5_operators(tables, b_re, b_im, c_re, c_im, chunk):
    g = b_re.shape[0]
    n = SSM_GROUP * chunk
    assert chunk <= PROMPT_CHUNK
    dup = lambda a: jnp.concatenate([a, a], axis=-1)
    per_g = lambda *s: pl.BlockSpec((GROUPS_PER_TILE,) + s, lambda i: (i,) + (0,) * len(s))
    return pl.pallas_call(
        functools.partial(_s5_prep_kernel, chunk=chunk),
        grid=(g // GROUPS_PER_TILE,),
        in_specs=[per_g(TABLE_ROWS, LANES), per_g(TABLE_ROWS, LANES),
                  per_g(SSM_STATE, SSM_GROUP), per_g(SSM_STATE, SSM_GROUP),
                  per_g(SSM_GROUP, SSM_STATE), per_g(SSM_GROUP, SSM_STATE),
                  per_g(SSM_GROUP, LANES), per_g(SSM_GROUP, LANES)],
        out_specs=(per_g(n + 2 * SSM_STATE, n), per_g(n, 2 * SSM_STATE),
                   per_g(SSM_STATE, LANES), per_g(SSM_STATE, LANES)),
        out_shape=(jax.ShapeDtypeStruct((g, n + 2 * SSM_STATE, n), BF16),
                   jax.ShapeDtypeStruct((g, n, 2 * SSM_STATE), BF16),
                   jax.ShapeDtypeStruct((g, SSM_STATE, LANES), F32),
                   jax.ShapeDtypeStruct((g, SSM_STATE, LANES), F32)),
        compiler_params=_cparams("parallel"),
        name="s5_prep",
    )(*tables, b_re, b_im, c_re, c_im, dup(c_re), dup(c_im))


def _s5_scan_kernel(u_ref, wt_ref, *rest, chunk, carry, split_ops):
    ws_ref = rest[0] if split_ops else wt_ref
    qt_ref, kr_ref, ki_ref, dsk_ref = rest[split_ops:split_ops + 4]
    rest = rest[split_ops + 4:]
    _s5_scan_body(u_ref, wt_ref, ws_ref, qt_ref, kr_ref, ki_ref, dsk_ref, *rest, chunk=chunk, carry=carry)


def _s5_scan_body(u_ref, wt_ref, ws_ref, qt_ref, kr_ref, ki_ref, dsk_ref, *rest, chunk, carry):
    if carry:
        y_ref, hf_ref, dall_ref, yall_ref = rest
    else:
        h0_ref, y_ref, hf_ref, dall_ref, yall_ref = rest
    nc = u_ref.shape[1] // POS_TILE
    at = lambda s: (s // POS_TILE, pl.ds(s % POS_TILE, nc, stride=POS_TILE), slice(None))
    ncp = dall_ref.shape[2]
    nseq = ncp // LANES if carry else nc
    n = SSM_GROUP * chunk
    iota = lambda shape, dim: lax.broadcasted_iota(jnp.int32, shape, dim)
    for s in range(chunk):
        dall_ref[s] = _pad_rows(u_ref[at(s)], ncp).T.astype(BF16)
    lane = iota((SSM_STATE, ncp), 1) & (LANES - 1)
    slot = iota((LANES, LANES), 1)
    if not carry:
        h0_all = _pad_rows(jnp.concatenate([h0_ref[:, gp, :] for gp in range(GROUPS_PER_TILE)], axis=0), LANES).T
    acc = jnp.zeros((LANES, LANES), F32)
    for gp in range(GROUPS_PER_TILE):
        rows = slice(SSM_GROUP * gp, SSM_GROUP * (gp + 1))
        d = dall_ref[:, rows, :].reshape(n, ncp)
        ys = jnp.dot(wt_ref[gp], d, preferred_element_type=F32)
        if ws_ref is wt_ref:
            st = ys[n:]
            ys = ys[:n]
        else:
            st = jnp.dot(ws_ref[gp], d, preferred_element_type=F32)
        sr = st[:SSM_STATE]
        si = st[SSM_STATE:]
        if carry:
            for i in range(N_SCAN):
                sh = 1 << i
                ar = kr_ref[gp, :, ROW_DBL + i:ROW_DBL + i + 1]
                ai = ki_ref[gp, :, ROW_DBL + i:ROW_DBL + i + 1]
                pr = jnp.where(lane >= sh, pltpu.roll(sr, sh, 1), 0.0)
                pi = jnp.where(lane >= sh, pltpu.roll(si, sh, 1), 0.0)
                sr, si = sr + ar * pr - ai * pi, si + ar * pi + ai * pr
            hp = jnp.concatenate([jnp.where(lane >= 1, pltpu.roll(sr, 1, 1), 0.0),
                                  jnp.where(lane >= 1, pltpu.roll(si, 1, 1), 0.0)], axis=0)
            for b in range(nseq):
                seq = slice(LANES * b, LANES * (b + 1))
                end = jnp.concatenate([sr[:, seq], si[:, seq]], axis=0)
                acc = jnp.where(slot == 8 * gp + b, pltpu.roll(end, (8 * gp + b + 1) % LANES, 1), acc)
        else:
            hp = h0_all if gp == 0 else pltpu.roll(h0_all, LANES - 8 * gp, 1)
            hp = jnp.where(slot < nseq, hp, 0.0)
            hpr, hpi = hp[:SSM_STATE], hp[SSM_STATE:]
            ar = kr_ref[gp, :, chunk:chunk + 1]
            ai = ki_ref[gp, :, chunk:chunk + 1]
            end = jnp.concatenate([sr + ar * hpr - ai * hpi, si + ar * hpi + ai * hpr], axis=0)
            if gp:
                end = pltpu.roll(end, 8 * gp, 1)
            acc = jnp.where((slot >= 8 * gp) & (slot < 8 * gp + nseq), end, acc)
        y = ys + jnp.dot(qt_ref[gp], hp.astype(BF16), preferred_element_type=F32)
        yall_ref[:, rows, :] = y.reshape(chunk, SSM_GROUP, ncp)
    hf_ref[...] = acc.T[:8 * GROUPS_PER_TILE].reshape(GROUPS_PER_TILE, 8, LANES)
    for t in range(chunk):
        y_ref[at(t)] = yall_ref[t].T[:nc] + dsk_ref[...] * u_ref[at(t)]


def _s5_scan(proj, ops, d_skip, h0, nseq, *, carry):
    wt, qt, kr, ki = ops
    chunk, nc = proj.shape[0] * POS_TILE, proj.shape[1] // POS_TILE
    n = SSM_GROUP * chunk
    n_full = SSM_GROUP * PROMPT_CHUNK
    assert nc == (nseq * LANES if carry else nseq) and (carry or nseq == 8) and nseq <= 8
    assert n_full % n == 0 and n_full % (2 * SSM_STATE) == 0
    ncp = max(nc, LANES)
    gt = GROUPS_PER_TILE
    tile3 = lambda a, b: pl.BlockSpec((gt, a, b), lambda i: (i, 0, 0))
    lane_tile = pl.BlockSpec(proj.shape[:2] + (LANES,), lambda i: (0, 0, i))
    split_ops = n != n_full
    if split_ops:
        end_rows = pl.BlockSpec((gt, 2 * SSM_STATE, n), lambda i: (i, n_full // (2 * SSM_STATE), n_full // n - 1))
        op_specs, op_args = [tile3(n, n), end_rows], [wt, wt]
    else:
        op_specs, op_args = [tile3(n + 2 * SSM_STATE, n)], [wt]
    in_specs = [lane_tile] + op_specs + [tile3(n, 2 * SSM_STATE), tile3(SSM_STATE, LANES),
                                         tile3(SSM_STATE, LANES), pl.BlockSpec((1, LANES), lambda i: (0, i))]
    args = [proj] + op_args + [qt, kr, ki, d_skip.reshape(1, -1)]
    if not carry:
        in_specs.append(pl.BlockSpec((nseq, gt, LANES), lambda i: (0, i, 0)))
        args.append(h0)
    return pl.pallas_call(
        functools.partial(_s5_scan_kernel, chunk=chunk, carry=carry, split_ops=split_ops),
        grid=(SSM_GROUPS // gt,),
        in_specs=in_specs,
        out_specs=(lane_tile, tile3(8, LANES)),
        out_shape=(jax.ShapeDtypeStruct(proj.shape, F32),
                   jax.ShapeDtypeStruct((SSM_GROUPS, 8, LANES), F32)),
        scratch_shapes=[pltpu.VMEM((chunk, LANES, ncp), BF16), pltpu.VMEM((chunk, LANES, ncp), F32)],
        compiler_params=_cparams("parallel"),
        name="s5_scan",
    )(*args)


def _ssm_out(y, x, gn_ref, wz_ref, wg_ref, bg_ref, wo_ref, gf_ref):
    z = jnp.dot(_rms_norm(x, gn_ref[...]).astype(BF16), wz_ref[...], preferred_element_type=F32)
    g = jax.nn.gelu(y, approximate=True)
    gate = jnp.dot(g.astype(BF16), wg_ref[...], preferred_element_type=F32) + bg_ref[...]
    yy = (g * jax.nn.sigmoid(gate)) * _silu(z)
    return _rms_norm(x + jnp.dot(yy.astype(BF16), wo_ref[...], preferred_element_type=F32), gf_ref[...])


def _glu_out_kernel(y_ref, x_ref, *rest):
    o_ref = rest[-1]
    o_ref[...] = _ssm_out(y_ref[...], x_ref[...].reshape(y_ref.shape), *rest[:-1]).reshape(o_ref.shape)


def _glu_out(y, x, g_norm, w_z, w_glu, b_glu, w_out, g_final, *, tm=None, ct=None):
    m, w = x.shape
    vec = pl.BlockSpec((1, w), lambda i: (0, 0))
    mat = pl.BlockSpec((w, w), lambda i: (0, 0), pipeline_mode=pl.Buffered(1))
    if ct is None:
        y_spec = x_spec = pl.BlockSpec((tm, w), lambda i: (i, 0))
        steps = m // tm
    else:
        sg, nc = y.shape[0], y.shape[1] // POS_TILE
        steps = (nc // ct) * sg
        x = x.reshape(nc, sg * POS_TILE, w)
        y_spec = pl.BlockSpec((None, ct * POS_TILE, w), lambda i: (i % sg, i // sg, 0))
        x_spec = pl.BlockSpec((ct, POS_TILE, w), lambda i: (i // sg, i % sg, 0))
    return pl.pallas_call(
        _glu_out_kernel,
        grid=(steps,),
        in_specs=[y_spec, x_spec, vec, mat, mat, vec, mat, vec],
        out_specs=x_spec,
        out_shape=jax.ShapeDtypeStruct(x.shape, F32),
        compiler_params=_cparams("parallel"),
        name="glu_out",
    )(y, x, g_norm.reshape(1, w), w_z, w_glu, b_glu.reshape(1, w), w_out, g_final.reshape(1, w)).reshape(m, w)


PAST_LEN = 16384


def kernel(x_prompt, x_sample, cache_win_k, cache_win_v, state_conv, state_ssm_re, state_ssm_im, attn_norm, w_in_ab, conv_w, w_out_ab, ssm_norm, w_in_c, lam_re, lam_im, log_step, b_re, b_im, c_re, c_im, d_skip, w_glu, b_glu, w_out_c, final_norm):
    bp, tp, _ = x_prompt.shape
    bs, ts, _ = x_sample.shape
    n_keep = min(2048, tp)
    xp = x_prompt.reshape(bp * tp, D_MODEL)
    xs = x_sample.reshape(bs * ts, D_MODEL)

    qkv_cols = 3 * ATTN_WIDTH
    w_qkv = w_in_ab[0, :, :qkv_cols].astype(BF16)
    gate_casts = [(w_in_ab[0], CONV_WIDTH, qkv_cols // CONV_WIDTH + c) for c in range(5)]
    proj_p, *w_gates, w_out0 = _norm_proj(
        xp, attn_norm[0], w_qkv, tm=512, rope=_rope_tables(tp, tp, 0), rope_cols=2 * ATTN_WIDTH,
        tiles_per_seq=tp // 512, casts=gate_casts + [(w_out_ab[0], D_MODEL, 0)])
    proj_s = _norm_proj(xs, attn_norm[0], w_qkv, tm=bs * ts, rope=_rope_tables(bs * ts, ts, PAST_LEN),
                        rope_cols=2 * ATTN_WIDTH)
    o_p, k_p, v_p, w_u, w_z, w_glu1, w_out1 = _attn_prompt(
        proj_p, bp, tp, n_keep,
        casts=[(w_in_c[0], D_MODEL, 0), (w_in_c[0], D_MODEL, 1), (w_glu[0], D_MODEL, 0), (w_out_c[0], D_MODEL, 0)])
    o_p = o_p.reshape(bp * tp, ATTN_WIDTH)
    k_p = k_p.reshape(1, bp, n_keep, N_HEADS, HEAD_DIM)
    v_p = v_p.reshape(1, bp, n_keep, N_HEADS, HEAD_DIM)
    o_s = _attn_sample(proj_s, cache_win_k[0], cache_win_v[0], bs, ts)
    h1_p, conv_p = _mix_out(xp, o_p, attn_norm[0], w_gates, jnp.zeros((bp, 2, CONV_WIDTH), F32), conv_w[0],
                            w_out0, tm=256, tiles_per_seq=tp // 256)
    h1_s, conv_s = _mix_out(xs, o_s, attn_norm[0], w_gates, state_conv[0], conv_w[0], w_out0,
                            tm=bs * ts, seqs_per_tile=bs)
    k_s = proj_s[:, ATTN_WIDTH:2 * ATTN_WIDTH].reshape(1, bs, ts, N_HEADS, HEAD_DIM)
    v_s = proj_s[:, 2 * ATTN_WIDTH:3 * ATTN_WIDTH].reshape(1, bs, ts, N_HEADS, HEAD_DIM)

    out_params = (ssm_norm[0], w_z, w_glu1, b_glu[0], w_out1, final_norm)
    tables = _s5_tables(lam_re[0], lam_im[0], log_step[0])
    ops = _s5_operators(tables, b_re[0], b_im[0], c_re[0], c_im[0], PROMPT_CHUNK)
    u_p = _norm_proj(h1_p, ssm_norm[0], w_u, chunk=PROMPT_CHUNK, ct=64)
    y_p, hf_p = _s5_scan(u_p, ops, d_skip[0], None, bp, carry=True)
    out_p = _glu_out(y_p, h1_p, *out_params, ct=32)
    assert ts == POS_TILE
    u_s = _norm_proj(h1_s, ssm_norm[0], w_u, tm=bs * ts)
    h0 = jnp.concatenate([state_ssm_re[0], state_ssm_im[0]], axis=-1)
    y_s, hf_s = _s5_scan(u_s[None], ops, d_skip[0], h0, bs, carry=False)
    out_s = _glu_out(y_s[0], h1_s, *out_params, tm=bs * ts)
    hf_p = hf_p[:, :bp].transpose(1, 0, 2)[None]
    hf_s = hf_s[:, :bs].transpose(1, 0, 2)[None]
    return (out_p.reshape(bp, tp, D_MODEL), out_s.reshape(bs, ts, D_MODEL),
            k_p, v_p, conv_p[None], hf_p[..., :SSM_STATE], hf_p[..., SSM_STATE:],
            k_s, v_s, conv_s[None], hf_s[..., :SSM_STATE], hf_s[..., SSM_STATE:])
```

```python
import functools
import math

import jax
import jax.numpy as jnp
from jax import lax
from jax.experimental import pallas as pl
from jax.experimental.pallas import tpu as pltpu

D_MODEL = 2048
HEAD_DIM = 128
N_HEADS = 8
ATTN_WIDTH = 1024
CONV_WIDTH = 1024
DILATIONS = (1, 4, 16)
N_BACK = 128
ROPE_THETA = 10000.0
RMS_EPS = 1e-6
SSM_GROUP = 16
SSM_GROUPS = 128
SSM_STATE = 64
LANES = 128
VMEM_LIMIT = 56 * 1024 * 1024

F32 = jnp.float32
BF16 = jnp.bfloat16


def _cparams(*sem):
    return pltpu.CompilerParams(dimension_semantics=sem, vmem_limit_bytes=VMEM_LIMIT)


def _rope_table_kernel(inv_ref, *refs, period, offset):
    n_cast = (len(refs) - 2) // 2
    cos_ref, sin_ref = refs[n_cast:n_cast + 2]
    _cast_blocks(refs[:n_cast], refs[n_cast + 2:])
    rows = cos_ref.shape[0]
    r = lax.broadcasted_iota(jnp.int32, (rows, LANES), 0) + pl.program_id(0) * rows
    pos = (offset + lax.rem(r, period)).astype(F32)
    ang = pos * inv_ref[...]
    lane = lax.broadcasted_iota(jnp.int32, (rows, LANES), 1)
    cos_ref[...] = jnp.cos(ang)
    sin_ref[...] = jnp.where(lane < HEAD_DIM // 2, -1.0, 1.0) * jnp.sin(ang)


def _rope_tables(rows, period, offset, casts=()):
    half = HEAD_DIM // 2
    inv = ROPE_THETA ** (-jnp.arange(half, dtype=F32) / half)
    inv2 = jnp.concatenate([inv, inv])[None, :]
    tr = min(rows, 256)
    table = pl.BlockSpec((tr, LANES), lambda i: (i, 0))
    cast_in, cast_out, cast_shapes = _cast_specs(casts, rows // tr, lambda i: i)
    return pl.pallas_call(
        functools.partial(_rope_table_kernel, period=period, offset=offset),
        grid=(rows // tr,),
        in_specs=[pl.BlockSpec((1, LANES), lambda i: (0, 0))] + cast_in,
        out_specs=[table, table] + cast_out,
        out_shape=[jax.ShapeDtypeStruct((rows, LANES), F32)] * 2 + cast_shapes,
        compiler_params=_cparams("parallel"),
        name="rope_table",
    )(inv2, *[c[0] for c in casts])


def _rms_norm(x, g):
    return x * lax.rsqrt(jnp.mean(x * x, axis=-1, keepdims=True) + RMS_EPS) * g


POS_TILE = 8
COL_SLAB = 256


def _cast_blocks(cast_in, cast_out):
    for src, dst in zip(cast_in, cast_out):
        dst[...] = src[...].astype(BF16)


def _norm_proj_kernel(*refs, rope_cols, n_cast, has_tail):
    it = iter(refs)
    take = lambda n: [next(it) for _ in range(n)]
    x_ref, g_ref, w_ref = take(3)
    rope = take(2) if rope_cols else None
    cast_in = take(n_cast)
    tail_x = take(1)[0] if has_tail else None
    tail_rope = take(2) if has_tail and rope_cols else None
    o_ref = take(1)[0]
    cast_out = take(n_cast)
    tail_o = take(1)[0] if has_tail else None
    hn0, hn1 = take(2)
    k = pl.program_id(0)

    def normalise(dst):
        x = x_ref[...]
        dst[...] = _rms_norm(x.reshape(dst.shape), g_ref[...]).astype(BF16)
        _cast_blocks(cast_in, cast_out)

    def project(hn, out, tables):
        for c in range(w_ref.shape[1] // COL_SLAB):
            cols = slice(c * COL_SLAB, (c + 1) * COL_SLAB)
            acc = jnp.dot(hn, w_ref[:, cols], preferred_element_type=F32)
            if c * COL_SLAB < rope_cols:
                cos_ref, sin_ref = tables
                for h in range(COL_SLAB // HEAD_DIM):
                    xh = acc[:, h * HEAD_DIM:(h + 1) * HEAD_DIM]
                    lanes = slice(c * COL_SLAB + h * HEAD_DIM, c * COL_SLAB + (h + 1) * HEAD_DIM)
                    out[:, lanes] = xh * cos_ref[...] + pltpu.roll(xh, HEAD_DIM // 2, 1) * sin_ref[...]
            else:
                out[:, cols] = acc

    def step(cur, prev):
        normalise(cur)
        project(prev[...], o_ref, rope)

    pl.when(k == 0)(lambda: normalise(hn0))
    pl.when((k > 0) & (k % 2 == 0))(lambda: step(hn0, hn1))
    pl.when(k % 2 == 1)(lambda: step(hn1, hn0))
    if has_tail:
        @pl.when(k == pl.num_programs(0) - 1)
        def _():
            project(_rms_norm(tail_x[...], g_ref[...]).astype(BF16), tail_o, tail_rope)


def _cast_specs(casts, n_blocks, index):
    ins, outs, shapes = [], [], []
    for arr, width, col in casts:
        rows = arr.shape[0] // n_blocks
        ins.append(pl.BlockSpec((rows, width), lambda *k, col=col: (index(*k), col)))
        outs.append(pl.BlockSpec((rows, width), lambda *k: (index(*k), 0)))
        shapes.append(jax.ShapeDtypeStruct((arr.shape[0], width), BF16))
    return ins, outs, shapes


def _norm_proj(x, g, w, *, tm=None, chunk=None, ct=None, rope=None, rope_cols=0, tiles_per_seq=1, casts=(),
               tail=None, tail_rope=None):
    m, kd = x.shape
    n = w.shape[1]
    whole = lambda a: pl.BlockSpec(a.shape, lambda k: (0,) * len(a.shape))
    tail_in = [] if tail is None else [tail] + (list(tail_rope) if rope_cols else [])
    tail_shape = [] if tail is None else [jax.ShapeDtypeStruct((tail.shape[0], n), F32)]
    cur = lambda k: jnp.minimum(k, t - 1)
    prv = lambda k: jnp.maximum(k - 1, 0)
    if chunk is None:
        t, rows = m // tm, tm
        x_spec = pl.BlockSpec((tm, kd), lambda k: (cur(k), 0))
        out_spec = pl.BlockSpec((tm, n), lambda k: (prv(k), 0))
        out_shape = jax.ShapeDtypeStruct((m, n), F32)
    else:
        nc, sg = m // chunk, chunk // POS_TILE
        t, rows = (nc // ct) * sg, ct * POS_TILE
        x = x.reshape(nc, chunk, kd)
        x_spec = pl.BlockSpec((ct, POS_TILE, kd), lambda k: (cur(k) // sg, cur(k) % sg, 0))
        out_spec = pl.BlockSpec((None, rows, n), lambda k: (prv(k) % sg, prv(k) // sg, 0))
        out_shape = jax.ShapeDtypeStruct((sg, nc * POS_TILE, n), F32)
    table = pl.BlockSpec((rows, LANES), lambda k: (prv(k) % tiles_per_seq, 0))
    cast_in, cast_out, cast_shapes = _cast_specs(casts, t, cur)
    res = pl.pallas_call(
        functools.partial(_norm_proj_kernel, rope_cols=rope_cols, n_cast=len(casts), has_tail=tail is not None),
        grid=(t + 1,),
        in_specs=[x_spec, pl.BlockSpec((1, kd), lambda k: (0, 0)),
                  pl.BlockSpec((kd, n), lambda k: (0, 0), pipeline_mode=pl.Buffered(1))]
        + ([table, table] if rope_cols else []) + cast_in + [whole(a) for a in tail_in],
        out_specs=[out_spec] + cast_out + [whole(s) for s in tail_shape],
        out_shape=[out_shape] + cast_shapes + tail_shape,
        scratch_shapes=[pltpu.VMEM((rows, kd), BF16)] * 2,
        compiler_params=_cparams("arbitrary"),
        name="norm_proj",
    )(x, g.reshape(1, kd), w, *(rope if rope_cols else ()), *[c[0] for c in casts], *tail_in)
    return res if len(res) > 1 else res[0]


def _attn_prompt_kernel(q_ref, k_ref, v_ref, *refs, n_cast):
    cast_in, (o_ref, ks_ref, vs_ref) = refs[:n_cast], refs[n_cast:n_cast + 3]
    cast_out = refs[n_cast + 3:2 * n_cast + 3]
    qn_ref, kn_ref, vn_ref, q16_ref, k16_ref, v16_ref, st_ref, acc_ref, mm_ref, ll_ref = refs[2 * n_cast + 3:]
    _cast_blocks(cast_in, cast_out)
    t = q_ref.shape[1]
    n_keep = ks_ref.shape[1]
    ks_ref[0] = k_ref[0, t - n_keep:, :]
    vs_ref[0] = v_ref[0, t - n_keep:, :]
    l4, l16 = t // 4, t // 16
    nblk = t // N_BACK
    piece = N_BACK // 4
    scale = HEAD_DIM ** -0.5
    nt = (((1,), (1,)), ((), ()))
    pad = jnp.zeros((N_BACK, HEAD_DIM), BF16)
    for ref in (kn_ref, vn_ref, k16_ref, v16_ref):
        ref[0:N_BACK, :] = pad

    qn_ref[...] = q_ref[0].astype(BF16)
    kn_ref[N_BACK:, :] = k_ref[0].astype(BF16)
    vn_ref[N_BACK:, :] = v_ref[0].astype(BF16)
    for src, dst, off in ((q_ref, q16_ref, 0), (k_ref, k16_ref, N_BACK), (v_ref, v16_ref, N_BACK)):
        for r4 in range(4):
            st_ref[r4 * l4:(r4 + 1) * l4, :] = src[0, pl.ds(r4, l4, stride=4), :]
        for r4 in range(4):
            for a in range(4):
                r16 = r4 + 4 * a
                dst[off + r16 * l16:off + (r16 + 1) * l16, :] = (
                    st_ref[pl.ds(r4 * l4 + a, l16, stride=4), :].astype(BF16))

    iota = lambda dim: lax.broadcasted_iota(jnp.int32, (N_BACK, 2 * N_BACK), dim)
    qi, kj = iota(0), iota(1)
    dist = N_BACK + qi - kj
    band = (dist >= 0) & (dist <= N_BACK)
    has_prev = kj >= N_BACK
    dist4 = 4 * (qi % piece - kj % (2 * piece) + piece) + (qi // piece - kj // (2 * piece))
    band4 = (dist4 >= 0) & (dist4 <= N_BACK)
    has_prev4 = kj % (2 * piece) >= piece

    def scores(q, k2, valid):
        s = lax.dot_general(q, k2, nt, preferred_element_type=F32) * scale
        return jnp.where(valid, s, -jnp.inf)

    def fresh(s, v2):
        m = jnp.max(s, axis=-1, keepdims=True)
        p = jnp.exp(s - m)
        wide = lambda c: jnp.broadcast_to(c, (N_BACK, HEAD_DIM))
        return wide(m), wide(jnp.sum(p, axis=-1, keepdims=True)), jnp.dot(p.astype(BF16), v2,
                                                                         preferred_element_type=F32)

    def merged(s, v2, m_old, l_old, a_old):
        m_new = jnp.maximum(m_old, jnp.max(s, axis=-1, keepdims=True))
        alpha = jnp.exp(m_old - m_new)
        p = jnp.exp(s - jnp.concatenate([m_new, m_new], axis=1))
        l_new = alpha * l_old + jnp.sum(p, axis=-1, keepdims=True)
        return m_new, l_new, alpha * a_old + jnp.dot(p.astype(BF16), v2, preferred_element_type=F32)

    def block16(b, carry):
        row0 = pl.multiple_of(b * N_BACK, N_BACK)
        rows = pl.ds(row0, N_BACK)
        later = lax.rem(b, l16 // N_BACK) > 0
        s = scores(q16_ref[rows, :], k16_ref[pl.ds(row0, 2 * N_BACK), :], band & (has_prev | later))
        mm_ref[rows, :], ll_ref[rows, :], acc_ref[rows, :] = fresh(s, v16_ref[pl.ds(row0, 2 * N_BACK), :])
        return carry

    lax.fori_loop(0, nblk, block16, 0, unroll=16)

    n_j = l16 // piece

    def block4(b, carry):
        rho, j = b // n_j, lax.rem(b, n_j)
        base = pl.multiple_of(rho * l16 + j * piece, piece)
        q_rows = [pl.ds(base + a * 4 * l16, piece) for a in range(4)]
        k_rows = [pl.ds(base + a * 4 * l16 + N_BACK - piece, 2 * piece) for a in range(4)]
        gather = lambda ref, rows: jnp.concatenate([ref[r, :] for r in rows], axis=0)
        s = scores(gather(q16_ref, q_rows), gather(k16_ref, k_rows), band4 & (has_prev4 | (j > 0)))
        m, l, acc = merged(s, gather(v16_ref, k_rows), gather(mm_ref, q_rows), gather(ll_ref, q_rows),
                           gather(acc_ref, q_rows))
        for a, r in enumerate(q_rows):
            part = slice(a * piece, (a + 1) * piece)
            mm_ref[r, :], ll_ref[r, :], acc_ref[r, :] = m[part], l[part], acc[part]
        return carry

    lax.fori_loop(0, nblk, block4, 0, unroll=16)

    for src, dst in ((acc_ref, o_ref.at[0]), (mm_ref, acc_ref), (ll_ref, mm_ref)):
        for r4 in range(4):
            for a in range(4):
                r16 = r4 + 4 * a
                st_ref[pl.ds(r4 * l4 + a, l16, stride=4), :] = src[r16 * l16:(r16 + 1) * l16, :]
        for r4 in range(4):
            dst[pl.ds(r4, l4, stride=4), :] = st_ref[r4 * l4:(r4 + 1) * l4, :]

    def block1(b, carry):
        row0 = pl.multiple_of(b * N_BACK, N_BACK)
        rows = pl.ds(row0, N_BACK)
        s = scores(qn_ref[rows, :], kn_ref[pl.ds(row0, 2 * N_BACK), :], band & (has_prev | (b > 0)))
        _, l, acc = merged(s, vn_ref[pl.ds(row0, 2 * N_BACK), :], acc_ref[rows, :], mm_ref[rows, :],
                           o_ref[0, rows, :])
        o_ref[0, rows, :] = acc / l
        return carry

    lax.fori_loop(0, nblk, block1, 0, unroll=16)


def _attn_prompt(proj, b, t, n_keep, casts=()):
    assert t % (16 * N_BACK) == 0
    p3 = proj.reshape(b, t, proj.shape[1])
    blk = lambda off: pl.BlockSpec((1, t, HEAD_DIM), lambda i, h: (i, 0, off + h))
    keep = pl.BlockSpec((1, n_keep, HEAD_DIM), lambda i, h: (i, 0, h))
    cast_in, cast_out, cast_shapes = _cast_specs(casts, b * N_HEADS, lambda i, h: i * N_HEADS + h)
    return pl.pallas_call(
        functools.partial(_attn_prompt_kernel, n_cast=len(casts)),
        grid=(b, N_HEADS),
        in_specs=[blk(0), blk(N_HEADS), blk(2 * N_HEADS)] + cast_in,
        out_specs=[pl.BlockSpec((1, t, HEAD_DIM), lambda i, h: (i, 0, h)), keep, keep] + cast_out,
        out_shape=[jax.ShapeDtypeStruct((b, t, ATTN_WIDTH), F32),
                   jax.ShapeDtypeStruct((b, n_keep, ATTN_WIDTH), F32),
                   jax.ShapeDtypeStruct((b, n_keep, ATTN_WIDTH), F32)] + cast_shapes,
        scratch_shapes=[
            pltpu.VMEM((t, HEAD_DIM), BF16),
            pltpu.VMEM((t + N_BACK, HEAD_DIM), BF16),
            pltpu.VMEM((t + N_BACK, HEAD_DIM), BF16),
            pltpu.VMEM((t, HEAD_DIM), BF16),
            pltpu.VMEM((t + N_BACK, HEAD_DIM), BF16),
            pltpu.VMEM((t + N_BACK, HEAD_DIM), BF16),
            pltpu.VMEM((t, HEAD_DIM), F32),
            pltpu.VMEM((t, HEAD_DIM), F32),
            pltpu.VMEM((t, HEAD_DIM), F32),
            pltpu.VMEM((t, HEAD_DIM), F32),
        ],
        compiler_params=_cparams("parallel", "parallel"),
        name="attn_prompt",
    )(p3, p3, p3, *[c[0] for c in casts])


def _attn_sample_kernel(q_ref, kn_ref, vn_ref, kc_ref, vc_ref, o_ref):
    s_len = q_ref.shape[0]
    n_buf = kc_ref.shape[1] // N_HEADS
    scale = HEAD_DIM ** -0.5
    nt = (((1,), (1,)), ((), ()))

    def count(dist):
        c = jnp.zeros(dist.shape, F32)
        for d in DILATIONS:
            hit = (dist >= 0) & (dist <= N_BACK * d) & ((dist & (d - 1)) == 0)
            c = c + jnp.where(hit, 1.0, 0.0)
        return c

    iota = lambda shape, dim: lax.broadcasted_iota(jnp.int32, shape, dim)
    cc = count(n_buf + iota((s_len, n_buf), 0) - iota((s_len, n_buf), 1))
    cn = count(iota((s_len, s_len), 0) - iota((s_len, s_len), 1))
    for h in range(N_HEADS):
        cols = slice(h * HEAD_DIM, (h + 1) * HEAD_DIM)
        head_rows = pl.ds(h, n_buf, stride=N_HEADS)
        q = q_ref[:, cols].astype(BF16)
        sc = lax.dot_general(q, kc_ref[0, head_rows, :].astype(BF16), nt, preferred_element_type=F32) * scale
        sn = lax.dot_general(q, kn_ref[:, cols].astype(BF16), nt, preferred_element_type=F32) * scale
        sc = jnp.where(cc > 0, sc, -jnp.inf)
        sn = jnp.where(cn > 0, sn, -jnp.inf)
        m = jnp.maximum(jnp.max(sc, axis=-1, keepdims=True), jnp.max(sn, axis=-1, keepdims=True))
        pc = cc * jnp.exp(sc - m)
        pn = cn * jnp.exp(sn - m)
        l = jnp.sum(pc, axis=-1, keepdims=True) + jnp.sum(pn, axis=-1, keepdims=True)
        o = (jnp.dot(pc.astype(BF16), vc_ref[0, head_rows, :].astype(BF16), preferred_element_type=F32)
             + jnp.dot(pn.astype(BF16), vn_ref[:, cols].astype(BF16), preferred_element_type=F32))
        o_ref[:, cols] = o / l


def _attn_sample(proj, cache_k, cache_v, b, s_len):
    n_buf = cache_k.shape[1]
    ck = cache_k.reshape(b, n_buf * N_HEADS, HEAD_DIM)
    cv = cache_v.reshape(b, n_buf * N_HEADS, HEAD_DIM)
    new = lambda c: pl.BlockSpec((s_len, ATTN_WIDTH), lambda i: (i, c))
    old = pl.BlockSpec((1, n_buf * N_HEADS, HEAD_DIM), lambda i: (i, 0, 0))
    return pl.pallas_call(
        _attn_sample_kernel,
        grid=(b,),
        in_specs=[new(0), new(1), new(2), old, old],
        out_specs=new(0),
        out_shape=jax.ShapeDtypeStruct((b * s_len, ATTN_WIDTH), F32),
        compiler_params=_cparams("parallel"),
        name="attn_sample",
    )(proj, proj, proj, ck, cv)


def _silu(z):
    return z * jax.nn.sigmoid(z)


def _mix_out_kernel(x_ref, o_ref, g_ref, wza_ref, wgb_ref, wgc_ref, whi_ref, wzb_ref, cw_ref, w_ref, *refs,
                    tiles_per_seq):
    xs_ref, os_ref, p2_ref, p1_ref, h_ref, cs_ref, hs_ref, css_ref, tail_ref = refs

    def tile(x, o_attn, ln, prev2, prev1):
        tm = x.shape[0]
        hn = _rms_norm(x, g_ref[...]).astype(BF16)
        gate = lambda w: jnp.dot(hn, w[...], preferred_element_type=F32)
        ch = gate(wgc_ref) * gate(whi_ref)
        pos = lax.rem(lax.broadcasted_iota(jnp.int32, (tm, 1), 0), ln)
        ch1 = jnp.where(pos == 0, prev1, pltpu.roll(ch, 1, 0))
        ch2 = jnp.where(pos == 0, prev2, jnp.where(pos == 1, prev1, pltpu.roll(ch, 2, 0)))
        conv = ch2 * cw_ref[0:1, :] + ch1 * cw_ref[1:2, :] + ch * cw_ref[2:3, :]
        o_b = (gate(wgb_ref) * conv * _silu(gate(wzb_ref))).astype(BF16)
        y = jnp.dot(o_b, w_ref[ATTN_WIDTH:, :], preferred_element_type=F32)
        o_a = (o_attn * _silu(gate(wza_ref))).astype(BF16)
        y = y + jnp.dot(o_a, w_ref[0:ATTN_WIDTH, :], preferred_element_type=F32)
        return x + y, ch

    i = pl.program_id(0)
    tm = x_ref.shape[0]

    @pl.when(i % tiles_per_seq == 0)
    def _():
        tail_ref[...] = jnp.zeros(tail_ref.shape, F32)

    h, ch = tile(x_ref[...], o_ref[...], tm, tail_ref[0:1, :], tail_ref[1:2, :])
    h_ref[...] = h
    tail_ref[...] = ch[tm - 2:tm, :]
    cs_ref[0] = ch[tm - 2:tm, :]

    @pl.when(i == pl.num_programs(0) - 1)
    def _():
        rows = xs_ref.shape[0]
        k = p1_ref.shape[0]
        ln = rows // k
        per_row = lambda p: jnp.broadcast_to(p[...], (k, ln, CONV_WIDTH)).reshape(rows, CONV_WIDTH)
        hs, chs = tile(xs_ref[...], os_ref[...], ln, per_row(p2_ref), per_row(p1_ref))
        hs_ref[...] = hs
        css_ref[...] = chs.reshape(k, ln, CONV_WIDTH)[:, ln - 2:, :]


def _mix_out(x, o_attn, xs, os_attn, conv_init_s, g, w_gates, conv_w, w_out, *, tm, tiles_per_seq):
    m, ms = x.shape[0], xs.shape[0]
    nseq = m // (tm * tiles_per_seq)
    once = lambda a: pl.BlockSpec(a.shape, lambda i: (0,) * len(a.shape), pipeline_mode=pl.Buffered(1))
    row = lambda w: pl.BlockSpec((tm, w), lambda i: (i, 0))
    p2, p1 = conv_init_s[:, 0:1], conv_init_s[:, 1:2]
    out_shape = (jax.ShapeDtypeStruct((m, D_MODEL), F32), jax.ShapeDtypeStruct((nseq, 2, CONV_WIDTH), F32),
                 jax.ShapeDtypeStruct((ms, D_MODEL), F32), jax.ShapeDtypeStruct(conv_init_s.shape, F32))
    return pl.pallas_call(
        functools.partial(_mix_out_kernel, tiles_per_seq=tiles_per_seq),
        grid=(m // tm,),
        in_specs=[row(D_MODEL), row(ATTN_WIDTH), pl.BlockSpec((1, D_MODEL), lambda i: (0, 0))]
        + [once(w) for w in w_gates] + [pl.BlockSpec((3, CONV_WIDTH), lambda i: (0, 0)), once(w_out)]
        + [once(a) for a in (xs, os_attn, p2, p1)],
        out_specs=(row(D_MODEL), pl.BlockSpec((1, 2, CONV_WIDTH), lambda i: (i // tiles_per_seq, 0, 0)))
        + tuple(pl.BlockSpec(s.shape, lambda i, n=len(s.shape): (0,) * n) for s in out_shape[2:]),
        out_shape=out_shape,
        scratch_shapes=[pltpu.VMEM((2, CONV_WIDTH), F32)],
        compiler_params=_cparams("arbitrary"),
        name="mix_out",
    )(x, o_attn, g.reshape(1, -1), *w_gates, conv_w, w_out, xs, os_attn, p2, p1)


N_SCAN = 7
GROUPS_PER_TILE = LANES // SSM_GROUP


PROMPT_CHUNK = 32
ROW_C = PROMPT_CHUNK + 1
ROW_DBL = PROMPT_CHUNK + 2
TABLE_ROWS = ROW_DBL + N_SCAN


def _s5_disc_kernel(lr_ref, li_ref, ls_ref, tr_ref, ti_ref):
    lr = lr_ref[...]
    li = li_ref[...]
    step = jnp.exp(ls_ref[...])
    mag = jnp.exp(lr * step)
    ar = mag * jnp.cos(li * step)
    ai = mag * jnp.sin(li * step)
    den = lr * lr + li * li
    nr = ar - 1.0
    tr_ref[ROW_C] = (nr * lr + ai * li) / den
    ti_ref[ROW_C] = (ai * lr - nr * li) / den
    pr = jnp.ones_like(ar)
    pi = jnp.zeros_like(ar)
    for tau in range(PROMPT_CHUNK + 1):
        tr_ref[tau] = pr
        ti_ref[tau] = pi
        dr, di = pr, pi
        pr, pi = pr * ar - pi * ai, pr * ai + pi * ar
    for i in range(N_SCAN):
        tr_ref[ROW_DBL + i] = dr
        ti_ref[ROW_DBL + i] = di
        dr, di = dr * dr - di * di, 2.0 * dr * di


def _s5_tables(lam_re, lam_im, log_step):
    g = lam_re.shape[0]
    dup = lambda a: jnp.concatenate([a, a], axis=-1)
    tr, ti = pl.pallas_call(
        _s5_disc_kernel,
        out_shape=(jax.ShapeDtypeStruct((TABLE_ROWS, g, LANES), F32),) * 2,
        name="s5_disc",
    )(dup(lam_re), dup(lam_im), log_step[:, None])
    return tr.transpose(1, 0, 2), ti.transpose(1, 0, 2)


def _split_bf16(x):
    hi = x.astype(BF16)
    return hi, (x - hi.astype(F32)).astype(BF16)


def _dot_split(a, b, b_is_bf16_exact=False):
    dot = lambda x, y: jnp.dot(x, y, preferred_element_type=F32)
    ah, al = _split_bf16(a)
    if b_is_bf16_exact:
        bh = b.astype(BF16)
        return dot(ah, bh) + dot(al, bh)
    bh, bl = _split_bf16(b)
    return dot(ah, bh) + (dot(ah, bl) + dot(al, bh))


def _pad_rows(x, rows):
    if x.shape[0] == rows:
        return x
    return jnp.concatenate([x, jnp.zeros((rows - x.shape[0], x.shape[1]), x.dtype)], axis=0)


def _s5_prep_kernel(tr_ref, ti_ref, br_ref, bi_ref, cr_ref, ci_ref, c2r_ref, c2i_ref,
                    wt_ref, qt_ref, kr_ref, ki_ref, *, chunk):
    n = SSM_GROUP * chunk
    iota = lambda shape, dim: lax.broadcasted_iota(jnp.int32, shape, dim)
    lane = iota((SSM_GROUP, 2 * SSM_STATE), 1)
    e_sel = jnp.where(iota((chunk, n), 0) == chunk - 1 - iota((chunk, n), 1) // SSM_GROUP, 1.0, 0.0)
    tile = jnp.where(iota((SSM_GROUP, n), 0) == iota((SSM_GROUP, n), 1) % SSM_GROUP, 1.0, 0.0)

    def one_group(g, carry):
        tab_r, tab_i = tr_ref[g], ti_ref[g]
        col_r = _pad_rows(tab_r, LANES).T[:SSM_STATE]
        col_i = _pad_rows(tab_i, LANES).T[:SSM_STATE]
        kr_ref[g] = col_r
        ki_ref[g] = col_i
        ccr, cci = col_r[:, ROW_C:ROW_C + 1], col_i[:, ROW_C:ROW_C + 1]
        br, bi = br_ref[g], bi_ref[g]
        bbr = ccr * br - cci * bi
        bbi = ccr * bi + cci * br
        aer = _dot_split(col_r[:, :chunk], e_sel, True)
        aei = _dot_split(col_i[:, :chunk], e_sel, True)
        btr = _dot_split(bbr, tile, True)
        bti = _dot_split(bbi, tile, True)
        ptr = aer * btr - aei * bti
        pti = aer * bti + aei * btr
        wt_ref[g, n:n + SSM_STATE, :] = ptr.astype(BF16)
        wt_ref[g, n + SSM_STATE:n + 2 * SSM_STATE, :] = pti.astype(BF16)
        taps = _dot_split(cr_ref[g], ptr) - _dot_split(ci_ref[g], pti)
        padded = jnp.concatenate([taps, jnp.zeros_like(taps)], axis=1)
        for t in range(chunk):
            sh = SSM_GROUP * (chunk - 1 - t)
            blk = padded if sh == 0 else pltpu.roll(padded, 2 * n - sh, 1)
            wt_ref[g, SSM_GROUP * t:SSM_GROUP * (t + 1), :] = blk[:, :n].astype(BF16)
        c2r, c2i = c2r_ref[g], c2i_ref[g]
        for t in range(chunk):
            ar = tab_r[t + 1:t + 2, :]
            ai = tab_i[t + 1:t + 2, :]
            x1 = jnp.where(lane < SSM_STATE, ar, -ai)
            x2 = jnp.where(lane < SSM_STATE, ai, ar)
            qt_ref[g, SSM_GROUP * t:SSM_GROUP * (t + 1), :] = (c2r * x1 - c2i * x2).astype(BF16)
        return carry

    lax.fori_loop(0, wt_ref.shape[0], one_group, 0, unroll=4)


def _s5_operators(tables, b_re, b_im, c_re, c_im, chunk):
    g = b_re.shape[0]
    n = SSM_GROUP * chunk
    assert chunk <= PROMPT_CHUNK
    dup = lambda a: jnp.concatenate([a, a], axis=-1)
    per_g = lambda *s: pl.BlockSpec((GROUPS_PER_TILE,) + s, lambda i: (i,) + (0,) * len(s))
    return pl.pallas_call(
        functools.partial(_s5_prep_kernel, chunk=chunk),
        grid=(g // GROUPS_PER_TILE,),
        in_specs=[per_g(TABLE_ROWS, LANES), per_g(TABLE_ROWS, LANES),
                  per_g(SSM_STATE, SSM_GROUP), per_g(SSM_STATE, SSM_GROUP),
                  per_g(SSM_GROUP, SSM_STATE), per_g(SSM_GROUP, SSM_STATE),
                  per_g(SSM_GROUP, LANES), per_g(SSM_GROUP, LANES)],
        out_specs=(per_g(n + 2 * SSM_STATE, n), per_g(n, 2 * SSM_STATE),
                   per_g(SSM_STATE, LANES), per_g(SSM_STATE, LANES)),
        out_shape=(jax.ShapeDtypeStruct((g, n + 2 * SSM_STATE, n), BF16),
                   jax.ShapeDtypeStruct((g, n, 2 * SSM_STATE), BF16),
                   jax.ShapeDtypeStruct((g, SSM_STATE, LANES), F32),
                   jax.ShapeDtypeStruct((g, SSM_STATE, LANES), F32)),
        compiler_params=_cparams("parallel"),
        name="s5_prep",
    )(*tables, b_re, b_im, c_re, c_im, dup(c_re), dup(c_im))


def _s5_scan_kernel(u_ref, wt_ref, *rest, chunk, carry, split_ops):
    ws_ref = rest[0] if split_ops else wt_ref
    qt_ref, kr_ref, ki_ref, dsk_ref = rest[split_ops:split_ops + 4]
    rest = rest[split_ops + 4:]
    _s5_scan_body(u_ref, wt_ref, ws_ref, qt_ref, kr_ref, ki_ref, dsk_ref, *rest, chunk=chunk, carry=carry)


def _s5_scan_body(u_ref, wt_ref, ws_ref, qt_ref, kr_ref, ki_ref, dsk_ref, *rest, chunk, carry):
    if carry:
        y_ref, hf_ref, dall_ref, yall_ref = rest
    else:
        h0_ref, y_ref, hf_ref, dall_ref, yall_ref = rest
    nc = u_ref.shape[1] // POS_TILE
    at = lambda s: (s // POS_TILE, pl.ds(s % POS_TILE, nc, stride=POS_TILE), slice(None))
    ncp = dall_ref.shape[2]
    nseq = ncp // LANES if carry else nc
    n = SSM_GROUP * chunk
    iota = lambda shape, dim: lax.broadcasted_iota(jnp.int32, shape, dim)
    for s in range(chunk):
        dall_ref[s] = _pad_rows(u_ref[at(s)], ncp).T.astype(BF16)
    lane = iota((SSM_STATE, ncp), 1) & (LANES - 1)
    slot = iota((LANES, LANES), 1)
    if not carry:
        h0_all = _pad_rows(jnp.concatenate([h0_ref[:, gp, :] for gp in range(GROUPS_PER_TILE)], axis=0), LANES).T
    acc = jnp.zeros((LANES, LANES), F32)
    for gp in range(GROUPS_PER_TILE):
        rows = slice(SSM_GROUP * gp, SSM_GROUP * (gp + 1))
        d = dall_ref[:, rows, :].reshape(n, ncp)
        ys = jnp.dot(wt_ref[gp], d, preferred_element_type=F32)
        if ws_ref is wt_ref:
            st = ys[n:]
            ys = ys[:n]
        else:
            st = jnp.dot(ws_ref[gp], d, preferred_element_type=F32)
        sr = st[:SSM_STATE]
        si = st[SSM_STATE:]
        if carry:
            for i in range(N_SCAN):
                sh = 1 << i
                ar = kr_ref[gp, :, ROW_DBL + i:ROW_DBL + i + 1]
                ai = ki_ref[gp, :, ROW_DBL + i:ROW_DBL + i + 1]
                pr = jnp.where(lane >= sh, pltpu.roll(sr, sh, 1), 0.0)
                pi = jnp.where(lane >= sh, pltpu.roll(si, sh, 1), 0.0)
                sr, si = sr + ar * pr - ai * pi, si + ar * pi + ai * pr
            hp = jnp.concatenate([jnp.where(lane >= 1, pltpu.roll(sr, 1, 1), 0.0),
                                  jnp.where(lane >= 1, pltpu.roll(si, 1, 1), 0.0)], axis=0)
            for b in range(nseq):
                seq = slice(LANES * b, LANES * (b + 1))
                end = jnp.concatenate([sr[:, seq], si[:, seq]], axis=0)
                acc = jnp.where(slot == 8 * gp + b, pltpu.roll(end, (8 * gp + b + 1) % LANES, 1), acc)
        else:
            hp = h0_all if gp == 0 else pltpu.roll(h0_all, LANES - 8 * gp, 1)
            hp = jnp.where(slot < nseq, hp, 0.0)
            hpr, hpi = hp[:SSM_STATE], hp[SSM_STATE:]
            ar = kr_ref[gp, :, chunk:chunk + 1]
            ai = ki_ref[gp, :, chunk:chunk + 1]
            end = jnp.concatenate([sr + ar * hpr - ai * hpi, si + ar * hpi + ai * hpr], axis=0)
            if gp:
                end = pltpu.roll(end, 8 * gp, 1)
            acc = jnp.where((slot >= 8 * gp) & (slot < 8 * gp + nseq), end, acc)
        y = ys + jnp.dot(qt_ref[gp], hp.astype(BF16), preferred_element_type=F32)
        yall_ref[:, rows, :] = y.reshape(chunk, SSM_GROUP, ncp)
    hf_ref[...] = acc.T[:8 * GROUPS_PER_TILE].reshape(GROUPS_PER_TILE, 8, LANES)
    for t in range(chunk):
        y_ref[at(t)] = yall_ref[t].T[:nc] + dsk_ref[...] * u_ref[at(t)]


def _s5_scan(proj, ops, d_skip, h0, nseq, *, carry):
    wt, qt, kr, ki = ops
    chunk, nc = proj.shape[0] * POS_TILE, proj.shape[1] // POS_TILE
    n = SSM_GROUP * chunk
    n_full = SSM_GROUP * PROMPT_CHUNK
    assert nc == (nseq * LANES if carry else nseq) and (carry or nseq == 8) and nseq <= 8
    assert n_full % n == 0 and n_full % (2 * SSM_STATE) == 0
    ncp = max(nc, LANES)
    gt = GROUPS_PER_TILE
    tile3 = lambda a, b: pl.BlockSpec((gt, a, b), lambda i: (i, 0, 0))
    lane_tile = pl.BlockSpec(proj.shape[:2] + (LANES,), lambda i: (0, 0, i))
    split_ops = n != n_full
    if split_ops:
        end_rows = pl.BlockSpec((gt, 2 * SSM_STATE, n), lambda i: (i, n_full // (2 * SSM_STATE), n_full // n - 1))
        op_specs, op_args = [tile3(n, n), end_rows], [wt, wt]
    else:
        op_specs, op_args = [tile3(n + 2 * SSM_STATE, n)], [wt]
    in_specs = [lane_tile] + op_specs + [tile3(n, 2 * SSM_STATE), tile3(SSM_STATE, LANES),
                                         tile3(SSM_STATE, LANES), pl.BlockSpec((1, LANES), lambda i: (0, i))]
    args = [proj] + op_args + [qt, kr, ki, d_skip.reshape(1, -1)]
    if not carry:
        in_specs.append(pl.BlockSpec((nseq, gt, LANES), lambda i: (0, i, 0)))
        args.append(h0)
    return pl.pallas_call(
        functools.partial(_s5_scan_kernel, chunk=chunk, carry=carry, split_ops=split_ops),
        grid=(SSM_GROUPS // gt,),
        in_specs=in_specs,
        out_specs=(lane_tile, tile3(8, LANES)),
        out_shape=(jax.ShapeDtypeStruct(proj.shape, F32),
                   jax.ShapeDtypeStruct((SSM_GROUPS, 8, LANES), F32)),
        scratch_shapes=[pltpu.VMEM((chunk, LANES, ncp), BF16), pltpu.VMEM((chunk, LANES, ncp), F32)],
        compiler_params=_cparams("parallel"),
        name="s5_scan",
    )(*args)


def _ssm_out(y, x, gn_ref, wz_ref, wg_ref, bg_ref, wo_ref, gf_ref):
    z = jnp.dot(_rms_norm(x, gn_ref[...]).astype(BF16), wz_ref[...], preferred_element_type=F32)
    g = jax.nn.gelu(y, approximate=True)
    gate = jnp.dot(g.astype(BF16), wg_ref[...], preferred_element_type=F32) + bg_ref[...]
    yy = (g * jax.nn.sigmoid(gate)) * _silu(z)
    return _rms_norm(x + jnp.dot(yy.astype(BF16), wo_ref[...], preferred_element_type=F32), gf_ref[...])


def _glu_out_kernel(y_ref, x_ref, ys_ref, xs_ref, *rest):
    params, (o_ref, os_ref) = rest[:-2], rest[-2:]
    o_ref[...] = _ssm_out(y_ref[...], x_ref[...].reshape(y_ref.shape), *params).reshape(o_ref.shape)

    @pl.when(pl.program_id(0) == pl.num_programs(0) - 1)
    def _():
        os_ref[...] = _ssm_out(ys_ref[...], xs_ref[...], *params)


def _glu_out(y, x, ys, xs, g_norm, w_z, w_glu, b_glu, w_out, g_final, *, ct):
    m, w = x.shape
    vec = pl.BlockSpec((1, w), lambda i: (0, 0))
    mat = pl.BlockSpec((w, w), lambda i: (0, 0), pipeline_mode=pl.Buffered(1))
    small = pl.BlockSpec(xs.shape, lambda i: (0, 0))
    sg, nc = y.shape[0], y.shape[1] // POS_TILE
    x_spec = pl.BlockSpec((ct, POS_TILE, w), lambda i: (i // sg, i % sg, 0))
    out, out_s = pl.pallas_call(
        _glu_out_kernel,
        grid=((nc // ct) * sg,),
        in_specs=[pl.BlockSpec((None, ct * POS_TILE, w), lambda i: (i % sg, i // sg, 0)), x_spec, small, small,
                  vec, mat, mat, vec, mat, vec],
        out_specs=(x_spec, small),
        out_shape=(jax.ShapeDtypeStruct((nc, sg * POS_TILE, w), F32), jax.ShapeDtypeStruct(xs.shape, F32)),
        compiler_params=_cparams("arbitrary"),
        name="glu_out",
    )(y, x.reshape(nc, sg * POS_TILE, w), ys, xs, g_norm.reshape(1, w), w_z, w_glu, b_glu.reshape(1, w), w_out,
      g_final.reshape(1, w))
    return out.reshape(m, w), out_s


PAST_LEN = 16384


def kernel(x_prompt, x_sample, cache_win_k, cache_win_v, state_conv, state_ssm_re, state_ssm_im, attn_norm, w_in_ab, conv_w, w_out_ab, ssm_norm, w_in_c, lam_re, lam_im, log_step, b_re, b_im, c_re, c_im, d_skip, w_glu, b_glu, w_out_c, final_norm):
    bp, tp, _ = x_prompt.shape
    bs, ts, _ = x_sample.shape
    n_keep = min(2048, tp)
    xp = x_prompt.reshape(bp * tp, D_MODEL)
    xs = x_sample.reshape(bs * ts, D_MODEL)

    qkv_cols = 3 * ATTN_WIDTH
    gate_casts = [(w_in_ab[0], CONV_WIDTH, qkv_cols // CONV_WIDTH + c) for c in range(5)]
    layer1_casts = [(w_in_c[0], D_MODEL, 0), (w_in_c[0], D_MODEL, 1), (w_glu[0], D_MODEL, 0), (w_out_c[0], D_MODEL, 0)]

    cos_p, sin_p, w_qkv = _rope_tables(tp, tp, 0, casts=[(w_in_ab[0], qkv_cols, 0)])
    proj_p, *w_gates, w_out0, proj_s = _norm_proj(
        xp, attn_norm[0], w_qkv, tm=512, rope=(cos_p, sin_p), rope_cols=2 * ATTN_WIDTH, tiles_per_seq=tp // 512,
        casts=gate_casts + [(w_out_ab[0], D_MODEL, 0)], tail=xs, tail_rope=_rope_tables(bs * ts, ts, PAST_LEN))
    o_p, k_p, v_p, w_u, w_z, w_glu1, w_out1 = _attn_prompt(proj_p, bp, tp, n_keep, casts=layer1_casts)
    o_p = o_p.reshape(bp * tp, ATTN_WIDTH)
    k_p = k_p.reshape(1, bp, n_keep, N_HEADS, HEAD_DIM)
    v_p = v_p.reshape(1, bp, n_keep, N_HEADS, HEAD_DIM)
    o_s = _attn_sample(proj_s, cache_win_k[0], cache_win_v[0], bs, ts)
    h1_p, conv_p, h1_s, conv_s = _mix_out(xp, o_p, xs, o_s, state_conv[0], attn_norm[0], w_gates, conv_w[0], w_out0,
                                          tm=256, tiles_per_seq=tp // 256)
    k_s = proj_s[:, ATTN_WIDTH:2 * ATTN_WIDTH].reshape(1, bs, ts, N_HEADS, HEAD_DIM)
    v_s = proj_s[:, 2 * ATTN_WIDTH:3 * ATTN_WIDTH].reshape(1, bs, ts, N_HEADS, HEAD_DIM)

    assert ts == POS_TILE
    tables = _s5_tables(lam_re[0], lam_im[0], log_step[0])
    ops = _s5_operators(tables, b_re[0], b_im[0], c_re[0], c_im[0], PROMPT_CHUNK)
    u_p, u_s = _norm_proj(h1_p, ssm_norm[0], w_u, chunk=PROMPT_CHUNK, ct=64, tail=h1_s)
    y_p, hf_p = _s5_scan(u_p, ops, d_skip[0], None, bp, carry=True)
    h0 = jnp.concatenate([state_ssm_re[0], state_ssm_im[0]], axis=-1)
    y_s, hf_s = _s5_scan(u_s[None], ops, d_skip[0], h0, bs, carry=False)
    out_p, out_s = _glu_out(y_p, h1_p, y_s[0], h1_s, ssm_norm[0], w_z, w_glu1, b_glu[0], w_out1, final_norm, ct=32)
    hf_p = hf_p[:, :bp].transpose(1, 0, 2)[None]
    hf_s = hf_s[:, :bs].transpose(1, 0, 2)[None]
    return (out_p.reshape(bp, tp, D_MODEL), out_s.reshape(bs, ts, D_MODEL),
            k_p, v_p, conv_p[None], hf_p[..., :SSM_STATE], hf_p[..., SSM_STATE:],
            k_s, v_s, conv_s[None], hf_s[..., :SSM_STATE], hf_s[..., SSM_STATE:])
```

```python
import functools
import math

import jax
import jax.numpy as jnp
from jax import lax
from jax.experimental import pallas as pl
from jax.experimental.pallas import tpu as pltpu

D_MODEL = 2048
HEAD_DIM = 128
N_HEADS = 8
ATTN_WIDTH = 1024
CONV_WIDTH = 1024
DILATIONS = (1, 4, 16)
N_BACK = 128
ROPE_THETA = 10000.0
RMS_EPS = 1e-6
SSM_GROUP = 16
SSM_GROUPS = 128
SSM_STATE = 64
LANES = 128
VMEM_LIMIT = 56 * 1024 * 1024

F32 = jnp.float32
BF16 = jnp.bfloat16


def _cparams(*sem):
    return pltpu.CompilerParams(dimension_semantics=sem, vmem_limit_bytes=VMEM_LIMIT)


def _rope_table_kernel(inv_ref, *refs, period, offset):
    n_cast = (len(refs) - 2) // 2
    cos_ref, sin_ref = refs[n_cast:n_cast + 2]
    _cast_blocks(refs[:n_cast], refs[n_cast + 2:])
    rows = cos_ref.shape[0]
    r = lax.broadcasted_iota(jnp.int32, (rows, LANES), 0) + pl.program_id(0) * rows
    pos = (offset + lax.rem(r, period)).astype(F32)
    ang = pos * inv_ref[...]
    lane = lax.broadcasted_iota(jnp.int32, (rows, LANES), 1)
    cos_ref[...] = jnp.cos(ang)
    sin_ref[...] = jnp.where(lane < HEAD_DIM // 2, -1.0, 1.0) * jnp.sin(ang)


def _rope_tables(rows, period, offset, casts=()):
    half = HEAD_DIM // 2
    inv = ROPE_THETA ** (-jnp.arange(half, dtype=F32) / half)
    inv2 = jnp.concatenate([inv, inv])[None, :]
    tr = min(rows, 256)
    table = pl.BlockSpec((tr, LANES), lambda i: (i, 0))
    cast_in, cast_out, cast_shapes = _cast_specs(casts, rows // tr, lambda i: i)
    return pl.pallas_call(
        functools.partial(_rope_table_kernel, period=period, offset=offset),
        grid=(rows // tr,),
        in_specs=[pl.BlockSpec((1, LANES), lambda i: (0, 0))] + cast_in,
        out_specs=[table, table] + cast_out,
        out_shape=[jax.ShapeDtypeStruct((rows, LANES), F32)] * 2 + cast_shapes,
        compiler_params=_cparams("parallel"),
        name="rope_table",
    )(inv2, *[c[0] for c in casts])


def _rms_norm(x, g):
    return x * lax.rsqrt(jnp.mean(x * x, axis=-1, keepdims=True) + RMS_EPS) * g


POS_TILE = 8
COL_SLAB = 256


def _cast_blocks(cast_in, cast_out):
    for src, dst in zip(cast_in, cast_out):
        dst[...] = src[...].astype(BF16)


def _norm_proj_kernel(*refs, rope_cols, n_cast, has_tail):
    it = iter(refs)
    take = lambda n: [next(it) for _ in range(n)]
    x_ref, g_ref, w_ref = take(3)
    rope = take(2) if rope_cols else None
    cast_in = take(n_cast)
    tail_x = take(1)[0] if has_tail else None
    tail_rope = take(2) if has_tail and rope_cols else None
    o_ref = take(1)[0]
    cast_out = take(n_cast)
    tail_o = take(1)[0] if has_tail else None
    hn0, hn1 = take(2)
    k = pl.program_id(0)

    def normalise(dst):
        x = x_ref[...]
        dst[...] = _rms_norm(x.reshape(dst.shape), g_ref[...]).astype(BF16)
        _cast_blocks(cast_in, cast_out)

    def project(hn, out, tables):
        for c in range(w_ref.shape[1] // COL_SLAB):
            cols = slice(c * COL_SLAB, (c + 1) * COL_SLAB)
            acc = jnp.dot(hn, w_ref[:, cols], preferred_element_type=F32)
            if c * COL_SLAB < rope_cols:
                cos_ref, sin_ref = tables
                for h in range(COL_SLAB // HEAD_DIM):
                    xh = acc[:, h * HEAD_DIM:(h + 1) * HEAD_DIM]
                    lanes = slice(c * COL_SLAB + h * HEAD_DIM, c * COL_SLAB + (h + 1) * HEAD_DIM)
                    out[:, lanes] = xh * cos_ref[...] + pltpu.roll(xh, HEAD_DIM // 2, 1) * sin_ref[...]
            else:
                out[:, cols] = acc

    def step(cur, prev):
        normalise(cur)
        project(prev[...], o_ref, rope)

    pl.when(k == 0)(lambda: normalise(hn0))
    pl.when((k > 0) & (k % 2 == 0))(lambda: step(hn0, hn1))
    pl.when(k % 2 == 1)(lambda: step(hn1, hn0))
    if has_tail:
        @pl.when(k == pl.num_programs(0) - 1)
        def _():
            project(_rms_norm(tail_x[...], g_ref[...]).astype(BF16), tail_o, tail_rope)


def _cast_specs(casts, n_blocks, index):
    ins, outs, shapes = [], [], []
    for arr, width, col in casts:
        rows = arr.shape[0] // n_blocks
        ins.append(pl.BlockSpec((rows, width), lambda *k, col=col: (index(*k), col)))
        outs.append(pl.BlockSpec((rows, width), lambda *k: (index(*k), 0)))
        shapes.append(jax.ShapeDtypeStruct((arr.shape[0], width), BF16))
    return ins, outs, shapes


def _norm_proj(x, g, w, *, tm=None, chunk=None, ct=None, rope=None, rope_cols=0, tiles_per_seq=1, casts=(),
               tail=None, tail_rope=None):
    m, kd = x.shape
    n = w.shape[1]
    whole = lambda a: pl.BlockSpec(a.shape, lambda k: (0,) * len(a.shape))
    tail_in = [] if tail is None else [tail] + (list(tail_rope) if rope_cols else [])
    tail_shape = [] if tail is None else [jax.ShapeDtypeStruct((tail.shape[0], n), F32)]
    cur = lambda k: jnp.minimum(k, t - 1)
    prv = lambda k: jnp.maximum(k - 1, 0)
    if chunk is None:
        t, rows = m // tm, tm
        x_spec = pl.BlockSpec((tm, kd), lambda k: (cur(k), 0))
        out_spec = pl.BlockSpec((tm, n), lambda k: (prv(k), 0))
        out_shape = jax.ShapeDtypeStruct((m, n), F32)
    else:
        nc, sg = m // chunk, chunk // POS_TILE
        t, rows = (nc // ct) * sg, ct * POS_TILE
        x = x.reshape(nc, chunk, kd)
        x_spec = pl.BlockSpec((ct, POS_TILE, kd), lambda k: (cur(k) // sg, cur(k) % sg, 0))
        out_spec = pl.BlockSpec((None, rows, n), lambda k: (prv(k) % sg, prv(k) // sg, 0))
        out_shape = jax.ShapeDtypeStruct((sg, nc * POS_TILE, n), F32)
    table = pl.BlockSpec((rows, LANES), lambda k: (prv(k) % tiles_per_seq, 0))
    cast_in, cast_out, cast_shapes = _cast_specs(casts, t, cur)
    res = pl.pallas_call(
        functools.partial(_norm_proj_kernel, rope_cols=rope_cols, n_cast=len(casts), has_tail=tail is not None),
        grid=(t + 1,),
        in_specs=[x_spec, pl.BlockSpec((1, kd), lambda k: (0, 0)),
                  pl.BlockSpec((kd, n), lambda k: (0, 0), pipeline_mode=pl.Buffered(1))]
        + ([table, table] if rope_cols else []) + cast_in + [whole(a) for a in tail_in],
        out_specs=[out_spec] + cast_out + [whole(s) for s in tail_shape],
        out_shape=[out_shape] + cast_shapes + tail_shape,
        scratch_shapes=[pltpu.VMEM((rows, kd), BF16)] * 2,
        compiler_params=_cparams("arbitrary"),
        name="norm_proj",
    )(x, g.reshape(1, kd), w, *(rope if rope_cols else ()), *[c[0] for c in casts], *tail_in)
    return res if len(res) > 1 else res[0]


def _attn_prompt_kernel(q_ref, k_ref, v_ref, *refs, n_cast):
    cast_in, (o_ref, ks_ref, vs_ref) = refs[:n_cast], refs[n_cast:n_cast + 3]
    cast_out = refs[n_cast + 3:2 * n_cast + 3]
    qn_ref, kn_ref, vn_ref, q16_ref, k16_ref, v16_ref, st_ref, acc_ref, mm_ref, ll_ref = refs[2 * n_cast + 3:]
    _cast_blocks(cast_in, cast_out)
    t = q_ref.shape[1]
    n_keep = ks_ref.shape[1]
    ks_ref[0] = k_ref[0, t - n_keep:, :]
    vs_ref[0] = v_ref[0, t - n_keep:, :]
    l4, l16 = t // 4, t // 16
    nblk = t // N_BACK
    piece = N_BACK // 4
    scale = HEAD_DIM ** -0.5
    nt = (((1,), (1,)), ((), ()))
    pad = jnp.zeros((N_BACK, HEAD_DIM), BF16)
    for ref in (kn_ref, vn_ref, k16_ref, v16_ref):
        ref[0:N_BACK, :] = pad

    qn_ref[...] = q_ref[0].astype(BF16)
    kn_ref[N_BACK:, :] = k_ref[0].astype(BF16)
    vn_ref[N_BACK:, :] = v_ref[0].astype(BF16)
    for src, dst, off in ((q_ref, q16_ref, 0), (k_ref, k16_ref, N_BACK), (v_ref, v16_ref, N_BACK)):
        for r4 in range(4):
            st_ref[r4 * l4:(r4 + 1) * l4, :] = src[0, pl.ds(r4, l4, stride=4), :]
        for r4 in range(4):
            for a in range(4):
                r16 = r4 + 4 * a
                dst[off + r16 * l16:off + (r16 + 1) * l16, :] = (
                    st_ref[pl.ds(r4 * l4 + a, l16, stride=4), :].astype(BF16))

    iota = lambda dim: lax.broadcasted_iota(jnp.int32, (N_BACK, 2 * N_BACK), dim)
    qi, kj = iota(0), iota(1)
    dist = N_BACK + qi - kj
    band = (dist >= 0) & (dist <= N_BACK)
    has_prev = kj >= N_BACK
    dist4 = 4 * (qi % piece - kj % (2 * piece) + piece) + (qi // piece - kj // (2 * piece))
    band4 = (dist4 >= 0) & (dist4 <= N_BACK)
    has_prev4 = kj % (2 * piece) >= piece

    def scores(q, k2, valid):
        s = lax.dot_general(q, k2, nt, preferred_element_type=F32) * scale
        return jnp.where(valid, s, -jnp.inf)

    def fresh(s, v2):
        m = jnp.max(s, axis=-1, keepdims=True)
        p = jnp.exp(s - m)
        wide = lambda c: jnp.broadcast_to(c, (N_BACK, HEAD_DIM))
        return wide(m), wide(jnp.sum(p, axis=-1, keepdims=True)), jnp.dot(p.astype(BF16), v2,
                                                                         preferred_element_type=F32)

    def merged(s, v2, m_old, l_old, a_old):
        m_new = jnp.maximum(m_old, jnp.max(s, axis=-1, keepdims=True))
        alpha = jnp.exp(m_old - m_new)
        p = jnp.exp(s - jnp.concatenate([m_new, m_new], axis=1))
        l_new = alpha * l_old + jnp.sum(p, axis=-1, keepdims=True)
        return m_new, l_new, alpha * a_old + jnp.dot(p.astype(BF16), v2, preferred_element_type=F32)

    def block16(b, carry):
        row0 = pl.multiple_of(b * N_BACK, N_BACK)
        rows = pl.ds(row0, N_BACK)
        later = lax.rem(b, l16 // N_BACK) > 0
        s = scores(q16_ref[rows, :], k16_ref[pl.ds(row0, 2 * N_BACK), :], band & (has_prev | later))
        mm_ref[rows, :], ll_ref[rows, :], acc_ref[rows, :] = fresh(s, v16_ref[pl.ds(row0, 2 * N_BACK), :])
        return carry

    lax.fori_loop(0, nblk, block16, 0, unroll=16)

    n_j = l16 // piece

    def block4(b, carry):
        rho, j = b // n_j, lax.rem(b, n_j)
        base = pl.multiple_of(rho * l16 + j * piece, piece)
        q_rows = [pl.ds(base + a * 4 * l16, piece) for a in range(4)]
        k_rows = [pl.ds(base + a * 4 * l16 + N_BACK - piece, 2 * piece) for a in range(4)]
        gather = lambda ref, rows: jnp.concatenate([ref[r, :] for r in rows], axis=0)
        s = scores(gather(q16_ref, q_rows), gather(k16_ref, k_rows), band4 & (has_prev4 | (j > 0)))
        m, l, acc = merged(s, gather(v16_ref, k_rows), gather(mm_ref, q_rows), gather(ll_ref, q_rows),
                           gather(acc_ref, q_rows))
        for a, r in enumerate(q_rows):
            part = slice(a * piece, (a + 1) * piece)
            mm_ref[r, :], ll_ref[r, :], acc_ref[r, :] = m[part], l[part], acc[part]
        return carry

    lax.fori_loop(0, nblk, block4, 0, unroll=16)

    for src, dst in ((acc_ref, o_ref.at[0]), (mm_ref, acc_ref), (ll_ref, mm_ref)):
        for r4 in range(4):
            for a in range(4):
                r16 = r4 + 4 * a
                st_ref[pl.ds(r4 * l4 + a, l16, stride=4), :] = src[r16 * l16:(r16 + 1) * l16, :]
        for r4 in range(4):
            dst[pl.ds(r4, l4, stride=4), :] = st_ref[r4 * l4:(r4 + 1) * l4, :]

    def block1(b, carry):
        row0 = pl.multiple_of(b * N_BACK, N_BACK)
        rows = pl.ds(row0, N_BACK)
        s = scores(qn_ref[rows, :], kn_ref[pl.ds(row0, 2 * N_BACK), :], band & (has_prev | (b > 0)))
        _, l, acc = merged(s, vn_ref[pl.ds(row0, 2 * N_BACK), :], acc_ref[rows, :], mm_ref[rows, :],
                           o_ref[0, rows, :])
        o_ref[0, rows, :] = acc / l
        return carry

    lax.fori_loop(0, nblk, block1, 0, unroll=16)


def _attn_prompt(proj, b, t, n_keep, casts=()):
    assert t % (16 * N_BACK) == 0
    p3 = proj.reshape(b, t, proj.shape[1])
    blk = lambda off: pl.BlockSpec((1, t, HEAD_DIM), lambda i, h: (i, 0, off + h))
    keep = pl.BlockSpec((1, n_keep, HEAD_DIM), lambda i, h: (i, 0, h))
    cast_in, cast_out, cast_shapes = _cast_specs(casts, b * N_HEADS, lambda i, h: i * N_HEADS + h)
    return pl.pallas_call(
        functools.partial(_attn_prompt_kernel, n_cast=len(casts)),
        grid=(b, N_HEADS),
        in_specs=[blk(0), blk(N_HEADS), blk(2 * N_HEADS)] + cast_in,
        out_specs=[pl.BlockSpec((1, t, HEAD_DIM), lambda i, h: (i, 0, h)), keep, keep] + cast_out,
        out_shape=[jax.ShapeDtypeStruct((b, t, ATTN_WIDTH), F32),
                   jax.ShapeDtypeStruct((b, n_keep, ATTN_WIDTH), F32),
                   jax.ShapeDtypeStruct((b, n_keep, ATTN_WIDTH), F32)] + cast_shapes,
        scratch_shapes=[
            pltpu.VMEM((t, HEAD_DIM), BF16),
            pltpu.VMEM((t + N_BACK, HEAD_DIM), BF16),
            pltpu.VMEM((t + N_BACK, HEAD_DIM), BF16),
            pltpu.VMEM((t, HEAD_DIM), BF16),
            pltpu.VMEM((t + N_BACK, HEAD_DIM), BF16),
            pltpu.VMEM((t + N_BACK, HEAD_DIM), BF16),
            pltpu.VMEM((t, HEAD_DIM), F32),
            pltpu.VMEM((t, HEAD_DIM), F32),
            pltpu.VMEM((t, HEAD_DIM), F32),
            pltpu.VMEM((t, HEAD_DIM), F32),
        ],
        compiler_params=_cparams("parallel", "parallel"),
        name="attn_prompt",
    )(p3, p3, p3, *[c[0] for c in casts])


def _attn_sample_kernel(q_ref, kn_ref, vn_ref, kc_ref, vc_ref, o_ref):
    s_len = q_ref.shape[0]
    n_buf = kc_ref.shape[1] // N_HEADS
    scale = HEAD_DIM ** -0.5
    nt = (((1,), (1,)), ((), ()))

    def count(dist):
        c = jnp.zeros(dist.shape, F32)
        for d in DILATIONS:
            hit = (dist >= 0) & (dist <= N_BACK * d) & ((dist & (d - 1)) == 0)
            c = c + jnp.where(hit, 1.0, 0.0)
        return c

    iota = lambda shape, dim: lax.broadcasted_iota(jnp.int32, shape, dim)
    cc = count(n_buf + iota((s_len, n_buf), 0) - iota((s_len, n_buf), 1))
    cn = count(iota((s_len, s_len), 0) - iota((s_len, s_len), 1))
    for h in range(N_HEADS):
        cols = slice(h * HEAD_DIM, (h + 1) * HEAD_DIM)
        head_rows = pl.ds(h, n_buf, stride=N_HEADS)
        q = q_ref[:, cols].astype(BF16)
        sc = lax.dot_general(q, kc_ref[0, head_rows, :].astype(BF16), nt, preferred_element_type=F32) * scale
        sn = lax.dot_general(q, kn_ref[:, cols].astype(BF16), nt, preferred_element_type=F32) * scale
        sc = jnp.where(cc > 0, sc, -jnp.inf)
        sn = jnp.where(cn > 0, sn, -jnp.inf)
        m = jnp.maximum(jnp.max(sc, axis=-1, keepdims=True), jnp.max(sn, axis=-1, keepdims=True))
        pc = cc * jnp.exp(sc - m)
        pn = cn * jnp.exp(sn - m)
        l = jnp.sum(pc, axis=-1, keepdims=True) + jnp.sum(pn, axis=-1, keepdims=True)
        o = (jnp.dot(pc.astype(BF16), vc_ref[0, head_rows, :].astype(BF16), preferred_element_type=F32)
             + jnp.dot(pn.astype(BF16), vn_ref[:, cols].astype(BF16), preferred_element_type=F32))
        o_ref[:, cols] = o / l


def _attn_sample(proj, cache_k, cache_v, b, s_len):
    n_buf = cache_k.shape[1]
    ck = cache_k.reshape(b, n_buf * N_HEADS, HEAD_DIM)
    cv = cache_v.reshape(b, n_buf * N_HEADS, HEAD_DIM)
    new = lambda c: pl.BlockSpec((s_len, ATTN_WIDTH), lambda i: (i, c))
    old = pl.BlockSpec((1, n_buf * N_HEADS, HEAD_DIM), lambda i: (i, 0, 0))
    return pl.pallas_call(
        _attn_sample_kernel,
        grid=(b,),
        in_specs=[new(0), new(1), new(2), old, old],
        out_specs=new(0),
        out_shape=jax.ShapeDtypeStruct((b * s_len, ATTN_WIDTH), F32),
        compiler_params=_cparams("parallel"),
        name="attn_sample",
    )(proj, proj, proj, ck, cv)


def _silu(z):
    return z * jax.nn.sigmoid(z)


def _mix_out_kernel(x_ref, o_ref, g_ref, wza_ref, wgb_ref, wgc_ref, whi_ref, wzb_ref, cw_ref, w_ref, *refs,
                    tiles_per_seq):
    xs_ref, os_ref, p2_ref, p1_ref, h_ref, cs_ref, hs_ref, css_ref, tail_ref = refs

    def tile(x, o_attn, ln, prev2, prev1):
        tm = x.shape[0]
        hn = _rms_norm(x, g_ref[...]).astype(BF16)
        gate = lambda w: jnp.dot(hn, w[...], preferred_element_type=F32)
        ch = gate(wgc_ref) * gate(whi_ref)
        pos = lax.rem(lax.broadcasted_iota(jnp.int32, (tm, 1), 0), ln)
        ch1 = jnp.where(pos == 0, prev1, pltpu.roll(ch, 1, 0))
        ch2 = jnp.where(pos == 0, prev2, jnp.where(pos == 1, prev1, pltpu.roll(ch, 2, 0)))
        conv = ch2 * cw_ref[0:1, :] + ch1 * cw_ref[1:2, :] + ch * cw_ref[2:3, :]
        o_b = (gate(wgb_ref) * conv * _silu(gate(wzb_ref))).astype(BF16)
        y = jnp.dot(o_b, w_ref[ATTN_WIDTH:, :], preferred_element_type=F32)
        o_a = (o_attn * _silu(gate(wza_ref))).astype(BF16)
        y = y + jnp.dot(o_a, w_ref[0:ATTN_WIDTH, :], preferred_element_type=F32)
        return x + y, ch

    i = pl.program_id(0)
    tm = x_ref.shape[0]

    @pl.when(i % tiles_per_seq == 0)
    def _():
        tail_ref[...] = jnp.zeros(tail_ref.shape, F32)

    h, ch = tile(x_ref[...], o_ref[...], tm, tail_ref[0:1, :], tail_ref[1:2, :])
    h_ref[...] = h
    tail_ref[...] = ch[tm - 2:tm, :]
    cs_ref[0] = ch[tm - 2:tm, :]

    @pl.when(i == pl.num_programs(0) - 1)
    def _():
        rows = xs_ref.shape[0]
        k = p1_ref.shape[0]
        ln = rows // k
        per_row = lambda p: jnp.broadcast_to(p[...], (k, ln, CONV_WIDTH)).reshape(rows, CONV_WIDTH)
        hs, chs = tile(xs_ref[...], os_ref[...], ln, per_row(p2_ref), per_row(p1_ref))
        hs_ref[...] = hs
        css_ref[...] = chs.reshape(k, ln, CONV_WIDTH)[:, ln - 2:, :]


def _mix_out(x, o_attn, xs, os_attn, conv_init_s, g, w_gates, conv_w, w_out, *, tm, tiles_per_seq):
    m, ms = x.shape[0], xs.shape[0]
    nseq = m // (tm * tiles_per_seq)
    once = lambda a: pl.BlockSpec(a.shape, lambda i: (0,) * len(a.shape), pipeline_mode=pl.Buffered(1))
    row = lambda w: pl.BlockSpec((tm, w), lambda i: (i, 0))
    p2, p1 = conv_init_s[:, 0:1], conv_init_s[:, 1:2]
    out_shape = (jax.ShapeDtypeStruct((m, D_MODEL), F32), jax.ShapeDtypeStruct((nseq, 2, CONV_WIDTH), F32),
                 jax.ShapeDtypeStruct((ms, D_MODEL), F32), jax.ShapeDtypeStruct(conv_init_s.shape, F32))
    return pl.pallas_call(
        functools.partial(_mix_out_kernel, tiles_per_seq=tiles_per_seq),
        grid=(m // tm,),
        in_specs=[row(D_MODEL), row(ATTN_WIDTH), pl.BlockSpec((1, D_MODEL), lambda i: (0, 0))]
        + [once(w) for w in w_gates] + [pl.BlockSpec((3, CONV_WIDTH), lambda i: (0, 0)), once(w_out)]
        + [once(a) for a in (xs, os_attn, p2, p1)],
        out_specs=(row(D_MODEL), pl.BlockSpec((1, 2, CONV_WIDTH), lambda i: (i // tiles_per_seq, 0, 0)))
        + tuple(pl.BlockSpec(s.shape, lambda i, n=len(s.shape): (0,) * n) for s in out_shape[2:]),
        out_shape=out_shape,
        scratch_shapes=[pltpu.VMEM((2, CONV_WIDTH), F32)],
        compiler_params=_cparams("arbitrary"),
        name="mix_out",
    )(x, o_attn, g.reshape(1, -1), *w_gates, conv_w, w_out, xs, os_attn, p2, p1)


N_SCAN = 7
GROUPS_PER_TILE = LANES // SSM_GROUP


PROMPT_CHUNK = 32
ROW_C = PROMPT_CHUNK + 1
ROW_DBL = PROMPT_CHUNK + 2
TABLE_ROWS = ROW_DBL + N_SCAN


def _s5_disc_kernel(lr_ref, li_ref, ls_ref, tr_ref, ti_ref):
    lr = lr_ref[...]
    li = li_ref[...]
    step = jnp.exp(ls_ref[...])
    mag = jnp.exp(lr * step)
    ar = mag * jnp.cos(li * step)
    ai = mag * jnp.sin(li * step)
    den = lr * lr + li * li
    nr = ar - 1.0
    tr_ref[ROW_C] = (nr * lr + ai * li) / den
    ti_ref[ROW_C] = (ai * lr - nr * li) / den
    pr = jnp.ones_like(ar)
    pi = jnp.zeros_like(ar)
    for tau in range(PROMPT_CHUNK + 1):
        tr_ref[tau] = pr
        ti_ref[tau] = pi
        dr, di = pr, pi
        pr, pi = pr * ar - pi * ai, pr * ai + pi * ar
    for i in range(N_SCAN):
        tr_ref[ROW_DBL + i] = dr
        ti_ref[ROW_DBL + i] = di
        dr, di = dr * dr - di * di, 2.0 * dr * di


def _s5_tables(lam_re, lam_im, log_step):
    g = lam_re.shape[0]
    dup = lambda a: jnp.concatenate([a, a], axis=-1)
    tr, ti = pl.pallas_call(
        _s5_disc_kernel,
        out_shape=(jax.ShapeDtypeStruct((TABLE_ROWS, g, LANES), F32),) * 2,
        name="s5_disc",
    )(dup(lam_re), dup(lam_im), log_step[:, None])
    return tr.transpose(1, 0, 2), ti.transpose(1, 0, 2)


def _split_bf16(x):
    hi = x.astype(BF16)
    return hi, (x - hi.astype(F32)).astype(BF16)


def _dot_split(a, b, b_is_bf16_exact=False):
    dot = lambda x, y: jnp.dot(x, y, preferred_element_type=F32)
    ah, al = _split_bf16(a)
    if b_is_bf16_exact:
        bh = b.astype(BF16)
        return dot(ah, bh) + dot(al, bh)
    bh, bl = _split_bf16(b)
    return dot(ah, bh) + (dot(ah, bl) + dot(al, bh))


def _pad_rows(x, rows):
    if x.shape[0] == rows:
        return x
    return jnp.concatenate([x, jnp.zeros((rows - x.shape[0], x.shape[1]), x.dtype)], axis=0)


def _s5_build_operators(tr_ref, ti_ref, br_ref, bi_ref, cr_ref, ci_ref, c2r_ref, c2i_ref,
                        wt_ref, qt_ref, kr_ref, ki_ref):
    chunk = PROMPT_CHUNK
    n = SSM_GROUP * chunk
    iota = lambda shape, dim: lax.broadcasted_iota(jnp.int32, shape, dim)
    lane = iota((SSM_GROUP, 2 * SSM_STATE), 1)
    e_sel = jnp.where(iota((chunk, n), 0) == chunk - 1 - iota((chunk, n), 1) // SSM_GROUP, 1.0, 0.0)
    tile = jnp.where(iota((SSM_GROUP, n), 0) == iota((SSM_GROUP, n), 1) % SSM_GROUP, 1.0, 0.0)

    for g in range(GROUPS_PER_TILE):
        tab_r, tab_i = tr_ref[g], ti_ref[g]
        col_r = _pad_rows(tab_r, LANES).T[:SSM_STATE]
        col_i = _pad_rows(tab_i, LANES).T[:SSM_STATE]
        kr_ref[g] = col_r
        ki_ref[g] = col_i
        ccr, cci = col_r[:, ROW_C:ROW_C + 1], col_i[:, ROW_C:ROW_C + 1]
        br, bi = br_ref[g], bi_ref[g]
        bbr = ccr * br - cci * bi
        bbi = ccr * bi + cci * br
        aer = _dot_split(col_r[:, :chunk], e_sel, True)
        aei = _dot_split(col_i[:, :chunk], e_sel, True)
        btr = _dot_split(bbr, tile, True)
        bti = _dot_split(bbi, tile, True)
        ptr = aer * btr - aei * bti
        pti = aer * bti + aei * btr
        wt_ref[g, n:n + SSM_STATE, :] = ptr.astype(BF16)
        wt_ref[g, n + SSM_STATE:n + 2 * SSM_STATE, :] = pti.astype(BF16)
        taps = _dot_split(cr_ref[g], ptr) - _dot_split(ci_ref[g], pti)
        padded = jnp.concatenate([taps, jnp.zeros_like(taps)], axis=1)
        for t in range(chunk):
            sh = SSM_GROUP * (chunk - 1 - t)
            blk = padded if sh == 0 else pltpu.roll(padded, 2 * n - sh, 1)
            wt_ref[g, SSM_GROUP * t:SSM_GROUP * (t + 1), :] = blk[:, :n].astype(BF16)
        c2r, c2i = c2r_ref[g], c2i_ref[g]
        for t in range(chunk):
            ar = tab_r[t + 1:t + 2, :]
            ai = tab_i[t + 1:t + 2, :]
            x1 = jnp.where(lane < SSM_STATE, ar, -ai)
            x2 = jnp.where(lane < SSM_STATE, ai, ar)
            qt_ref[g, SSM_GROUP * t:SSM_GROUP * (t + 1), :] = (c2r * x1 - c2i * x2).astype(BF16)


def _s5_kernel(*refs):
    build_in, (u_ref, us_ref, h0_ref, dsk_ref) = refs[:8], refs[8:12]
    y_ref, hf_ref, ys_ref, hfs_ref = refs[12:16]
    ops0, ops1 = refs[16:20], refs[20:24]
    long_bufs, short_bufs = refs[24:26], refs[26:28]
    k = pl.program_id(0)

    def step(cur, prev):
        _s5_build_operators(*build_in, *cur)
        _s5_scan_body(u_ref, *prev, dsk_ref, y_ref, hf_ref, *long_bufs, chunk=PROMPT_CHUNK, carry=True)
        _s5_scan_body(us_ref, *prev, dsk_ref, h0_ref, ys_ref, hfs_ref, *short_bufs,
                      chunk=us_ref.shape[0] * POS_TILE, carry=False)

    pl.when(k == 0)(lambda: _s5_build_operators(*build_in, *ops0))
    pl.when((k > 0) & (k % 2 == 0))(lambda: step(ops0, ops1))
    pl.when(k % 2 == 1)(lambda: step(ops1, ops0))


def _s5_scan_body(u_ref, wt_ref, qt_ref, kr_ref, ki_ref, dsk_ref, *rest, chunk, carry):
    if carry:
        y_ref, hf_ref, dall_ref, yall_ref = rest
    else:
        h0_ref, y_ref, hf_ref, dall_ref, yall_ref = rest
    nc = u_ref.shape[1] // POS_TILE
    at = lambda s: (s // POS_TILE, pl.ds(s % POS_TILE, nc, stride=POS_TILE), slice(None))
    ncp = dall_ref.shape[2]
    nseq = ncp // LANES if carry else nc
    n = SSM_GROUP * chunk
    n_full = SSM_GROUP * PROMPT_CHUNK
    iota = lambda shape, dim: lax.broadcasted_iota(jnp.int32, shape, dim)
    for s in range(chunk):
        dall_ref[s] = _pad_rows(u_ref[at(s)], ncp).T.astype(BF16)
    lane = iota((SSM_STATE, ncp), 1) & (LANES - 1)
    slot = iota((LANES, LANES), 1)
    if not carry:
        h0_all = _pad_rows(jnp.concatenate([h0_ref[:, gp, :] for gp in range(GROUPS_PER_TILE)], axis=0), LANES).T
    acc = jnp.zeros((LANES, LANES), F32)
    for gp in range(GROUPS_PER_TILE):
        rows = slice(SSM_GROUP * gp, SSM_GROUP * (gp + 1))
        d = dall_ref[:, rows, :].reshape(n, ncp)
        if n == n_full:
            ys = jnp.dot(wt_ref[gp], d, preferred_element_type=F32)
            st = ys[n:]
            ys = ys[:n]
        else:
            ys = jnp.dot(wt_ref[gp, 0:n, 0:n], d, preferred_element_type=F32)
            st = jnp.dot(wt_ref[gp, n_full:, n_full - n:], d, preferred_element_type=F32)
        sr = st[:SSM_STATE]
        si = st[SSM_STATE:]
        if carry:
            for i in range(N_SCAN):
                sh = 1 << i
                ar = kr_ref[gp, :, ROW_DBL + i:ROW_DBL + i + 1]
                ai = ki_ref[gp, :, ROW_DBL + i:ROW_DBL + i + 1]
                pr = jnp.where(lane >= sh, pltpu.roll(sr, sh, 1), 0.0)
                pi = jnp.where(lane >= sh, pltpu.roll(si, sh, 1), 0.0)
                sr, si = sr + ar * pr - ai * pi, si + ar * pi + ai * pr
            hp = jnp.concatenate([jnp.where(lane >= 1, pltpu.roll(sr, 1, 1), 0.0),
                                  jnp.where(lane >= 1, pltpu.roll(si, 1, 1), 0.0)], axis=0)
            for b in range(nseq):
                seq = slice(LANES * b, LANES * (b + 1))
                end = jnp.concatenate([sr[:, seq], si[:, seq]], axis=0)
                acc = jnp.where(slot == 8 * gp + b, pltpu.roll(end, (8 * gp + b + 1) % LANES, 1), acc)
        else:
            hp = h0_all if gp == 0 else pltpu.roll(h0_all, LANES - 8 * gp, 1)
            hp = jnp.where(slot < nseq, hp, 0.0)
            hpr, hpi = hp[:SSM_STATE], hp[SSM_STATE:]
            ar = kr_ref[gp, :, chunk:chunk + 1]
            ai = ki_ref[gp, :, chunk:chunk + 1]
            end = jnp.concatenate([sr + ar * hpr - ai * hpi, si + ar * hpi + ai * hpr], axis=0)
            if gp:
                end = pltpu.roll(end, 8 * gp, 1)
            acc = jnp.where((slot >= 8 * gp) & (slot < 8 * gp + nseq), end, acc)
        y = ys + jnp.dot(qt_ref[gp, 0:n, :], hp.astype(BF16), preferred_element_type=F32)
        yall_ref[:, rows, :] = y.reshape(chunk, SSM_GROUP, ncp)
    hf_ref[...] = acc.T[:8 * GROUPS_PER_TILE].reshape(GROUPS_PER_TILE, 8, LANES)
    for t in range(chunk):
        y_ref[at(t)] = yall_ref[t].T[:nc] + dsk_ref[...] * u_ref[at(t)]


def _s5_layer(tables, b_re, b_im, c_re, c_im, d_skip, u, n_long, us, h0):
    ns = h0.shape[0]
    chunk_s = us.shape[0] * POS_TILE
    n, n_s = SSM_GROUP * PROMPT_CHUNK, SSM_GROUP * chunk_s
    assert u.shape[0] * POS_TILE == PROMPT_CHUNK and u.shape[1] == n_long * LANES * POS_TILE and n_long <= 8
    assert us.shape[1] == ns * POS_TILE and ns == 8 and n % n_s == 0
    gt = GROUPS_PER_TILE
    t = SSM_GROUPS // gt
    cur = lambda k: jnp.minimum(k, t - 1)
    prv = lambda k: jnp.maximum(k - 1, 0)
    built = lambda *s: pl.BlockSpec((gt,) + s, lambda k: (cur(k),) + (0,) * len(s))
    lanes = lambda a: pl.BlockSpec(a.shape[:-1] + (LANES,), lambda k: (0,) * (a.ndim - 1) + (prv(k),))
    states = pl.BlockSpec((gt, 8, LANES), lambda k: (prv(k), 0, 0))
    dup = lambda a: jnp.concatenate([a, a], axis=-1)
    op_set = [pltpu.VMEM((gt, n + 2 * SSM_STATE, n), BF16), pltpu.VMEM((gt, n, 2 * SSM_STATE), BF16),
              pltpu.VMEM((gt, SSM_STATE, LANES), F32), pltpu.VMEM((gt, SSM_STATE, LANES), F32)]
    regroup = lambda c, width: [pltpu.VMEM((c, LANES, width), BF16), pltpu.VMEM((c, LANES, width), F32)]
    return pl.pallas_call(
        _s5_kernel,
        grid=(t + 1,),
        in_specs=[built(TABLE_ROWS, LANES), built(TABLE_ROWS, LANES),
                  built(SSM_STATE, SSM_GROUP), built(SSM_STATE, SSM_GROUP),
                  built(SSM_GROUP, SSM_STATE), built(SSM_GROUP, SSM_STATE),
                  built(SSM_GROUP, LANES), built(SSM_GROUP, LANES),
                  lanes(u), lanes(us), pl.BlockSpec((ns, gt, LANES), lambda k: (0, prv(k), 0)),
                  pl.BlockSpec((1, LANES), lambda k: (0, prv(k)))],
        out_specs=(lanes(u), states, lanes(us), states),
        out_shape=(jax.ShapeDtypeStruct(u.shape, F32), jax.ShapeDtypeStruct((SSM_GROUPS, 8, LANES), F32),
                   jax.ShapeDtypeStruct(us.shape, F32), jax.ShapeDtypeStruct((SSM_GROUPS, 8, LANES), F32)),
        scratch_shapes=op_set + op_set + regroup(PROMPT_CHUNK, n_long * LANES) + regroup(chunk_s, LANES),
        compiler_params=_cparams("arbitrary"),
        name="s5",
    )(*tables, b_re, b_im, c_re, c_im, dup(c_re), dup(c_im), u, us, h0, d_skip.reshape(1, -1))


def _ssm_out(y, x, gn_ref, wz_ref, wg_ref, bg_ref, wo_ref, gf_ref):
    z = jnp.dot(_rms_norm(x, gn_ref[...]).astype(BF16), wz_ref[...], preferred_element_type=F32)
    g = jax.nn.gelu(y, approximate=True)
    gate = jnp.dot(g.astype(BF16), wg_ref[...], preferred_element_type=F32) + bg_ref[...]
    yy = (g * jax.nn.sigmoid(gate)) * _silu(z)
    return _rms_norm(x + jnp.dot(yy.astype(BF16), wo_ref[...], preferred_element_type=F32), gf_ref[...])


def _glu_out_kernel(y_ref, x_ref, ys_ref, xs_ref, *rest):
    params, (o_ref, os_ref) = rest[:-2], rest[-2:]
    o_ref[...] = _ssm_out(y_ref[...], x_ref[...].reshape(y_ref.shape), *params).reshape(o_ref.shape)

    @pl.when(pl.program_id(0) == pl.num_programs(0) - 1)
    def _():
        os_ref[...] = _ssm_out(ys_ref[...], xs_ref[...], *params)


def _glu_out(y, x, ys, xs, g_norm, w_z, w_glu, b_glu, w_out, g_final, *, ct):
    m, w = x.shape
    vec = pl.BlockSpec((1, w), lambda i: (0, 0))
    mat = pl.BlockSpec((w, w), lambda i: (0, 0), pipeline_mode=pl.Buffered(1))
    small = pl.BlockSpec(xs.shape, lambda i: (0, 0))
    sg, nc = y.shape[0], y.shape[1] // POS_TILE
    x_spec = pl.BlockSpec((ct, POS_TILE, w), lambda i: (i // sg, i % sg, 0))
    out, out_s = pl.pallas_call(
        _glu_out_kernel,
        grid=((nc // ct) * sg,),
        in_specs=[pl.BlockSpec((None, ct * POS_TILE, w), lambda i: (i % sg, i // sg, 0)), x_spec, small, small,
                  vec, mat, mat, vec, mat, vec],
        out_specs=(x_spec, small),
        out_shape=(jax.ShapeDtypeStruct((nc, sg * POS_TILE, w), F32), jax.ShapeDtypeStruct(xs.shape, F32)),
        compiler_params=_cparams("arbitrary"),
        name="glu_out",
    )(y, x.reshape(nc, sg * POS_TILE, w), ys, xs, g_norm.reshape(1, w), w_z, w_glu, b_glu.reshape(1, w), w_out,
      g_final.reshape(1, w))
    return out.reshape(m, w), out_s


PAST_LEN = 16384


def kernel(x_prompt, x_sample, cache_win_k, cache_win_v, state_conv, state_ssm_re, state_ssm_im, attn_norm, w_in_ab, conv_w, w_out_ab, ssm_norm, w_in_c, lam_re, lam_im, log_step, b_re, b_im, c_re, c_im, d_skip, w_glu, b_glu, w_out_c, final_norm):
    bp, tp, _ = x_prompt.shape
    bs, ts, _ = x_sample.shape
    n_keep = min(2048, tp)
    xp = x_prompt.reshape(bp * tp, D_MODEL)
    xs = x_sample.reshape(bs * ts, D_MODEL)

    qkv_cols = 3 * ATTN_WIDTH
    gate_casts = [(w_in_ab[0], CONV_WIDTH, qkv_cols // CONV_WIDTH + c) for c in range(5)]
    layer1_casts = [(w_in_c[0], D_MODEL, 0), (w_in_c[0], D_MODEL, 1), (w_glu[0], D_MODEL, 0), (w_out_c[0], D_MODEL, 0)]

    cos_p, sin_p, w_qkv = _rope_tables(tp, tp, 0, casts=[(w_in_ab[0], qkv_cols, 0)])
    proj_p, *w_gates, w_out0, proj_s = _norm_proj(
        xp, attn_norm[0], w_qkv, tm=512, rope=(cos_p, sin_p), rope_cols=2 * ATTN_WIDTH, tiles_per_seq=tp // 512,
        casts=gate_casts + [(w_out_ab[0], D_MODEL, 0)], tail=xs, tail_rope=_rope_tables(bs * ts, ts, PAST_LEN))
    o_p, k_p, v_p, w_u, w_z, w_glu1, w_out1 = _attn_prompt(proj_p, bp, tp, n_keep, casts=layer1_casts)
    o_p = o_p.reshape(bp * tp, ATTN_WIDTH)
    k_p = k_p.reshape(1, bp, n_keep, N_HEADS, HEAD_DIM)
    v_p = v_p.reshape(1, bp, n_keep, N_HEADS, HEAD_DIM)
    o_s = _attn_sample(proj_s, cache_win_k[0], cache_win_v[0], bs, ts)
    h1_p, conv_p, h1_s, conv_s = _mix_out(xp, o_p, xs, o_s, state_conv[0], attn_norm[0], w_gates, conv_w[0], w_out0,
                                          tm=256, tiles_per_seq=tp // 256)
    k_s = proj_s[:, ATTN_WIDTH:2 * ATTN_WIDTH].reshape(1, bs, ts, N_HEADS, HEAD_DIM)
    v_s = proj_s[:, 2 * ATTN_WIDTH:3 * ATTN_WIDTH].reshape(1, bs, ts, N_HEADS, HEAD_DIM)

    assert ts == POS_TILE
    tables = _s5_tables(lam_re[0], lam_im[0], log_step[0])
    u_p, u_s = _norm_proj(h1_p, ssm_norm[0], w_u, chunk=PROMPT_CHUNK, ct=64, tail=h1_s)
    h0 = jnp.concatenate([state_ssm_re[0], state_ssm_im[0]], axis=-1)
    y_p, hf_p, y_s, hf_s = _s5_layer(tables, b_re[0], b_im[0], c_re[0], c_im[0], d_skip[0], u_p, bp, u_s[None], h0)
    out_p, out_s = _glu_out(y_p, h1_p, y_s[0], h1_s, ssm_norm[0], w_z, w_glu1, b_glu[0], w_out1, final_norm, ct=32)
    hf_p = hf_p[:, :bp].transpose(1, 0, 2)[None]
    hf_s = hf_s[:, :bs].transpose(1, 0, 2)[None]
    return (out_p.reshape(bp, tp, D_MODEL), out_s.reshape(bs, ts, D_MODEL),
            k_p, v_p, conv_p[None], hf_p[..., :SSM_STATE], hf_p[..., SSM_STATE:],
            k_s, v_s, conv_s[None], hf_s[..., :SSM_STATE], hf_s[..., SSM_STATE:])
```

```python
import functools
import math

import jax
import jax.numpy as jnp
from jax import lax
from jax.experimental import pallas as pl
from jax.experimental.pallas import tpu as pltpu

D_MODEL = 2048
HEAD_DIM = 128
N_HEADS = 8
ATTN_WIDTH = 1024
CONV_WIDTH = 1024
DILATIONS = (1, 4, 16)
N_BACK = 128
ROPE_THETA = 10000.0
RMS_EPS = 1e-6
SSM_GROUP = 16
SSM_GROUPS = 128
SSM_STATE = 64
LANES = 128
VMEM_LIMIT = 56 * 1024 * 1024

F32 = jnp.float32
BF16 = jnp.bfloat16


def _cparams(*sem):
    return pltpu.CompilerParams(dimension_semantics=sem, vmem_limit_bytes=VMEM_LIMIT)


def _rope_table_kernel(inv_ref, *refs, period, offset):
    n_cast = (len(refs) - 2) // 2
    cos_ref, sin_ref = refs[n_cast:n_cast + 2]
    _cast_blocks(refs[:n_cast], refs[n_cast + 2:])
    rows = cos_ref.shape[0]
    r = lax.broadcasted_iota(jnp.int32, (rows, LANES), 0) + pl.program_id(0) * rows
    pos = (offset + lax.rem(r, period)).astype(F32)
    ang = pos * inv_ref[...]
    lane = lax.broadcasted_iota(jnp.int32, (rows, LANES), 1)
    cos_ref[...] = jnp.cos(ang)
    sin_ref[...] = jnp.where(lane < HEAD_DIM // 2, -1.0, 1.0) * jnp.sin(ang)


def _rope_tables(rows, period, offset, casts=()):
    half = HEAD_DIM // 2
    inv = ROPE_THETA ** (-jnp.arange(half, dtype=F32) / half)
    inv2 = jnp.concatenate([inv, inv])[None, :]
    tr = min(rows, 256)
    table = pl.BlockSpec((tr, LANES), lambda i: (i, 0))
    cast_in, cast_out, cast_shapes = _cast_specs(casts, rows // tr, lambda i: i)
    return pl.pallas_call(
        functools.partial(_rope_table_kernel, period=period, offset=offset),
        grid=(rows // tr,),
        in_specs=[pl.BlockSpec((1, LANES), lambda i: (0, 0))] + cast_in,
        out_specs=[table, table] + cast_out,
        out_shape=[jax.ShapeDtypeStruct((rows, LANES), F32)] * 2 + cast_shapes,
        compiler_params=_cparams("parallel"),
        name="rope_table",
    )(inv2, *[c[0] for c in casts])


def _rms_norm(x, g):
    return x * lax.rsqrt(jnp.mean(x * x, axis=-1, keepdims=True) + RMS_EPS) * g


POS_TILE = 8
COL_SLAB = 256


def _cast_blocks(cast_in, cast_out):
    for src, dst in zip(cast_in, cast_out):
        dst[...] = src[...].astype(BF16)


def _norm_proj_kernel(*refs, rope_cols, n_cast, has_tail):
    it = iter(refs)
    take = lambda n: [next(it) for _ in range(n)]
    x_ref, g_ref, w_ref = take(3)
    rope = take(2) if rope_cols else None
    cast_in = take(n_cast)
    tail_x = take(1)[0] if has_tail else None
    tail_rope = take(2) if has_tail and rope_cols else None
    o_ref = take(1)[0]
    cast_out = take(n_cast)
    tail_o = take(1)[0] if has_tail else None
    hn0, hn1 = take(2)
    k = pl.program_id(0)

    def normalise(dst):
        x = x_ref[...]
        dst[...] = _rms_norm(x.reshape(dst.shape), g_ref[...]).astype(BF16)
        _cast_blocks(cast_in, cast_out)

    def project(hn, out, tables):
        for c in range(w_ref.shape[1] // COL_SLAB):
            cols = slice(c * COL_SLAB, (c + 1) * COL_SLAB)
            acc = jnp.dot(hn, w_ref[:, cols], preferred_element_type=F32)
            if c * COL_SLAB < rope_cols:
                cos_ref, sin_ref = tables
                for h in range(COL_SLAB // HEAD_DIM):
                    xh = acc[:, h * HEAD_DIM:(h + 1) * HEAD_DIM]
                    lanes = slice(c * COL_SLAB + h * HEAD_DIM, c * COL_SLAB + (h + 1) * HEAD_DIM)
                    out[:, lanes] = xh * cos_ref[...] + pltpu.roll(xh, HEAD_DIM // 2, 1) * sin_ref[...]
            else:
                out[:, cols] = acc

    def step(cur, prev):
        normalise(cur)
        project(prev[...], o_ref, rope)

    pl.when(k == 0)(lambda: normalise(hn0))
    pl.when((k > 0) & (k % 2 == 0))(lambda: step(hn0, hn1))
    pl.when(k % 2 == 1)(lambda: step(hn1, hn0))
    if has_tail:
        @pl.when(k == pl.num_programs(0) - 1)
        def _():
            project(_rms_norm(tail_x[...], g_ref[...]).astype(BF16), tail_o, tail_rope)


def _cast_specs(casts, n_blocks, index):
    ins, outs, shapes = [], [], []
    for arr, width, col in casts:
        rows = arr.shape[0] // n_blocks
        ins.append(pl.BlockSpec((rows, width), lambda *k, col=col: (index(*k), col)))
        outs.append(pl.BlockSpec((rows, width), lambda *k: (index(*k), 0)))
        shapes.append(jax.ShapeDtypeStruct((arr.shape[0], width), BF16))
    return ins, outs, shapes


def _norm_proj(x, g, w, *, tm=None, chunk=None, ct=None, rope=None, rope_cols=0, tiles_per_seq=1, casts=(),
               tail=None, tail_rope=None):
    m, kd = x.shape
    n = w.shape[1]
    whole = lambda a: pl.BlockSpec(a.shape, lambda k: (0,) * len(a.shape))
    tail_in = [] if tail is None else [tail] + (list(tail_rope) if rope_cols else [])
    tail_shape = [] if tail is None else [jax.ShapeDtypeStruct((tail.shape[0], n), F32)]
    cur = lambda k: jnp.minimum(k, t - 1)
    prv = lambda k: jnp.maximum(k - 1, 0)
    if chunk is None:
        t, rows = m // tm, tm
        x_spec = pl.BlockSpec((tm, kd), lambda k: (cur(k), 0))
        out_spec = pl.BlockSpec((tm, n), lambda k: (prv(k), 0))
        out_shape = jax.ShapeDtypeStruct((m, n), F32)
    else:
        nc, sg = m // chunk, chunk // POS_TILE
        t, rows = (nc // ct) * sg, ct * POS_TILE
        x = x.reshape(nc, chunk, kd)
        x_spec = pl.BlockSpec((ct, POS_TILE, kd), lambda k: (cur(k) // sg, cur(k) % sg, 0))
        out_spec = pl.BlockSpec((None, rows, n), lambda k: (prv(k) % sg, prv(k) // sg, 0))
        out_shape = jax.ShapeDtypeStruct((sg, nc * POS_TILE, n), F32)
    table = pl.BlockSpec((rows, LANES), lambda k: (prv(k) % tiles_per_seq, 0))
    cast_in, cast_out, cast_shapes = _cast_specs(casts, t, cur)
    res = pl.pallas_call(
        functools.partial(_norm_proj_kernel, rope_cols=rope_cols, n_cast=len(casts), has_tail=tail is not None),
        grid=(t + 1,),
        in_specs=[x_spec, pl.BlockSpec((1, kd), lambda k: (0, 0)),
                  pl.BlockSpec((kd, n), lambda k: (0, 0), pipeline_mode=pl.Buffered(1))]
        + ([table, table] if rope_cols else []) + cast_in + [whole(a) for a in tail_in],
        out_specs=[out_spec] + cast_out + [whole(s) for s in tail_shape],
        out_shape=[out_shape] + cast_shapes + tail_shape,
        scratch_shapes=[pltpu.VMEM((rows, kd), BF16)] * 2,
        compiler_params=_cparams("arbitrary"),
        name="norm_proj",
    )(x, g.reshape(1, kd), w, *(rope if rope_cols else ()), *[c[0] for c in casts], *tail_in)
    return res if len(res) > 1 else res[0]


def _attn_prompt_kernel(q_ref, k_ref, v_ref, *refs, n_cast):
    cast_in, (o_ref, ks_ref, vs_ref) = refs[:n_cast], refs[n_cast:n_cast + 3]
    cast_out = refs[n_cast + 3:2 * n_cast + 3]
    qn_ref, kn_ref, vn_ref, q16_ref, k16_ref, v16_ref, st_ref, acc_ref, mm_ref, ll_ref = refs[2 * n_cast + 3:]
    _cast_blocks(cast_in, cast_out)
    t = q_ref.shape[1]
    n_keep = ks_ref.shape[1]
    ks_ref[0] = k_ref[0, t - n_keep:, :]
    vs_ref[0] = v_ref[0, t - n_keep:, :]
    l4, l16 = t // 4, t // 16
    nblk = t // N_BACK
    piece = N_BACK // 4
    scale = HEAD_DIM ** -0.5
    nt = (((1,), (1,)), ((), ()))
    pad = jnp.zeros((N_BACK, HEAD_DIM), BF16)
    for ref in (kn_ref, vn_ref, k16_ref, v16_ref):
        ref[0:N_BACK, :] = pad

    qn_ref[...] = q_ref[0].astype(BF16)
    kn_ref[N_BACK:, :] = k_ref[0].astype(BF16)
    vn_ref[N_BACK:, :] = v_ref[0].astype(BF16)
    for src, dst, off in ((q_ref, q16_ref, 0), (k_ref, k16_ref, N_BACK), (v_ref, v16_ref, N_BACK)):
        for r4 in range(4):
            st_ref[r4 * l4:(r4 + 1) * l4, :] = src[0, pl.ds(r4, l4, stride=4), :]
        for r4 in range(4):
            for a in range(4):
                r16 = r4 + 4 * a
                dst[off + r16 * l16:off + (r16 + 1) * l16, :] = (
                    st_ref[pl.ds(r4 * l4 + a, l16, stride=4), :].astype(BF16))

    iota = lambda dim: lax.broadcasted_iota(jnp.int32, (N_BACK, 2 * N_BACK), dim)
    qi, kj = iota(0), iota(1)
    dist = N_BACK + qi - kj
    band = (dist >= 0) & (dist <= N_BACK)
    has_prev = kj >= N_BACK
    dist4 = 4 * (qi % piece - kj % (2 * piece) + piece) + (qi // piece - kj // (2 * piece))
    band4 = (dist4 >= 0) & (dist4 <= N_BACK)
    has_prev4 = kj % (2 * piece) >= piece

    def scores(q, k2, valid):
        s = lax.dot_general(q, k2, nt, preferred_element_type=F32) * scale
        return jnp.where(valid, s, -jnp.inf)

    def fresh(s, v2):
        m = jnp.max(s, axis=-1, keepdims=True)
        p = jnp.exp(s - m)
        wide = lambda c: jnp.broadcast_to(c, (N_BACK, HEAD_DIM))
        return wide(m), wide(jnp.sum(p, axis=-1, keepdims=True)), jnp.dot(p.astype(BF16), v2,
                                                                         preferred_element_type=F32)

    def merged(s, v2, m_old, l_old, a_old):
        m_new = jnp.maximum(m_old, jnp.max(s, axis=-1, keepdims=True))
        alpha = jnp.exp(m_old - m_new)
        p = jnp.exp(s - jnp.concatenate([m_new, m_new], axis=1))
        l_new = alpha * l_old + jnp.sum(p, axis=-1, keepdims=True)
        return m_new, l_new, alpha * a_old + jnp.dot(p.astype(BF16), v2, preferred_element_type=F32)

    def block16(b, carry):
        row0 = pl.multiple_of(b * N_BACK, N_BACK)
        rows = pl.ds(row0, N_BACK)
        later = lax.rem(b, l16 // N_BACK) > 0
        s = scores(q16_ref[rows, :], k16_ref[pl.ds(row0, 2 * N_BACK), :], band & (has_prev | later))
        mm_ref[rows, :], ll_ref[rows, :], acc_ref[rows, :] = fresh(s, v16_ref[pl.ds(row0, 2 * N_BACK), :])
        return carry

    lax.fori_loop(0, nblk, block16, 0, unroll=16)

    n_j = l16 // piece

    def block4(b, carry):
        rho, j = b // n_j, lax.rem(b, n_j)
        base = pl.multiple_of(rho * l16 + j * piece, piece)
        q_rows = [pl.ds(base + a * 4 * l16, piece) for a in range(4)]
        k_rows = [pl.ds(base + a * 4 * l16 + N_BACK - piece, 2 * piece) for a in range(4)]
        gather = lambda ref, rows: jnp.concatenate([ref[r, :] for r in rows], axis=0)
        s = scores(gather(q16_ref, q_rows), gather(k16_ref, k_rows), band4 & (has_prev4 | (j > 0)))
        m, l, acc = merged(s, gather(v16_ref, k_rows), gather(mm_ref, q_rows), gather(ll_ref, q_rows),
                           gather(acc_ref, q_rows))
        for a, r in enumerate(q_rows):
            part = slice(a * piece, (a + 1) * piece)
            mm_ref[r, :], ll_ref[r, :], acc_ref[r, :] = m[part], l[part], acc[part]
        return carry

    lax.fori_loop(0, nblk, block4, 0, unroll=16)

    for src, dst in ((acc_ref, o_ref.at[0]), (mm_ref, acc_ref), (ll_ref, mm_ref)):
        for r4 in range(4):
            for a in range(4):
                r16 = r4 + 4 * a
                st_ref[pl.ds(r4 * l4 + a, l16, stride=4), :] = src[r16 * l16:(r16 + 1) * l16, :]
        for r4 in range(4):
            dst[pl.ds(r4, l4, stride=4), :] = st_ref[r4 * l4:(r4 + 1) * l4, :]

    def block1(b, carry):
        row0 = pl.multiple_of(b * N_BACK, N_BACK)
        rows = pl.ds(row0, N_BACK)
        s = scores(qn_ref[rows, :], kn_ref[pl.ds(row0, 2 * N_BACK), :], band & (has_prev | (b > 0)))
        _, l, acc = merged(s, vn_ref[pl.ds(row0, 2 * N_BACK), :], acc_ref[rows, :], mm_ref[rows, :],
                           o_ref[0, rows, :])
        o_ref[0, rows, :] = acc / l
        return carry

    lax.fori_loop(0, nblk, block1, 0, unroll=16)


def _attn_prompt(proj, b, t, n_keep, casts=()):
    assert t % (16 * N_BACK) == 0
    p3 = proj.reshape(b, t, proj.shape[1])
    blk = lambda off: pl.BlockSpec((1, t, HEAD_DIM), lambda i, h: (i, 0, off + h))
    keep = pl.BlockSpec((1, n_keep, HEAD_DIM), lambda i, h: (i, 0, h))
    cast_in, cast_out, cast_shapes = _cast_specs(casts, b * N_HEADS, lambda i, h: i * N_HEADS + h)
    return pl.pallas_call(
        functools.partial(_attn_prompt_kernel, n_cast=len(casts)),
        grid=(b, N_HEADS),
        in_specs=[blk(0), blk(N_HEADS), blk(2 * N_HEADS)] + cast_in,
        out_specs=[pl.BlockSpec((1, t, HEAD_DIM), lambda i, h: (i, 0, h)), keep, keep] + cast_out,
        out_shape=[jax.ShapeDtypeStruct((b, t, ATTN_WIDTH), F32),
                   jax.ShapeDtypeStruct((b, n_keep, ATTN_WIDTH), F32),
                   jax.ShapeDtypeStruct((b, n_keep, ATTN_WIDTH), F32)] + cast_shapes,
        scratch_shapes=[
            pltpu.VMEM((t, HEAD_DIM), BF16),
            pltpu.VMEM((t + N_BACK, HEAD_DIM), BF16),
            pltpu.VMEM((t + N_BACK, HEAD_DIM), BF16),
            pltpu.VMEM((t, HEAD_DIM), BF16),
            pltpu.VMEM((t + N_BACK, HEAD_DIM), BF16),
            pltpu.VMEM((t + N_BACK, HEAD_DIM), BF16),
            pltpu.VMEM((t, HEAD_DIM), F32),
            pltpu.VMEM((t, HEAD_DIM), F32),
            pltpu.VMEM((t, HEAD_DIM), F32),
            pltpu.VMEM((t, HEAD_DIM), F32),
        ],
        compiler_params=_cparams("parallel", "parallel"),
        name="attn_prompt",
    )(p3, p3, p3, *[c[0] for c in casts])


COARSE = DILATIONS[-1]
NEAR = N_BACK * DILATIONS[-2]


def _attn_sample_kernel(q_ref, kn_ref, vn_ref, kfar_ref, knear_ref, vfar_ref, vnear_ref, o_ref):
    s_len = q_ref.shape[0]
    far_groups, near_groups = kfar_ref.shape[0], knear_ref.shape[0]
    n_far, n_near = far_groups * s_len, near_groups * COARSE
    n_buf = (far_groups + near_groups) * COARSE
    scale = HEAD_DIM ** -0.5
    nt = (((1,), (1,)), ((), ()))

    def count(dist):
        c = jnp.zeros(dist.shape, F32)
        for d in DILATIONS:
            hit = (dist >= 0) & (dist <= N_BACK * d) & ((dist & (d - 1)) == 0)
            c = c + jnp.where(hit, 1.0, 0.0)
        return c

    iota = lambda shape, dim: lax.broadcasted_iota(jnp.int32, shape, dim)
    query = lambda width: n_buf + iota((s_len, width), 0)
    col = iota((s_len, n_far + n_near), 1)
    far_pos = COARSE * (col // s_len) + col % s_len
    near_pos = far_groups * COARSE + (col - n_far)
    cc = count(query(n_far + n_near) - jnp.where(col < n_far, far_pos, near_pos))
    cn = count(iota((s_len, s_len), 0) - iota((s_len, s_len), 1))

    def head_rows(far_ref, near_ref, h):
        far = far_ref[:, pl.ds(h, s_len, stride=N_HEADS), :].reshape(n_far, HEAD_DIM)
        near = near_ref[:, pl.ds(h, COARSE, stride=N_HEADS), :].reshape(n_near, HEAD_DIM)
        return jnp.concatenate([far, near], axis=0).astype(BF16)

    for h in range(N_HEADS):
        cols = slice(h * HEAD_DIM, (h + 1) * HEAD_DIM)
        q = q_ref[:, cols].astype(BF16)
        sc = lax.dot_general(q, head_rows(kfar_ref, knear_ref, h), nt, preferred_element_type=F32) * scale
        sn = lax.dot_general(q, kn_ref[:, cols].astype(BF16), nt, preferred_element_type=F32) * scale
        sc = jnp.where(cc > 0, sc, -jnp.inf)
        sn = jnp.where(cn > 0, sn, -jnp.inf)
        m = jnp.maximum(jnp.max(sc, axis=-1, keepdims=True), jnp.max(sn, axis=-1, keepdims=True))
        pc = cc * jnp.exp(sc - m)
        pn = cn * jnp.exp(sn - m)
        l = jnp.sum(pc, axis=-1, keepdims=True) + jnp.sum(pn, axis=-1, keepdims=True)
        o = (jnp.dot(pc.astype(BF16), head_rows(vfar_ref, vnear_ref, h), preferred_element_type=F32)
             + jnp.dot(pn.astype(BF16), vn_ref[:, cols].astype(BF16), preferred_element_type=F32))
        o_ref[:, cols] = o / l


def _attn_sample(proj, cache_k, cache_v, b, s_len):
    n_buf = cache_k.shape[1]
    near_groups = NEAR // COARSE
    far_groups = n_buf // COARSE - near_groups
    assert n_buf % COARSE == 0 and s_len <= COARSE and far_groups % near_groups == 0 and far_groups > 0
    grouped = lambda c: c.reshape(b, n_buf // COARSE, COARSE * N_HEADS, HEAD_DIM)
    new = lambda c: pl.BlockSpec((s_len, ATTN_WIDTH), lambda i: (i, c))
    far = pl.BlockSpec((None, far_groups, s_len * N_HEADS, HEAD_DIM), lambda i: (i, 0, 0, 0))
    near = pl.BlockSpec((None, near_groups, COARSE * N_HEADS, HEAD_DIM),
                        lambda i: (i, far_groups // near_groups, 0, 0))
    ck, cv = grouped(cache_k), grouped(cache_v)
    return pl.pallas_call(
        _attn_sample_kernel,
        grid=(b,),
        in_specs=[new(0), new(1), new(2), far, near, far, near],
        out_specs=new(0),
        out_shape=jax.ShapeDtypeStruct((b * s_len, ATTN_WIDTH), F32),
        compiler_params=_cparams("parallel"),
        name="attn_sample",
    )(proj, proj, proj, ck, ck, cv, cv)


def _silu(z):
    return z * jax.nn.sigmoid(z)


def _mix_out_kernel(x_ref, o_ref, g_ref, wza_ref, wgb_ref, wgc_ref, whi_ref, wzb_ref, cw_ref, w_ref, *refs,
                    tiles_per_seq):
    xs_ref, os_ref, p2_ref, p1_ref, h_ref, cs_ref, hs_ref, css_ref, tail_ref = refs

    def tile(x, o_attn, ln, prev2, prev1):
        tm = x.shape[0]
        hn = _rms_norm(x, g_ref[...]).astype(BF16)
        gate = lambda w: jnp.dot(hn, w[...], preferred_element_type=F32)
        ch = gate(wgc_ref) * gate(whi_ref)
        pos = lax.rem(lax.broadcasted_iota(jnp.int32, (tm, 1), 0), ln)
        ch1 = jnp.where(pos == 0, prev1, pltpu.roll(ch, 1, 0))
        ch2 = jnp.where(pos == 0, prev2, jnp.where(pos == 1, prev1, pltpu.roll(ch, 2, 0)))
        conv = ch2 * cw_ref[0:1, :] + ch1 * cw_ref[1:2, :] + ch * cw_ref[2:3, :]
        o_b = (gate(wgb_ref) * conv * _silu(gate(wzb_ref))).astype(BF16)
        y = jnp.dot(o_b, w_ref[ATTN_WIDTH:, :], preferred_element_type=F32)
        o_a = (o_attn * _silu(gate(wza_ref))).astype(BF16)
        y = y + jnp.dot(o_a, w_ref[0:ATTN_WIDTH, :], preferred_element_type=F32)
        return x + y, ch

    i = pl.program_id(0)
    tm = x_ref.shape[0]

    @pl.when(i % tiles_per_seq == 0)
    def _():
        tail_ref[...] = jnp.zeros(tail_ref.shape, F32)

    h, ch = tile(x_ref[...], o_ref[...], tm, tail_ref[0:1, :], tail_ref[1:2, :])
    h_ref[...] = h
    tail_ref[...] = ch[tm - 2:tm, :]
    cs_ref[0] = ch[tm - 2:tm, :]

    @pl.when(i == pl.num_programs(0) - 1)
    def _():
        rows = xs_ref.shape[0]
        k = p1_ref.shape[0]
        ln = rows // k
        per_row = lambda p: jnp.broadcast_to(p[...], (k, ln, CONV_WIDTH)).reshape(rows, CONV_WIDTH)
        hs, chs = tile(xs_ref[...], os_ref[...], ln, per_row(p2_ref), per_row(p1_ref))
        hs_ref[...] = hs
        css_ref[...] = chs.reshape(k, ln, CONV_WIDTH)[:, ln - 2:, :]


def _mix_out(x, o_attn, xs, os_attn, conv_init_s, g, w_gates, conv_w, w_out, *, tm, tiles_per_seq):
    m, ms = x.shape[0], xs.shape[0]
    nseq = m // (tm * tiles_per_seq)
    once = lambda a: pl.BlockSpec(a.shape, lambda i: (0,) * len(a.shape), pipeline_mode=pl.Buffered(1))
    row = lambda w: pl.BlockSpec((tm, w), lambda i: (i, 0))
    p2, p1 = conv_init_s[:, 0:1], conv_init_s[:, 1:2]
    out_shape = (jax.ShapeDtypeStruct((m, D_MODEL), F32), jax.ShapeDtypeStruct((nseq, 2, CONV_WIDTH), F32),
                 jax.ShapeDtypeStruct((ms, D_MODEL), F32), jax.ShapeDtypeStruct(conv_init_s.shape, F32))
    return pl.pallas_call(
        functools.partial(_mix_out_kernel, tiles_per_seq=tiles_per_seq),
        grid=(m // tm,),
        in_specs=[row(D_MODEL), row(ATTN_WIDTH), pl.BlockSpec((1, D_MODEL), lambda i: (0, 0))]
        + [once(w) for w in w_gates] + [pl.BlockSpec((3, CONV_WIDTH), lambda i: (0, 0)), once(w_out)]
        + [once(a) for a in (xs, os_attn, p2, p1)],
        out_specs=(row(D_MODEL), pl.BlockSpec((1, 2, CONV_WIDTH), lambda i: (i // tiles_per_seq, 0, 0)))
        + tuple(pl.BlockSpec(s.shape, lambda i, n=len(s.shape): (0,) * n) for s in out_shape[2:]),
        out_shape=out_shape,
        scratch_shapes=[pltpu.VMEM((2, CONV_WIDTH), F32)],
        compiler_params=_cparams("arbitrary"),
        name="mix_out",
    )(x, o_attn, g.reshape(1, -1), *w_gates, conv_w, w_out, xs, os_attn, p2, p1)


N_SCAN = 7
GROUPS_PER_TILE = LANES // SSM_GROUP


PROMPT_CHUNK = 32
ROW_C = PROMPT_CHUNK + 1
ROW_DBL = PROMPT_CHUNK + 2
TABLE_ROWS = ROW_DBL + N_SCAN


def _s5_disc_kernel(lr_ref, li_ref, ls_ref, tr_ref, ti_ref):
    lr = lr_ref[...]
    li = li_ref[...]
    step = jnp.exp(ls_ref[...])
    mag = jnp.exp(lr * step)
    ar = mag * jnp.cos(li * step)
    ai = mag * jnp.sin(li * step)
    den = lr * lr + li * li
    nr = ar - 1.0
    tr_ref[ROW_C] = (nr * lr + ai * li) / den
    ti_ref[ROW_C] = (ai * lr - nr * li) / den
    pr = jnp.ones_like(ar)
    pi = jnp.zeros_like(ar)
    for tau in range(PROMPT_CHUNK + 1):
        tr_ref[tau] = pr
        ti_ref[tau] = pi
        dr, di = pr, pi
        pr, pi = pr * ar - pi * ai, pr * ai + pi * ar
    for i in range(N_SCAN):
        tr_ref[ROW_DBL + i] = dr
        ti_ref[ROW_DBL + i] = di
        dr, di = dr * dr - di * di, 2.0 * dr * di


def _s5_tables(lam_re, lam_im, log_step):
    g = lam_re.shape[0]
    dup = lambda a: jnp.concatenate([a, a], axis=-1)
    tr, ti = pl.pallas_call(
        _s5_disc_kernel,
        out_shape=(jax.ShapeDtypeStruct((TABLE_ROWS, g, LANES), F32),) * 2,
        name="s5_disc",
    )(dup(lam_re), dup(lam_im), log_step[:, None])
    return tr.transpose(1, 0, 2), ti.transpose(1, 0, 2)


def _split_bf16(x):
    hi = x.astype(BF16)
    return hi, (x - hi.astype(F32)).astype(BF16)


def _dot_split(a, b, b_is_bf16_exact=False):
    dot = lambda x, y: jnp.dot(x, y, preferred_element_type=F32)
    ah, al = _split_bf16(a)
    if b_is_bf16_exact:
        bh = b.astype(BF16)
        return dot(ah, bh) + dot(al, bh)
    bh, bl = _split_bf16(b)
    return dot(ah, bh) + (dot(ah, bl) + dot(al, bh))


def _pad_rows(x, rows):
    if x.shape[0] == rows:
        return x
    return jnp.concatenate([x, jnp.zeros((rows - x.shape[0], x.shape[1]), x.dtype)], axis=0)


def _s5_build_operators(tr_ref, ti_ref, br_ref, bi_ref, cr_ref, ci_ref, c2r_ref, c2i_ref,
                        wt_ref, qt_ref, kr_ref, ki_ref):
    chunk = PROMPT_CHUNK
    n = SSM_GROUP * chunk
    iota = lambda shape, dim: lax.broadcasted_iota(jnp.int32, shape, dim)
    lane = iota((SSM_GROUP, 2 * SSM_STATE), 1)
    e_sel = jnp.where(iota((chunk, n), 0) == chunk - 1 - iota((chunk, n), 1) // SSM_GROUP, 1.0, 0.0)
    tile = jnp.where(iota((SSM_GROUP, n), 0) == iota((SSM_GROUP, n), 1) % SSM_GROUP, 1.0, 0.0)

    for g in range(GROUPS_PER_TILE):
        tab_r, tab_i = tr_ref[g], ti_ref[g]
        col_r = _pad_rows(tab_r, LANES).T[:SSM_STATE]
        col_i = _pad_rows(tab_i, LANES).T[:SSM_STATE]
        kr_ref[g] = col_r
        ki_ref[g] = col_i
        ccr, cci = col_r[:, ROW_C:ROW_C + 1], col_i[:, ROW_C:ROW_C + 1]
        br, bi = br_ref[g], bi_ref[g]
        bbr = ccr * br - cci * bi
        bbi = ccr * bi + cci * br
        aer = _dot_split(col_r[:, :chunk], e_sel, True)
        aei = _dot_split(col_i[:, :chunk], e_sel, True)
        btr = _dot_split(bbr, tile, True)
        bti = _dot_split(bbi, tile, True)
        ptr = aer * btr - aei * bti
        pti = aer * bti + aei * btr
        wt_ref[g, n:n + SSM_STATE, :] = ptr.astype(BF16)
        wt_ref[g, n + SSM_STATE:n + 2 * SSM_STATE, :] = pti.astype(BF16)
        taps = _dot_split(cr_ref[g], ptr) - _dot_split(ci_ref[g], pti)
        padded = jnp.concatenate([taps, jnp.zeros_like(taps)], axis=1)
        for t in range(chunk):
            sh = SSM_GROUP * (chunk - 1 - t)
            blk = padded if sh == 0 else pltpu.roll(padded, 2 * n - sh, 1)
            wt_ref[g, SSM_GROUP * t:SSM_GROUP * (t + 1), :] = blk[:, :n].astype(BF16)
        c2r, c2i = c2r_ref[g], c2i_ref[g]
        for t in range(chunk):
            ar = tab_r[t + 1:t + 2, :]
            ai = tab_i[t + 1:t + 2, :]
            x1 = jnp.where(lane < SSM_STATE, ar, -ai)
            x2 = jnp.where(lane < SSM_STATE, ai, ar)
            qt_ref[g, SSM_GROUP * t:SSM_GROUP * (t + 1), :] = (c2r * x1 - c2i * x2).astype(BF16)


def _s5_kernel(*refs):
    build_in, (u_ref, us_ref, h0_ref, dsk_ref) = refs[:8], refs[8:12]
    y_ref, hf_ref, ys_ref, hfs_ref = refs[12:16]
    ops0, ops1 = refs[16:20], refs[20:24]
    long_bufs, short_bufs = refs[24:26], refs[26:28]
    k = pl.program_id(0)

    def step(cur, prev):
        _s5_build_operators(*build_in, *cur)
        _s5_scan_body(u_ref, *prev, dsk_ref, y_ref, hf_ref, *long_bufs, chunk=PROMPT_CHUNK, carry=True)
        _s5_scan_body(us_ref, *prev, dsk_ref, h0_ref, ys_ref, hfs_ref, *short_bufs,
                      chunk=us_ref.shape[0] * POS_TILE, carry=False)

    pl.when(k == 0)(lambda: _s5_build_operators(*build_in, *ops0))
    pl.when((k > 0) & (k % 2 == 0))(lambda: step(ops0, ops1))
    pl.when(k % 2 == 1)(lambda: step(ops1, ops0))


def _s5_scan_body(u_ref, wt_ref, qt_ref, kr_ref, ki_ref, dsk_ref, *rest, chunk, carry):
    if carry:
        y_ref, hf_ref, dall_ref, yall_ref = rest
    else:
        h0_ref, y_ref, hf_ref, dall_ref, yall_ref = rest
    nc = u_ref.shape[1] // POS_TILE
    at = lambda s: (s // POS_TILE, pl.ds(s % POS_TILE, nc, stride=POS_TILE), slice(None))
    ncp = dall_ref.shape[2]
    nseq = ncp // LANES if carry else nc
    n = SSM_GROUP * chunk
    n_full = SSM_GROUP * PROMPT_CHUNK
    iota = lambda shape, dim: lax.broadcasted_iota(jnp.int32, shape, dim)
    for s in range(chunk):
        dall_ref[s] = _pad_rows(u_ref[at(s)], ncp).T.astype(BF16)
    lane = iota((SSM_STATE, ncp), 1) & (LANES - 1)
    slot = iota((LANES, LANES), 1)
    if not carry:
        h0_all = _pad_rows(jnp.concatenate([h0_ref[:, gp, :] for gp in range(GROUPS_PER_TILE)], axis=0), LANES).T
    acc = jnp.zeros((LANES, LANES), F32)
    for gp in range(GROUPS_PER_TILE):
        rows = slice(SSM_GROUP * gp, SSM_GROUP * (gp + 1))
        d = dall_ref[:, rows, :].reshape(n, ncp)
        if n == n_full:
            ys = jnp.dot(wt_ref[gp], d, preferred_element_type=F32)
            st = ys[n:]
            ys = ys[:n]
        else:
            ys = jnp.dot(wt_ref[gp, 0:n, 0:n], d, preferred_element_type=F32)
            st = jnp.dot(wt_ref[gp, n_full:, n_full - n:], d, preferred_element_type=F32)
        sr = st[:SSM_STATE]
        si = st[SSM_STATE:]
        if carry:
            for i in range(N_SCAN):
                sh = 1 << i
                ar = kr_ref[gp, :, ROW_DBL + i:ROW_DBL + i + 1]
                ai = ki_ref[gp, :, ROW_DBL + i:ROW_DBL + i + 1]
                pr = jnp.where(lane >= sh, pltpu.roll(sr, sh, 1), 0.0)
                pi = jnp.where(lane >= sh, pltpu.roll(si, sh, 1), 0.0)
                sr, si = sr + ar * pr - ai * pi, si + ar * pi + ai * pr
            hp = jnp.concatenate([jnp.where(lane >= 1, pltpu.roll(sr, 1, 1), 0.0),
                                  jnp.where(lane >= 1, pltpu.roll(si, 1, 1), 0.0)], axis=0)
            for b in range(nseq):
                seq = slice(LANES * b, LANES * (b + 1))
                end = jnp.concatenate([sr[:, seq], si[:, seq]], axis=0)
                acc = jnp.where(slot == 8 * gp + b, pltpu.roll(end, (8 * gp + b + 1) % LANES, 1), acc)
        else:
            hp = h0_all if gp == 0 else pltpu.roll(h0_all, LANES - 8 * gp, 1)
            hp = jnp.where(slot < nseq, hp, 0.0)
            hpr, hpi = hp[:SSM_STATE], hp[SSM_STATE:]
            ar = kr_ref[gp, :, chunk:chunk + 1]
            ai = ki_ref[gp, :, chunk:chunk + 1]
            end = jnp.concatenate([sr + ar * hpr - ai * hpi, si + ar * hpi + ai * hpr], axis=0)
            if gp:
                end = pltpu.roll(end, 8 * gp, 1)
            acc = jnp.where((slot >= 8 * gp) & (slot < 8 * gp + nseq), end, acc)
        y = ys + jnp.dot(qt_ref[gp, 0:n, :], hp.astype(BF16), preferred_element_type=F32)
        yall_ref[:, rows, :] = y.reshape(chunk, SSM_GROUP, ncp)
    hf_ref[...] = acc.T[:8 * GROUPS_PER_TILE].reshape(GROUPS_PER_TILE, 8, LANES)
    for t in range(chunk):
        y_ref[at(t)] = yall_ref[t].T[:nc] + dsk_ref[...] * u_ref[at(t)]


def _s5_layer(tables, b_re, b_im, c_re, c_im, d_skip, u, n_long, us, h0):
    ns = h0.shape[0]
    chunk_s = us.shape[0] * POS_TILE
    n, n_s = SSM_GROUP * PROMPT_CHUNK, SSM_GROUP * chunk_s
    assert u.shape[0] * POS_TILE == PROMPT_CHUNK and u.shape[1] == n_long * LANES * POS_TILE and n_long <= 8
    assert us.shape[1] == ns * POS_TILE and ns == 8 and n % n_s == 0
    gt = GROUPS_PER_TILE
    t = SSM_GROUPS // gt
    cur = lambda k: jnp.minimum(k, t - 1)
    prv = lambda k: jnp.maximum(k - 1, 0)
    built = lambda *s: pl.BlockSpec((gt,) + s, lambda k: (cur(k),) + (0,) * len(s))
    lanes = lambda a: pl.BlockSpec(a.shape[:-1] + (LANES,), lambda k: (0,) * (a.ndim - 1) + (prv(k),))
    states = pl.BlockSpec((gt, 8, LANES), lambda k: (prv(k), 0, 0))
    dup = lambda a: jnp.concatenate([a, a], axis=-1)
    op_set = [pltpu.VMEM((gt, n + 2 * SSM_STATE, n), BF16), pltpu.VMEM((gt, n, 2 * SSM_STATE), BF16),
              pltpu.VMEM((gt, SSM_STATE, LANES), F32), pltpu.VMEM((gt, SSM_STATE, LANES), F32)]
    regroup = lambda c, width: [pltpu.VMEM((c, LANES, width), BF16), pltpu.VMEM((c, LANES, width), F32)]
    return pl.pallas_call(
        _s5_kernel,
        grid=(t + 1,),
        in_specs=[built(TABLE_ROWS, LANES), built(TABLE_ROWS, LANES),
                  built(SSM_STATE, SSM_GROUP), built(SSM_STATE, SSM_GROUP),
                  built(SSM_GROUP, SSM_STATE), built(SSM_GROUP, SSM_STATE),
                  built(SSM_GROUP, LANES), built(SSM_GROUP, LANES),
                  lanes(u), lanes(us), pl.BlockSpec((ns, gt, LANES), lambda k: (0, prv(k), 0)),
                  pl.BlockSpec((1, LANES), lambda k: (0, prv(k)))],
        out_specs=(lanes(u), states, lanes(us), states),
        out_shape=(jax.ShapeDtypeStruct(u.shape, F32), jax.ShapeDtypeStruct((SSM_GROUPS, 8, LANES), F32),
                   jax.ShapeDtypeStruct(us.shape, F32), jax.ShapeDtypeStruct((SSM_GROUPS, 8, LANES), F32)),
        scratch_shapes=op_set + op_set + regroup(PROMPT_CHUNK, n_long * LANES) + regroup(chunk_s, LANES),
        compiler_params=_cparams("arbitrary"),
        name="s5",
    )(*tables, b_re, b_im, c_re, c_im, dup(c_re), dup(c_im), u, us, h0, d_skip.reshape(1, -1))


def _ssm_out(y, x, gn_ref, wz_ref, wg_ref, bg_ref, wo_ref, gf_ref):
    z = jnp.dot(_rms_norm(x, gn_ref[...]).astype(BF16), wz_ref[...], preferred_element_type=F32)
    g = jax.nn.gelu(y, approximate=True)
    gate = jnp.dot(g.astype(BF16), wg_ref[...], preferred_element_type=F32) + bg_ref[...]
    yy = (g * jax.nn.sigmoid(gate)) * _silu(z)
    return _rms_norm(x + jnp.dot(yy.astype(BF16), wo_ref[...], preferred_element_type=F32), gf_ref[...])


def _glu_out_kernel(y_ref, x_ref, ys_ref, xs_ref, *rest):
    params, (o_ref, os_ref) = rest[:-2], rest[-2:]
    o_ref[...] = _ssm_out(y_ref[...], x_ref[...].reshape(y_ref.shape), *params).reshape(o_ref.shape)

    @pl.when(pl.program_id(0) == pl.num_programs(0) - 1)
    def _():
        os_ref[...] = _ssm_out(ys_ref[...], xs_ref[...], *params)


def _glu_out(y, x, ys, xs, g_norm, w_z, w_glu, b_glu, w_out, g_final, *, ct):
    m, w = x.shape
    vec = pl.BlockSpec((1, w), lambda i: (0, 0))
    mat = pl.BlockSpec((w, w), lambda i: (0, 0), pipeline_mode=pl.Buffered(1))
    small = pl.BlockSpec(xs.shape, lambda i: (0, 0))
    sg, nc = y.shape[0], y.shape[1] // POS_TILE
    x_spec = pl.BlockSpec((ct, POS_TILE, w), lambda i: (i // sg, i % sg, 0))
    out, out_s = pl.pallas_call(
        _glu_out_kernel,
        grid=((nc // ct) * sg,),
        in_specs=[pl.BlockSpec((None, ct * POS_TILE, w), lambda i: (i % sg, i // sg, 0)), x_spec, small, small,
                  vec, mat, mat, vec, mat, vec],
        out_specs=(x_spec, small),
        out_shape=(jax.ShapeDtypeStruct((nc, sg * POS_TILE, w), F32), jax.ShapeDtypeStruct(xs.shape, F32)),
        compiler_params=_cparams("arbitrary"),
        name="glu_out",
    )(y, x.reshape(nc, sg * POS_TILE, w), ys, xs, g_norm.reshape(1, w), w_z, w_glu, b_glu.reshape(1, w), w_out,
      g_final.reshape(1, w))
    return out.reshape(m, w), out_s


PAST_LEN = 16384


def kernel(x_prompt, x_sample, cache_win_k, cache_win_v, state_conv, state_ssm_re, state_ssm_im, attn_norm, w_in_ab, conv_w, w_out_ab, ssm_norm, w_in_c, lam_re, lam_im, log_step, b_re, b_im, c_re, c_im, d_skip, w_glu, b_glu, w_out_c, final_norm):
    bp, tp, _ = x_prompt.shape
    bs, ts, _ = x_sample.shape
    n_keep = min(2048, tp)
    xp = x_prompt.reshape(bp * tp, D_MODEL)
    xs = x_sample.reshape(bs * ts, D_MODEL)

    qkv_cols = 3 * ATTN_WIDTH
    gate_casts = [(w_in_ab[0], CONV_WIDTH, qkv_cols // CONV_WIDTH + c) for c in range(5)]
    layer1_casts = [(w_in_c[0], D_MODEL, 0), (w_in_c[0], D_MODEL, 1), (w_glu[0], D_MODEL, 0), (w_out_c[0], D_MODEL, 0)]

    cos_p, sin_p, w_qkv = _rope_tables(tp, tp, 0, casts=[(w_in_ab[0], qkv_cols, 0)])
    proj_p, *w_gates, w_out0, proj_s = _norm_proj(
        xp, attn_norm[0], w_qkv, tm=512, rope=(cos_p, sin_p), rope_cols=2 * ATTN_WIDTH, tiles_per_seq=tp // 512,
        casts=gate_casts + [(w_out_ab[0], D_MODEL, 0)], tail=xs, tail_rope=_rope_tables(bs * ts, ts, PAST_LEN))
    o_p, k_p, v_p, w_u, w_z, w_glu1, w_out1 = _attn_prompt(proj_p, bp, tp, n_keep, casts=layer1_casts)
    o_p = o_p.reshape(bp * tp, ATTN_WIDTH)
    k_p = k_p.reshape(1, bp, n_keep, N_HEADS, HEAD_DIM)
    v_p = v_p.reshape(1, bp, n_keep, N_HEADS, HEAD_DIM)
    o_s = _attn_sample(proj_s, cache_win_k[0], cache_win_v[0], bs, ts)
    h1_p, conv_p, h1_s, conv_s = _mix_out(xp, o_p, xs, o_s, state_conv[0], attn_norm[0], w_gates, conv_w[0], w_out0,
                                          tm=512, tiles_per_seq=tp // 512)
    k_s = proj_s[:, ATTN_WIDTH:2 * ATTN_WIDTH].reshape(1, bs, ts, N_HEADS, HEAD_DIM)
    v_s = proj_s[:, 2 * ATTN_WIDTH:3 * ATTN_WIDTH].reshape(1, bs, ts, N_HEADS, HEAD_DIM)

    assert ts == POS_TILE
    tables = _s5_tables(lam_re[0], lam_im[0], log_step[0])
    u_p, u_s = _norm_proj(h1_p, ssm_norm[0], w_u, chunk=PROMPT_CHUNK, ct=64, tail=h1_s)
    h0 = jnp.concatenate([state_ssm_re[0], state_ssm_im[0]], axis=-1)
    y_p, hf_p, y_s, hf_s = _s5_layer(tables, b_re[0], b_im[0], c_re[0], c_im[0], d_skip[0], u_p, bp, u_s[None], h0)
    out_p, out_s = _glu_out(y_p, h1_p, y_s[0], h1_s, ssm_norm[0], w_z, w_glu1, b_glu[0], w_out1, final_norm, ct=32)
    hf_p = hf_p[:, :bp].transpose(1, 0, 2)[None]
    hf_s = hf_s[:, :bs].transpose(1, 0, 2)[None]
    return (out_p.reshape(bp, tp, D_MODEL), out_s.reshape(bs, ts, D_MODEL),
            k_p, v_p, conv_p[None], hf_p[..., :SSM_STATE], hf_p[..., SSM_STATE:],
            k_s, v_s, conv_s[None], hf_s[..., :SSM_STATE], hf_s[..., SSM_STATE:])
```

```python
import functools
import math

import jax
import jax.numpy as jnp
from jax import lax
from jax.experimental import pallas as pl
from jax.experimental.pallas import tpu as pltpu

D_MODEL = 2048
HEAD_DIM = 128
N_HEADS = 8
ATTN_WIDTH = 1024
CONV_WIDTH = 1024
DILATIONS = (1, 4, 16)
N_BACK = 128
ROPE_THETA = 10000.0
RMS_EPS = 1e-6
SSM_GROUP = 16
SSM_GROUPS = 128
SSM_STATE = 64
LANES = 128
VMEM_LIMIT = 56 * 1024 * 1024

F32 = jnp.float32
BF16 = jnp.bfloat16


def _cparams(*sem):
    return pltpu.CompilerParams(dimension_semantics=sem, vmem_limit_bytes=VMEM_LIMIT)


def _rope_table_kernel(inv_ref, *refs, period, offset, n_cast):
    cos_ref, sin_ref = refs[n_cast:n_cast + 2]
    cr_ref, sr_ref = refs[-2:]
    _cast_blocks(refs[:n_cast], refs[n_cast + 2:-2])
    rows = cos_ref.shape[0]
    i = pl.program_id(0)

    @pl.when(i == 0)
    def _():
        ang = lax.broadcasted_iota(jnp.int32, (rows, LANES), 0).astype(F32) * inv_ref[...]
        cr_ref[...] = jnp.cos(ang)
        sr_ref[...] = jnp.sin(ang)

    base = (offset + lax.rem(i * rows, period)).astype(F32) * inv_ref[...]
    cb, sb = jnp.cos(base), jnp.sin(base)
    lane = lax.broadcasted_iota(jnp.int32, (rows, LANES), 1)
    cos_ref[...] = cb * cr_ref[...] - sb * sr_ref[...]
    sin_ref[...] = jnp.where(lane < HEAD_DIM // 2, -1.0, 1.0) * (sb * cr_ref[...] + cb * sr_ref[...])


def _rope_tables(rows, period, offset, casts=()):
    half = HEAD_DIM // 2
    inv = ROPE_THETA ** (-jnp.arange(half, dtype=F32) / half)
    inv2 = jnp.concatenate([inv, inv])[None, :]
    tr = math.gcd(math.gcd(rows, period), 256)
    table = pl.BlockSpec((tr, LANES), lambda i: (i, 0))
    cast_in, cast_out, cast_shapes = _cast_specs(casts, rows // tr, lambda i: i)
    return pl.pallas_call(
        functools.partial(_rope_table_kernel, period=period, offset=offset, n_cast=len(casts)),
        grid=(rows // tr,),
        in_specs=[pl.BlockSpec((1, LANES), lambda i: (0, 0))] + cast_in,
        out_specs=[table, table] + cast_out,
        out_shape=[jax.ShapeDtypeStruct((rows, LANES), F32)] * 2 + cast_shapes,
        scratch_shapes=[pltpu.VMEM((tr, LANES), F32)] * 2,
        compiler_params=_cparams("arbitrary"),
        name="rope_table",
    )(inv2, *[c[0] for c in casts])


def _rms_norm(x, g):
    return x * lax.rsqrt(jnp.mean(x * x, axis=-1, keepdims=True) + RMS_EPS) * g


POS_TILE = 8
COL_SLAB = 256


def _cast_blocks(cast_in, cast_out):
    for src, dst in zip(cast_in, cast_out):
        dst[...] = src[...].astype(BF16)


def _norm_proj_kernel(*refs, rope_cols, n_cast, has_tail):
    it = iter(refs)
    take = lambda n: [next(it) for _ in range(n)]
    x_ref, g_ref, w_ref = take(3)
    rope = take(2) if rope_cols else None
    cast_in = take(n_cast)
    tail_x = take(1)[0] if has_tail else None
    tail_rope = take(2) if has_tail and rope_cols else None
    o_ref = take(1)[0]
    cast_out = take(n_cast)
    tail_o = take(1)[0] if has_tail else None
    hn0, hn1 = take(2)
    k = pl.program_id(0)

    def normalise(dst):
        x = x_ref[...]
        dst[...] = _rms_norm(x.reshape(dst.shape), g_ref[...]).astype(BF16)
        _cast_blocks(cast_in, cast_out)

    def project(hn, out, tables):
        for c in range(w_ref.shape[1] // COL_SLAB):
            cols = slice(c * COL_SLAB, (c + 1) * COL_SLAB)
            acc = jnp.dot(hn, w_ref[:, cols], preferred_element_type=F32)
            if c * COL_SLAB < rope_cols:
                cos_ref, sin_ref = tables
                for h in range(COL_SLAB // HEAD_DIM):
                    xh = acc[:, h * HEAD_DIM:(h + 1) * HEAD_DIM]
                    lanes = slice(c * COL_SLAB + h * HEAD_DIM, c * COL_SLAB + (h + 1) * HEAD_DIM)
                    out[:, lanes] = xh * cos_ref[...] + pltpu.roll(xh, HEAD_DIM // 2, 1) * sin_ref[...]
            else:
                out[:, cols] = acc

    def step(cur, prev):
        normalise(cur)
        project(prev[...], o_ref, rope)

    pl.when(k == 0)(lambda: normalise(hn0))
    pl.when((k > 0) & (k % 2 == 0))(lambda: step(hn0, hn1))
    pl.when(k % 2 == 1)(lambda: step(hn1, hn0))
    if has_tail:
        @pl.when(k == pl.num_programs(0) - 1)
        def _():
            project(_rms_norm(tail_x[...], g_ref[...]).astype(BF16), tail_o, tail_rope)


def _cast_specs(casts, n_blocks, index):
    ins, outs, shapes = [], [], []
    for arr, width, col in casts:
        rows = arr.shape[0] // n_blocks
        ins.append(pl.BlockSpec((rows, width), lambda *k, col=col: (index(*k), col)))
        outs.append(pl.BlockSpec((rows, width), lambda *k: (index(*k), 0)))
        shapes.append(jax.ShapeDtypeStruct((arr.shape[0], width), BF16))
    return ins, outs, shapes


def _norm_proj(x, g, w, *, tm=None, chunk=None, ct=None, rope=None, rope_cols=0, tiles_per_seq=1, casts=(),
               tail=None, tail_rope=None):
    m, kd = x.shape
    n = w.shape[1]
    whole = lambda a: pl.BlockSpec(a.shape, lambda k: (0,) * len(a.shape))
    tail_in = [] if tail is None else [tail] + (list(tail_rope) if rope_cols else [])
    tail_shape = [] if tail is None else [jax.ShapeDtypeStruct((tail.shape[0], n), F32)]
    cur = lambda k: jnp.minimum(k, t - 1)
    prv = lambda k: jnp.maximum(k - 1, 0)
    if chunk is None:
        t, rows = m // tm, tm
        x_spec = pl.BlockSpec((tm, kd), lambda k: (cur(k), 0))
        out_spec = pl.BlockSpec((tm, n), lambda k: (prv(k), 0))
        out_shape = jax.ShapeDtypeStruct((m, n), F32)
    else:
        nc, sg = m // chunk, chunk // POS_TILE
        t, rows = (nc // ct) * sg, ct * POS_TILE
        x = x.reshape(nc, chunk, kd)
        x_spec = pl.BlockSpec((ct, POS_TILE, kd), lambda k: (cur(k) // sg, cur(k) % sg, 0))
        out_spec = pl.BlockSpec((None, rows, n), lambda k: (prv(k) % sg, prv(k) // sg, 0))
        out_shape = jax.ShapeDtypeStruct((sg, nc * POS_TILE, n), F32)
    table = pl.BlockSpec((rows, LANES), lambda k: (prv(k) % tiles_per_seq, 0))
    cast_in, cast_out, cast_shapes = _cast_specs(casts, t, cur)
    res = pl.pallas_call(
        functools.partial(_norm_proj_kernel, rope_cols=rope_cols, n_cast=len(casts), has_tail=tail is not None),
        grid=(t + 1,),
        in_specs=[x_spec, pl.BlockSpec((1, kd), lambda k: (0, 0)),
                  pl.BlockSpec((kd, n), lambda k: (0, 0), pipeline_mode=pl.Buffered(1))]
        + ([table, table] if rope_cols else []) + cast_in + [whole(a) for a in tail_in],
        out_specs=[out_spec] + cast_out + [whole(s) for s in tail_shape],
        out_shape=[out_shape] + cast_shapes + tail_shape,
        scratch_shapes=[pltpu.VMEM((rows, kd), BF16)] * 2,
        compiler_params=_cparams("arbitrary"),
        name="norm_proj",
    )(x, g.reshape(1, kd), w, *(rope if rope_cols else ()), *[c[0] for c in casts], *tail_in)
    return res if len(res) > 1 else res[0]


def _attn_prompt_kernel(q_ref, k_ref, v_ref, *refs, n_cast):
    cast_in, (o_ref, ks_ref, vs_ref) = refs[:n_cast], refs[n_cast:n_cast + 3]
    cast_out = refs[n_cast + 3:2 * n_cast + 3]
    qn_ref, kn_ref, vn_ref, q16_ref, k16_ref, v16_ref, st_ref, acc_ref, mm_ref, ll_ref = refs[2 * n_cast + 3:]
    _cast_blocks(cast_in, cast_out)
    t = q_ref.shape[1]
    n_keep = ks_ref.shape[1]
    ks_ref[0] = k_ref[0, t - n_keep:, :]
    vs_ref[0] = v_ref[0, t - n_keep:, :]
    l4, l16 = t // 4, t // 16
    nblk = t // N_BACK
    piece = N_BACK // 4
    scale = HEAD_DIM ** -0.5
    nt = (((1,), (1,)), ((), ()))
    pad = jnp.zeros((N_BACK, HEAD_DIM), BF16)
    for ref in (kn_ref, vn_ref, k16_ref, v16_ref):
        ref[0:N_BACK, :] = pad

    qn_ref[...] = q_ref[0].astype(BF16)
    kn_ref[N_BACK:, :] = k_ref[0].astype(BF16)
    vn_ref[N_BACK:, :] = v_ref[0].astype(BF16)
    for src, dst, off in ((q_ref, q16_ref, 0), (k_ref, k16_ref, N_BACK), (v_ref, v16_ref, N_BACK)):
        for r4 in range(4):
            st_ref[r4 * l4:(r4 + 1) * l4, :] = src[0, pl.ds(r4, l4, stride=4), :]
        for r4 in range(4):
            for a in range(4):
                r16 = r4 + 4 * a
                dst[off + r16 * l16:off + (r16 + 1) * l16, :] = (
                    st_ref[pl.ds(r4 * l4 + a, l16, stride=4), :].astype(BF16))

    iota = lambda dim: lax.broadcasted_iota(jnp.int32, (N_BACK, 2 * N_BACK), dim)
    qi, kj = iota(0), iota(1)
    dist = N_BACK + qi - kj
    band = (dist >= 0) & (dist <= N_BACK)
    has_prev = kj >= N_BACK
    dist4 = 4 * (qi % piece - kj % (2 * piece) + piece) + (qi // piece - kj // (2 * piece))
    band4 = (dist4 >= 0) & (dist4 <= N_BACK)
    has_prev4 = kj % (2 * piece) >= piece

    def scores(q, k2, valid):
        s = lax.dot_general(q, k2, nt, preferred_element_type=F32) * scale
        return jnp.where(valid, s, -jnp.inf)

    def fresh(s, v2):
        m = jnp.max(s, axis=-1, keepdims=True)
        p = jnp.exp(s - m)
        wide = lambda c: jnp.broadcast_to(c, (N_BACK, HEAD_DIM))
        return wide(m), wide(jnp.sum(p, axis=-1, keepdims=True)), jnp.dot(p.astype(BF16), v2,
                                                                         preferred_element_type=F32)

    def merged(s, v2, m_old, l_old, a_old):
        m_new = jnp.maximum(m_old, jnp.max(s, axis=-1, keepdims=True))
        alpha = jnp.exp(m_old - m_new)
        p = jnp.exp(s - jnp.concatenate([m_new, m_new], axis=1))
        l_new = alpha * l_old + jnp.sum(p, axis=-1, keepdims=True)
        return m_new, l_new, alpha * a_old + jnp.dot(p.astype(BF16), v2, preferred_element_type=F32)

    def block16(b, carry):
        row0 = pl.multiple_of(b * N_BACK, N_BACK)
        rows = pl.ds(row0, N_BACK)
        later = lax.rem(b, l16 // N_BACK) > 0
        s = scores(q16_ref[rows, :], k16_ref[pl.ds(row0, 2 * N_BACK), :], band & (has_prev | later))
        mm_ref[rows, :], ll_ref[rows, :], acc_ref[rows, :] = fresh(s, v16_ref[pl.ds(row0, 2 * N_BACK), :])
        return carry

    lax.fori_loop(0, nblk, block16, 0, unroll=16)

    n_j = l16 // piece

    def block4(b, carry):
        rho, j = b // n_j, lax.rem(b, n_j)
        base = pl.multiple_of(rho * l16 + j * piece, piece)
        q_rows = [pl.ds(base + a * 4 * l16, piece) for a in range(4)]
        k_rows = [pl.ds(base + a * 4 * l16 + N_BACK - piece, 2 * piece) for a in range(4)]
        gather = lambda ref, rows: jnp.concatenate([ref[r, :] for r in rows], axis=0)
        s = scores(gather(q16_ref, q_rows), gather(k16_ref, k_rows), band4 & (has_prev4 | (j > 0)))
        m, l, acc = merged(s, gather(v16_ref, k_rows), gather(mm_ref, q_rows), gather(ll_ref, q_rows),
                           gather(acc_ref, q_rows))
        for a, r in enumerate(q_rows):
            part = slice(a * piece, (a + 1) * piece)
            mm_ref[r, :], ll_ref[r, :], acc_ref[r, :] = m[part], l[part], acc[part]
        return carry

    lax.fori_loop(0, nblk, block4, 0, unroll=16)

    for src, dst in ((acc_ref, o_ref.at[0]), (mm_ref, acc_ref), (ll_ref, mm_ref)):
        for r4 in range(4):
            for a in range(4):
                r16 = r4 + 4 * a
                st_ref[pl.ds(r4 * l4 + a, l16, stride=4), :] = src[r16 * l16:(r16 + 1) * l16, :]
        for r4 in range(4):
            dst[pl.ds(r4, l4, stride=4), :] = st_ref[r4 * l4:(r4 + 1) * l4, :]

    def block1(b, carry):
        row0 = pl.multiple_of(b * N_BACK, N_BACK)
        rows = pl.ds(row0, N_BACK)
        s = scores(qn_ref[rows, :], kn_ref[pl.ds(row0, 2 * N_BACK), :], band & (has_prev | (b > 0)))
        _, l, acc = merged(s, vn_ref[pl.ds(row0, 2 * N_BACK), :], acc_ref[rows, :], mm_ref[rows, :],
                           o_ref[0, rows, :])
        o_ref[0, rows, :] = acc / l
        return carry

    lax.fori_loop(0, nblk, block1, 0, unroll=16)


def _attn_prompt(proj, b, t, n_keep, casts=()):
    assert t % (16 * N_BACK) == 0
    p3 = proj.reshape(b, t, proj.shape[1])
    blk = lambda off: pl.BlockSpec((1, t, HEAD_DIM), lambda i, h: (i, 0, off + h))
    keep = pl.BlockSpec((1, n_keep, HEAD_DIM), lambda i, h: (i, 0, h))
    cast_in, cast_out, cast_shapes = _cast_specs(casts, b * N_HEADS, lambda i, h: i * N_HEADS + h)
    return pl.pallas_call(
        functools.partial(_attn_prompt_kernel, n_cast=len(casts)),
        grid=(b, N_HEADS),
        in_specs=[blk(0), blk(N_HEADS), blk(2 * N_HEADS)] + cast_in,
        out_specs=[pl.BlockSpec((1, t, HEAD_DIM), lambda i, h: (i, 0, h)), keep, keep] + cast_out,
        out_shape=[jax.ShapeDtypeStruct((b, t, ATTN_WIDTH), F32),
                   jax.ShapeDtypeStruct((b, n_keep, ATTN_WIDTH), F32),
                   jax.ShapeDtypeStruct((b, n_keep, ATTN_WIDTH), F32)] + cast_shapes,
        scratch_shapes=[
            pltpu.VMEM((t, HEAD_DIM), BF16),
            pltpu.VMEM((t + N_BACK, HEAD_DIM), BF16),
            pltpu.VMEM((t + N_BACK, HEAD_DIM), BF16),
            pltpu.VMEM((t, HEAD_DIM), BF16),
            pltpu.VMEM((t + N_BACK, HEAD_DIM), BF16),
            pltpu.VMEM((t + N_BACK, HEAD_DIM), BF16),
            pltpu.VMEM((t, HEAD_DIM), F32),
            pltpu.VMEM((t, HEAD_DIM), F32),
            pltpu.VMEM((t, HEAD_DIM), F32),
            pltpu.VMEM((t, HEAD_DIM), F32),
        ],
        compiler_params=_cparams("parallel", "parallel"),
        name="attn_prompt",
    )(p3, p3, p3, *[c[0] for c in casts])


COARSE = DILATIONS[-1]
NEAR = N_BACK * DILATIONS[-2]


def _attn_sample_kernel(q_ref, kn_ref, vn_ref, kfar_ref, knear_ref, vfar_ref, vnear_ref, o_ref):
    s_len = q_ref.shape[0]
    far_groups, near_groups = kfar_ref.shape[0], knear_ref.shape[0]
    n_far, n_near = far_groups * s_len, near_groups * COARSE
    n_buf = (far_groups + near_groups) * COARSE
    scale = HEAD_DIM ** -0.5
    nt = (((1,), (1,)), ((), ()))

    def count(dist):
        c = jnp.zeros(dist.shape, F32)
        for d in DILATIONS:
            hit = (dist >= 0) & (dist <= N_BACK * d) & ((dist & (d - 1)) == 0)
            c = c + jnp.where(hit, 1.0, 0.0)
        return c

    iota = lambda shape, dim: lax.broadcasted_iota(jnp.int32, shape, dim)
    query = lambda width: n_buf + iota((s_len, width), 0)
    col = iota((s_len, n_far + n_near), 1)
    far_pos = COARSE * (col // s_len) + col % s_len
    near_pos = far_groups * COARSE + (col - n_far)
    cc = count(query(n_far + n_near) - jnp.where(col < n_far, far_pos, near_pos))
    cn = count(iota((s_len, s_len), 0) - iota((s_len, s_len), 1))

    def head_rows(far_ref, near_ref, h):
        far = far_ref[:, pl.ds(h, s_len, stride=N_HEADS), :].reshape(n_far, HEAD_DIM)
        near = near_ref[:, pl.ds(h, COARSE, stride=N_HEADS), :].reshape(n_near, HEAD_DIM)
        return jnp.concatenate([far, near], axis=0).astype(BF16)

    for h in range(N_HEADS):
        cols = slice(h * HEAD_DIM, (h + 1) * HEAD_DIM)
        q = q_ref[:, cols].astype(BF16)
        sc = lax.dot_general(q, head_rows(kfar_ref, knear_ref, h), nt, preferred_element_type=F32) * scale
        sn = lax.dot_general(q, kn_ref[:, cols].astype(BF16), nt, preferred_element_type=F32) * scale
        sc = jnp.where(cc > 0, sc, -jnp.inf)
        sn = jnp.where(cn > 0, sn, -jnp.inf)
        m = jnp.maximum(jnp.max(sc, axis=-1, keepdims=True), jnp.max(sn, axis=-1, keepdims=True))
        pc = cc * jnp.exp(sc - m)
        pn = cn * jnp.exp(sn - m)
        l = jnp.sum(pc, axis=-1, keepdims=True) + jnp.sum(pn, axis=-1, keepdims=True)
        o = (jnp.dot(pc.astype(BF16), head_rows(vfar_ref, vnear_ref, h), preferred_element_type=F32)
             + jnp.dot(pn.astype(BF16), vn_ref[:, cols].astype(BF16), preferred_element_type=F32))
        o_ref[:, cols] = o / l


def _attn_sample(proj, cache_k, cache_v, b, s_len):
    n_buf = cache_k.shape[1]
    near_groups = NEAR // COARSE
    far_groups = n_buf // COARSE - near_groups
    assert n_buf % COARSE == 0 and s_len <= COARSE and far_groups % near_groups == 0 and far_groups > 0
    grouped = lambda c: c.reshape(b, n_buf // COARSE, COARSE * N_HEADS, HEAD_DIM)
    new = lambda c: pl.BlockSpec((s_len, ATTN_WIDTH), lambda i: (i, c))
    far = pl.BlockSpec((None, far_groups, s_len * N_HEADS, HEAD_DIM), lambda i: (i, 0, 0, 0))
    near = pl.BlockSpec((None, near_groups, COARSE * N_HEADS, HEAD_DIM),
                        lambda i: (i, far_groups // near_groups, 0, 0))
    ck, cv = grouped(cache_k), grouped(cache_v)
    return pl.pallas_call(
        _attn_sample_kernel,
        grid=(b,),
        in_specs=[new(0), new(1), new(2), far, near, far, near],
        out_specs=new(0),
        out_shape=jax.ShapeDtypeStruct((b * s_len, ATTN_WIDTH), F32),
        compiler_params=_cparams("parallel"),
        name="attn_sample",
    )(proj, proj, proj, ck, ck, cv, cv)


def _silu(z):
    return z * jax.nn.sigmoid(z)


def _mix_out_kernel(x_ref, o_ref, g_ref, wza_ref, wgb_ref, wgc_ref, whi_ref, wzb_ref, cw_ref, w_ref, *refs,
                    tiles_per_seq):
    xs_ref, os_ref, p2_ref, p1_ref, h_ref, cs_ref, hs_ref, css_ref, tail_ref = refs

    def tile(x, o_attn, ln, prev2, prev1):
        tm = x.shape[0]
        hn = _rms_norm(x, g_ref[...]).astype(BF16)
        gate = lambda w: jnp.dot(hn, w[...], preferred_element_type=F32)
        ch = gate(wgc_ref) * gate(whi_ref)
        pos = lax.rem(lax.broadcasted_iota(jnp.int32, (tm, 1), 0), ln)
        ch1 = jnp.where(pos == 0, prev1, pltpu.roll(ch, 1, 0))
        ch2 = jnp.where(pos == 0, prev2, jnp.where(pos == 1, prev1, pltpu.roll(ch, 2, 0)))
        conv = ch2 * cw_ref[0:1, :] + ch1 * cw_ref[1:2, :] + ch * cw_ref[2:3, :]
        o_b = (gate(wgb_ref) * conv * _silu(gate(wzb_ref))).astype(BF16)
        y = jnp.dot(o_b, w_ref[ATTN_WIDTH:, :], preferred_element_type=F32)
        o_a = (o_attn * _silu(gate(wza_ref))).astype(BF16)
        y = y + jnp.dot(o_a, w_ref[0:ATTN_WIDTH, :], preferred_element_type=F32)
        return x + y, ch

    i = pl.program_id(0)
    tm = x_ref.shape[0]

    @pl.when(i % tiles_per_seq == 0)
    def _():
        tail_ref[...] = jnp.zeros(tail_ref.shape, F32)

    h, ch = tile(x_ref[...], o_ref[...], tm, tail_ref[0:1, :], tail_ref[1:2, :])
    h_ref[...] = h
    tail_ref[...] = ch[tm - 2:tm, :]
    cs_ref[0] = ch[tm - 2:tm, :]

    @pl.when(i == pl.num_programs(0) - 1)
    def _():
        rows = xs_ref.shape[0]
        k = p1_ref.shape[0]
        ln = rows // k
        per_row = lambda p: jnp.broadcast_to(p[...], (k, ln, CONV_WIDTH)).reshape(rows, CONV_WIDTH)
        hs, chs = tile(xs_ref[...], os_ref[...], ln, per_row(p2_ref), per_row(p1_ref))
        hs_ref[...] = hs
        css_ref[...] = chs.reshape(k, ln, CONV_WIDTH)[:, ln - 2:, :]


def _mix_out(x, o_attn, xs, os_attn, conv_init_s, g, w_gates, conv_w, w_out, *, tm, tiles_per_seq):
    m, ms = x.shape[0], xs.shape[0]
    nseq = m // (tm * tiles_per_seq)
    once = lambda a: pl.BlockSpec(a.shape, lambda i: (0,) * len(a.shape), pipeline_mode=pl.Buffered(1))
    row = lambda w: pl.BlockSpec((tm, w), lambda i: (i, 0))
    p2, p1 = conv_init_s[:, 0:1], conv_init_s[:, 1:2]
    out_shape = (jax.ShapeDtypeStruct((m, D_MODEL), F32), jax.ShapeDtypeStruct((nseq, 2, CONV_WIDTH), F32),
                 jax.ShapeDtypeStruct((ms, D_MODEL), F32), jax.ShapeDtypeStruct(conv_init_s.shape, F32))
    return pl.pallas_call(
        functools.partial(_mix_out_kernel, tiles_per_seq=tiles_per_seq),
        grid=(m // tm,),
        in_specs=[row(D_MODEL), row(ATTN_WIDTH), pl.BlockSpec((1, D_MODEL), lambda i: (0, 0))]
        + [once(w) for w in w_gates] + [pl.BlockSpec((3, CONV_WIDTH), lambda i: (0, 0)), once(w_out)]
        + [once(a) for a in (xs, os_attn, p2, p1)],
        out_specs=(row(D_MODEL), pl.BlockSpec((1, 2, CONV_WIDTH), lambda i: (i // tiles_per_seq, 0, 0)))
        + tuple(pl.BlockSpec(s.shape, lambda i, n=len(s.shape): (0,) * n) for s in out_shape[2:]),
        out_shape=out_shape,
        scratch_shapes=[pltpu.VMEM((2, CONV_WIDTH), F32)],
        compiler_params=_cparams("arbitrary"),
        name="mix_out",
    )(x, o_attn, g.reshape(1, -1), *w_gates, conv_w, w_out, xs, os_attn, p2, p1)


N_SCAN = 7
GROUPS_PER_TILE = LANES // SSM_GROUP


PROMPT_CHUNK = 32
ROW_C = PROMPT_CHUNK + 1
ROW_DBL = PROMPT_CHUNK + 2
TABLE_ROWS = ROW_DBL + N_SCAN


def _s5_disc_kernel(lr_ref, li_ref, ls_ref, tr_ref, ti_ref):
    lr = lr_ref[...]
    li = li_ref[...]
    step = jnp.exp(ls_ref[...])
    mag = jnp.exp(lr * step)
    ar = mag * jnp.cos(li * step)
    ai = mag * jnp.sin(li * step)
    den = lr * lr + li * li
    nr = ar - 1.0
    tr_ref[ROW_C] = (nr * lr + ai * li) / den
    ti_ref[ROW_C] = (ai * lr - nr * li) / den
    pr = jnp.ones_like(ar)
    pi = jnp.zeros_like(ar)
    for tau in range(PROMPT_CHUNK + 1):
        tr_ref[tau] = pr
        ti_ref[tau] = pi
        dr, di = pr, pi
        pr, pi = pr * ar - pi * ai, pr * ai + pi * ar
    for i in range(N_SCAN):
        tr_ref[ROW_DBL + i] = dr
        ti_ref[ROW_DBL + i] = di
        dr, di = dr * dr - di * di, 2.0 * dr * di


def _s5_tables(lam_re, lam_im, log_step):
    g = lam_re.shape[0]
    dup = lambda a: jnp.concatenate([a, a], axis=-1)
    tr, ti = pl.pallas_call(
        _s5_disc_kernel,
        out_shape=(jax.ShapeDtypeStruct((TABLE_ROWS, g, LANES), F32),) * 2,
        name="s5_disc",
    )(dup(lam_re), dup(lam_im), log_step[:, None])
    return tr, ti


def _split_bf16(x):
    hi = x.astype(BF16)
    return hi, (x - hi.astype(F32)).astype(BF16)


def _dot_split(a, b, b_is_bf16_exact=False):
    dot = lambda x, y: jnp.dot(x, y, preferred_element_type=F32)
    ah, al = _split_bf16(a)
    if b_is_bf16_exact:
        bh = b.astype(BF16)
        return dot(ah, bh) + dot(al, bh)
    bh, bl = _split_bf16(b)
    return dot(ah, bh) + (dot(ah, bl) + dot(al, bh))


def _pad_rows(x, rows):
    if x.shape[0] == rows:
        return x
    return jnp.concatenate([x, jnp.zeros((rows - x.shape[0], x.shape[1]), x.dtype)], axis=0)


def _s5_build_operators(tr_ref, ti_ref, br_ref, bi_ref, cr_ref, ci_ref, c2r_ref, c2i_ref,
                        wt_ref, qt_ref, kr_ref, ki_ref):
    chunk = PROMPT_CHUNK
    n = SSM_GROUP * chunk
    iota = lambda shape, dim: lax.broadcasted_iota(jnp.int32, shape, dim)
    lane = iota((SSM_GROUP, 2 * SSM_STATE), 1)
    e_sel = jnp.where(iota((chunk, n), 0) == chunk - 1 - iota((chunk, n), 1) // SSM_GROUP, 1.0, 0.0)
    tile = jnp.where(iota((SSM_GROUP, n), 0) == iota((SSM_GROUP, n), 1) % SSM_GROUP, 1.0, 0.0)

    for g in range(GROUPS_PER_TILE):
        tab_r, tab_i = tr_ref[:, g, :], ti_ref[:, g, :]
        col_r = _pad_rows(tab_r, LANES).T[:SSM_STATE]
        col_i = _pad_rows(tab_i, LANES).T[:SSM_STATE]
        kr_ref[g] = col_r
        ki_ref[g] = col_i
        ccr, cci = col_r[:, ROW_C:ROW_C + 1], col_i[:, ROW_C:ROW_C + 1]
        br, bi = br_ref[g], bi_ref[g]
        bbr = ccr * br - cci * bi
        bbi = ccr * bi + cci * br
        aer = _dot_split(col_r[:, :chunk], e_sel, True)
        aei = _dot_split(col_i[:, :chunk], e_sel, True)
        btr = _dot_split(bbr, tile, True)
        bti = _dot_split(bbi, tile, True)
        ptr = aer * btr - aei * bti
        pti = aer * bti + aei * btr
        wt_ref[g, n:n + SSM_STATE, :] = ptr.astype(BF16)
        wt_ref[g, n + SSM_STATE:n + 2 * SSM_STATE, :] = pti.astype(BF16)
        taps = _dot_split(cr_ref[g], ptr) - _dot_split(ci_ref[g], pti)
        padded = jnp.concatenate([taps, jnp.zeros_like(taps)], axis=1)
        for t in range(chunk):
            sh = SSM_GROUP * (chunk - 1 - t)
            blk = padded if sh == 0 else pltpu.roll(padded, 2 * n - sh, 1)
            wt_ref[g, SSM_GROUP * t:SSM_GROUP * (t + 1), :] = blk[:, :n].astype(BF16)
        c2r, c2i = c2r_ref[g], c2i_ref[g]
        for t in range(chunk):
            ar = tab_r[t + 1:t + 2, :]
            ai = tab_i[t + 1:t + 2, :]
            x1 = jnp.where(lane < SSM_STATE, ar, -ai)
            x2 = jnp.where(lane < SSM_STATE, ai, ar)
            qt_ref[g, SSM_GROUP * t:SSM_GROUP * (t + 1), :] = (c2r * x1 - c2i * x2).astype(BF16)


def _s5_kernel(*refs):
    build_in, (u_ref, us_ref, h0_ref, dsk_ref) = refs[:8], refs[8:12]
    y_ref, hf_ref, ys_ref, hfs_ref = refs[12:16]
    ops0, ops1 = refs[16:20], refs[20:24]
    long_bufs, short_bufs = refs[24:26], refs[26:28]
    k = pl.program_id(0)

    def step(cur, prev):
        _s5_build_operators(*build_in, *cur)
        _s5_scan_body(u_ref, *prev, dsk_ref, y_ref, hf_ref, *long_bufs, chunk=PROMPT_CHUNK, carry=True)
        _s5_scan_body(us_ref, *prev, dsk_ref, h0_ref, ys_ref, hfs_ref, *short_bufs,
                      chunk=us_ref.shape[0] * POS_TILE, carry=False)

    pl.when(k == 0)(lambda: _s5_build_operators(*build_in, *ops0))
    pl.when((k > 0) & (k % 2 == 0))(lambda: step(ops0, ops1))
    pl.when(k % 2 == 1)(lambda: step(ops1, ops0))


def _s5_scan_body(u_ref, wt_ref, qt_ref, kr_ref, ki_ref, dsk_ref, *rest, chunk, carry):
    if carry:
        y_ref, hf_ref, dall_ref, yall_ref = rest
    else:
        h0_ref, y_ref, hf_ref, dall_ref, yall_ref = rest
    nc = u_ref.shape[1] // POS_TILE
    at = lambda s: (s // POS_TILE, pl.ds(s % POS_TILE, nc, stride=POS_TILE), slice(None))
    ncp = dall_ref.shape[2]
    nseq = ncp // LANES if carry else nc
    n = SSM_GROUP * chunk
    n_full = SSM_GROUP * PROMPT_CHUNK
    iota = lambda shape, dim: lax.broadcasted_iota(jnp.int32, shape, dim)
    for s in range(chunk):
        dall_ref[s] = _pad_rows(u_ref[at(s)], ncp).T.astype(BF16)
    lane = iota((SSM_STATE, ncp), 1) & (LANES - 1)
    slot = iota((LANES, LANES), 1)
    if not carry:
        h0_all = _pad_rows(jnp.concatenate([h0_ref[:, gp, :] for gp in range(GROUPS_PER_TILE)], axis=0), LANES).T
    acc = jnp.zeros((LANES, LANES), F32)
    for gp in range(GROUPS_PER_TILE):
        rows = slice(SSM_GROUP * gp, SSM_GROUP * (gp + 1))
        d = dall_ref[:, rows, :].reshape(n, ncp)
        if n == n_full:
            ys = jnp.dot(wt_ref[gp], d, preferred_element_type=F32)
            st = ys[n:]
            ys = ys[:n]
        else:
            ys = jnp.dot(wt_ref[gp, 0:n, 0:n], d, preferred_element_type=F32)
            st = jnp.dot(wt_ref[gp, n_full:, n_full - n:], d, preferred_element_type=F32)
        sr = st[:SSM_STATE]
        si = st[SSM_STATE:]
        if carry:
            for i in range(N_SCAN):
                sh = 1 << i
                ar = kr_ref[gp, :, ROW_DBL + i:ROW_DBL + i + 1]
                ai = ki_ref[gp, :, ROW_DBL + i:ROW_DBL + i + 1]
                pr = jnp.where(lane >= sh, pltpu.roll(sr, sh, 1), 0.0)
                pi = jnp.where(lane >= sh, pltpu.roll(si, sh, 1), 0.0)
                sr, si = sr + ar * pr - ai * pi, si + ar * pi + ai * pr
            hp = jnp.concatenate([jnp.where(lane >= 1, pltpu.roll(sr, 1, 1), 0.0),
                                  jnp.where(lane >= 1, pltpu.roll(si, 1, 1), 0.0)], axis=0)
            for b in range(nseq):
                seq = slice(LANES * b, LANES * (b + 1))
                end = jnp.concatenate([sr[:, seq], si[:, seq]], axis=0)
                acc = jnp.where(slot == 8 * gp + b, pltpu.roll(end, (8 * gp + b + 1) % LANES, 1), acc)
        else:
            hp = h0_all if gp == 0 else pltpu.roll(h0_all, LANES - 8 * gp, 1)
            hp = jnp.where(slot < nseq, hp, 0.0)
            hpr, hpi = hp[:SSM_STATE], hp[SSM_STATE:]
            ar = kr_ref[gp, :, chunk:chunk + 1]
            ai = ki_ref[gp, :, chunk:chunk + 1]
            end = jnp.concatenate([sr + ar * hpr - ai * hpi, si + ar * hpi + ai * hpr], axis=0)
            if gp:
                end = pltpu.roll(end, 8 * gp, 1)
            acc = jnp.where((slot >= 8 * gp) & (slot < 8 * gp + nseq), end, acc)
        y = ys + jnp.dot(qt_ref[gp, 0:n, :], hp.astype(BF16), preferred_element_type=F32)
        yall_ref[:, rows, :] = y.reshape(chunk, SSM_GROUP, ncp)
    hf_ref[...] = acc.T[:8 * GROUPS_PER_TILE].reshape(GROUPS_PER_TILE, 8, LANES)
    for t in range(chunk):
        y_ref[at(t)] = yall_ref[t].T[:nc] + dsk_ref[...] * u_ref[at(t)]


def _s5_layer(tables, b_re, b_im, c_re, c_im, d_skip, u, n_long, us, h0):
    ns = h0.shape[0]
    chunk_s = us.shape[0] * POS_TILE
    n, n_s = SSM_GROUP * PROMPT_CHUNK, SSM_GROUP * chunk_s
    assert u.shape[0] * POS_TILE == PROMPT_CHUNK and u.shape[1] == n_long * LANES * POS_TILE and n_long <= 8
    assert us.shape[1] == ns * POS_TILE and ns == 8 and n % n_s == 0
    gt = GROUPS_PER_TILE
    t = SSM_GROUPS // gt
    cur = lambda k: jnp.minimum(k, t - 1)
    prv = lambda k: jnp.maximum(k - 1, 0)
    built = lambda *s: pl.BlockSpec((gt,) + s, lambda k: (cur(k),) + (0,) * len(s))
    lanes = lambda a: pl.BlockSpec(a.shape[:-1] + (LANES,), lambda k: (0,) * (a.ndim - 1) + (prv(k),))
    states = pl.BlockSpec((gt, 8, LANES), lambda k: (prv(k), 0, 0))
    dup = lambda a: jnp.concatenate([a, a], axis=-1)
    op_set = [pltpu.VMEM((gt, n + 2 * SSM_STATE, n), BF16), pltpu.VMEM((gt, n, 2 * SSM_STATE), BF16),
              pltpu.VMEM((gt, SSM_STATE, LANES), F32), pltpu.VMEM((gt, SSM_STATE, LANES), F32)]
    regroup = lambda c, width: [pltpu.VMEM((c, LANES, width), BF16), pltpu.VMEM((c, LANES, width), F32)]
    return pl.pallas_call(
        _s5_kernel,
        grid=(t + 1,),
        in_specs=[pl.BlockSpec((TABLE_ROWS, gt, LANES), lambda k: (0, cur(k), 0))] * 2 + [
                  built(SSM_STATE, SSM_GROUP), built(SSM_STATE, SSM_GROUP),
                  built(SSM_GROUP, SSM_STATE), built(SSM_GROUP, SSM_STATE),
                  built(SSM_GROUP, LANES), built(SSM_GROUP, LANES),
                  lanes(u), lanes(us), pl.BlockSpec((ns, gt, LANES), lambda k: (0, prv(k), 0)),
                  pl.BlockSpec((1, LANES), lambda k: (0, prv(k)))],
        out_specs=(lanes(u), states, lanes(us), states),
        out_shape=(jax.ShapeDtypeStruct(u.shape, F32), jax.ShapeDtypeStruct((SSM_GROUPS, 8, LANES), F32),
                   jax.ShapeDtypeStruct(us.shape, F32), jax.ShapeDtypeStruct((SSM_GROUPS, 8, LANES), F32)),
        scratch_shapes=op_set + op_set + regroup(PROMPT_CHUNK, n_long * LANES) + regroup(chunk_s, LANES),
        compiler_params=_cparams("arbitrary"),
        name="s5",
    )(*tables, b_re, b_im, c_re, c_im, dup(c_re), dup(c_im), u, us, h0, d_skip.reshape(1, -1))


def _ssm_out(y, x, gn_ref, wz_ref, wg_ref, bg_ref, wo_ref, gf_ref):
    z = jnp.dot(_rms_norm(x, gn_ref[...]).astype(BF16), wz_ref[...], preferred_element_type=F32)
    g = jax.nn.gelu(y, approximate=True)
    gate = jnp.dot(g.astype(BF16), wg_ref[...], preferred_element_type=F32) + bg_ref[...]
    yy = (g * jax.nn.sigmoid(gate)) * _silu(z)
    return _rms_norm(x + jnp.dot(yy.astype(BF16), wo_ref[...], preferred_element_type=F32), gf_ref[...])


def _glu_out_kernel(y_ref, x_ref, ys_ref, xs_ref, *rest):
    params, (o_ref, os_ref) = rest[:-2], rest[-2:]
    o_ref[...] = _ssm_out(y_ref[...], x_ref[...].reshape(y_ref.shape), *params).reshape(o_ref.shape)

    @pl.when(pl.program_id(0) == pl.num_programs(0) - 1)
    def _():
        os_ref[...] = _ssm_out(ys_ref[...], xs_ref[...], *params)


def _glu_out(y, x, ys, xs, g_norm, w_z, w_glu, b_glu, w_out, g_final, *, ct):
    m, w = x.shape
    vec = pl.BlockSpec((1, w), lambda i: (0, 0))
    mat = pl.BlockSpec((w, w), lambda i: (0, 0), pipeline_mode=pl.Buffered(1))
    small = pl.BlockSpec(xs.shape, lambda i: (0, 0))
    sg, nc = y.shape[0], y.shape[1] // POS_TILE
    x_spec = pl.BlockSpec((ct, POS_TILE, w), lambda i: (i // sg, i % sg, 0))
    out, out_s = pl.pallas_call(
        _glu_out_kernel,
        grid=((nc // ct) * sg,),
        in_specs=[pl.BlockSpec((None, ct * POS_TILE, w), lambda i: (i % sg, i // sg, 0)), x_spec, small, small,
                  vec, mat, mat, vec, mat, vec],
        out_specs=(x_spec, small),
        out_shape=(jax.ShapeDtypeStruct((nc, sg * POS_TILE, w), F32), jax.ShapeDtypeStruct(xs.shape, F32)),
        compiler_params=_cparams("arbitrary"),
        name="glu_out",
    )(y, x.reshape(nc, sg * POS_TILE, w), ys, xs, g_norm.reshape(1, w), w_z, w_glu, b_glu.reshape(1, w), w_out,
      g_final.reshape(1, w))
    return out.reshape(m, w), out_s


PAST_LEN = 16384


def kernel(x_prompt, x_sample, cache_win_k, cache_win_v, state_conv, state_ssm_re, state_ssm_im, attn_norm, w_in_ab, conv_w, w_out_ab, ssm_norm, w_in_c, lam_re, lam_im, log_step, b_re, b_im, c_re, c_im, d_skip, w_glu, b_glu, w_out_c, final_norm):
    bp, tp, _ = x_prompt.shape
    bs, ts, _ = x_sample.shape
    n_keep = min(2048, tp)
    xp = x_prompt.reshape(bp * tp, D_MODEL)
    xs = x_sample.reshape(bs * ts, D_MODEL)

    qkv_cols = 3 * ATTN_WIDTH
    gate_casts = [(w_in_ab[0], CONV_WIDTH, qkv_cols // CONV_WIDTH + c) for c in range(5)]
    layer1_casts = [(w_in_c[0], D_MODEL, 0), (w_in_c[0], D_MODEL, 1), (w_glu[0], D_MODEL, 0), (w_out_c[0], D_MODEL, 0)]

    cos_p, sin_p, w_qkv = _rope_tables(tp, tp, 0, casts=[(w_in_ab[0], qkv_cols, 0)])
    proj_p, *w_gates, w_out0, proj_s = _norm_proj(
        xp, attn_norm[0], w_qkv, tm=512, rope=(cos_p, sin_p), rope_cols=2 * ATTN_WIDTH, tiles_per_seq=tp // 512,
        casts=gate_casts + [(w_out_ab[0], D_MODEL, 0)], tail=xs, tail_rope=_rope_tables(bs * ts, ts, PAST_LEN))
    o_p, k_p, v_p, w_u, w_z, w_glu1, w_out1 = _attn_prompt(proj_p, bp, tp, n_keep, casts=layer1_casts)
    o_p = o_p.reshape(bp * tp, ATTN_WIDTH)
    k_p = k_p.reshape(1, bp, n_keep, N_HEADS, HEAD_DIM)
    v_p = v_p.reshape(1, bp, n_keep, N_HEADS, HEAD_DIM)
    o_s = _attn_sample(proj_s, cache_win_k[0], cache_win_v[0], bs, ts)
    h1_p, conv_p, h1_s, conv_s = _mix_out(xp, o_p, xs, o_s, state_conv[0], attn_norm[0], w_gates, conv_w[0], w_out0,
                                          tm=512, tiles_per_seq=tp // 512)
    k_s = proj_s[:, ATTN_WIDTH:2 * ATTN_WIDTH].reshape(1, bs, ts, N_HEADS, HEAD_DIM)
    v_s = proj_s[:, 2 * ATTN_WIDTH:3 * ATTN_WIDTH].reshape(1, bs, ts, N_HEADS, HEAD_DIM)

    assert ts == POS_TILE
    tables = _s5_tables(lam_re[0], lam_im[0], log_step[0])
    u_p, u_s = _norm_proj(h1_p, ssm_norm[0], w_u, chunk=PROMPT_CHUNK, ct=64, tail=h1_s)
    h0 = jnp.concatenate([state_ssm_re[0], state_ssm_im[0]], axis=-1)
    y_p, hf_p, y_s, hf_s = _s5_layer(tables, b_re[0], b_im[0], c_re[0], c_im[0], d_skip[0], u_p, bp, u_s[None], h0)
    out_p, out_s = _glu_out(y_p, h1_p, y_s[0], h1_s, ssm_norm[0], w_z, w_glu1, b_glu[0], w_out1, final_norm, ct=32)
    hf_p = hf_p[:, :bp].transpose(1, 0, 2)[None]
    hf_s = hf_s[:, :bs].transpose(1, 0, 2)[None]
    return (out_p.reshape(bp, tp, D_MODEL), out_s.reshape(bs, ts, D_MODEL),
            k_p, v_p, conv_p[None], hf_p[..., :SSM_STATE], hf_p[..., SSM_STATE:],
            k_s, v_s, conv_s[None], hf_s[..., :SSM_STATE], hf_s[..., SSM_STATE:])
```

```python
import functools
import math

import jax
import jax.numpy as jnp
from jax import lax
from jax.experimental import pallas as pl
from jax.experimental.pallas import tpu as pltpu

D_MODEL = 2048
HEAD_DIM = 128
N_HEADS = 8
ATTN_WIDTH = 1024
CONV_WIDTH = 1024
DILATIONS = (1, 4, 16)
N_BACK = 128
ROPE_THETA = 10000.0
RMS_EPS = 1e-6
SSM_GROUP = 16
SSM_GROUPS = 128
SSM_STATE = 64
LANES = 128
VMEM_LIMIT = 56 * 1024 * 1024

F32 = jnp.float32
BF16 = jnp.bfloat16


def _cparams(*sem):
    return pltpu.CompilerParams(dimension_semantics=sem, vmem_limit_bytes=VMEM_LIMIT)


def _rope_table_kernel(inv_ref, *refs, period, offset, n_cast):
    cos_ref, sin_ref = refs[n_cast:n_cast + 2]
    cr_ref, sr_ref = refs[-2:]
    _cast_blocks(refs[:n_cast], refs[n_cast + 2:-2])
    rows = cos_ref.shape[0]
    i = pl.program_id(0)

    @pl.when(i == 0)
    def _():
        ang = lax.broadcasted_iota(jnp.int32, (rows, LANES), 0).astype(F32) * inv_ref[...]
        cr_ref[...] = jnp.cos(ang)
        sr_ref[...] = jnp.sin(ang)

    base = (offset + lax.rem(i * rows, period)).astype(F32) * inv_ref[...]
    cb, sb = jnp.cos(base), jnp.sin(base)
    lane = lax.broadcasted_iota(jnp.int32, (rows, LANES), 1)
    cos_ref[...] = cb * cr_ref[...] - sb * sr_ref[...]
    sin_ref[...] = jnp.where(lane < HEAD_DIM // 2, -1.0, 1.0) * (sb * cr_ref[...] + cb * sr_ref[...])


def _rope_tables(rows, period, offset, casts=()):
    half = HEAD_DIM // 2
    inv = ROPE_THETA ** (-jnp.arange(half, dtype=F32) / half)
    inv2 = jnp.concatenate([inv, inv])[None, :]
    tr = math.gcd(math.gcd(rows, period), 256)
    table = pl.BlockSpec((tr, LANES), lambda i: (i, 0))
    cast_in, cast_out, cast_shapes = _cast_specs(casts, rows // tr, lambda i: i)
    return pl.pallas_call(
        functools.partial(_rope_table_kernel, period=period, offset=offset, n_cast=len(casts)),
        grid=(rows // tr,),
        in_specs=[pl.BlockSpec((1, LANES), lambda i: (0, 0))] + cast_in,
        out_specs=[table, table] + cast_out,
        out_shape=[jax.ShapeDtypeStruct((rows, LANES), F32)] * 2 + cast_shapes,
        scratch_shapes=[pltpu.VMEM((tr, LANES), F32)] * 2,
        compiler_params=_cparams("arbitrary"),
        name="rope_table",
    )(inv2, *[c[0] for c in casts])


def _rms_norm(x, g):
    return x * lax.rsqrt(jnp.mean(x * x, axis=-1, keepdims=True) + RMS_EPS) * g


POS_TILE = 8
COL_SLAB = 256


def _cast_blocks(cast_in, cast_out):
    for src, dst in zip(cast_in, cast_out):
        dst[...] = src[...].astype(BF16)


def _norm_proj_kernel(*refs, rope_cols, n_cast, has_tail):
    it = iter(refs)
    take = lambda n: [next(it) for _ in range(n)]
    x_ref, g_ref, w_ref = take(3)
    rope = take(2) if rope_cols else None
    cast_in = take(n_cast)
    tail_x = take(1)[0] if has_tail else None
    tail_rope = take(2) if has_tail and rope_cols else None
    o_ref = take(1)[0]
    cast_out = take(n_cast)
    tail_o = take(1)[0] if has_tail else None
    hn0, hn1 = take(2)
    k = pl.program_id(0)

    def normalise(dst):
        x = x_ref[...]
        dst[...] = _rms_norm(x.reshape(dst.shape), g_ref[...]).astype(BF16)
        _cast_blocks(cast_in, cast_out)

    def project(hn, out, tables):
        for c in range(w_ref.shape[1] // COL_SLAB):
            cols = slice(c * COL_SLAB, (c + 1) * COL_SLAB)
            acc = jnp.dot(hn, w_ref[:, cols], preferred_element_type=F32)
            if c * COL_SLAB < rope_cols:
                cos_ref, sin_ref = tables
                for h in range(COL_SLAB // HEAD_DIM):
                    xh = acc[:, h * HEAD_DIM:(h + 1) * HEAD_DIM]
                    lanes = slice(c * COL_SLAB + h * HEAD_DIM, c * COL_SLAB + (h + 1) * HEAD_DIM)
                    out[:, lanes] = xh * cos_ref[...] + pltpu.roll(xh, HEAD_DIM // 2, 1) * sin_ref[...]
            else:
                out[:, cols] = acc

    def step(cur, prev):
        normalise(cur)
        project(prev[...], o_ref, rope)

    pl.when(k == 0)(lambda: normalise(hn0))
    pl.when((k > 0) & (k % 2 == 0))(lambda: step(hn0, hn1))
    pl.when(k % 2 == 1)(lambda: step(hn1, hn0))
    if has_tail:
        @pl.when(k == pl.num_programs(0) - 1)
        def _():
            project(_rms_norm(tail_x[...], g_ref[...]).astype(BF16), tail_o, tail_rope)


def _cast_specs(casts, n_blocks, index):
    ins, outs, shapes = [], [], []
    for arr, width, col in casts:
        rows = arr.shape[0] // n_blocks
        ins.append(pl.BlockSpec((rows, width), lambda *k, col=col: (index(*k), col)))
        outs.append(pl.BlockSpec((rows, width), lambda *k: (index(*k), 0)))
        shapes.append(jax.ShapeDtypeStruct((arr.shape[0], width), BF16))
    return ins, outs, shapes


def _norm_proj(x, g, w, *, tm=None, chunk=None, ct=None, rope=None, rope_cols=0, tiles_per_seq=1, casts=(),
               tail=None, tail_rope=None):
    m, kd = x.shape
    n = w.shape[1]
    whole = lambda a: pl.BlockSpec(a.shape, lambda k: (0,) * len(a.shape))
    tail_in = [] if tail is None else [tail] + (list(tail_rope) if rope_cols else [])
    tail_shape = [] if tail is None else [jax.ShapeDtypeStruct((tail.shape[0], n), F32)]
    cur = lambda k: jnp.minimum(k, t - 1)
    prv = lambda k: jnp.maximum(k - 1, 0)
    if chunk is None:
        t, rows = m // tm, tm
        x_spec = pl.BlockSpec((tm, kd), lambda k: (cur(k), 0))
        out_spec = pl.BlockSpec((tm, n), lambda k: (prv(k), 0))
        out_shape = jax.ShapeDtypeStruct((m, n), F32)
    else:
        nc, sg = m // chunk, chunk // POS_TILE
        t, rows = (nc // ct) * sg, ct * POS_TILE
        x = x.reshape(nc, chunk, kd)
        x_spec = pl.BlockSpec((ct, POS_TILE, kd), lambda k: (cur(k) // sg, cur(k) % sg, 0))
        out_spec = pl.BlockSpec((None, rows, n), lambda k: (prv(k) % sg, prv(k) // sg, 0))
        out_shape = jax.ShapeDtypeStruct((sg, nc * POS_TILE, n), F32)
    table = pl.BlockSpec((rows, LANES), lambda k: (prv(k) % tiles_per_seq, 0))
    cast_in, cast_out, cast_shapes = _cast_specs(casts, t, cur)
    res = pl.pallas_call(
        functools.partial(_norm_proj_kernel, rope_cols=rope_cols, n_cast=len(casts), has_tail=tail is not None),
        grid=(t + 1,),
        in_specs=[x_spec, pl.BlockSpec((1, kd), lambda k: (0, 0)),
                  pl.BlockSpec((kd, n), lambda k: (0, 0), pipeline_mode=pl.Buffered(1))]
        + ([table, table] if rope_cols else []) + cast_in + [whole(a) for a in tail_in],
        out_specs=[out_spec] + cast_out + [whole(s) for s in tail_shape],
        out_shape=[out_shape] + cast_shapes + tail_shape,
        scratch_shapes=[pltpu.VMEM((rows, kd), BF16)] * 2,
        compiler_params=_cparams("arbitrary"),
        name="norm_proj",
    )(x, g.reshape(1, kd), w, *(rope if rope_cols else ()), *[c[0] for c in casts], *tail_in)
    return res if len(res) > 1 else res[0]


def _attn_prompt_kernel(q_ref, k_ref, v_ref, *refs, n_cast):
    cast_in, (o_ref, ks_ref, vs_ref) = refs[:n_cast], refs[n_cast:n_cast + 3]
    cast_out = refs[n_cast + 3:2 * n_cast + 3]
    qn_ref, kn_ref, vn_ref, q16_ref, k16_ref, v16_ref, st_ref, acc_ref, mm_ref, ll_ref = refs[2 * n_cast + 3:]
    _cast_blocks(cast_in, cast_out)
    t = q_ref.shape[1]
    n_keep = ks_ref.shape[1]
    ks_ref[0] = k_ref[0, t - n_keep:, :]
    vs_ref[0] = v_ref[0, t - n_keep:, :]
    l4, l16 = t // 4, t // 16
    nblk = t // N_BACK
    piece = N_BACK // 4
    scale = HEAD_DIM ** -0.5
    nt = (((1,), (1,)), ((), ()))
    pad = jnp.zeros((N_BACK, HEAD_DIM), BF16)
    for ref in (kn_ref, vn_ref, k16_ref, v16_ref):
        ref[0:N_BACK, :] = pad

    qn_ref[...] = q_ref[0].astype(BF16)
    kn_ref[N_BACK:, :] = k_ref[0].astype(BF16)
    vn_ref[N_BACK:, :] = v_ref[0].astype(BF16)
    for src, dst, off in ((q_ref, q16_ref, 0), (k_ref, k16_ref, N_BACK), (v_ref, v16_ref, N_BACK)):
        for r4 in range(4):
            st_ref[r4 * l4:(r4 + 1) * l4, :] = src[0, pl.ds(r4, l4, stride=4), :]
        for r4 in range(4):
            for a in range(4):
                r16 = r4 + 4 * a
                dst[off + r16 * l16:off + (r16 + 1) * l16, :] = (
                    st_ref[pl.ds(r4 * l4 + a, l16, stride=4), :].astype(BF16))

    iota = lambda dim: lax.broadcasted_iota(jnp.int32, (N_BACK, 2 * N_BACK), dim)
    qi, kj = iota(0), iota(1)
    dist = N_BACK + qi - kj
    band = (dist >= 0) & (dist <= N_BACK)
    has_prev = kj >= N_BACK
    dist4 = 4 * (qi % piece - kj % (2 * piece) + piece) + (qi // piece - kj // (2 * piece))
    band4 = (dist4 >= 0) & (dist4 <= N_BACK)
    has_prev4 = kj % (2 * piece) >= piece

    def scores(q, k2, valid):
        s = lax.dot_general(q, k2, nt, preferred_element_type=F32) * scale
        return jnp.where(valid, s, -jnp.inf)

    def fresh(s, v2):
        m = jnp.max(s, axis=-1, keepdims=True)
        p = jnp.exp(s - m)
        wide = lambda c: jnp.broadcast_to(c, (N_BACK, HEAD_DIM))
        return wide(m), wide(jnp.sum(p, axis=-1, keepdims=True)), jnp.dot(p.astype(BF16), v2,
                                                                         preferred_element_type=F32)

    def merged(s, v2, m_old, l_old, a_old):
        m_new = jnp.maximum(m_old, jnp.max(s, axis=-1, keepdims=True))
        alpha = jnp.exp(m_old - m_new)
        p = jnp.exp(s - jnp.concatenate([m_new, m_new], axis=1))
        l_new = alpha * l_old + jnp.sum(p, axis=-1, keepdims=True)
        return m_new, l_new, alpha * a_old + jnp.dot(p.astype(BF16), v2, preferred_element_type=F32)

    def block16(b, carry):
        row0 = pl.multiple_of(b * N_BACK, N_BACK)
        rows = pl.ds(row0, N_BACK)
        later = lax.rem(b, l16 // N_BACK) > 0
        s = scores(q16_ref[rows, :], k16_ref[pl.ds(row0, 2 * N_BACK), :], band & (has_prev | later))
        mm_ref[rows, :], ll_ref[rows, :], acc_ref[rows, :] = fresh(s, v16_ref[pl.ds(row0, 2 * N_BACK), :])
        return carry

    lax.fori_loop(0, nblk, block16, 0, unroll=16)

    n_j = l16 // piece

    def block4(b, carry):
        rho, j = b // n_j, lax.rem(b, n_j)
        base = pl.multiple_of(rho * l16 + j * piece, piece)
        q_rows = [pl.ds(base + a * 4 * l16, piece) for a in range(4)]
        k_rows = [pl.ds(base + a * 4 * l16 + N_BACK - piece, 2 * piece) for a in range(4)]
        gather = lambda ref, rows: jnp.concatenate([ref[r, :] for r in rows], axis=0)
        s = scores(gather(q16_ref, q_rows), gather(k16_ref, k_rows), band4 & (has_prev4 | (j > 0)))
        m, l, acc = merged(s, gather(v16_ref, k_rows), gather(mm_ref, q_rows), gather(ll_ref, q_rows),
                           gather(acc_ref, q_rows))
        for a, r in enumerate(q_rows):
            part = slice(a * piece, (a + 1) * piece)
            mm_ref[r, :], ll_ref[r, :], acc_ref[r, :] = m[part], l[part], acc[part]
        return carry

    lax.fori_loop(0, nblk, block4, 0, unroll=16)

    for src, dst in ((acc_ref, o_ref.at[0]), (mm_ref, acc_ref), (ll_ref, mm_ref)):
        for r4 in range(4):
            for a in range(4):
                r16 = r4 + 4 * a
                st_ref[pl.ds(r4 * l4 + a, l16, stride=4), :] = src[r16 * l16:(r16 + 1) * l16, :]
        for r4 in range(4):
            dst[pl.ds(r4, l4, stride=4), :] = st_ref[r4 * l4:(r4 + 1) * l4, :]

    def block1(b, carry):
        row0 = pl.multiple_of(b * N_BACK, N_BACK)
        rows = pl.ds(row0, N_BACK)
        s = scores(qn_ref[rows, :], kn_ref[pl.ds(row0, 2 * N_BACK), :], band & (has_prev | (b > 0)))
        _, l, acc = merged(s, vn_ref[pl.ds(row0, 2 * N_BACK), :], acc_ref[rows, :], mm_ref[rows, :],
                           o_ref[0, rows, :])
        o_ref[0, rows, :] = acc / l
        return carry

    lax.fori_loop(0, nblk, block1, 0, unroll=16)


def _attn_prompt(proj, b, t, n_keep, casts=()):
    assert t % (16 * N_BACK) == 0
    p3 = proj.reshape(b, t, proj.shape[1])
    blk = lambda off: pl.BlockSpec((1, t, HEAD_DIM), lambda i, h: (i, 0, off + h))
    keep = pl.BlockSpec((1, n_keep, HEAD_DIM), lambda i, h: (i, 0, h))
    cast_in, cast_out, cast_shapes = _cast_specs(casts, b * N_HEADS, lambda i, h: i * N_HEADS + h)
    return pl.pallas_call(
        functools.partial(_attn_prompt_kernel, n_cast=len(casts)),
        grid=(b, N_HEADS),
        in_specs=[blk(0), blk(N_HEADS), blk(2 * N_HEADS)] + cast_in,
        out_specs=[pl.BlockSpec((1, t, HEAD_DIM), lambda i, h: (i, 0, h)), keep, keep] + cast_out,
        out_shape=[jax.ShapeDtypeStruct((b, t, ATTN_WIDTH), F32),
                   jax.ShapeDtypeStruct((b, n_keep, ATTN_WIDTH), F32),
                   jax.ShapeDtypeStruct((b, n_keep, ATTN_WIDTH), F32)] + cast_shapes,
        scratch_shapes=[
            pltpu.VMEM((t, HEAD_DIM), BF16),
            pltpu.VMEM((t + N_BACK, HEAD_DIM), BF16),
            pltpu.VMEM((t + N_BACK, HEAD_DIM), BF16),
            pltpu.VMEM((t, HEAD_DIM), BF16),
            pltpu.VMEM((t + N_BACK, HEAD_DIM), BF16),
            pltpu.VMEM((t + N_BACK, HEAD_DIM), BF16),
            pltpu.VMEM((t, HEAD_DIM), F32),
            pltpu.VMEM((t, HEAD_DIM), F32),
            pltpu.VMEM((t, HEAD_DIM), F32),
            pltpu.VMEM((t, HEAD_DIM), F32),
        ],
        compiler_params=_cparams("parallel", "parallel"),
        name="attn_prompt",
    )(p3, p3, p3, *[c[0] for c in casts])


COARSE = DILATIONS[-1]
NEAR = N_BACK * DILATIONS[-2]


def _attn_sample_kernel(q_ref, kn_ref, vn_ref, kfar_ref, knear_ref, vfar_ref, vnear_ref, o_ref):
    s_len = q_ref.shape[0]
    far_groups, near_groups = kfar_ref.shape[0], knear_ref.shape[0]
    n_far, n_near = far_groups * s_len, near_groups * COARSE
    n_buf = (far_groups + near_groups) * COARSE
    scale = HEAD_DIM ** -0.5
    nt = (((1,), (1,)), ((), ()))

    def count(dist):
        c = jnp.zeros(dist.shape, F32)
        for d in DILATIONS:
            hit = (dist >= 0) & (dist <= N_BACK * d) & ((dist & (d - 1)) == 0)
            c = c + jnp.where(hit, 1.0, 0.0)
        return c

    iota = lambda shape, dim: lax.broadcasted_iota(jnp.int32, shape, dim)
    query = lambda width: n_buf + iota((s_len, width), 0)
    col = iota((s_len, n_far + n_near), 1)
    far_pos = COARSE * (col // s_len) + col % s_len
    near_pos = far_groups * COARSE + (col - n_far)
    cc = count(query(n_far + n_near) - jnp.where(col < n_far, far_pos, near_pos))
    cn = count(iota((s_len, s_len), 0) - iota((s_len, s_len), 1))

    def head_rows(far_ref, near_ref, h):
        far = far_ref[:, pl.ds(h, s_len, stride=N_HEADS), :].reshape(n_far, HEAD_DIM)
        near = near_ref[:, pl.ds(h, COARSE, stride=N_HEADS), :].reshape(n_near, HEAD_DIM)
        return jnp.concatenate([far, near], axis=0).astype(BF16)

    for h in range(N_HEADS):
        cols = slice(h * HEAD_DIM, (h + 1) * HEAD_DIM)
        q = q_ref[:, cols].astype(BF16)
        sc = lax.dot_general(q, head_rows(kfar_ref, knear_ref, h), nt, preferred_element_type=F32) * scale
        sn = lax.dot_general(q, kn_ref[:, cols].astype(BF16), nt, preferred_element_type=F32) * scale
        sc = jnp.where(cc > 0, sc, -jnp.inf)
        sn = jnp.where(cn > 0, sn, -jnp.inf)
        m = jnp.maximum(jnp.max(sc, axis=-1, keepdims=True), jnp.max(sn, axis=-1, keepdims=True))
        pc = cc * jnp.exp(sc - m)
        pn = cn * jnp.exp(sn - m)
        l = jnp.sum(pc, axis=-1, keepdims=True) + jnp.sum(pn, axis=-1, keepdims=True)
        o = (jnp.dot(pc.astype(BF16), head_rows(vfar_ref, vnear_ref, h), preferred_element_type=F32)
             + jnp.dot(pn.astype(BF16), vn_ref[:, cols].astype(BF16), preferred_element_type=F32))
        o_ref[:, cols] = o / l


def _attn_sample(proj, cache_k, cache_v, b, s_len):
    n_buf = cache_k.shape[1]
    near_groups = NEAR // COARSE
    far_groups = n_buf // COARSE - near_groups
    assert n_buf % COARSE == 0 and s_len <= COARSE and far_groups % near_groups == 0 and far_groups > 0
    grouped = lambda c: c.reshape(b, n_buf // COARSE, COARSE * N_HEADS, HEAD_DIM)
    new = lambda c: pl.BlockSpec((s_len, ATTN_WIDTH), lambda i: (i, c))
    far = pl.BlockSpec((None, far_groups, s_len * N_HEADS, HEAD_DIM), lambda i: (i, 0, 0, 0))
    near = pl.BlockSpec((None, near_groups, COARSE * N_HEADS, HEAD_DIM),
                        lambda i: (i, far_groups // near_groups, 0, 0))
    ck, cv = grouped(cache_k), grouped(cache_v)
    return pl.pallas_call(
        _attn_sample_kernel,
        grid=(b,),
        in_specs=[new(0), new(1), new(2), far, near, far, near],
        out_specs=new(0),
        out_shape=jax.ShapeDtypeStruct((b * s_len, ATTN_WIDTH), F32),
        compiler_params=_cparams("parallel"),
        name="attn_sample",
    )(proj, proj, proj, ck, ck, cv, cv)


def _silu(z):
    return z * jax.nn.sigmoid(z)


def _mix_out_kernel(x_ref, o_ref, g_ref, wza_ref, wgb_ref, wgc_ref, whi_ref, wzb_ref, cw_ref, w_ref, *refs,
                    tiles_per_seq):
    xs_ref, os_ref, p2_ref, p1_ref, h_ref, cs_ref, hs_ref, css_ref, tail_ref = refs

    def tile(x, o_attn, ln, prev2, prev1):
        tm = x.shape[0]
        hn = _rms_norm(x, g_ref[...]).astype(BF16)
        gate = lambda w: jnp.dot(hn, w[...], preferred_element_type=F32)
        ch = gate(wgc_ref) * gate(whi_ref)
        pos = lax.rem(lax.broadcasted_iota(jnp.int32, (tm, 1), 0), ln)
        ch1 = jnp.where(pos == 0, prev1, pltpu.roll(ch, 1, 0))
        ch2 = jnp.where(pos == 0, prev2, jnp.where(pos == 1, prev1, pltpu.roll(ch, 2, 0)))
        conv = ch2 * cw_ref[0:1, :] + ch1 * cw_ref[1:2, :] + ch * cw_ref[2:3, :]
        o_b = (gate(wgb_ref) * conv * _silu(gate(wzb_ref))).astype(BF16)
        y = jnp.dot(o_b, w_ref[ATTN_WIDTH:, :], preferred_element_type=F32)
        o_a = (o_attn * _silu(gate(wza_ref))).astype(BF16)
        y = y + jnp.dot(o_a, w_ref[0:ATTN_WIDTH, :], preferred_element_type=F32)
        return x + y, ch

    i = pl.program_id(0)
    tm = x_ref.shape[0]

    @pl.when(i % tiles_per_seq == 0)
    def _():
        tail_ref[...] = jnp.zeros(tail_ref.shape, F32)

    h, ch = tile(x_ref[...], o_ref[...], tm, tail_ref[0:1, :], tail_ref[1:2, :])
    h_ref[...] = h
    tail_ref[...] = ch[tm - 2:tm, :]
    cs_ref[0] = ch[tm - 2:tm, :]

    @pl.when(i == pl.num_programs(0) - 1)
    def _():
        rows = xs_ref.shape[0]
        k = p1_ref.shape[0]
        ln = rows // k
        per_row = lambda p: jnp.broadcast_to(p[...], (k, ln, CONV_WIDTH)).reshape(rows, CONV_WIDTH)
        hs, chs = tile(xs_ref[...], os_ref[...], ln, per_row(p2_ref), per_row(p1_ref))
        hs_ref[...] = hs
        css_ref[...] = chs.reshape(k, ln, CONV_WIDTH)[:, ln - 2:, :]


def _mix_out(x, o_attn, xs, os_attn, conv_init_s, g, w_gates, conv_w, w_out, *, tm, tiles_per_seq):
    m, ms = x.shape[0], xs.shape[0]
    nseq = m // (tm * tiles_per_seq)
    once = lambda a: pl.BlockSpec(a.shape, lambda i: (0,) * len(a.shape), pipeline_mode=pl.Buffered(1))
    row = lambda w: pl.BlockSpec((tm, w), lambda i: (i, 0))
    p2, p1 = conv_init_s[:, 0:1], conv_init_s[:, 1:2]
    out_shape = (jax.ShapeDtypeStruct((m, D_MODEL), F32), jax.ShapeDtypeStruct((nseq, 2, CONV_WIDTH), F32),
                 jax.ShapeDtypeStruct((ms, D_MODEL), F32), jax.ShapeDtypeStruct(conv_init_s.shape, F32))
    return pl.pallas_call(
        functools.partial(_mix_out_kernel, tiles_per_seq=tiles_per_seq),
        grid=(m // tm,),
        in_specs=[row(D_MODEL), row(ATTN_WIDTH), pl.BlockSpec((1, D_MODEL), lambda i: (0, 0))]
        + [once(w) for w in w_gates] + [pl.BlockSpec((3, CONV_WIDTH), lambda i: (0, 0)), once(w_out)]
        + [once(a) for a in (xs, os_attn, p2, p1)],
        out_specs=(row(D_MODEL), pl.BlockSpec((1, 2, CONV_WIDTH), lambda i: (i // tiles_per_seq, 0, 0)))
        + tuple(pl.BlockSpec(s.shape, lambda i, n=len(s.shape): (0,) * n) for s in out_shape[2:]),
        out_shape=out_shape,
        scratch_shapes=[pltpu.VMEM((2, CONV_WIDTH), F32)],
        compiler_params=_cparams("arbitrary"),
        name="mix_out",
    )(x, o_attn, g.reshape(1, -1), *w_gates, conv_w, w_out, xs, os_attn, p2, p1)


N_SCAN = 7
GROUPS_PER_TILE = LANES // SSM_GROUP


PROMPT_CHUNK = 32
ROW_C = PROMPT_CHUNK + 1
ROW_DBL = PROMPT_CHUNK + 2
TABLE_ROWS = ROW_DBL + N_SCAN


def _s5_disc_kernel(lr_ref, li_ref, ls_ref, tr_ref, ti_ref):
    lr = lr_ref[...]
    li = li_ref[...]
    step = jnp.exp(ls_ref[...])
    mag = jnp.exp(lr * step)
    ar = mag * jnp.cos(li * step)
    ai = mag * jnp.sin(li * step)
    den = lr * lr + li * li
    nr = ar - 1.0
    tr_ref[ROW_C] = (nr * lr + ai * li) / den
    ti_ref[ROW_C] = (ai * lr - nr * li) / den
    pr = jnp.ones_like(ar)
    pi = jnp.zeros_like(ar)
    for tau in range(PROMPT_CHUNK + 1):
        tr_ref[tau] = pr
        ti_ref[tau] = pi
        dr, di = pr, pi
        pr, pi = pr * ar - pi * ai, pr * ai + pi * ar
    for i in range(N_SCAN):
        tr_ref[ROW_DBL + i] = dr
        ti_ref[ROW_DBL + i] = di
        dr, di = dr * dr - di * di, 2.0 * dr * di


def _s5_tables(lam_re, lam_im, log_step):
    g = lam_re.shape[0]
    dup = lambda a: jnp.concatenate([a, a], axis=-1)
    tr, ti = pl.pallas_call(
        _s5_disc_kernel,
        out_shape=(jax.ShapeDtypeStruct((TABLE_ROWS, g, LANES), F32),) * 2,
        name="s5_disc",
    )(dup(lam_re), dup(lam_im), log_step[:, None])
    return tr, ti


def _split_bf16(x):
    hi = x.astype(BF16)
    return hi, (x - hi.astype(F32)).astype(BF16)


def _dot_split(a, b, b_is_bf16_exact=False):
    dot = lambda x, y: jnp.dot(x, y, preferred_element_type=F32)
    ah, al = _split_bf16(a)
    if b_is_bf16_exact:
        bh = b.astype(BF16)
        return dot(ah, bh) + dot(al, bh)
    bh, bl = _split_bf16(b)
    return dot(ah, bh) + (dot(ah, bl) + dot(al, bh))


def _pad_rows(x, rows):
    if x.shape[0] == rows:
        return x
    return jnp.concatenate([x, jnp.zeros((rows - x.shape[0], x.shape[1]), x.dtype)], axis=0)


def _s5_build_operators(tr_ref, ti_ref, br_ref, bi_ref, cr_ref, ci_ref, c2r_ref, c2i_ref,
                        wt_ref, qt_ref, kr_ref, ki_ref):
    chunk = PROMPT_CHUNK
    n = SSM_GROUP * chunk
    iota = lambda shape, dim: lax.broadcasted_iota(jnp.int32, shape, dim)
    lane = iota((SSM_GROUP, 2 * SSM_STATE), 1)
    e_sel = jnp.where(iota((chunk, n), 0) == chunk - 1 - iota((chunk, n), 1) // SSM_GROUP, 1.0, 0.0)
    tile = jnp.where(iota((SSM_GROUP, n), 0) == iota((SSM_GROUP, n), 1) % SSM_GROUP, 1.0, 0.0)

    for g in range(GROUPS_PER_TILE):
        tab_r, tab_i = tr_ref[:, g, :], ti_ref[:, g, :]
        col_r = _pad_rows(tab_r, LANES).T[:SSM_STATE]
        col_i = _pad_rows(tab_i, LANES).T[:SSM_STATE]
        kr_ref[g] = col_r
        ki_ref[g] = col_i
        ccr, cci = col_r[:, ROW_C:ROW_C + 1], col_i[:, ROW_C:ROW_C + 1]
        br, bi = br_ref[g], bi_ref[g]
        bbr = ccr * br - cci * bi
        bbi = ccr * bi + cci * br
        aer = _dot_split(col_r[:, :chunk], e_sel, True)
        aei = _dot_split(col_i[:, :chunk], e_sel, True)
        btr = _dot_split(bbr, tile, True)
        bti = _dot_split(bbi, tile, True)
        ptr = aer * btr - aei * bti
        pti = aer * bti + aei * btr
        wt_ref[g, n:n + SSM_STATE, :] = ptr.astype(BF16)
        wt_ref[g, n + SSM_STATE:n + 2 * SSM_STATE, :] = pti.astype(BF16)
        taps = _dot_split(cr_ref[g], ptr) - _dot_split(ci_ref[g], pti)
        padded = jnp.concatenate([taps, jnp.zeros_like(taps)], axis=1)
        for t in range(chunk):
            sh = SSM_GROUP * (chunk - 1 - t)
            blk = padded if sh == 0 else pltpu.roll(padded, 2 * n - sh, 1)
            wt_ref[g, SSM_GROUP * t:SSM_GROUP * (t + 1), :] = blk[:, :n].astype(BF16)
        c2r, c2i = c2r_ref[g], c2i_ref[g]
        for t in range(chunk):
            ar = tab_r[t + 1:t + 2, :]
            ai = tab_i[t + 1:t + 2, :]
            x1 = jnp.where(lane < SSM_STATE, ar, -ai)
            x2 = jnp.where(lane < SSM_STATE, ai, ar)
            qt_ref[g, SSM_GROUP * t:SSM_GROUP * (t + 1), :] = (c2r * x1 - c2i * x2).astype(BF16)


def _s5_kernel(*refs):
    build_in, (u_ref, us_ref, h0_ref, dsk_ref) = refs[:8], refs[8:12]
    y_ref, hf_ref, ys_ref, hfs_ref = refs[12:16]
    ops = refs[16:20], refs[20:24]
    (dall_ref, yall_ref), (dalls_ref, yalls_ref) = refs[24:26], refs[26:28]
    k = pl.program_id(0)
    chunk_s = us_ref.shape[0] * POS_TILE

    def step(p):
        _s5_build_operators(*build_in, *ops[p])
        for u, dall, yall, h0, hf, y, chunk in ((u_ref, dall_ref, yall_ref, None, hf_ref, y_ref, PROMPT_CHUNK),
                                                (us_ref, dalls_ref, yalls_ref, h0_ref, hfs_ref, ys_ref, chunk_s)):
            _s5_regroup_in(u, dall, chunk)
            _s5_scan_groups(dall, *ops[1 - p], h0, hf, yall, chunk=chunk)
            _s5_regroup_out(yall, u, dsk_ref, y, chunk)

    pl.when(k == 0)(lambda: _s5_build_operators(*build_in, *ops[0]))
    pl.when((k > 0) & (k % 2 == 0))(lambda: step(0))
    pl.when(k % 2 == 1)(lambda: step(1))


def _s5_rows(ref, s):
    return s // POS_TILE, pl.ds(s % POS_TILE, ref.shape[1] // POS_TILE, stride=POS_TILE), slice(None)


def _s5_regroup_in(u_ref, dall_ref, chunk):
    for s in range(chunk):
        dall_ref[s] = _pad_rows(u_ref[_s5_rows(u_ref, s)], dall_ref.shape[2]).T.astype(BF16)


def _s5_regroup_out(yall_ref, u_ref, dsk_ref, y_ref, chunk):
    nc = u_ref.shape[1] // POS_TILE
    for t in range(chunk):
        at = _s5_rows(u_ref, t)
        y_ref[at] = yall_ref[t].T[:nc] + dsk_ref[...] * u_ref[at]


def _s5_scan_groups(dall_ref, wt_ref, qt_ref, kr_ref, ki_ref, h0_ref, hf_ref, yall_ref, *, chunk):
    carry = h0_ref is None
    ncp = dall_ref.shape[2]
    nseq = ncp // LANES if carry else h0_ref.shape[0]
    n = SSM_GROUP * chunk
    n_full = SSM_GROUP * PROMPT_CHUNK
    iota = lambda shape, dim: lax.broadcasted_iota(jnp.int32, shape, dim)
    lane = iota((SSM_STATE, ncp), 1) & (LANES - 1)
    slot = iota((LANES, LANES), 1)
    if not carry:
        h0_all = _pad_rows(jnp.concatenate([h0_ref[:, gp, :] for gp in range(GROUPS_PER_TILE)], axis=0), LANES).T
    acc = jnp.zeros((LANES, LANES), F32)
    for gp in range(GROUPS_PER_TILE):
        rows = slice(SSM_GROUP * gp, SSM_GROUP * (gp + 1))
        d = dall_ref[:, rows, :].reshape(n, ncp)
        if n == n_full:
            ys = jnp.dot(wt_ref[gp], d, preferred_element_type=F32)
            st = ys[n:]
            ys = ys[:n]
        else:
            ys = jnp.dot(wt_ref[gp, 0:n, 0:n], d, preferred_element_type=F32)
            st = jnp.dot(wt_ref[gp, n_full:, n_full - n:], d, preferred_element_type=F32)
        sr = st[:SSM_STATE]
        si = st[SSM_STATE:]
        if carry:
            for i in range(N_SCAN):
                sh = 1 << i
                ar = kr_ref[gp, :, ROW_DBL + i:ROW_DBL + i + 1]
                ai = ki_ref[gp, :, ROW_DBL + i:ROW_DBL + i + 1]
                pr = jnp.where(lane >= sh, pltpu.roll(sr, sh, 1), 0.0)
                pi = jnp.where(lane >= sh, pltpu.roll(si, sh, 1), 0.0)
                sr, si = sr + ar * pr - ai * pi, si + ar * pi + ai * pr
            hp = jnp.concatenate([jnp.where(lane >= 1, pltpu.roll(sr, 1, 1), 0.0),
                                  jnp.where(lane >= 1, pltpu.roll(si, 1, 1), 0.0)], axis=0)
            for b in range(nseq):
                seq = slice(LANES * b, LANES * (b + 1))
                end = jnp.concatenate([sr[:, seq], si[:, seq]], axis=0)
                acc = jnp.where(slot == 8 * gp + b, pltpu.roll(end, (8 * gp + b + 1) % LANES, 1), acc)
        else:
            hp = h0_all if gp == 0 else pltpu.roll(h0_all, LANES - 8 * gp, 1)
            hp = jnp.where(slot < nseq, hp, 0.0)
            hpr, hpi = hp[:SSM_STATE], hp[SSM_STATE:]
            ar = kr_ref[gp, :, chunk:chunk + 1]
            ai = ki_ref[gp, :, chunk:chunk + 1]
            end = jnp.concatenate([sr + ar * hpr - ai * hpi, si + ar * hpi + ai * hpr], axis=0)
            if gp:
                end = pltpu.roll(end, 8 * gp, 1)
            acc = jnp.where((slot >= 8 * gp) & (slot < 8 * gp + nseq), end, acc)
        y = ys + jnp.dot(qt_ref[gp, 0:n, :], hp.astype(BF16), preferred_element_type=F32)
        yall_ref[:, rows, :] = y.reshape(chunk, SSM_GROUP, ncp)
    hf_ref[...] = acc.T[:8 * GROUPS_PER_TILE].reshape(GROUPS_PER_TILE, 8, LANES)


def _s5_layer(tables, b_re, b_im, c_re, c_im, d_skip, u, n_long, us, h0):
    ns = h0.shape[0]
    chunk_s = us.shape[0] * POS_TILE
    n, n_s = SSM_GROUP * PROMPT_CHUNK, SSM_GROUP * chunk_s
    assert u.shape[0] * POS_TILE == PROMPT_CHUNK and u.shape[1] == n_long * LANES * POS_TILE and n_long <= 8
    assert us.shape[1] == ns * POS_TILE and ns == 8 and n % n_s == 0
    gt = GROUPS_PER_TILE
    t = SSM_GROUPS // gt
    cur = lambda k: jnp.minimum(k, t - 1)
    prv = lambda k: jnp.maximum(k - 1, 0)
    built = lambda *s: pl.BlockSpec((gt,) + s, lambda k: (cur(k),) + (0,) * len(s))
    lanes = lambda a: pl.BlockSpec(a.shape[:-1] + (LANES,), lambda k: (0,) * (a.ndim - 1) + (prv(k),))
    states = pl.BlockSpec((gt, 8, LANES), lambda k: (prv(k), 0, 0))
    dup = lambda a: jnp.concatenate([a, a], axis=-1)
    op_set = [pltpu.VMEM((gt, n + 2 * SSM_STATE, n), BF16), pltpu.VMEM((gt, n, 2 * SSM_STATE), BF16),
              pltpu.VMEM((gt, SSM_STATE, LANES), F32), pltpu.VMEM((gt, SSM_STATE, LANES), F32)]
    regroup = lambda c, width: [pltpu.VMEM((c, LANES, width), BF16), pltpu.VMEM((c, LANES, width), F32)]
    return pl.pallas_call(
        _s5_kernel,
        grid=(t + 1,),
        in_specs=[pl.BlockSpec((TABLE_ROWS, gt, LANES), lambda k: (0, cur(k), 0))] * 2 + [
                  built(SSM_STATE, SSM_GROUP), built(SSM_STATE, SSM_GROUP),
                  built(SSM_GROUP, SSM_STATE), built(SSM_GROUP, SSM_STATE),
                  built(SSM_GROUP, LANES), built(SSM_GROUP, LANES),
                  lanes(u), lanes(us), pl.BlockSpec((ns, gt, LANES), lambda k: (0, prv(k), 0)),
                  pl.BlockSpec((1, LANES), lambda k: (0, prv(k)))],
        out_specs=(lanes(u), states, lanes(us), states),
        out_shape=(jax.ShapeDtypeStruct(u.shape, F32), jax.ShapeDtypeStruct((SSM_GROUPS, 8, LANES), F32),
                   jax.ShapeDtypeStruct(us.shape, F32), jax.ShapeDtypeStruct((SSM_GROUPS, 8, LANES), F32)),
        scratch_shapes=op_set + op_set + regroup(PROMPT_CHUNK, n_long * LANES) + regroup(chunk_s, LANES),
        compiler_params=_cparams("arbitrary"),
        name="s5",
    )(*tables, b_re, b_im, c_re, c_im, dup(c_re), dup(c_im), u, us, h0, d_skip.reshape(1, -1))


def _ssm_out(y, x, gn_ref, wz_ref, wg_ref, bg_ref, wo_ref, gf_ref):
    z = jnp.dot(_rms_norm(x, gn_ref[...]).astype(BF16), wz_ref[...], preferred_element_type=F32)
    g = jax.nn.gelu(y, approximate=True)
    gate = jnp.dot(g.astype(BF16), wg_ref[...], preferred_element_type=F32) + bg_ref[...]
    yy = (g * jax.nn.sigmoid(gate)) * _silu(z)
    return _rms_norm(x + jnp.dot(yy.astype(BF16), wo_ref[...], preferred_element_type=F32), gf_ref[...])


def _glu_out_kernel(y_ref, x_ref, ys_ref, xs_ref, *rest):
    params, (o_ref, os_ref) = rest[:-2], rest[-2:]
    o_ref[...] = _ssm_out(y_ref[...], x_ref[...].reshape(y_ref.shape), *params).reshape(o_ref.shape)

    @pl.when(pl.program_id(0) == pl.num_programs(0) - 1)
    def _():
        os_ref[...] = _ssm_out(ys_ref[...], xs_ref[...], *params)


def _glu_out(y, x, ys, xs, g_norm, w_z, w_glu, b_glu, w_out, g_final, *, ct):
    m, w = x.shape
    vec = pl.BlockSpec((1, w), lambda i: (0, 0))
    mat = pl.BlockSpec((w, w), lambda i: (0, 0), pipeline_mode=pl.Buffered(1))
    small = pl.BlockSpec(xs.shape, lambda i: (0, 0))
    sg, nc = y.shape[0], y.shape[1] // POS_TILE
    x_spec = pl.BlockSpec((ct, POS_TILE, w), lambda i: (i // sg, i % sg, 0))
    out, out_s = pl.pallas_call(
        _glu_out_kernel,
        grid=((nc // ct) * sg,),
        in_specs=[pl.BlockSpec((None, ct * POS_TILE, w), lambda i: (i % sg, i // sg, 0)), x_spec, small, small,
                  vec, mat, mat, vec, mat, vec],
        out_specs=(x_spec, small),
        out_shape=(jax.ShapeDtypeStruct((nc, sg * POS_TILE, w), F32), jax.ShapeDtypeStruct(xs.shape, F32)),
        compiler_params=_cparams("arbitrary"),
        name="glu_out",
    )(y, x.reshape(nc, sg * POS_TILE, w), ys, xs, g_norm.reshape(1, w), w_z, w_glu, b_glu.reshape(1, w), w_out,
      g_final.reshape(1, w))
    return out.reshape(m, w), out_s


PAST_LEN = 16384


def kernel(x_prompt, x_sample, cache_win_k, cache_win_v, state_conv, state_ssm_re, state_ssm_im, attn_norm, w_in_ab, conv_w, w_out_ab, ssm_norm, w_in_c, lam_re, lam_im, log_step, b_re, b_im, c_re, c_im, d_skip, w_glu, b_glu, w_out_c, final_norm):
    bp, tp, _ = x_prompt.shape
    bs, ts, _ = x_sample.shape
    n_keep = min(2048, tp)
    xp = x_prompt.reshape(bp * tp, D_MODEL)
    xs = x_sample.reshape(bs * ts, D_MODEL)

    qkv_cols = 3 * ATTN_WIDTH
    gate_casts = [(w_in_ab[0], CONV_WIDTH, qkv_cols // CONV_WIDTH + c) for c in range(5)]
    layer1_casts = [(w_in_c[0], D_MODEL, 0), (w_in_c[0], D_MODEL, 1), (w_glu[0], D_MODEL, 0), (w_out_c[0], D_MODEL, 0)]

    cos_p, sin_p, w_qkv = _rope_tables(tp, tp, 0, casts=[(w_in_ab[0], qkv_cols, 0)])
    proj_p, *w_gates, w_out0, proj_s = _norm_proj(
        xp, attn_norm[0], w_qkv, tm=512, rope=(cos_p, sin_p), rope_cols=2 * ATTN_WIDTH, tiles_per_seq=tp // 512,
        casts=gate_casts + [(w_out_ab[0], D_MODEL, 0)], tail=xs, tail_rope=_rope_tables(bs * ts, ts, PAST_LEN))
    o_p, k_p, v_p, w_u, w_z, w_glu1, w_out1 = _attn_prompt(proj_p, bp, tp, n_keep, casts=layer1_casts)
    o_p = o_p.reshape(bp * tp, ATTN_WIDTH)
    k_p = k_p.reshape(1, bp, n_keep, N_HEADS, HEAD_DIM)
    v_p = v_p.reshape(1, bp, n_keep, N_HEADS, HEAD_DIM)
    o_s = _attn_sample(proj_s, cache_win_k[0], cache_win_v[0], bs, ts)
    h1_p, conv_p, h1_s, conv_s = _mix_out(xp, o_p, xs, o_s, state_conv[0], attn_norm[0], w_gates, conv_w[0], w_out0,
                                          tm=512, tiles_per_seq=tp // 512)
    k_s = proj_s[:, ATTN_WIDTH:2 * ATTN_WIDTH].reshape(1, bs, ts, N_HEADS, HEAD_DIM)
    v_s = proj_s[:, 2 * ATTN_WIDTH:3 * ATTN_WIDTH].reshape(1, bs, ts, N_HEADS, HEAD_DIM)

    assert ts == POS_TILE
    tables = _s5_tables(lam_re[0], lam_im[0], log_step[0])
    u_p, u_s = _norm_proj(h1_p, ssm_norm[0], w_u, chunk=PROMPT_CHUNK, ct=64, tail=h1_s)
    h0 = jnp.concatenate([state_ssm_re[0], state_ssm_im[0]], axis=-1)
    y_p, hf_p, y_s, hf_s = _s5_layer(tables, b_re[0], b_im[0], c_re[0], c_im[0], d_skip[0], u_p, bp, u_s[None], h0)
    out_p, out_s = _glu_out(y_p, h1_p, y_s[0], h1_s, ssm_norm[0], w_z, w_glu1, b_glu[0], w_out1, final_norm, ct=32)
    hf_p = hf_p[:, :bp].transpose(1, 0, 2)[None]
    hf_s = hf_s[:, :bs].transpose(1, 0, 2)[None]
    return (out_p.reshape(bp, tp, D_MODEL), out_s.reshape(bs, ts, D_MODEL),
            k_p, v_p, conv_p[None], hf_p[..., :SSM_STATE], hf_p[..., SSM_STATE:],
            k_s, v_s, conv_s[None], hf_s[..., :SSM_STATE], hf_s[..., SSM_STATE:])
```

```python
import functools
import math

import jax
import jax.numpy as jnp
from jax import lax
from jax.experimental import pallas as pl
from jax.experimental.pallas import tpu as pltpu

D_MODEL = 2048
HEAD_DIM = 128
N_HEADS = 8
ATTN_WIDTH = 1024
CONV_WIDTH = 1024
DILATIONS = (1, 4, 16)
N_BACK = 128
ROPE_THETA = 10000.0
RMS_EPS = 1e-6
SSM_GROUP = 16
SSM_GROUPS = 128
SSM_STATE = 64
LANES = 128
VMEM_LIMIT = 56 * 1024 * 1024

F32 = jnp.float32
BF16 = jnp.bfloat16


def _cparams(*sem):
    return pltpu.CompilerParams(dimension_semantics=sem, vmem_limit_bytes=VMEM_LIMIT)


def _rope_table_kernel(inv_ref, *refs, period, offset, n_cast):
    cos_ref, sin_ref = refs[n_cast:n_cast + 2]
    cr_ref, sr_ref = refs[-2:]
    _cast_blocks(refs[:n_cast], refs[n_cast + 2:-2])
    rows = cos_ref.shape[0]
    i = pl.program_id(0)

    @pl.when(i == 0)
    def _():
        ang = lax.broadcasted_iota(jnp.int32, (rows, LANES), 0).astype(F32) * inv_ref[...]
        cr_ref[...] = jnp.cos(ang)
        sr_ref[...] = jnp.sin(ang)

    base = (offset + lax.rem(i * rows, period)).astype(F32) * inv_ref[...]
    cb, sb = jnp.cos(base), jnp.sin(base)
    lane = lax.broadcasted_iota(jnp.int32, (rows, LANES), 1)
    cos_ref[...] = cb * cr_ref[...] - sb * sr_ref[...]
    sin_ref[...] = jnp.where(lane < HEAD_DIM // 2, -1.0, 1.0) * (sb * cr_ref[...] + cb * sr_ref[...])


def _rope_tables(rows, period, offset, casts=()):
    half = HEAD_DIM // 2
    inv = ROPE_THETA ** (-jnp.arange(half, dtype=F32) / half)
    inv2 = jnp.concatenate([inv, inv])[None, :]
    tr = math.gcd(math.gcd(rows, period), 256)
    table = pl.BlockSpec((tr, LANES), lambda i: (i, 0))
    cast_in, cast_out, cast_shapes = _cast_specs(casts, rows // tr, lambda i: i)
    return pl.pallas_call(
        functools.partial(_rope_table_kernel, period=period, offset=offset, n_cast=len(casts)),
        grid=(rows // tr,),
        in_specs=[pl.BlockSpec((1, LANES), lambda i: (0, 0))] + cast_in,
        out_specs=[table, table] + cast_out,
        out_shape=[jax.ShapeDtypeStruct((rows, LANES), F32)] * 2 + cast_shapes,
        scratch_shapes=[pltpu.VMEM((tr, LANES), F32)] * 2,
        compiler_params=_cparams("arbitrary"),
        name="rope_table",
    )(inv2, *[c[0] for c in casts])


def _rms_norm(x, g):
    return x * lax.rsqrt(jnp.mean(x * x, axis=-1, keepdims=True) + RMS_EPS) * g


POS_TILE = 8
COL_SLAB = 256


def _cast_blocks(cast_in, cast_out):
    for src, dst in zip(cast_in, cast_out):
        dst[...] = src[...].astype(BF16)


def _norm_proj_kernel(*refs, rope_cols, n_cast, has_tail):
    it = iter(refs)
    take = lambda n: [next(it) for _ in range(n)]
    x_ref, g_ref, w_ref = take(3)
    rope = take(2) if rope_cols else None
    cast_in = take(n_cast)
    tail_x = take(1)[0] if has_tail else None
    tail_rope = take(2) if has_tail and rope_cols else None
    o_ref = take(1)[0]
    cast_out = take(n_cast)
    tail_o = take(1)[0] if has_tail else None
    hn0, hn1 = take(2)
    k = pl.program_id(0)

    def normalise(dst):
        x = x_ref[...]
        dst[...] = _rms_norm(x.reshape(dst.shape), g_ref[...]).astype(BF16)
        _cast_blocks(cast_in, cast_out)

    def project(hn, out, tables):
        for c in range(w_ref.shape[1] // COL_SLAB):
            cols = slice(c * COL_SLAB, (c + 1) * COL_SLAB)
            acc = jnp.dot(hn, w_ref[:, cols], preferred_element_type=F32)
            if c * COL_SLAB < rope_cols:
                cos_ref, sin_ref = tables
                for h in range(COL_SLAB // HEAD_DIM):
                    xh = acc[:, h * HEAD_DIM:(h + 1) * HEAD_DIM]
                    lanes = slice(c * COL_SLAB + h * HEAD_DIM, c * COL_SLAB + (h + 1) * HEAD_DIM)
                    out[:, lanes] = xh * cos_ref[...] + pltpu.roll(xh, HEAD_DIM // 2, 1) * sin_ref[...]
            else:
                out[:, cols] = acc

    def step(cur, prev):
        normalise(cur)
        project(prev[...], o_ref, rope)

    pl.when(k == 0)(lambda: normalise(hn0))
    pl.when((k > 0) & (k % 2 == 0))(lambda: step(hn0, hn1))
    pl.when(k % 2 == 1)(lambda: step(hn1, hn0))
    if has_tail:
        @pl.when(k == pl.num_programs(0) - 1)
        def _():
            project(_rms_norm(tail_x[...], g_ref[...]).astype(BF16), tail_o, tail_rope)


def _cast_specs(casts, n_blocks, index):
    ins, outs, shapes = [], [], []
    for arr, width, col in casts:
        rows = arr.shape[0] // n_blocks
        ins.append(pl.BlockSpec((rows, width), lambda *k, col=col: (index(*k), col)))
        outs.append(pl.BlockSpec((rows, width), lambda *k: (index(*k), 0)))
        shapes.append(jax.ShapeDtypeStruct((arr.shape[0], width), BF16))
    return ins, outs, shapes


def _norm_proj(x, g, w, *, tm=None, chunk=None, ct=None, rope=None, rope_cols=0, tiles_per_seq=1, casts=(),
               tail=None, tail_rope=None):
    m, kd = x.shape
    n = w.shape[1]
    whole = lambda a: pl.BlockSpec(a.shape, lambda k: (0,) * len(a.shape))
    tail_in = [] if tail is None else [tail] + (list(tail_rope) if rope_cols else [])
    tail_shape = [] if tail is None else [jax.ShapeDtypeStruct((tail.shape[0], n), F32)]
    cur = lambda k: jnp.minimum(k, t - 1)
    prv = lambda k: jnp.maximum(k - 1, 0)
    if chunk is None:
        t, rows = m // tm, tm
        x_spec = pl.BlockSpec((tm, kd), lambda k: (cur(k), 0))
        out_spec = pl.BlockSpec((tm, n), lambda k: (prv(k), 0))
        out_shape = jax.ShapeDtypeStruct((m, n), F32)
    else:
        nc, sg = m // chunk, chunk // POS_TILE
        t, rows = (nc // ct) * sg, ct * POS_TILE
        x = x.reshape(nc, chunk, kd)
        x_spec = pl.BlockSpec((ct, POS_TILE, kd), lambda k: (cur(k) // sg, cur(k) % sg, 0))
        out_spec = pl.BlockSpec((None, rows, n), lambda k: (prv(k) % sg, prv(k) // sg, 0))
        out_shape = jax.ShapeDtypeStruct((sg, nc * POS_TILE, n), F32)
    table = pl.BlockSpec((rows, LANES), lambda k: (prv(k) % tiles_per_seq, 0))
    cast_in, cast_out, cast_shapes = _cast_specs(casts, t, cur)
    res = pl.pallas_call(
        functools.partial(_norm_proj_kernel, rope_cols=rope_cols, n_cast=len(casts), has_tail=tail is not None),
        grid=(t + 1,),
        in_specs=[x_spec, pl.BlockSpec((1, kd), lambda k: (0, 0)),
                  pl.BlockSpec((kd, n), lambda k: (0, 0), pipeline_mode=pl.Buffered(1))]
        + ([table, table] if rope_cols else []) + cast_in + [whole(a) for a in tail_in],
        out_specs=[out_spec] + cast_out + [whole(s) for s in tail_shape],
        out_shape=[out_shape] + cast_shapes + tail_shape,
        scratch_shapes=[pltpu.VMEM((rows, kd), BF16)] * 2,
        compiler_params=_cparams("arbitrary"),
        name="norm_proj",
    )(x, g.reshape(1, kd), w, *(rope if rope_cols else ()), *[c[0] for c in casts], *tail_in)
    return res if len(res) > 1 else res[0]


def _attn_prompt_kernel(q_ref, k_ref, v_ref, *refs, n_cast):
    cast_in, (o_ref, ks_ref, vs_ref) = refs[:n_cast], refs[n_cast:n_cast + 3]
    cast_out = refs[n_cast + 3:2 * n_cast + 3]
    qn_ref, kn_ref, vn_ref, q16_ref, k16_ref, v16_ref, st_ref, acc_ref, mm_ref, ll_ref = refs[2 * n_cast + 3:]
    _cast_blocks(cast_in, cast_out)
    t = q_ref.shape[1]
    n_keep = ks_ref.shape[1]
    ks_ref[0] = k_ref[0, t - n_keep:, :]
    vs_ref[0] = v_ref[0, t - n_keep:, :]
    l4, l16 = t // 4, t // 16
    nblk = t // N_BACK
    piece = N_BACK // 4
    scale = HEAD_DIM ** -0.5
    nt = (((1,), (1,)), ((), ()))
    pad = jnp.zeros((N_BACK, HEAD_DIM), BF16)
    for ref in (kn_ref, vn_ref, k16_ref, v16_ref):
        ref[0:N_BACK, :] = pad

    qn_ref[...] = q_ref[0].astype(BF16)
    kn_ref[N_BACK:, :] = k_ref[0].astype(BF16)
    vn_ref[N_BACK:, :] = v_ref[0].astype(BF16)
    for src, dst, off in ((q_ref, q16_ref, 0), (k_ref, k16_ref, N_BACK), (v_ref, v16_ref, N_BACK)):
        for r4 in range(4):
            st_ref[r4 * l4:(r4 + 1) * l4, :] = src[0, pl.ds(r4, l4, stride=4), :]
        for r4 in range(4):
            for a in range(4):
                r16 = r4 + 4 * a
                dst[off + r16 * l16:off + (r16 + 1) * l16, :] = (
                    st_ref[pl.ds(r4 * l4 + a, l16, stride=4), :].astype(BF16))

    iota = lambda dim: lax.broadcasted_iota(jnp.int32, (N_BACK, 2 * N_BACK), dim)
    qi, kj = iota(0), iota(1)
    dist = N_BACK + qi - kj
    band = (dist >= 0) & (dist <= N_BACK)
    has_prev = kj >= N_BACK
    dist4 = 4 * (qi % piece - kj % (2 * piece) + piece) + (qi // piece - kj // (2 * piece))
    band4 = (dist4 >= 0) & (dist4 <= N_BACK)
    has_prev4 = kj % (2 * piece) >= piece

    def scores(q, k2, valid):
        s = lax.dot_general(q, k2, nt, preferred_element_type=F32) * scale
        return jnp.where(valid, s, -jnp.inf)

    def fresh(s, v2):
        m = jnp.max(s, axis=-1, keepdims=True)
        p = jnp.exp(s - m)
        wide = lambda c: jnp.broadcast_to(c, (N_BACK, HEAD_DIM))
        return wide(m), wide(jnp.sum(p, axis=-1, keepdims=True)), jnp.dot(p.astype(BF16), v2,
                                                                         preferred_element_type=F32)

    def merged(s, v2, m_old, l_old, a_old):
        m_new = jnp.maximum(m_old, jnp.max(s, axis=-1, keepdims=True))
        alpha = jnp.exp(m_old - m_new)
        p = jnp.exp(s - jnp.concatenate([m_new, m_new], axis=1))
        l_new = alpha * l_old + jnp.sum(p, axis=-1, keepdims=True)
        return m_new, l_new, alpha * a_old + jnp.dot(p.astype(BF16), v2, preferred_element_type=F32)

    def block16(b, carry):
        row0 = pl.multiple_of(b * N_BACK, N_BACK)
        rows = pl.ds(row0, N_BACK)
        later = lax.rem(b, l16 // N_BACK) > 0
        s = scores(q16_ref[rows, :], k16_ref[pl.ds(row0, 2 * N_BACK), :], band & (has_prev | later))
        mm_ref[rows, :], ll_ref[rows, :], acc_ref[rows, :] = fresh(s, v16_ref[pl.ds(row0, 2 * N_BACK), :])
        return carry

    lax.fori_loop(0, nblk, block16, 0, unroll=16)

    n_j = l16 // piece

    def block4(b, carry):
        rho, j = b // n_j, lax.rem(b, n_j)
        base = pl.multiple_of(rho * l16 + j * piece, piece)
        q_rows = [pl.ds(base + a * 4 * l16, piece) for a in range(4)]
        k_rows = [pl.ds(base + a * 4 * l16 + N_BACK - piece, 2 * piece) for a in range(4)]
        gather = lambda ref, rows: jnp.concatenate([ref[r, :] for r in rows], axis=0)
        s = scores(gather(q16_ref, q_rows), gather(k16_ref, k_rows), band4 & (has_prev4 | (j > 0)))
        m, l, acc = merged(s, gather(v16_ref, k_rows), gather(mm_ref, q_rows), gather(ll_ref, q_rows),
                           gather(acc_ref, q_rows))
        for a, r in enumerate(q_rows):
            part = slice(a * piece, (a + 1) * piece)
            mm_ref[r, :], ll_ref[r, :], acc_ref[r, :] = m[part], l[part], acc[part]
        return carry

    lax.fori_loop(0, nblk, block4, 0, unroll=16)

    for src, dst in ((acc_ref, o_ref.at[0]), (mm_ref, acc_ref), (ll_ref, mm_ref)):
        for r4 in range(4):
            for a in range(4):
                r16 = r4 + 4 * a
                st_ref[pl.ds(r4 * l4 + a, l16, stride=4), :] = src[r16 * l16:(r16 + 1) * l16, :]
        for r4 in range(4):
            dst[pl.ds(r4, l4, stride=4), :] = st_ref[r4 * l4:(r4 + 1) * l4, :]

    def block1(b, carry):
        row0 = pl.multiple_of(b * N_BACK, N_BACK)
        rows = pl.ds(row0, N_BACK)
        s = scores(qn_ref[rows, :], kn_ref[pl.ds(row0, 2 * N_BACK), :], band & (has_prev | (b > 0)))
        _, l, acc = merged(s, vn_ref[pl.ds(row0, 2 * N_BACK), :], acc_ref[rows, :], mm_ref[rows, :],
                           o_ref[0, rows, :])
        o_ref[0, rows, :] = acc / l
        return carry

    lax.fori_loop(0, nblk, block1, 0, unroll=16)


def _attn_prompt(proj, b, t, n_keep, casts=()):
    assert t % (16 * N_BACK) == 0
    p3 = proj.reshape(b, t, proj.shape[1])
    blk = lambda off: pl.BlockSpec((1, t, HEAD_DIM), lambda i, h: (i, 0, off + h))
    keep = pl.BlockSpec((1, n_keep, HEAD_DIM), lambda i, h: (i, 0, h))
    cast_in, cast_out, cast_shapes = _cast_specs(casts, b * N_HEADS, lambda i, h: i * N_HEADS + h)
    return pl.pallas_call(
        functools.partial(_attn_prompt_kernel, n_cast=len(casts)),
        grid=(b, N_HEADS),
        in_specs=[blk(0), blk(N_HEADS), blk(2 * N_HEADS)] + cast_in,
        out_specs=[pl.BlockSpec((1, t, HEAD_DIM), lambda i, h: (i, 0, h)), keep, keep] + cast_out,
        out_shape=[jax.ShapeDtypeStruct((b, t, ATTN_WIDTH), F32),
                   jax.ShapeDtypeStruct((b, n_keep, ATTN_WIDTH), F32),
                   jax.ShapeDtypeStruct((b, n_keep, ATTN_WIDTH), F32)] + cast_shapes,
        scratch_shapes=[
            pltpu.VMEM((t, HEAD_DIM), BF16),
            pltpu.VMEM((t + N_BACK, HEAD_DIM), BF16),
            pltpu.VMEM((t + N_BACK, HEAD_DIM), BF16),
            pltpu.VMEM((t, HEAD_DIM), BF16),
            pltpu.VMEM((t + N_BACK, HEAD_DIM), BF16),
            pltpu.VMEM((t + N_BACK, HEAD_DIM), BF16),
            pltpu.VMEM((t, HEAD_DIM), F32),
            pltpu.VMEM((t, HEAD_DIM), F32),
            pltpu.VMEM((t, HEAD_DIM), F32),
            pltpu.VMEM((t, HEAD_DIM), F32),
        ],
        compiler_params=_cparams("parallel", "parallel"),
        name="attn_prompt",
    )(p3, p3, p3, *[c[0] for c in casts])


COARSE = DILATIONS[-1]
NEAR = N_BACK * DILATIONS[-2]


def _attn_sample_kernel(q_ref, kn_ref, vn_ref, kfar_ref, knear_ref, vfar_ref, vnear_ref, o_ref):
    s_len = q_ref.shape[0]
    far_groups, near_groups = kfar_ref.shape[0], knear_ref.shape[0]
    n_far, n_near = far_groups * s_len, near_groups * COARSE
    n_buf = (far_groups + near_groups) * COARSE
    scale = HEAD_DIM ** -0.5
    nt = (((1,), (1,)), ((), ()))

    def count(dist):
        c = jnp.zeros(dist.shape, F32)
        for d in DILATIONS:
            hit = (dist >= 0) & (dist <= N_BACK * d) & ((dist & (d - 1)) == 0)
            c = c + jnp.where(hit, 1.0, 0.0)
        return c

    iota = lambda shape, dim: lax.broadcasted_iota(jnp.int32, shape, dim)
    query = lambda width: n_buf + iota((s_len, width), 0)
    col = iota((s_len, n_far + n_near), 1)
    far_pos = COARSE * (col // s_len) + col % s_len
    near_pos = far_groups * COARSE + (col - n_far)
    cc = count(query(n_far + n_near) - jnp.where(col < n_far, far_pos, near_pos))
    cn = count(iota((s_len, s_len), 0) - iota((s_len, s_len), 1))

    def head_rows(far_ref, near_ref, h):
        far = far_ref[:, pl.ds(h, s_len, stride=N_HEADS), :].reshape(n_far, HEAD_DIM)
        near = near_ref[:, pl.ds(h, COARSE, stride=N_HEADS), :].reshape(n_near, HEAD_DIM)
        return jnp.concatenate([far, near], axis=0).astype(BF16)

    heads = range(N_HEADS)
    cols = [slice(h * HEAD_DIM, (h + 1) * HEAD_DIM) for h in heads]
    stack = lambda parts: jnp.concatenate(parts, axis=0)
    qs = [q_ref[:, c].astype(BF16) for c in cols]
    sc = stack([lax.dot_general(qs[h], head_rows(kfar_ref, knear_ref, h), nt, preferred_element_type=F32)
                for h in heads]) * scale
    sn = stack([lax.dot_general(qs[h], kn_ref[:, cols[h]].astype(BF16), nt, preferred_element_type=F32)
                for h in heads]) * scale
    cc, cn = stack([cc] * N_HEADS), stack([cn] * N_HEADS)
    sc = jnp.where(cc > 0, sc, -jnp.inf)
    sn = jnp.where(cn > 0, sn, -jnp.inf)
    m = jnp.maximum(jnp.max(sc, axis=-1, keepdims=True), jnp.max(sn, axis=-1, keepdims=True))
    pc = cc * jnp.exp(sc - m)
    pn = cn * jnp.exp(sn - m)
    l = jnp.sum(pc, axis=-1, keepdims=True) + jnp.sum(pn, axis=-1, keepdims=True)
    for h in heads:
        rows = slice(h * s_len, (h + 1) * s_len)
        o = (jnp.dot(pc[rows].astype(BF16), head_rows(vfar_ref, vnear_ref, h), preferred_element_type=F32)
             + jnp.dot(pn[rows].astype(BF16), vn_ref[:, cols[h]].astype(BF16), preferred_element_type=F32))
        o_ref[:, cols[h]] = o / l[rows]


def _attn_sample(proj, cache_k, cache_v, b, s_len):
    n_buf = cache_k.shape[1]
    near_groups = NEAR // COARSE
    far_groups = n_buf // COARSE - near_groups
    assert n_buf % COARSE == 0 and s_len <= COARSE and far_groups % near_groups == 0 and far_groups > 0
    grouped = lambda c: c.reshape(b, n_buf // COARSE, COARSE * N_HEADS, HEAD_DIM)
    new = lambda c: pl.BlockSpec((s_len, ATTN_WIDTH), lambda i: (i, c))
    far = pl.BlockSpec((None, far_groups, s_len * N_HEADS, HEAD_DIM), lambda i: (i, 0, 0, 0))
    near = pl.BlockSpec((None, near_groups, COARSE * N_HEADS, HEAD_DIM),
                        lambda i: (i, far_groups // near_groups, 0, 0))
    ck, cv = grouped(cache_k), grouped(cache_v)
    return pl.pallas_call(
        _attn_sample_kernel,
        grid=(b,),
        in_specs=[new(0), new(1), new(2), far, near, far, near],
        out_specs=new(0),
        out_shape=jax.ShapeDtypeStruct((b * s_len, ATTN_WIDTH), F32),
        compiler_params=_cparams("parallel"),
        name="attn_sample",
    )(proj, proj, proj, ck, ck, cv, cv)


def _silu(z):
    return z * jax.nn.sigmoid(z)


def _mix_out_kernel(x_ref, o_ref, g_ref, wza_ref, wgb_ref, wgc_ref, whi_ref, wzb_ref, cw_ref, w_ref, *refs,
                    tiles_per_seq):
    xs_ref, os_ref, p2_ref, p1_ref, h_ref, cs_ref, hs_ref, css_ref, tail_ref = refs

    def tile(x, o_attn, ln, prev2, prev1):
        tm = x.shape[0]
        hn = _rms_norm(x, g_ref[...]).astype(BF16)
        gate = lambda w: jnp.dot(hn, w[...], preferred_element_type=F32)
        ch = gate(wgc_ref) * gate(whi_ref)
        pos = lax.rem(lax.broadcasted_iota(jnp.int32, (tm, 1), 0), ln)
        ch1 = jnp.where(pos == 0, prev1, pltpu.roll(ch, 1, 0))
        ch2 = jnp.where(pos == 0, prev2, jnp.where(pos == 1, prev1, pltpu.roll(ch, 2, 0)))
        conv = ch2 * cw_ref[0:1, :] + ch1 * cw_ref[1:2, :] + ch * cw_ref[2:3, :]
        o_b = (gate(wgb_ref) * conv * _silu(gate(wzb_ref))).astype(BF16)
        y = jnp.dot(o_b, w_ref[ATTN_WIDTH:, :], preferred_element_type=F32)
        o_a = (o_attn * _silu(gate(wza_ref))).astype(BF16)
        y = y + jnp.dot(o_a, w_ref[0:ATTN_WIDTH, :], preferred_element_type=F32)
        return x + y, ch

    i = pl.program_id(0)
    tm = x_ref.shape[0]

    @pl.when(i % tiles_per_seq == 0)
    def _():
        tail_ref[...] = jnp.zeros(tail_ref.shape, F32)

    h, ch = tile(x_ref[...], o_ref[...], tm, tail_ref[0:1, :], tail_ref[1:2, :])
    h_ref[...] = h
    tail_ref[...] = ch[tm - 2:tm, :]
    cs_ref[0] = ch[tm - 2:tm, :]

    @pl.when(i == pl.num_programs(0) - 1)
    def _():
        rows = xs_ref.shape[0]
        k = p1_ref.shape[0]
        ln = rows // k
        per_row = lambda p: jnp.broadcast_to(p[...], (k, ln, CONV_WIDTH)).reshape(rows, CONV_WIDTH)
        hs, chs = tile(xs_ref[...], os_ref[...], ln, per_row(p2_ref), per_row(p1_ref))
        hs_ref[...] = hs
        css_ref[...] = chs.reshape(k, ln, CONV_WIDTH)[:, ln - 2:, :]


def _mix_out(x, o_attn, xs, os_attn, conv_init_s, g, w_gates, conv_w, w_out, *, tm, tiles_per_seq):
    m, ms = x.shape[0], xs.shape[0]
    nseq = m // (tm * tiles_per_seq)
    once = lambda a: pl.BlockSpec(a.shape, lambda i: (0,) * len(a.shape), pipeline_mode=pl.Buffered(1))
    row = lambda w: pl.BlockSpec((tm, w), lambda i: (i, 0))
    p2, p1 = conv_init_s[:, 0:1], conv_init_s[:, 1:2]
    out_shape = (jax.ShapeDtypeStruct((m, D_MODEL), F32), jax.ShapeDtypeStruct((nseq, 2, CONV_WIDTH), F32),
                 jax.ShapeDtypeStruct((ms, D_MODEL), F32), jax.ShapeDtypeStruct(conv_init_s.shape, F32))
    return pl.pallas_call(
        functools.partial(_mix_out_kernel, tiles_per_seq=tiles_per_seq),
        grid=(m // tm,),
        in_specs=[row(D_MODEL), row(ATTN_WIDTH), pl.BlockSpec((1, D_MODEL), lambda i: (0, 0))]
        + [once(w) for w in w_gates] + [pl.BlockSpec((3, CONV_WIDTH), lambda i: (0, 0)), once(w_out)]
        + [once(a) for a in (xs, os_attn, p2, p1)],
        out_specs=(row(D_MODEL), pl.BlockSpec((1, 2, CONV_WIDTH), lambda i: (i // tiles_per_seq, 0, 0)))
        + tuple(pl.BlockSpec(s.shape, lambda i, n=len(s.shape): (0,) * n) for s in out_shape[2:]),
        out_shape=out_shape,
        scratch_shapes=[pltpu.VMEM((2, CONV_WIDTH), F32)],
        compiler_params=_cparams("arbitrary"),
        name="mix_out",
    )(x, o_attn, g.reshape(1, -1), *w_gates, conv_w, w_out, xs, os_attn, p2, p1)


N_SCAN = 7
GROUPS_PER_TILE = LANES // SSM_GROUP


PROMPT_CHUNK = 32
ROW_C = PROMPT_CHUNK + 1
ROW_DBL = PROMPT_CHUNK + 2
TABLE_ROWS = ROW_DBL + N_SCAN


def _s5_disc_kernel(lr_ref, li_ref, ls_ref, tr_ref, ti_ref):
    lr = lr_ref[...]
    li = li_ref[...]
    step = jnp.exp(ls_ref[...])
    mag = jnp.exp(lr * step)
    ar = mag * jnp.cos(li * step)
    ai = mag * jnp.sin(li * step)
    den = lr * lr + li * li
    nr = ar - 1.0
    tr_ref[ROW_C] = (nr * lr + ai * li) / den
    ti_ref[ROW_C] = (ai * lr - nr * li) / den
    pr = jnp.ones_like(ar)
    pi = jnp.zeros_like(ar)
    for tau in range(PROMPT_CHUNK + 1):
        tr_ref[tau] = pr
        ti_ref[tau] = pi
        dr, di = pr, pi
        pr, pi = pr * ar - pi * ai, pr * ai + pi * ar
    for i in range(N_SCAN):
        tr_ref[ROW_DBL + i] = dr
        ti_ref[ROW_DBL + i] = di
        dr, di = dr * dr - di * di, 2.0 * dr * di


def _s5_tables(lam_re, lam_im, log_step):
    g = lam_re.shape[0]
    dup = lambda a: jnp.concatenate([a, a], axis=-1)
    tr, ti = pl.pallas_call(
        _s5_disc_kernel,
        out_shape=(jax.ShapeDtypeStruct((TABLE_ROWS, g, LANES), F32),) * 2,
        name="s5_disc",
    )(dup(lam_re), dup(lam_im), log_step[:, None])
    return tr, ti


def _split_bf16(x):
    hi = x.astype(BF16)
    return hi, (x - hi.astype(F32)).astype(BF16)


def _dot_split(a, b, b_is_bf16_exact=False):
    dot = lambda x, y: jnp.dot(x, y, preferred_element_type=F32)
    ah, al = _split_bf16(a)
    if b_is_bf16_exact:
        bh = b.astype(BF16)
        return dot(ah, bh) + dot(al, bh)
    bh, bl = _split_bf16(b)
    return dot(ah, bh) + (dot(ah, bl) + dot(al, bh))


def _pad_rows(x, rows):
    if x.shape[0] == rows:
        return x
    return jnp.concatenate([x, jnp.zeros((rows - x.shape[0], x.shape[1]), x.dtype)], axis=0)


def _s5_build_operators(tr_ref, ti_ref, br_ref, bi_ref, cr_ref, ci_ref, c2r_ref, c2i_ref,
                        wt_ref, qt_ref, kr_ref, ki_ref):
    chunk = PROMPT_CHUNK
    n = SSM_GROUP * chunk
    iota = lambda shape, dim: lax.broadcasted_iota(jnp.int32, shape, dim)
    lane = iota((SSM_GROUP, 2 * SSM_STATE), 1)
    e_sel = jnp.where(iota((chunk, n), 0) == chunk - 1 - iota((chunk, n), 1) // SSM_GROUP, 1.0, 0.0)
    tile = jnp.where(iota((SSM_GROUP, n), 0) == iota((SSM_GROUP, n), 1) % SSM_GROUP, 1.0, 0.0)

    for g in range(GROUPS_PER_TILE):
        tab_r, tab_i = tr_ref[:, g, :], ti_ref[:, g, :]
        col_r = _pad_rows(tab_r, LANES).T[:SSM_STATE]
        col_i = _pad_rows(tab_i, LANES).T[:SSM_STATE]
        kr_ref[g] = col_r
        ki_ref[g] = col_i
        ccr, cci = col_r[:, ROW_C:ROW_C + 1], col_i[:, ROW_C:ROW_C + 1]
        br, bi = br_ref[g], bi_ref[g]
        bbr = ccr * br - cci * bi
        bbi = ccr * bi + cci * br
        aer = _dot_split(col_r[:, :chunk], e_sel, True)
        aei = _dot_split(col_i[:, :chunk], e_sel, True)
        btr = _dot_split(bbr, tile, True)
        bti = _dot_split(bbi, tile, True)
        ptr = aer * btr - aei * bti
        pti = aer * bti + aei * btr
        wt_ref[g, n:n + SSM_STATE, :] = ptr.astype(BF16)
        wt_ref[g, n + SSM_STATE:n + 2 * SSM_STATE, :] = pti.astype(BF16)
        taps = _dot_split(cr_ref[g], ptr) - _dot_split(ci_ref[g], pti)
        padded = jnp.concatenate([taps, jnp.zeros_like(taps)], axis=1)
        for t in range(chunk):
            sh = SSM_GROUP * (chunk - 1 - t)
            blk = padded if sh == 0 else pltpu.roll(padded, 2 * n - sh, 1)
            wt_ref[g, SSM_GROUP * t:SSM_GROUP * (t + 1), :] = blk[:, :n].astype(BF16)
        c2r, c2i = c2r_ref[g], c2i_ref[g]
        for t in range(chunk):
            ar = tab_r[t + 1:t + 2, :]
            ai = tab_i[t + 1:t + 2, :]
            x1 = jnp.where(lane < SSM_STATE, ar, -ai)
            x2 = jnp.where(lane < SSM_STATE, ai, ar)
            qt_ref[g, SSM_GROUP * t:SSM_GROUP * (t + 1), :] = (c2r * x1 - c2i * x2).astype(BF16)


def _s5_kernel(*refs):
    build_in, (u_ref, us_ref, h0_ref, dsk_ref) = refs[:8], refs[8:12]
    y_ref, hf_ref, ys_ref, hfs_ref = refs[12:16]
    ops = refs[16:20], refs[20:24]
    (dall_ref, yall_ref), (dalls_ref, yalls_ref) = refs[24:26], refs[26:28]
    k = pl.program_id(0)
    chunk_s = us_ref.shape[0] * POS_TILE

    def step(p):
        _s5_build_operators(*build_in, *ops[p])
        for u, dall, yall, h0, hf, y, chunk in ((u_ref, dall_ref, yall_ref, None, hf_ref, y_ref, PROMPT_CHUNK),
                                                (us_ref, dalls_ref, yalls_ref, h0_ref, hfs_ref, ys_ref, chunk_s)):
            _s5_regroup_in(u, dall, chunk)
            _s5_scan_groups(dall, *ops[1 - p], h0, hf, yall, chunk=chunk)
            _s5_regroup_out(yall, u, dsk_ref, y, chunk)

    pl.when(k == 0)(lambda: _s5_build_operators(*build_in, *ops[0]))
    pl.when((k > 0) & (k % 2 == 0))(lambda: step(0))
    pl.when(k % 2 == 1)(lambda: step(1))


def _s5_rows(ref, s):
    return s // POS_TILE, pl.ds(s % POS_TILE, ref.shape[1] // POS_TILE, stride=POS_TILE), slice(None)


def _s5_regroup_in(u_ref, dall_ref, chunk):
    for s in range(chunk):
        dall_ref[s] = _pad_rows(u_ref[_s5_rows(u_ref, s)], dall_ref.shape[2]).T.astype(BF16)


def _s5_regroup_out(yall_ref, u_ref, dsk_ref, y_ref, chunk):
    nc = u_ref.shape[1] // POS_TILE
    for t in range(chunk):
        at = _s5_rows(u_ref, t)
        y_ref[at] = yall_ref[t].T[:nc] + dsk_ref[...] * u_ref[at]


def _s5_scan_groups(dall_ref, wt_ref, qt_ref, kr_ref, ki_ref, h0_ref, hf_ref, yall_ref, *, chunk):
    carry = h0_ref is None
    ncp = dall_ref.shape[2]
    nseq = ncp // LANES if carry else h0_ref.shape[0]
    n = SSM_GROUP * chunk
    n_full = SSM_GROUP * PROMPT_CHUNK
    iota = lambda shape, dim: lax.broadcasted_iota(jnp.int32, shape, dim)
    lane = iota((SSM_STATE, ncp), 1) & (LANES - 1)
    slot = iota((LANES, LANES), 1)
    if not carry:
        h0_all = _pad_rows(jnp.concatenate([h0_ref[:, gp, :] for gp in range(GROUPS_PER_TILE)], axis=0), LANES).T
    acc = jnp.zeros((LANES, LANES), F32)
    for gp in range(GROUPS_PER_TILE):
        rows = slice(SSM_GROUP * gp, SSM_GROUP * (gp + 1))
        d = dall_ref[:, rows, :].reshape(n, ncp)
        if n == n_full:
            ys = jnp.dot(wt_ref[gp], d, preferred_element_type=F32)
            st = ys[n:]
            ys = ys[:n]
        else:
            ys = jnp.dot(wt_ref[gp, 0:n, 0:n], d, preferred_element_type=F32)
            st = jnp.dot(wt_ref[gp, n_full:, n_full - n:], d, preferred_element_type=F32)
        sr = st[:SSM_STATE]
        si = st[SSM_STATE:]
        if carry:
            for i in range(N_SCAN):
                sh = 1 << i
                ar = kr_ref[gp, :, ROW_DBL + i:ROW_DBL + i + 1]
                ai = ki_ref[gp, :, ROW_DBL + i:ROW_DBL + i + 1]
                pr = jnp.where(lane >= sh, pltpu.roll(sr, sh, 1), 0.0)
                pi = jnp.where(lane >= sh, pltpu.roll(si, sh, 1), 0.0)
                sr, si = sr + ar * pr - ai * pi, si + ar * pi + ai * pr
            hp = jnp.concatenate([jnp.where(lane >= 1, pltpu.roll(sr, 1, 1), 0.0),
                                  jnp.where(lane >= 1, pltpu.roll(si, 1, 1), 0.0)], axis=0)
            for b in range(nseq):
                seq = slice(LANES * b, LANES * (b + 1))
                end = jnp.concatenate([sr[:, seq], si[:, seq]], axis=0)
                acc = jnp.where(slot == 8 * gp + b, pltpu.roll(end, (8 * gp + b + 1) % LANES, 1), acc)
        else:
            hp = h0_all if gp == 0 else pltpu.roll(h0_all, LANES - 8 * gp, 1)
            hp = jnp.where(slot < nseq, hp, 0.0)
            hpr, hpi = hp[:SSM_STATE], hp[SSM_STATE:]
            ar = kr_ref[gp, :, chunk:chunk + 1]
            ai = ki_ref[gp, :, chunk:chunk + 1]
            end = jnp.concatenate([sr + ar * hpr - ai * hpi, si + ar * hpi + ai * hpr], axis=0)
            if gp:
                end = pltpu.roll(end, 8 * gp, 1)
            acc = jnp.where((slot >= 8 * gp) & (slot < 8 * gp + nseq), end, acc)
        y = ys + jnp.dot(qt_ref[gp, 0:n, :], hp.astype(BF16), preferred_element_type=F32)
        yall_ref[:, rows, :] = y.reshape(chunk, SSM_GROUP, ncp)
    hf_ref[...] = acc.T[:8 * GROUPS_PER_TILE].reshape(GROUPS_PER_TILE, 8, LANES)


def _s5_layer(tables, b_re, b_im, c_re, c_im, d_skip, u, n_long, us, h0):
    ns = h0.shape[0]
    chunk_s = us.shape[0] * POS_TILE
    n, n_s = SSM_GROUP * PROMPT_CHUNK, SSM_GROUP * chunk_s
    assert u.shape[0] * POS_TILE == PROMPT_CHUNK and u.shape[1] == n_long * LANES * POS_TILE and n_long <= 8
    assert us.shape[1] == ns * POS_TILE and ns == 8 and n % n_s == 0
    gt = GROUPS_PER_TILE
    t = SSM_GROUPS // gt
    cur = lambda k: jnp.minimum(k, t - 1)
    prv = lambda k: jnp.maximum(k - 1, 0)
    built = lambda *s: pl.BlockSpec((gt,) + s, lambda k: (cur(k),) + (0,) * len(s))
    lanes = lambda a: pl.BlockSpec(a.shape[:-1] + (LANES,), lambda k: (0,) * (a.ndim - 1) + (prv(k),))
    states = pl.BlockSpec((gt, 8, LANES), lambda k: (prv(k), 0, 0))
    dup = lambda a: jnp.concatenate([a, a], axis=-1)
    op_set = [pltpu.VMEM((gt, n + 2 * SSM_STATE, n), BF16), pltpu.VMEM((gt, n, 2 * SSM_STATE), BF16),
              pltpu.VMEM((gt, SSM_STATE, LANES), F32), pltpu.VMEM((gt, SSM_STATE, LANES), F32)]
    regroup = lambda c, width: [pltpu.VMEM((c, LANES, width), BF16), pltpu.VMEM((c, LANES, width), F32)]
    return pl.pallas_call(
        _s5_kernel,
        grid=(t + 1,),
        in_specs=[pl.BlockSpec((TABLE_ROWS, gt, LANES), lambda k: (0, cur(k), 0))] * 2 + [
                  built(SSM_STATE, SSM_GROUP), built(SSM_STATE, SSM_GROUP),
                  built(SSM_GROUP, SSM_STATE), built(SSM_GROUP, SSM_STATE),
                  built(SSM_GROUP, LANES), built(SSM_GROUP, LANES),
                  lanes(u), lanes(us), pl.BlockSpec((ns, gt, LANES), lambda k: (0, prv(k), 0)),
                  pl.BlockSpec((1, LANES), lambda k: (0, prv(k)))],
        out_specs=(lanes(u), states, lanes(us), states),
        out_shape=(jax.ShapeDtypeStruct(u.shape, F32), jax.ShapeDtypeStruct((SSM_GROUPS, 8, LANES), F32),
                   jax.ShapeDtypeStruct(us.shape, F32), jax.ShapeDtypeStruct((SSM_GROUPS, 8, LANES), F32)),
        scratch_shapes=op_set + op_set + regroup(PROMPT_CHUNK, n_long * LANES) + regroup(chunk_s, LANES),
        compiler_params=_cparams("arbitrary"),
        name="s5",
    )(*tables, b_re, b_im, c_re, c_im, dup(c_re), dup(c_im), u, us, h0, d_skip.reshape(1, -1))


def _ssm_out(y, x, gn_ref, wz_ref, wg_ref, bg_ref, wo_ref, gf_ref):
    z = jnp.dot(_rms_norm(x, gn_ref[...]).astype(BF16), wz_ref[...], preferred_element_type=F32)
    g = jax.nn.gelu(y, approximate=True)
    gate = jnp.dot(g.astype(BF16), wg_ref[...], preferred_element_type=F32) + bg_ref[...]
    yy = (g * jax.nn.sigmoid(gate)) * _silu(z)
    return _rms_norm(x + jnp.dot(yy.astype(BF16), wo_ref[...], preferred_element_type=F32), gf_ref[...])


def _glu_out_kernel(y_ref, x_ref, ys_ref, xs_ref, *rest):
    params, (o_ref, os_ref) = rest[:-2], rest[-2:]
    o_ref[...] = _ssm_out(y_ref[...], x_ref[...].reshape(y_ref.shape), *params).reshape(o_ref.shape)

    @pl.when(pl.program_id(0) == pl.num_programs(0) - 1)
    def _():
        os_ref[...] = _ssm_out(ys_ref[...], xs_ref[...], *params)


def _glu_out(y, x, ys, xs, g_norm, w_z, w_glu, b_glu, w_out, g_final, *, ct):
    m, w = x.shape
    vec = pl.BlockSpec((1, w), lambda i: (0, 0))
    mat = pl.BlockSpec((w, w), lambda i: (0, 0), pipeline_mode=pl.Buffered(1))
    small = pl.BlockSpec(xs.shape, lambda i: (0, 0))
    sg, nc = y.shape[0], y.shape[1] // POS_TILE
    x_spec = pl.BlockSpec((ct, POS_TILE, w), lambda i: (i // sg, i % sg, 0))
    out, out_s = pl.pallas_call(
        _glu_out_kernel,
        grid=((nc // ct) * sg,),
        in_specs=[pl.BlockSpec((None, ct * POS_TILE, w), lambda i: (i % sg, i // sg, 0)), x_spec, small, small,
                  vec, mat, mat, vec, mat, vec],
        out_specs=(x_spec, small),
        out_shape=(jax.ShapeDtypeStruct((nc, sg * POS_TILE, w), F32), jax.ShapeDtypeStruct(xs.shape, F32)),
        compiler_params=_cparams("arbitrary"),
        name="glu_out",
    )(y, x.reshape(nc, sg * POS_TILE, w), ys, xs, g_norm.reshape(1, w), w_z, w_glu, b_glu.reshape(1, w), w_out,
      g_final.reshape(1, w))
    return out.reshape(m, w), out_s


PAST_LEN = 16384


def kernel(x_prompt, x_sample, cache_win_k, cache_win_v, state_conv, state_ssm_re, state_ssm_im, attn_norm, w_in_ab, conv_w, w_out_ab, ssm_norm, w_in_c, lam_re, lam_im, log_step, b_re, b_im, c_re, c_im, d_skip, w_glu, b_glu, w_out_c, final_norm):
    bp, tp, _ = x_prompt.shape
    bs, ts, _ = x_sample.shape
    n_keep = min(2048, tp)
    xp = x_prompt.reshape(bp * tp, D_MODEL)
    xs = x_sample.reshape(bs * ts, D_MODEL)

    qkv_cols = 3 * ATTN_WIDTH
    gate_casts = [(w_in_ab[0], CONV_WIDTH, qkv_cols // CONV_WIDTH + c) for c in range(5)]
    layer1_casts = [(w_in_c[0], D_MODEL, 0), (w_in_c[0], D_MODEL, 1), (w_glu[0], D_MODEL, 0), (w_out_c[0], D_MODEL, 0)]

    cos_p, sin_p, w_qkv = _rope_tables(tp, tp, 0, casts=[(w_in_ab[0], qkv_cols, 0)])
    proj_p, *w_gates, w_out0, proj_s = _norm_proj(
        xp, attn_norm[0], w_qkv, tm=512, rope=(cos_p, sin_p), rope_cols=2 * ATTN_WIDTH, tiles_per_seq=tp // 512,
        casts=gate_casts + [(w_out_ab[0], D_MODEL, 0)], tail=xs, tail_rope=_rope_tables(bs * ts, ts, PAST_LEN))
    o_p, k_p, v_p, w_u, w_z, w_glu1, w_out1 = _attn_prompt(proj_p, bp, tp, n_keep, casts=layer1_casts)
    o_p = o_p.reshape(bp * tp, ATTN_WIDTH)
    k_p = k_p.reshape(1, bp, n_keep, N_HEADS, HEAD_DIM)
    v_p = v_p.reshape(1, bp, n_keep, N_HEADS, HEAD_DIM)
    o_s = _attn_sample(proj_s, cache_win_k[0], cache_win_v[0], bs, ts)
    h1_p, conv_p, h1_s, conv_s = _mix_out(xp, o_p, xs, o_s, state_conv[0], attn_norm[0], w_gates, conv_w[0], w_out0,
                                          tm=512, tiles_per_seq=tp // 512)
    k_s = proj_s[:, ATTN_WIDTH:2 * ATTN_WIDTH].reshape(1, bs, ts, N_HEADS, HEAD_DIM)
    v_s = proj_s[:, 2 * ATTN_WIDTH:3 * ATTN_WIDTH].reshape(1, bs, ts, N_HEADS, HEAD_DIM)

    assert ts == POS_TILE
    tables = _s5_tables(lam_re[0], lam_im[0], log_step[0])
    u_p, u_s = _norm_proj(h1_p, ssm_norm[0], w_u, chunk=PROMPT_CHUNK, ct=64, tail=h1_s)
    h0 = jnp.concatenate([state_ssm_re[0], state_ssm_im[0]], axis=-1)
    y_p, hf_p, y_s, hf_s = _s5_layer(tables, b_re[0], b_im[0], c_re[0], c_im[0], d_skip[0], u_p, bp, u_s[None], h0)
    out_p, out_s = _glu_out(y_p, h1_p, y_s[0], h1_s, ssm_norm[0], w_z, w_glu1, b_glu[0], w_out1, final_norm, ct=32)
    hf_p = hf_p[:, :bp].transpose(1, 0, 2)[None]
    hf_s = hf_s[:, :bs].transpose(1, 0, 2)[None]
    return (out_p.reshape(bp, tp, D_MODEL), out_s.reshape(bs, ts, D_MODEL),
            k_p, v_p, conv_p[None], hf_p[..., :SSM_STATE], hf_p[..., SSM_STATE:],
            k_s, v_s, conv_s[None], hf_s[..., :SSM_STATE], hf_s[..., SSM_STATE:])
```

```python
import functools
import math

import jax
import jax.numpy as jnp
from jax import lax
from jax.experimental import pallas as pl
from jax.experimental.pallas import tpu as pltpu

D_MODEL = 2048
HEAD_DIM = 128
N_HEADS = 8
ATTN_WIDTH = 1024
CONV_WIDTH = 1024
DILATIONS = (1, 4, 16)
N_BACK = 128
ROPE_THETA = 10000.0
RMS_EPS = 1e-6
SSM_GROUP = 16
SSM_GROUPS = 128
SSM_STATE = 64
LANES = 128
VMEM_LIMIT = 56 * 1024 * 1024

F32 = jnp.float32
BF16 = jnp.bfloat16


def _cparams(*sem):
    return pltpu.CompilerParams(dimension_semantics=sem, vmem_limit_bytes=VMEM_LIMIT)


def _rope_table_kernel(inv_ref, *refs, period, offset, n_cast):
    cos_ref, sin_ref = refs[n_cast:n_cast + 2]
    cr_ref, sr_ref = refs[-2:]
    _cast_blocks(refs[:n_cast], refs[n_cast + 2:-2])
    rows = cos_ref.shape[0]
    i = pl.program_id(0)

    @pl.when(i == 0)
    def _():
        ang = lax.broadcasted_iota(jnp.int32, (rows, LANES), 0).astype(F32) * inv_ref[...]
        cr_ref[...] = jnp.cos(ang)
        sr_ref[...] = jnp.sin(ang)

    base = (offset + lax.rem(i * rows, period)).astype(F32) * inv_ref[...]
    cb, sb = jnp.cos(base), jnp.sin(base)
    lane = lax.broadcasted_iota(jnp.int32, (rows, LANES), 1)
    cos_ref[...] = cb * cr_ref[...] - sb * sr_ref[...]
    sin_ref[...] = jnp.where(lane < HEAD_DIM // 2, -1.0, 1.0) * (sb * cr_ref[...] + cb * sr_ref[...])


def _rope_tables(rows, period, offset, casts=()):
    half = HEAD_DIM // 2
    inv = ROPE_THETA ** (-jnp.arange(half, dtype=F32) / half)
    inv2 = jnp.concatenate([inv, inv])[None, :]
    tr = math.gcd(math.gcd(rows, period), 512)
    table = pl.BlockSpec((tr, LANES), lambda i: (i, 0))
    cast_in, cast_out, cast_shapes = _cast_specs(casts, rows // tr, lambda i: i)
    return pl.pallas_call(
        functools.partial(_rope_table_kernel, period=period, offset=offset, n_cast=len(casts)),
        grid=(rows // tr,),
        in_specs=[pl.BlockSpec((1, LANES), lambda i: (0, 0))] + cast_in,
        out_specs=[table, table] + cast_out,
        out_shape=[jax.ShapeDtypeStruct((rows, LANES), F32)] * 2 + cast_shapes,
        scratch_shapes=[pltpu.VMEM((tr, LANES), F32)] * 2,
        compiler_params=_cparams("arbitrary"),
        name="rope_table",
    )(inv2, *[c[0] for c in casts])


def _rms_norm(x, g):
    return x * lax.rsqrt(jnp.mean(x * x, axis=-1, keepdims=True) + RMS_EPS) * g


POS_TILE = 8
COL_SLAB = 256


def _cast_blocks(cast_in, cast_out):
    for src, dst in zip(cast_in, cast_out):
        dst[...] = src[...].astype(BF16)


def _norm_proj_kernel(*refs, rope_cols, n_cast, has_tail):
    it = iter(refs)
    take = lambda n: [next(it) for _ in range(n)]
    x_ref, g_ref, w_ref = take(3)
    rope = take(2) if rope_cols else None
    cast_in = take(n_cast)
    tail_x = take(1)[0] if has_tail else None
    tail_rope = take(2) if has_tail and rope_cols else None
    o_ref = take(1)[0]
    cast_out = take(n_cast)
    tail_o = take(1)[0] if has_tail else None
    hn0, hn1 = take(2)
    k = pl.program_id(0)

    def normalise(dst):
        x = x_ref[...]
        dst[...] = _rms_norm(x.reshape(dst.shape), g_ref[...]).astype(BF16)
        _cast_blocks(cast_in, cast_out)

    def project(hn, out, tables):
        for c in range(w_ref.shape[1] // COL_SLAB):
            cols = slice(c * COL_SLAB, (c + 1) * COL_SLAB)
            acc = jnp.dot(hn, w_ref[:, cols], preferred_element_type=F32)
            if c * COL_SLAB < rope_cols:
                cos_ref, sin_ref = tables
                for h in range(COL_SLAB // HEAD_DIM):
                    xh = acc[:, h * HEAD_DIM:(h + 1) * HEAD_DIM]
                    lanes = slice(c * COL_SLAB + h * HEAD_DIM, c * COL_SLAB + (h + 1) * HEAD_DIM)
                    out[:, lanes] = xh * cos_ref[...] + pltpu.roll(xh, HEAD_DIM // 2, 1) * sin_ref[...]
            else:
                out[:, cols] = acc

    def step(cur, prev):
        normalise(cur)
        project(prev[...], o_ref, rope)

    pl.when(k == 0)(lambda: normalise(hn0))
    pl.when((k > 0) & (k % 2 == 0))(lambda: step(hn0, hn1))
    pl.when(k % 2 == 1)(lambda: step(hn1, hn0))
    if has_tail:
        @pl.when(k == pl.num_programs(0) - 1)
        def _():
            project(_rms_norm(tail_x[...], g_ref[...]).astype(BF16), tail_o, tail_rope)


def _cast_specs(casts, n_blocks, index):
    ins, outs, shapes = [], [], []
    for arr, width, col in casts:
        rows = arr.shape[0] // n_blocks
        ins.append(pl.BlockSpec((rows, width), lambda *k, col=col: (index(*k), col)))
        outs.append(pl.BlockSpec((rows, width), lambda *k: (index(*k), 0)))
        shapes.append(jax.ShapeDtypeStruct((arr.shape[0], width), BF16))
    return ins, outs, shapes


def _norm_proj(x, g, w, *, tm=None, chunk=None, ct=None, rope=None, rope_cols=0, tiles_per_seq=1, casts=(),
               tail=None, tail_rope=None):
    m, kd = x.shape
    n = w.shape[1]
    whole = lambda a: pl.BlockSpec(a.shape, lambda k: (0,) * len(a.shape))
    tail_in = [] if tail is None else [tail] + (list(tail_rope) if rope_cols else [])
    tail_shape = [] if tail is None else [jax.ShapeDtypeStruct((tail.shape[0], n), F32)]
    cur = lambda k: jnp.minimum(k, t - 1)
    prv = lambda k: jnp.maximum(k - 1, 0)
    if chunk is None:
        t, rows = m // tm, tm
        x_spec = pl.BlockSpec((tm, kd), lambda k: (cur(k), 0))
        out_spec = pl.BlockSpec((tm, n), lambda k: (prv(k), 0))
        out_shape = jax.ShapeDtypeStruct((m, n), F32)
    else:
        nc, sg = m // chunk, chunk // POS_TILE
        t, rows = (nc // ct) * sg, ct * POS_TILE
        x = x.reshape(nc, chunk, kd)
        x_spec = pl.BlockSpec((ct, POS_TILE, kd), lambda k: (cur(k) // sg, cur(k) % sg, 0))
        out_spec = pl.BlockSpec((None, rows, n), lambda k: (prv(k) % sg, prv(k) // sg, 0))
        out_shape = jax.ShapeDtypeStruct((sg, nc * POS_TILE, n), F32)
    table = pl.BlockSpec((rows, LANES), lambda k: (prv(k) % tiles_per_seq, 0))
    cast_in, cast_out, cast_shapes = _cast_specs(casts, t, cur)
    res = pl.pallas_call(
        functools.partial(_norm_proj_kernel, rope_cols=rope_cols, n_cast=len(casts), has_tail=tail is not None),
        grid=(t + 1,),
        in_specs=[x_spec, pl.BlockSpec((1, kd), lambda k: (0, 0)),
                  pl.BlockSpec((kd, n), lambda k: (0, 0), pipeline_mode=pl.Buffered(1))]
        + ([table, table] if rope_cols else []) + cast_in + [whole(a) for a in tail_in],
        out_specs=[out_spec] + cast_out + [whole(s) for s in tail_shape],
        out_shape=[out_shape] + cast_shapes + tail_shape,
        scratch_shapes=[pltpu.VMEM((rows, kd), BF16)] * 2,
        compiler_params=_cparams("arbitrary"),
        name="norm_proj",
    )(x, g.reshape(1, kd), w, *(rope if rope_cols else ()), *[c[0] for c in casts], *tail_in)
    return res if len(res) > 1 else res[0]


def _attn_prompt_kernel(q_ref, k_ref, v_ref, *refs, n_cast):
    cast_in, (o_ref, ks_ref, vs_ref) = refs[:n_cast], refs[n_cast:n_cast + 3]
    cast_out = refs[n_cast + 3:2 * n_cast + 3]
    qn_ref, kn_ref, vn_ref, q16_ref, k16_ref, v16_ref, st_ref, acc_ref, mm_ref, ll_ref = refs[2 * n_cast + 3:]
    _cast_blocks(cast_in, cast_out)
    t = q_ref.shape[1]
    n_keep = ks_ref.shape[1]
    ks_ref[0] = k_ref[0, t - n_keep:, :]
    vs_ref[0] = v_ref[0, t - n_keep:, :]
    l4, l16 = t // 4, t // 16
    nblk = t // N_BACK
    piece = N_BACK // 4
    scale = HEAD_DIM ** -0.5
    nt = (((1,), (1,)), ((), ()))
    pad = jnp.zeros((N_BACK, HEAD_DIM), BF16)
    for ref in (kn_ref, vn_ref, k16_ref, v16_ref):
        ref[0:N_BACK, :] = pad

    qn_ref[...] = q_ref[0].astype(BF16)
    kn_ref[N_BACK:, :] = k_ref[0].astype(BF16)
    vn_ref[N_BACK:, :] = v_ref[0].astype(BF16)
    for src, dst, off in ((q_ref, q16_ref, 0), (k_ref, k16_ref, N_BACK), (v_ref, v16_ref, N_BACK)):
        for r4 in range(4):
            st_ref[r4 * l4:(r4 + 1) * l4, :] = src[0, pl.ds(r4, l4, stride=4), :]
        for r4 in range(4):
            for a in range(4):
                r16 = r4 + 4 * a
                dst[off + r16 * l16:off + (r16 + 1) * l16, :] = (
                    st_ref[pl.ds(r4 * l4 + a, l16, stride=4), :].astype(BF16))

    iota = lambda dim: lax.broadcasted_iota(jnp.int32, (N_BACK, 2 * N_BACK), dim)
    qi, kj = iota(0), iota(1)
    dist = N_BACK + qi - kj
    band = (dist >= 0) & (dist <= N_BACK)
    has_prev = kj >= N_BACK
    dist4 = 4 * (qi % piece - kj % (2 * piece) + piece) + (qi // piece - kj // (2 * piece))
    band4 = (dist4 >= 0) & (dist4 <= N_BACK)
    has_prev4 = kj % (2 * piece) >= piece

    def scores(q, k2, valid):
        s = lax.dot_general(q, k2, nt, preferred_element_type=F32) * scale
        return jnp.where(valid, s, -jnp.inf)

    def fresh(s, v2):
        m = jnp.max(s, axis=-1, keepdims=True)
        p = jnp.exp(s - m)
        wide = lambda c: jnp.broadcast_to(c, (N_BACK, HEAD_DIM))
        return wide(m), wide(jnp.sum(p, axis=-1, keepdims=True)), jnp.dot(p.astype(BF16), v2,
                                                                         preferred_element_type=F32)

    def merged(s, v2, m_old, l_old, a_old):
        m_new = jnp.maximum(m_old, jnp.max(s, axis=-1, keepdims=True))
        alpha = jnp.exp(m_old - m_new)
        p = jnp.exp(s - jnp.concatenate([m_new, m_new], axis=1))
        l_new = alpha * l_old + jnp.sum(p, axis=-1, keepdims=True)
        return m_new, l_new, alpha * a_old + jnp.dot(p.astype(BF16), v2, preferred_element_type=F32)

    def block16(b, carry):
        row0 = pl.multiple_of(b * N_BACK, N_BACK)
        rows = pl.ds(row0, N_BACK)
        later = lax.rem(b, l16 // N_BACK) > 0
        s = scores(q16_ref[rows, :], k16_ref[pl.ds(row0, 2 * N_BACK), :], band & (has_prev | later))
        mm_ref[rows, :], ll_ref[rows, :], acc_ref[rows, :] = fresh(s, v16_ref[pl.ds(row0, 2 * N_BACK), :])
        return carry

    lax.fori_loop(0, nblk, block16, 0, unroll=32)

    n_j = l16 // piece

    def block4(b, carry):
        rho, j = b // n_j, lax.rem(b, n_j)
        base = pl.multiple_of(rho * l16 + j * piece, piece)
        q_rows = [pl.ds(base + a * 4 * l16, piece) for a in range(4)]
        k_rows = [pl.ds(base + a * 4 * l16 + N_BACK - piece, 2 * piece) for a in range(4)]
        gather = lambda ref, rows: jnp.concatenate([ref[r, :] for r in rows], axis=0)
        s = scores(gather(q16_ref, q_rows), gather(k16_ref, k_rows), band4 & (has_prev4 | (j > 0)))
        m, l, acc = merged(s, gather(v16_ref, k_rows), gather(mm_ref, q_rows), gather(ll_ref, q_rows),
                           gather(acc_ref, q_rows))
        for a, r in enumerate(q_rows):
            part = slice(a * piece, (a + 1) * piece)
            mm_ref[r, :], ll_ref[r, :], acc_ref[r, :] = m[part], l[part], acc[part]
        return carry

    lax.fori_loop(0, nblk, block4, 0, unroll=32)

    for src, dst in ((acc_ref, o_ref.at[0]), (mm_ref, acc_ref), (ll_ref, mm_ref)):
        for r4 in range(4):
            for a in range(4):
                r16 = r4 + 4 * a
                st_ref[pl.ds(r4 * l4 + a, l16, stride=4), :] = src[r16 * l16:(r16 + 1) * l16, :]
        for r4 in range(4):
            dst[pl.ds(r4, l4, stride=4), :] = st_ref[r4 * l4:(r4 + 1) * l4, :]

    def block1(b, carry):
        row0 = pl.multiple_of(b * N_BACK, N_BACK)
        rows = pl.ds(row0, N_BACK)
        s = scores(qn_ref[rows, :], kn_ref[pl.ds(row0, 2 * N_BACK), :], band & (has_prev | (b > 0)))
        _, l, acc = merged(s, vn_ref[pl.ds(row0, 2 * N_BACK), :], acc_ref[rows, :], mm_ref[rows, :],
                           o_ref[0, rows, :])
        o_ref[0, rows, :] = acc / l
        return carry

    lax.fori_loop(0, nblk, block1, 0, unroll=32)


def _attn_prompt(proj, b, t, n_keep, casts=()):
    assert t % (16 * N_BACK) == 0
    p3 = proj.reshape(b, t, proj.shape[1])
    blk = lambda off: pl.BlockSpec((1, t, HEAD_DIM), lambda i, h: (i, 0, off + h))
    keep = pl.BlockSpec((1, n_keep, HEAD_DIM), lambda i, h: (i, 0, h))
    cast_in, cast_out, cast_shapes = _cast_specs(casts, b * N_HEADS, lambda i, h: i * N_HEADS + h)
    return pl.pallas_call(
        functools.partial(_attn_prompt_kernel, n_cast=len(casts)),
        grid=(b, N_HEADS),
        in_specs=[blk(0), blk(N_HEADS), blk(2 * N_HEADS)] + cast_in,
        out_specs=[pl.BlockSpec((1, t, HEAD_DIM), lambda i, h: (i, 0, h)), keep, keep] + cast_out,
        out_shape=[jax.ShapeDtypeStruct((b, t, ATTN_WIDTH), F32),
                   jax.ShapeDtypeStruct((b, n_keep, ATTN_WIDTH), F32),
                   jax.ShapeDtypeStruct((b, n_keep, ATTN_WIDTH), F32)] + cast_shapes,
        scratch_shapes=[
            pltpu.VMEM((t, HEAD_DIM), BF16),
            pltpu.VMEM((t + N_BACK, HEAD_DIM), BF16),
            pltpu.VMEM((t + N_BACK, HEAD_DIM), BF16),
            pltpu.VMEM((t, HEAD_DIM), BF16),
            pltpu.VMEM((t + N_BACK, HEAD_DIM), BF16),
            pltpu.VMEM((t + N_BACK, HEAD_DIM), BF16),
            pltpu.VMEM((t, HEAD_DIM), F32),
            pltpu.VMEM((t, HEAD_DIM), F32),
            pltpu.VMEM((t, HEAD_DIM), F32),
            pltpu.VMEM((t, HEAD_DIM), F32),
        ],
        compiler_params=_cparams("parallel", "parallel"),
        name="attn_prompt",
    )(p3, p3, p3, *[c[0] for c in casts])


COARSE = DILATIONS[-1]
NEAR = N_BACK * DILATIONS[-2]


def _attn_sample_kernel(q_ref, kn_ref, vn_ref, kfar_ref, knear_ref, vfar_ref, vnear_ref, o_ref):
    s_len = q_ref.shape[0]
    far_groups, near_groups = kfar_ref.shape[0], knear_ref.shape[0]
    n_far, n_near = far_groups * s_len, near_groups * COARSE
    n_buf = (far_groups + near_groups) * COARSE
    scale = HEAD_DIM ** -0.5
    nt = (((1,), (1,)), ((), ()))

    def count(dist):
        c = jnp.zeros(dist.shape, F32)
        for d in DILATIONS:
            hit = (dist >= 0) & (dist <= N_BACK * d) & ((dist & (d - 1)) == 0)
            c = c + jnp.where(hit, 1.0, 0.0)
        return c

    iota = lambda shape, dim: lax.broadcasted_iota(jnp.int32, shape, dim)
    query = lambda width: n_buf + iota((s_len, width), 0)
    col = iota((s_len, n_far + n_near), 1)
    far_pos = COARSE * (col // s_len) + col % s_len
    near_pos = far_groups * COARSE + (col - n_far)
    cc = count(query(n_far + n_near) - jnp.where(col < n_far, far_pos, near_pos))
    cn = count(iota((s_len, s_len), 0) - iota((s_len, s_len), 1))

    def head_rows(far_ref, near_ref, h):
        far = far_ref[:, pl.ds(h, s_len, stride=N_HEADS), :].reshape(n_far, HEAD_DIM)
        near = near_ref[:, pl.ds(h, COARSE, stride=N_HEADS), :].reshape(n_near, HEAD_DIM)
        return jnp.concatenate([far, near], axis=0).astype(BF16)

    heads = range(N_HEADS)
    cols = [slice(h * HEAD_DIM, (h + 1) * HEAD_DIM) for h in heads]
    stack = lambda parts: jnp.concatenate(parts, axis=0)
    qs = [q_ref[:, c].astype(BF16) for c in cols]
    sc = stack([lax.dot_general(qs[h], head_rows(kfar_ref, knear_ref, h), nt, preferred_element_type=F32)
                for h in heads]) * scale
    sn = stack([lax.dot_general(qs[h], kn_ref[:, cols[h]].astype(BF16), nt, preferred_element_type=F32)
                for h in heads]) * scale
    cc, cn = stack([cc] * N_HEADS), stack([cn] * N_HEADS)
    sc = jnp.where(cc > 0, sc, -jnp.inf)
    sn = jnp.where(cn > 0, sn, -jnp.inf)
    m = jnp.maximum(jnp.max(sc, axis=-1, keepdims=True), jnp.max(sn, axis=-1, keepdims=True))
    pc = cc * jnp.exp(sc - m)
    pn = cn * jnp.exp(sn - m)
    l = jnp.sum(pc, axis=-1, keepdims=True) + jnp.sum(pn, axis=-1, keepdims=True)
    for h in heads:
        rows = slice(h * s_len, (h + 1) * s_len)
        o = (jnp.dot(pc[rows].astype(BF16), head_rows(vfar_ref, vnear_ref, h), preferred_element_type=F32)
             + jnp.dot(pn[rows].astype(BF16), vn_ref[:, cols[h]].astype(BF16), preferred_element_type=F32))
        o_ref[:, cols[h]] = o / l[rows]


def _attn_sample(proj, cache_k, cache_v, b, s_len):
    n_buf = cache_k.shape[1]
    near_groups = NEAR // COARSE
    far_groups = n_buf // COARSE - near_groups
    assert n_buf % COARSE == 0 and s_len <= COARSE and far_groups % near_groups == 0 and far_groups > 0
    grouped = lambda c: c.reshape(b, n_buf // COARSE, COARSE * N_HEADS, HEAD_DIM)
    new = lambda c: pl.BlockSpec((s_len, ATTN_WIDTH), lambda i: (i, c))
    far = pl.BlockSpec((None, far_groups, s_len * N_HEADS, HEAD_DIM), lambda i: (i, 0, 0, 0))
    near = pl.BlockSpec((None, near_groups, COARSE * N_HEADS, HEAD_DIM),
                        lambda i: (i, far_groups // near_groups, 0, 0))
    ck, cv = grouped(cache_k), grouped(cache_v)
    return pl.pallas_call(
        _attn_sample_kernel,
        grid=(b,),
        in_specs=[new(0), new(1), new(2), far, near, far, near],
        out_specs=new(0),
        out_shape=jax.ShapeDtypeStruct((b * s_len, ATTN_WIDTH), F32),
        compiler_params=_cparams("parallel"),
        name="attn_sample",
    )(proj, proj, proj, ck, ck, cv, cv)


def _silu(z):
    return z * jax.nn.sigmoid(z)


def _mix_out_kernel(x_ref, o_ref, g_ref, wza_ref, wgb_ref, wgc_ref, whi_ref, wzb_ref, cw_ref, w_ref, *refs,
                    tiles_per_seq):
    xs_ref, os_ref, p2_ref, p1_ref, h_ref, cs_ref, hs_ref, css_ref, tail_ref = refs

    def tile(x, o_attn, ln, prev2, prev1):
        tm = x.shape[0]
        hn = _rms_norm(x, g_ref[...]).astype(BF16)
        gate = lambda w: jnp.dot(hn, w[...], preferred_element_type=F32)
        ch = gate(wgc_ref) * gate(whi_ref)
        pos = lax.rem(lax.broadcasted_iota(jnp.int32, (tm, 1), 0), ln)
        ch1 = jnp.where(pos == 0, prev1, pltpu.roll(ch, 1, 0))
        ch2 = jnp.where(pos == 0, prev2, jnp.where(pos == 1, prev1, pltpu.roll(ch, 2, 0)))
        conv = ch2 * cw_ref[0:1, :] + ch1 * cw_ref[1:2, :] + ch * cw_ref[2:3, :]
        o_b = (gate(wgb_ref) * conv * _silu(gate(wzb_ref))).astype(BF16)
        y = jnp.dot(o_b, w_ref[ATTN_WIDTH:, :], preferred_element_type=F32)
        o_a = (o_attn * _silu(gate(wza_ref))).astype(BF16)
        y = y + jnp.dot(o_a, w_ref[0:ATTN_WIDTH, :], preferred_element_type=F32)
        return x + y, ch

    i = pl.program_id(0)
    tm = x_ref.shape[0]

    @pl.when(i % tiles_per_seq == 0)
    def _():
        tail_ref[...] = jnp.zeros(tail_ref.shape, F32)

    h, ch = tile(x_ref[...], o_ref[...], tm, tail_ref[0:1, :], tail_ref[1:2, :])
    h_ref[...] = h
    tail_ref[...] = ch[tm - 2:tm, :]
    cs_ref[0] = ch[tm - 2:tm, :]

    @pl.when(i == pl.num_programs(0) - 1)
    def _():
        rows = xs_ref.shape[0]
        k = p1_ref.shape[0]
        ln = rows // k
        per_row = lambda p: jnp.broadcast_to(p[...], (k, ln, CONV_WIDTH)).reshape(rows, CONV_WIDTH)
        hs, chs = tile(xs_ref[...], os_ref[...], ln, per_row(p2_ref), per_row(p1_ref))
        hs_ref[...] = hs
        css_ref[...] = chs.reshape(k, ln, CONV_WIDTH)[:, ln - 2:, :]


def _mix_out(x, o_attn, xs, os_attn, conv_init_s, g, w_gates, conv_w, w_out, *, tm, tiles_per_seq):
    m, ms = x.shape[0], xs.shape[0]
    nseq = m // (tm * tiles_per_seq)
    once = lambda a: pl.BlockSpec(a.shape, lambda i: (0,) * len(a.shape), pipeline_mode=pl.Buffered(1))
    row = lambda w: pl.BlockSpec((tm, w), lambda i: (i, 0))
    p2, p1 = conv_init_s[:, 0:1], conv_init_s[:, 1:2]
    out_shape = (jax.ShapeDtypeStruct((m, D_MODEL), F32), jax.ShapeDtypeStruct((nseq, 2, CONV_WIDTH), F32),
                 jax.ShapeDtypeStruct((ms, D_MODEL), F32), jax.ShapeDtypeStruct(conv_init_s.shape, F32))
    return pl.pallas_call(
        functools.partial(_mix_out_kernel, tiles_per_seq=tiles_per_seq),
        grid=(m // tm,),
        in_specs=[row(D_MODEL), row(ATTN_WIDTH), pl.BlockSpec((1, D_MODEL), lambda i: (0, 0))]
        + [once(w) for w in w_gates] + [pl.BlockSpec((3, CONV_WIDTH), lambda i: (0, 0)), once(w_out)]
        + [once(a) for a in (xs, os_attn, p2, p1)],
        out_specs=(row(D_MODEL), pl.BlockSpec((1, 2, CONV_WIDTH), lambda i: (i // tiles_per_seq, 0, 0)))
        + tuple(pl.BlockSpec(s.shape, lambda i, n=len(s.shape): (0,) * n) for s in out_shape[2:]),
        out_shape=out_shape,
        scratch_shapes=[pltpu.VMEM((2, CONV_WIDTH), F32)],
        compiler_params=_cparams("arbitrary"),
        name="mix_out",
    )(x, o_attn, g.reshape(1, -1), *w_gates, conv_w, w_out, xs, os_attn, p2, p1)


N_SCAN = 7
GROUPS_PER_TILE = LANES // SSM_GROUP


PROMPT_CHUNK = 32
ROW_C = PROMPT_CHUNK + 1
ROW_DBL = PROMPT_CHUNK + 2
TABLE_ROWS = ROW_DBL + N_SCAN


def _s5_disc_kernel(lr_ref, li_ref, ls_ref, tr_ref, ti_ref):
    lr = lr_ref[...]
    li = li_ref[...]
    step = jnp.exp(ls_ref[...])
    mag = jnp.exp(lr * step)
    ar = mag * jnp.cos(li * step)
    ai = mag * jnp.sin(li * step)
    den = lr * lr + li * li
    nr = ar - 1.0
    tr_ref[ROW_C] = (nr * lr + ai * li) / den
    ti_ref[ROW_C] = (ai * lr - nr * li) / den
    pr = jnp.ones_like(ar)
    pi = jnp.zeros_like(ar)
    for tau in range(PROMPT_CHUNK + 1):
        tr_ref[tau] = pr
        ti_ref[tau] = pi
        dr, di = pr, pi
        pr, pi = pr * ar - pi * ai, pr * ai + pi * ar
    for i in range(N_SCAN):
        tr_ref[ROW_DBL + i] = dr
        ti_ref[ROW_DBL + i] = di
        dr, di = dr * dr - di * di, 2.0 * dr * di


def _s5_tables(lam_re, lam_im, log_step):
    g = lam_re.shape[0]
    dup = lambda a: jnp.concatenate([a, a], axis=-1)
    tr, ti = pl.pallas_call(
        _s5_disc_kernel,
        out_shape=(jax.ShapeDtypeStruct((TABLE_ROWS, g, LANES), F32),) * 2,
        name="s5_disc",
    )(dup(lam_re), dup(lam_im), log_step[:, None])
    return tr, ti


def _split_bf16(x):
    hi = x.astype(BF16)
    return hi, (x - hi.astype(F32)).astype(BF16)


def _dot_split(a, b, b_is_bf16_exact=False):
    dot = lambda x, y: jnp.dot(x, y, preferred_element_type=F32)
    ah, al = _split_bf16(a)
    if b_is_bf16_exact:
        bh = b.astype(BF16)
        return dot(ah, bh) + dot(al, bh)
    bh, bl = _split_bf16(b)
    return dot(ah, bh) + (dot(ah, bl) + dot(al, bh))


def _pad_rows(x, rows):
    if x.shape[0] == rows:
        return x
    return jnp.concatenate([x, jnp.zeros((rows - x.shape[0], x.shape[1]), x.dtype)], axis=0)


def _s5_build_operators(tr_ref, ti_ref, br_ref, bi_ref, cr_ref, ci_ref, c2r_ref, c2i_ref,
                        wt_ref, qt_ref, kr_ref, ki_ref):
    chunk = PROMPT_CHUNK
    n = SSM_GROUP * chunk
    iota = lambda shape, dim: lax.broadcasted_iota(jnp.int32, shape, dim)
    lane = iota((SSM_GROUP, 2 * SSM_STATE), 1)
    e_sel = jnp.where(iota((chunk, n), 0) == chunk - 1 - iota((chunk, n), 1) // SSM_GROUP, 1.0, 0.0)
    tile = jnp.where(iota((SSM_GROUP, n), 0) == iota((SSM_GROUP, n), 1) % SSM_GROUP, 1.0, 0.0)

    for g in range(GROUPS_PER_TILE):
        tab_r, tab_i = tr_ref[:, g, :], ti_ref[:, g, :]
        col_r = _pad_rows(tab_r, LANES).T[:SSM_STATE]
        col_i = _pad_rows(tab_i, LANES).T[:SSM_STATE]
        kr_ref[g] = col_r
        ki_ref[g] = col_i
        ccr, cci = col_r[:, ROW_C:ROW_C + 1], col_i[:, ROW_C:ROW_C + 1]
        br, bi = br_ref[g], bi_ref[g]
        bbr = ccr * br - cci * bi
        bbi = ccr * bi + cci * br
        aer = _dot_split(col_r[:, :chunk], e_sel, True)
        aei = _dot_split(col_i[:, :chunk], e_sel, True)
        btr = _dot_split(bbr, tile, True)
        bti = _dot_split(bbi, tile, True)
        ptr = aer * btr - aei * bti
        pti = aer * bti + aei * btr
        wt_ref[g, n:n + SSM_STATE, :] = ptr.astype(BF16)
        wt_ref[g, n + SSM_STATE:n + 2 * SSM_STATE, :] = pti.astype(BF16)
        taps = _dot_split(cr_ref[g], ptr) - _dot_split(ci_ref[g], pti)
        padded = jnp.concatenate([taps, jnp.zeros_like(taps)], axis=1)
        for t in range(chunk):
            sh = SSM_GROUP * (chunk - 1 - t)
            blk = padded if sh == 0 else pltpu.roll(padded, 2 * n - sh, 1)
            wt_ref[g, SSM_GROUP * t:SSM_GROUP * (t + 1), :] = blk[:, :n].astype(BF16)
        c2r, c2i = c2r_ref[g], c2i_ref[g]
        for t in range(chunk):
            ar = tab_r[t + 1:t + 2, :]
            ai = tab_i[t + 1:t + 2, :]
            x1 = jnp.where(lane < SSM_STATE, ar, -ai)
            x2 = jnp.where(lane < SSM_STATE, ai, ar)
            qt_ref[g, SSM_GROUP * t:SSM_GROUP * (t + 1), :] = (c2r * x1 - c2i * x2).astype(BF16)


def _s5_kernel(*refs):
    build_in, (u_ref, us_ref, h0_ref, dsk_ref) = refs[:8], refs[8:12]
    y_ref, hf_ref, ys_ref, hfs_ref = refs[12:16]
    ops = refs[16:20], refs[20:24]
    (dall_ref, yall_ref), (dalls_ref, yalls_ref) = refs[24:26], refs[26:28]
    k = pl.program_id(0)
    chunk_s = us_ref.shape[0] * POS_TILE

    def step(p):
        _s5_build_operators(*build_in, *ops[p])
        for u, dall, yall, h0, hf, y, chunk in ((u_ref, dall_ref, yall_ref, None, hf_ref, y_ref, PROMPT_CHUNK),
                                                (us_ref, dalls_ref, yalls_ref, h0_ref, hfs_ref, ys_ref, chunk_s)):
            _s5_regroup_in(u, dall, chunk)
            _s5_scan_groups(dall, *ops[1 - p], h0, hf, yall, chunk=chunk)
            _s5_regroup_out(yall, u, dsk_ref, y, chunk)

    pl.when(k == 0)(lambda: _s5_build_operators(*build_in, *ops[0]))
    pl.when((k > 0) & (k % 2 == 0))(lambda: step(0))
    pl.when(k % 2 == 1)(lambda: step(1))


def _s5_rows(ref, s):
    return s // POS_TILE, pl.ds(s % POS_TILE, ref.shape[1] // POS_TILE, stride=POS_TILE), slice(None)


def _s5_regroup_in(u_ref, dall_ref, chunk):
    for s in range(chunk):
        dall_ref[s] = _pad_rows(u_ref[_s5_rows(u_ref, s)], dall_ref.shape[2]).T.astype(BF16)


def _s5_regroup_out(yall_ref, u_ref, dsk_ref, y_ref, chunk):
    nc = u_ref.shape[1] // POS_TILE
    for t in range(chunk):
        at = _s5_rows(u_ref, t)
        y_ref[at] = yall_ref[t].T[:nc] + dsk_ref[...] * u_ref[at]


def _s5_scan_groups(dall_ref, wt_ref, qt_ref, kr_ref, ki_ref, h0_ref, hf_ref, yall_ref, *, chunk):
    carry = h0_ref is None
    ncp = dall_ref.shape[2]
    nseq = ncp // LANES if carry else h0_ref.shape[0]
    n = SSM_GROUP * chunk
    n_full = SSM_GROUP * PROMPT_CHUNK
    iota = lambda shape, dim: lax.broadcasted_iota(jnp.int32, shape, dim)
    lane = iota((SSM_STATE, ncp), 1) & (LANES - 1)
    slot = iota((LANES, LANES), 1)
    if not carry:
        h0_all = _pad_rows(jnp.concatenate([h0_ref[:, gp, :] for gp in range(GROUPS_PER_TILE)], axis=0), LANES).T
    acc = jnp.zeros((LANES, LANES), F32)
    for gp in range(GROUPS_PER_TILE):
        rows = slice(SSM_GROUP * gp, SSM_GROUP * (gp + 1))
        d = dall_ref[:, rows, :].reshape(n, ncp)
        if n == n_full:
            ys = jnp.dot(wt_ref[gp], d, preferred_element_type=F32)
            st = ys[n:]
            ys = ys[:n]
        else:
            ys = jnp.dot(wt_ref[gp, 0:n, 0:n], d, preferred_element_type=F32)
            st = jnp.dot(wt_ref[gp, n_full:, n_full - n:], d, preferred_element_type=F32)
        sr = st[:SSM_STATE]
        si = st[SSM_STATE:]
        if carry:
            for i in range(N_SCAN):
                sh = 1 << i
                ar = kr_ref[gp, :, ROW_DBL + i:ROW_DBL + i + 1]
                ai = ki_ref[gp, :, ROW_DBL + i:ROW_DBL + i + 1]
                pr = jnp.where(lane >= sh, pltpu.roll(sr, sh, 1), 0.0)
                pi = jnp.where(lane >= sh, pltpu.roll(si, sh, 1), 0.0)
                sr, si = sr + ar * pr - ai * pi, si + ar * pi + ai * pr
            hp = jnp.concatenate([jnp.where(lane >= 1, pltpu.roll(sr, 1, 1), 0.0),
                                  jnp.where(lane >= 1, pltpu.roll(si, 1, 1), 0.0)], axis=0)
            for b in range(nseq):
                seq = slice(LANES * b, LANES * (b + 1))
                end = jnp.concatenate([sr[:, seq], si[:, seq]], axis=0)
                acc = jnp.where(slot == 8 * gp + b, pltpu.roll(end, (8 * gp + b + 1) % LANES, 1), acc)
        else:
            hp = h0_all if gp == 0 else pltpu.roll(h0_all, LANES - 8 * gp, 1)
            hp = jnp.where(slot < nseq, hp, 0.0)
            hpr, hpi = hp[:SSM_STATE], hp[SSM_STATE:]
            ar = kr_ref[gp, :, chunk:chunk + 1]
            ai = ki_ref[gp, :, chunk:chunk + 1]
            end = jnp.concatenate([sr + ar * hpr - ai * hpi, si + ar * hpi + ai * hpr], axis=0)
            if gp:
                end = pltpu.roll(end, 8 * gp, 1)
            acc = jnp.where((slot >= 8 * gp) & (slot < 8 * gp + nseq), end, acc)
        y = ys + jnp.dot(qt_ref[gp, 0:n, :], hp.astype(BF16), preferred_element_type=F32)
        yall_ref[:, rows, :] = y.reshape(chunk, SSM_GROUP, ncp)
    hf_ref[...] = acc.T[:8 * GROUPS_PER_TILE].reshape(GROUPS_PER_TILE, 8, LANES)


def _s5_layer(tables, b_re, b_im, c_re, c_im, d_skip, u, n_long, us, h0):
    ns = h0.shape[0]
    chunk_s = us.shape[0] * POS_TILE
    n, n_s = SSM_GROUP * PROMPT_CHUNK, SSM_GROUP * chunk_s
    assert u.shape[0] * POS_TILE == PROMPT_CHUNK and u.shape[1] == n_long * LANES * POS_TILE and n_long <= 8
    assert us.shape[1] == ns * POS_TILE and ns == 8 and n % n_s == 0
    gt = GROUPS_PER_TILE
    t = SSM_GROUPS // gt
    cur = lambda k: jnp.minimum(k, t - 1)
    prv = lambda k: jnp.maximum(k - 1, 0)
    built = lambda *s: pl.BlockSpec((gt,) + s, lambda k: (cur(k),) + (0,) * len(s))
    lanes = lambda a: pl.BlockSpec(a.shape[:-1] + (LANES,), lambda k: (0,) * (a.ndim - 1) + (prv(k),))
    states = pl.BlockSpec((gt, 8, LANES), lambda k: (prv(k), 0, 0))
    dup = lambda a: jnp.concatenate([a, a], axis=-1)
    op_set = [pltpu.VMEM((gt, n + 2 * SSM_STATE, n), BF16), pltpu.VMEM((gt, n, 2 * SSM_STATE), BF16),
              pltpu.VMEM((gt, SSM_STATE, LANES), F32), pltpu.VMEM((gt, SSM_STATE, LANES), F32)]
    regroup = lambda c, width: [pltpu.VMEM((c, LANES, width), BF16), pltpu.VMEM((c, LANES, width), F32)]
    return pl.pallas_call(
        _s5_kernel,
        grid=(t + 1,),
        in_specs=[pl.BlockSpec((TABLE_ROWS, gt, LANES), lambda k: (0, cur(k), 0))] * 2 + [
                  built(SSM_STATE, SSM_GROUP), built(SSM_STATE, SSM_GROUP),
                  built(SSM_GROUP, SSM_STATE), built(SSM_GROUP, SSM_STATE),
                  built(SSM_GROUP, LANES), built(SSM_GROUP, LANES),
                  lanes(u), lanes(us), pl.BlockSpec((ns, gt, LANES), lambda k: (0, prv(k), 0)),
                  pl.BlockSpec((1, LANES), lambda k: (0, prv(k)))],
        out_specs=(lanes(u), states, lanes(us), states),
        out_shape=(jax.ShapeDtypeStruct(u.shape, F32), jax.ShapeDtypeStruct((SSM_GROUPS, 8, LANES), F32),
                   jax.ShapeDtypeStruct(us.shape, F32), jax.ShapeDtypeStruct((SSM_GROUPS, 8, LANES), F32)),
        scratch_shapes=op_set + op_set + regroup(PROMPT_CHUNK, n_long * LANES) + regroup(chunk_s, LANES),
        compiler_params=_cparams("arbitrary"),
        name="s5",
    )(*tables, b_re, b_im, c_re, c_im, dup(c_re), dup(c_im), u, us, h0, d_skip.reshape(1, -1))


def _ssm_out(y, x, gn_ref, wz_ref, wg_ref, bg_ref, wo_ref, gf_ref):
    z = jnp.dot(_rms_norm(x, gn_ref[...]).astype(BF16), wz_ref[...], preferred_element_type=F32)
    g = jax.nn.gelu(y, approximate=True)
    gate = jnp.dot(g.astype(BF16), wg_ref[...], preferred_element_type=F32) + bg_ref[...]
    yy = (g * jax.nn.sigmoid(gate)) * _silu(z)
    return _rms_norm(x + jnp.dot(yy.astype(BF16), wo_ref[...], preferred_element_type=F32), gf_ref[...])


def _glu_out_kernel(y_ref, x_ref, ys_ref, xs_ref, *rest):
    params, (o_ref, os_ref) = rest[:-2], rest[-2:]
    o_ref[...] = _ssm_out(y_ref[...], x_ref[...].reshape(y_ref.shape), *params).reshape(o_ref.shape)

    @pl.when(pl.program_id(0) == pl.num_programs(0) - 1)
    def _():
        os_ref[...] = _ssm_out(ys_ref[...], xs_ref[...], *params)


def _glu_out(y, x, ys, xs, g_norm, w_z, w_glu, b_glu, w_out, g_final, *, ct):
    m, w = x.shape
    vec = pl.BlockSpec((1, w), lambda i: (0, 0))
    mat = pl.BlockSpec((w, w), lambda i: (0, 0), pipeline_mode=pl.Buffered(1))
    small = pl.BlockSpec(xs.shape, lambda i: (0, 0))
    sg, nc = y.shape[0], y.shape[1] // POS_TILE
    x_spec = pl.BlockSpec((ct, POS_TILE, w), lambda i: (i // sg, i % sg, 0))
    out, out_s = pl.pallas_call(
        _glu_out_kernel,
        grid=((nc // ct) * sg,),
        in_specs=[pl.BlockSpec((None, ct * POS_TILE, w), lambda i: (i % sg, i // sg, 0)), x_spec, small, small,
                  vec, mat, mat, vec, mat, vec],
        out_specs=(x_spec, small),
        out_shape=(jax.ShapeDtypeStruct((nc, sg * POS_TILE, w), F32), jax.ShapeDtypeStruct(xs.shape, F32)),
        compiler_params=_cparams("arbitrary"),
        name="glu_out",
    )(y, x.reshape(nc, sg * POS_TILE, w), ys, xs, g_norm.reshape(1, w), w_z, w_glu, b_glu.reshape(1, w), w_out,
      g_final.reshape(1, w))
    return out.reshape(m, w), out_s


PAST_LEN = 16384


def kernel(x_prompt, x_sample, cache_win_k, cache_win_v, state_conv, state_ssm_re, state_ssm_im, attn_norm, w_in_ab, conv_w, w_out_ab, ssm_norm, w_in_c, lam_re, lam_im, log_step, b_re, b_im, c_re, c_im, d_skip, w_glu, b_glu, w_out_c, final_norm):
    bp, tp, _ = x_prompt.shape
    bs, ts, _ = x_sample.shape
    n_keep = min(2048, tp)
    xp = x_prompt.reshape(bp * tp, D_MODEL)
    xs = x_sample.reshape(bs * ts, D_MODEL)

    qkv_cols = 3 * ATTN_WIDTH
    gate_casts = [(w_in_ab[0], CONV_WIDTH, qkv_cols // CONV_WIDTH + c) for c in range(5)]
    layer1_casts = [(w_in_c[0], D_MODEL, 0), (w_in_c[0], D_MODEL, 1), (w_glu[0], D_MODEL, 0), (w_out_c[0], D_MODEL, 0)]

    cos_p, sin_p, w_qkv = _rope_tables(tp, tp, 0, casts=[(w_in_ab[0], qkv_cols, 0)])
    proj_p, *w_gates, w_out0, proj_s = _norm_proj(
        xp, attn_norm[0], w_qkv, tm=512, rope=(cos_p, sin_p), rope_cols=2 * ATTN_WIDTH, tiles_per_seq=tp // 512,
        casts=gate_casts + [(w_out_ab[0], D_MODEL, 0)], tail=xs, tail_rope=_rope_tables(bs * ts, ts, PAST_LEN))
    o_p, k_p, v_p, w_u, w_z, w_glu1, w_out1 = _attn_prompt(proj_p, bp, tp, n_keep, casts=layer1_casts)
    o_p = o_p.reshape(bp * tp, ATTN_WIDTH)
    k_p = k_p.reshape(1, bp, n_keep, N_HEADS, HEAD_DIM)
    v_p = v_p.reshape(1, bp, n_keep, N_HEADS, HEAD_DIM)
    o_s = _attn_sample(proj_s, cache_win_k[0], cache_win_v[0], bs, ts)
    h1_p, conv_p, h1_s, conv_s = _mix_out(xp, o_p, xs, o_s, state_conv[0], attn_norm[0], w_gates, conv_w[0], w_out0,
                                          tm=512, tiles_per_seq=tp // 512)
    k_s = proj_s[:, ATTN_WIDTH:2 * ATTN_WIDTH].reshape(1, bs, ts, N_HEADS, HEAD_DIM)
    v_s = proj_s[:, 2 * ATTN_WIDTH:3 * ATTN_WIDTH].reshape(1, bs, ts, N_HEADS, HEAD_DIM)

    assert ts == POS_TILE
    tables = _s5_tables(lam_re[0], lam_im[0], log_step[0])
    u_p, u_s = _norm_proj(h1_p, ssm_norm[0], w_u, chunk=PROMPT_CHUNK, ct=64, tail=h1_s)
    h0 = jnp.concatenate([state_ssm_re[0], state_ssm_im[0]], axis=-1)
    y_p, hf_p, y_s, hf_s = _s5_layer(tables, b_re[0], b_im[0], c_re[0], c_im[0], d_skip[0], u_p, bp, u_s[None], h0)
    out_p, out_s = _glu_out(y_p, h1_p, y_s[0], h1_s, ssm_norm[0], w_z, w_glu1, b_glu[0], w_out1, final_norm, ct=32)
    hf_p = hf_p[:, :bp].transpose(1, 0, 2)[None]
    hf_s = hf_s[:, :bs].transpose(1, 0, 2)[None]
    return (out_p.reshape(bp, tp, D_MODEL), out_s.reshape(bs, ts, D_MODEL),
            k_p, v_p, conv_p[None], hf_p[..., :SSM_STATE], hf_p[..., SSM_STATE:],
            k_s, v_s, conv_s[None], hf_s[..., :SSM_STATE], hf_s[..., SSM_STATE:])
```

```python
import functools
import math

import jax
import jax.numpy as jnp
from jax import lax
from jax.experimental import pallas as pl
from jax.experimental.pallas import tpu as pltpu

D_MODEL = 2048
HEAD_DIM = 128
N_HEADS = 8
ATTN_WIDTH = 1024
CONV_WIDTH = 1024
DILATIONS = (1, 4, 16)
N_BACK = 128
ROPE_THETA = 10000.0
RMS_EPS = 1e-6
SSM_GROUP = 16
SSM_GROUPS = 128
SSM_STATE = 64
LANES = 128
VMEM_LIMIT = 56 * 1024 * 1024

F32 = jnp.float32
BF16 = jnp.bfloat16


def _cparams(*sem):
    return pltpu.CompilerParams(dimension_semantics=sem, vmem_limit_bytes=VMEM_LIMIT)


def _rope_table_kernel(inv_ref, *refs, period, offset, n_cast):
    cos_ref, sin_ref = refs[n_cast:n_cast + 2]
    cr_ref, sr_ref = refs[-2:]
    _cast_blocks(refs[:n_cast], refs[n_cast + 2:-2])
    rows = cos_ref.shape[0]
    i = pl.program_id(0)

    @pl.when(i == 0)
    def _():
        ang = lax.broadcasted_iota(jnp.int32, (rows, LANES), 0).astype(F32) * inv_ref[...]
        cr_ref[...] = jnp.cos(ang)
        sr_ref[...] = jnp.sin(ang)

    base = (offset + lax.rem(i * rows, period)).astype(F32) * inv_ref[...]
    cb, sb = jnp.cos(base), jnp.sin(base)
    lane = lax.broadcasted_iota(jnp.int32, (rows, LANES), 1)
    cos_ref[...] = cb * cr_ref[...] - sb * sr_ref[...]
    sin_ref[...] = jnp.where(lane < HEAD_DIM // 2, -1.0, 1.0) * (sb * cr_ref[...] + cb * sr_ref[...])


def _rope_tables(rows, period, offset, casts=()):
    half = HEAD_DIM // 2
    inv = ROPE_THETA ** (-jnp.arange(half, dtype=F32) / half)
    inv2 = jnp.concatenate([inv, inv])[None, :]
    tr = math.gcd(math.gcd(rows, period), 512)
    table = pl.BlockSpec((tr, LANES), lambda i: (i, 0))
    cast_in, cast_out, cast_shapes = _cast_specs(casts, rows // tr, lambda i: i)
    return pl.pallas_call(
        functools.partial(_rope_table_kernel, period=period, offset=offset, n_cast=len(casts)),
        grid=(rows // tr,),
        in_specs=[pl.BlockSpec((1, LANES), lambda i: (0, 0))] + cast_in,
        out_specs=[table, table] + cast_out,
        out_shape=[jax.ShapeDtypeStruct((rows, LANES), F32)] * 2 + cast_shapes,
        scratch_shapes=[pltpu.VMEM((tr, LANES), F32)] * 2,
        compiler_params=_cparams("arbitrary"),
        name="rope_table",
    )(inv2, *[c[0] for c in casts])


def _rms_norm(x, g):
    return x * lax.rsqrt(jnp.mean(x * x, axis=-1, keepdims=True) + RMS_EPS) * g


POS_TILE = 8
COL_SLAB = 256


def _cast_blocks(cast_in, cast_out):
    for src, dst in zip(cast_in, cast_out):
        dst[...] = src[...].astype(BF16)


def _norm_proj_kernel(*refs, rope_cols, n_cast, has_tail):
    it = iter(refs)
    take = lambda n: [next(it) for _ in range(n)]
    x_ref, g_ref, w_ref = take(3)
    rope = take(2) if rope_cols else None
    cast_in = take(n_cast)
    tail_x = take(1)[0] if has_tail else None
    tail_rope = take(2) if has_tail and rope_cols else None
    o_ref = take(1)[0]
    cast_out = take(n_cast)
    tail_o = take(1)[0] if has_tail else None
    hn0, hn1 = take(2)
    k = pl.program_id(0)

    def normalise(dst):
        x = x_ref[...]
        dst[...] = _rms_norm(x.reshape(dst.shape), g_ref[...]).astype(BF16)
        _cast_blocks(cast_in, cast_out)

    def project(hn, out, tables):
        for c in range(w_ref.shape[1] // COL_SLAB):
            cols = slice(c * COL_SLAB, (c + 1) * COL_SLAB)
            acc = jnp.dot(hn, w_ref[:, cols], preferred_element_type=F32)
            if c * COL_SLAB < rope_cols:
                cos_ref, sin_ref = tables
                for h in range(COL_SLAB // HEAD_DIM):
                    xh = acc[:, h * HEAD_DIM:(h + 1) * HEAD_DIM]
                    lanes = slice(c * COL_SLAB + h * HEAD_DIM, c * COL_SLAB + (h + 1) * HEAD_DIM)
                    out[:, lanes] = xh * cos_ref[...] + pltpu.roll(xh, HEAD_DIM // 2, 1) * sin_ref[...]
            else:
                out[:, cols] = acc

    def step(cur, prev):
        normalise(cur)
        project(prev[...], o_ref, rope)

    pl.when(k == 0)(lambda: normalise(hn0))
    pl.when((k > 0) & (k % 2 == 0))(lambda: step(hn0, hn1))
    pl.when(k % 2 == 1)(lambda: step(hn1, hn0))
    if has_tail:
        @pl.when(k == pl.num_programs(0) - 1)
        def _():
            project(_rms_norm(tail_x[...], g_ref[...]).astype(BF16), tail_o, tail_rope)


def _cast_specs(casts, n_blocks, index):
    ins, outs, shapes = [], [], []
    for arr, width, col in casts:
        rows = arr.shape[0] // n_blocks
        ins.append(pl.BlockSpec((rows, width), lambda *k, col=col: (index(*k), col)))
        outs.append(pl.BlockSpec((rows, width), lambda *k: (index(*k), 0)))
        shapes.append(jax.ShapeDtypeStruct((arr.shape[0], width), BF16))
    return ins, outs, shapes


def _norm_proj(x, g, w, *, tm=None, chunk=None, ct=None, rope=None, rope_cols=0, tiles_per_seq=1, casts=(),
               tail=None, tail_rope=None):
    m, kd = x.shape
    n = w.shape[1]
    whole = lambda a: pl.BlockSpec(a.shape, lambda k: (0,) * len(a.shape))
    tail_in = [] if tail is None else [tail] + (list(tail_rope) if rope_cols else [])
    tail_shape = [] if tail is None else [jax.ShapeDtypeStruct((tail.shape[0], n), F32)]
    cur = lambda k: jnp.minimum(k, t - 1)
    prv = lambda k: jnp.maximum(k - 1, 0)
    if chunk is None:
        t, rows = m // tm, tm
        x_spec = pl.BlockSpec((tm, kd), lambda k: (cur(k), 0))
        out_spec = pl.BlockSpec((tm, n), lambda k: (prv(k), 0))
        out_shape = jax.ShapeDtypeStruct((m, n), F32)
    else:
        nc, sg = m // chunk, chunk // POS_TILE
        t, rows = (nc // ct) * sg, ct * POS_TILE
        x = x.reshape(nc, chunk, kd)
        x_spec = pl.BlockSpec((ct, POS_TILE, kd), lambda k: (cur(k) // sg, cur(k) % sg, 0))
        out_spec = pl.BlockSpec((None, rows, n), lambda k: (prv(k) % sg, prv(k) // sg, 0))
        out_shape = jax.ShapeDtypeStruct((sg, nc * POS_TILE, n), F32)
    table = pl.BlockSpec((rows, LANES), lambda k: (prv(k) % tiles_per_seq, 0))
    cast_in, cast_out, cast_shapes = _cast_specs(casts, t, cur)
    res = pl.pallas_call(
        functools.partial(_norm_proj_kernel, rope_cols=rope_cols, n_cast=len(casts), has_tail=tail is not None),
        grid=(t + 1,),
        in_specs=[x_spec, pl.BlockSpec((1, kd), lambda k: (0, 0)),
                  pl.BlockSpec((kd, n), lambda k: (0, 0), pipeline_mode=pl.Buffered(1))]
        + ([table, table] if rope_cols else []) + cast_in + [whole(a) for a in tail_in],
        out_specs=[out_spec] + cast_out + [whole(s) for s in tail_shape],
        out_shape=[out_shape] + cast_shapes + tail_shape,
        scratch_shapes=[pltpu.VMEM((rows, kd), BF16)] * 2,
        compiler_params=_cparams("arbitrary"),
        name="norm_proj",
    )(x, g.reshape(1, kd), w, *(rope if rope_cols else ()), *[c[0] for c in casts], *tail_in)
    return res if len(res) > 1 else res[0]


Q_BLOCK = 64
K_BLOCK = 256
KEY_PAD = K_BLOCK - Q_BLOCK

def _attn_prompt_kernel(q_ref, k_ref, v_ref, *refs, n_cast):
    cast_in, (o_ref, ks_ref, vs_ref) = refs[:n_cast], refs[n_cast:n_cast + 3]
    cast_out = refs[n_cast + 3:2 * n_cast + 3]
    qn_ref, kn_ref, vn_ref, q16_ref, k16_ref, v16_ref, st_ref, acc_ref, mm_ref, ll_ref = refs[2 * n_cast + 3:]
    _cast_blocks(cast_in, cast_out)
    t = q_ref.shape[1]
    n_keep = ks_ref.shape[1]
    ks_ref[0] = k_ref[0, t - n_keep:, :]
    vs_ref[0] = v_ref[0, t - n_keep:, :]
    l4, l16 = t // 4, t // 16
    qb, nk, back = Q_BLOCK, K_BLOCK, KEY_PAD
    nblk = t // qb
    piece = qb // 4
    kpiece = nk // 4
    scale = HEAD_DIM ** -0.5
    nt = (((1,), (1,)), ((), ()))
    pad = jnp.zeros((back, HEAD_DIM), BF16)
    for ref in (kn_ref, vn_ref, k16_ref, v16_ref):
        ref[0:back, :] = pad

    qn_ref[...] = q_ref[0].astype(BF16)
    kn_ref[back:, :] = k_ref[0].astype(BF16)
    vn_ref[back:, :] = v_ref[0].astype(BF16)
    for src, dst, off in ((q_ref, q16_ref, 0), (k_ref, k16_ref, back), (v_ref, v16_ref, back)):
        for r4 in range(4):
            st_ref[r4 * l4:(r4 + 1) * l4, :] = src[0, pl.ds(r4, l4, stride=4), :]
        for r4 in range(4):
            for a in range(4):
                r16 = r4 + 4 * a
                dst[off + r16 * l16:off + (r16 + 1) * l16, :] = (
                    st_ref[pl.ds(r4 * l4 + a, l16, stride=4), :].astype(BF16))

    iota = lambda dim: lax.broadcasted_iota(jnp.int32, (qb, nk), dim)
    qi, kj = iota(0), iota(1)
    dist = back + qi - kj
    band = (dist >= 0) & (dist <= N_BACK)
    back4 = back // 4
    dist4 = 4 * (qi % piece - kj % kpiece + back4) + (qi // piece - kj // kpiece)
    band4 = (dist4 >= 0) & (dist4 <= N_BACK)

    def scores(q, k2, valid):
        s = lax.dot_general(q, k2, nt, preferred_element_type=F32) * scale
        return jnp.where(valid, s, -jnp.inf)

    def fresh(s, v2):
        m = jnp.max(s, axis=-1, keepdims=True)
        p = jnp.exp(s - m)
        wide = lambda c: jnp.broadcast_to(c, (qb, HEAD_DIM))
        return wide(m), wide(jnp.sum(p, axis=-1, keepdims=True)), jnp.dot(p.astype(BF16), v2,
                                                                         preferred_element_type=F32)

    def merged(s, v2, m_old, l_old, a_old):
        m_new = jnp.maximum(m_old, jnp.max(s, axis=-1, keepdims=True))
        alpha = jnp.exp(m_old - m_new)
        p = jnp.exp(s - jnp.broadcast_to(m_new[:, :1], s.shape))
        l_new = alpha * l_old + jnp.sum(p, axis=-1, keepdims=True)
        return m_new, l_new, alpha * a_old + jnp.dot(p.astype(BF16), v2, preferred_element_type=F32)

    def block16(b, carry):
        row0 = pl.multiple_of(b * qb, qb)
        rows = pl.ds(row0, qb)
        first_key = back - qb * lax.rem(b, l16 // qb)
        s = scores(q16_ref[rows, :], k16_ref[pl.ds(row0, nk), :], band & (kj >= first_key))
        mm_ref[rows, :], ll_ref[rows, :], acc_ref[rows, :] = fresh(s, v16_ref[pl.ds(row0, nk), :])
        return carry

    lax.fori_loop(0, nblk, block16, 0, unroll=nblk // 2)

    n_j = l16 // piece

    def block4(b, carry):
        rho, j = b // n_j, lax.rem(b, n_j)
        base = pl.multiple_of(rho * l16 + j * piece, piece)
        q_rows = [pl.ds(base + a * 4 * l16, piece) for a in range(4)]
        k_rows = [pl.ds(base + a * 4 * l16 + back - back4, kpiece) for a in range(4)]
        gather = lambda ref, rows: jnp.concatenate([ref[r, :] for r in rows], axis=0)
        s = scores(gather(q16_ref, q_rows), gather(k16_ref, k_rows), band4 & (kj % kpiece >= back4 - piece * j))
        m, l, acc = merged(s, gather(v16_ref, k_rows), gather(mm_ref, q_rows), gather(ll_ref, q_rows),
                           gather(acc_ref, q_rows))
        for a, r in enumerate(q_rows):
            part = slice(a * piece, (a + 1) * piece)
            mm_ref[r, :], ll_ref[r, :], acc_ref[r, :] = m[part], l[part], acc[part]
        return carry

    lax.fori_loop(0, nblk, block4, 0, unroll=nblk // 2)

    for src, dst in ((acc_ref, o_ref.at[0]), (mm_ref, acc_ref), (ll_ref, mm_ref)):
        for r4 in range(4):
            for a in range(4):
                r16 = r4 + 4 * a
                st_ref[pl.ds(r4 * l4 + a, l16, stride=4), :] = src[r16 * l16:(r16 + 1) * l16, :]
        for r4 in range(4):
            dst[pl.ds(r4, l4, stride=4), :] = st_ref[r4 * l4:(r4 + 1) * l4, :]

    def block1(b, carry):
        row0 = pl.multiple_of(b * qb, qb)
        rows = pl.ds(row0, qb)
        s = scores(qn_ref[rows, :], kn_ref[pl.ds(row0, nk), :], band & (kj >= back - qb * b))
        _, l, acc = merged(s, vn_ref[pl.ds(row0, nk), :], acc_ref[rows, :], mm_ref[rows, :], o_ref[0, rows, :])
        o_ref[0, rows, :] = acc / l
        return carry

    lax.fori_loop(0, nblk, block1, 0, unroll=nblk // 2)


def _attn_prompt(proj, b, t, n_keep, casts=()):
    assert t % (16 * N_BACK) == 0
    p3 = proj.reshape(b, t, proj.shape[1])
    blk = lambda off: pl.BlockSpec((1, t, HEAD_DIM), lambda i, h: (i, 0, off + h))
    keep = pl.BlockSpec((1, n_keep, HEAD_DIM), lambda i, h: (i, 0, h))
    cast_in, cast_out, cast_shapes = _cast_specs(casts, b * N_HEADS, lambda i, h: i * N_HEADS + h)
    return pl.pallas_call(
        functools.partial(_attn_prompt_kernel, n_cast=len(casts)),
        grid=(b, N_HEADS),
        in_specs=[blk(0), blk(N_HEADS), blk(2 * N_HEADS)] + cast_in,
        out_specs=[pl.BlockSpec((1, t, HEAD_DIM), lambda i, h: (i, 0, h)), keep, keep] + cast_out,
        out_shape=[jax.ShapeDtypeStruct((b, t, ATTN_WIDTH), F32),
                   jax.ShapeDtypeStruct((b, n_keep, ATTN_WIDTH), F32),
                   jax.ShapeDtypeStruct((b, n_keep, ATTN_WIDTH), F32)] + cast_shapes,
        scratch_shapes=[
            pltpu.VMEM((t, HEAD_DIM), BF16),
            pltpu.VMEM((t + KEY_PAD, HEAD_DIM), BF16),
            pltpu.VMEM((t + KEY_PAD, HEAD_DIM), BF16),
            pltpu.VMEM((t, HEAD_DIM), BF16),
            pltpu.VMEM((t + KEY_PAD, HEAD_DIM), BF16),
            pltpu.VMEM((t + KEY_PAD, HEAD_DIM), BF16),
            pltpu.VMEM((t, HEAD_DIM), F32),
            pltpu.VMEM((t, HEAD_DIM), F32),
            pltpu.VMEM((t, HEAD_DIM), F32),
            pltpu.VMEM((t, HEAD_DIM), F32),
        ],
        compiler_params=_cparams("parallel", "parallel"),
        name="attn_prompt",
    )(p3, p3, p3, *[c[0] for c in casts])


COARSE = DILATIONS[-1]
NEAR = N_BACK * DILATIONS[-2]


def _attn_sample_kernel(q_ref, kn_ref, vn_ref, kfar_ref, knear_ref, vfar_ref, vnear_ref, o_ref):
    s_len = q_ref.shape[0]
    far_groups, near_groups = kfar_ref.shape[0], knear_ref.shape[0]
    n_far, n_near = far_groups * s_len, near_groups * COARSE
    n_buf = (far_groups + near_groups) * COARSE
    scale = HEAD_DIM ** -0.5
    nt = (((1,), (1,)), ((), ()))

    def count(dist):
        c = jnp.zeros(dist.shape, F32)
        for d in DILATIONS:
            hit = (dist >= 0) & (dist <= N_BACK * d) & ((dist & (d - 1)) == 0)
            c = c + jnp.where(hit, 1.0, 0.0)
        return c

    iota = lambda shape, dim: lax.broadcasted_iota(jnp.int32, shape, dim)
    query = lambda width: n_buf + iota((s_len, width), 0)
    col = iota((s_len, n_far + n_near), 1)
    far_pos = COARSE * (col // s_len) + col % s_len
    near_pos = far_groups * COARSE + (col - n_far)
    cc = count(query(n_far + n_near) - jnp.where(col < n_far, far_pos, near_pos))
    cn = count(iota((s_len, s_len), 0) - iota((s_len, s_len), 1))

    def head_rows(far_ref, near_ref, h):
        far = far_ref[:, pl.ds(h, s_len, stride=N_HEADS), :].reshape(n_far, HEAD_DIM)
        near = near_ref[:, pl.ds(h, COARSE, stride=N_HEADS), :].reshape(n_near, HEAD_DIM)
        return jnp.concatenate([far, near], axis=0).astype(BF16)

    heads = range(N_HEADS)
    cols = [slice(h * HEAD_DIM, (h + 1) * HEAD_DIM) for h in heads]
    stack = lambda parts: jnp.concatenate(parts, axis=0)
    qs = [q_ref[:, c].astype(BF16) for c in cols]
    sc = stack([lax.dot_general(qs[h], head_rows(kfar_ref, knear_ref, h), nt, preferred_element_type=F32)
                for h in heads]) * scale
    sn = stack([lax.dot_general(qs[h], kn_ref[:, cols[h]].astype(BF16), nt, preferred_element_type=F32)
                for h in heads]) * scale
    cc, cn = stack([cc] * N_HEADS), stack([cn] * N_HEADS)
    sc = jnp.where(cc > 0, sc, -jnp.inf)
    sn = jnp.where(cn > 0, sn, -jnp.inf)
    m = jnp.maximum(jnp.max(sc, axis=-1, keepdims=True), jnp.max(sn, axis=-1, keepdims=True))
    pc = cc * jnp.exp(sc - m)
    pn = cn * jnp.exp(sn - m)
    l = jnp.sum(pc, axis=-1, keepdims=True) + jnp.sum(pn, axis=-1, keepdims=True)
    for h in heads:
        rows = slice(h * s_len, (h + 1) * s_len)
        o = (jnp.dot(pc[rows].astype(BF16), head_rows(vfar_ref, vnear_ref, h), preferred_element_type=F32)
             + jnp.dot(pn[rows].astype(BF16), vn_ref[:, cols[h]].astype(BF16), preferred_element_type=F32))
        o_ref[:, cols[h]] = o / l[rows]


def _attn_sample(proj, cache_k, cache_v, b, s_len):
    n_buf = cache_k.shape[1]
    near_groups = NEAR // COARSE
    far_groups = n_buf // COARSE - near_groups
    assert n_buf % COARSE == 0 and s_len <= COARSE and far_groups % near_groups == 0 and far_groups > 0
    grouped = lambda c: c.reshape(b, n_buf // COARSE, COARSE * N_HEADS, HEAD_DIM)
    new = lambda c: pl.BlockSpec((s_len, ATTN_WIDTH), lambda i: (i, c))
    far = pl.BlockSpec((None, far_groups, s_len * N_HEADS, HEAD_DIM), lambda i: (i, 0, 0, 0))
    near = pl.BlockSpec((None, near_groups, COARSE * N_HEADS, HEAD_DIM),
                        lambda i: (i, far_groups // near_groups, 0, 0))
    ck, cv = grouped(cache_k), grouped(cache_v)
    return pl.pallas_call(
        _attn_sample_kernel,
        grid=(b,),
        in_specs=[new(0), new(1), new(2), far, near, far, near],
        out_specs=new(0),
        out_shape=jax.ShapeDtypeStruct((b * s_len, ATTN_WIDTH), F32),
        compiler_params=_cparams("parallel"),
        name="attn_sample",
    )(proj, proj, proj, ck, ck, cv, cv)


def _silu(z):
    return z * jax.nn.sigmoid(z)


def _mix_out_kernel(x_ref, o_ref, g_ref, wza_ref, wgb_ref, wgc_ref, whi_ref, wzb_ref, cw_ref, w_ref, *refs,
                    tiles_per_seq):
    xs_ref, os_ref, p2_ref, p1_ref, h_ref, cs_ref, hs_ref, css_ref, tail_ref = refs

    def tile(x, o_attn, ln, prev2, prev1):
        tm = x.shape[0]
        hn = _rms_norm(x, g_ref[...]).astype(BF16)
        gate = lambda w: jnp.dot(hn, w[...], preferred_element_type=F32)
        ch = gate(wgc_ref) * gate(whi_ref)
        pos = lax.rem(lax.broadcasted_iota(jnp.int32, (tm, 1), 0), ln)
        ch1 = jnp.where(pos == 0, prev1, pltpu.roll(ch, 1, 0))
        ch2 = jnp.where(pos == 0, prev2, jnp.where(pos == 1, prev1, pltpu.roll(ch, 2, 0)))
        conv = ch2 * cw_ref[0:1, :] + ch1 * cw_ref[1:2, :] + ch * cw_ref[2:3, :]
        o_b = (gate(wgb_ref) * conv * _silu(gate(wzb_ref))).astype(BF16)
        y = jnp.dot(o_b, w_ref[ATTN_WIDTH:, :], preferred_element_type=F32)
        o_a = (o_attn * _silu(gate(wza_ref))).astype(BF16)
        y = y + jnp.dot(o_a, w_ref[0:ATTN_WIDTH, :], preferred_element_type=F32)
        return x + y, ch

    i = pl.program_id(0)
    tm = x_ref.shape[0]

    @pl.when(i % tiles_per_seq == 0)
    def _():
        tail_ref[...] = jnp.zeros(tail_ref.shape, F32)

    h, ch = tile(x_ref[...], o_ref[...], tm, tail_ref[0:1, :], tail_ref[1:2, :])
    h_ref[...] = h
    tail_ref[...] = ch[tm - 2:tm, :]
    cs_ref[0] = ch[tm - 2:tm, :]

    @pl.when(i == pl.num_programs(0) - 1)
    def _():
        rows = xs_ref.shape[0]
        k = p1_ref.shape[0]
        ln = rows // k
        per_row = lambda p: jnp.broadcast_to(p[...], (k, ln, CONV_WIDTH)).reshape(rows, CONV_WIDTH)
        hs, chs = tile(xs_ref[...], os_ref[...], ln, per_row(p2_ref), per_row(p1_ref))
        hs_ref[...] = hs
        css_ref[...] = chs.reshape(k, ln, CONV_WIDTH)[:, ln - 2:, :]


def _mix_out(x, o_attn, xs, os_attn, conv_init_s, g, w_gates, conv_w, w_out, *, tm, tiles_per_seq):
    m, ms = x.shape[0], xs.shape[0]
    nseq = m // (tm * tiles_per_seq)
    once = lambda a: pl.BlockSpec(a.shape, lambda i: (0,) * len(a.shape), pipeline_mode=pl.Buffered(1))
    row = lambda w: pl.BlockSpec((tm, w), lambda i: (i, 0))
    p2, p1 = conv_init_s[:, 0:1], conv_init_s[:, 1:2]
    out_shape = (jax.ShapeDtypeStruct((m, D_MODEL), F32), jax.ShapeDtypeStruct((nseq, 2, CONV_WIDTH), F32),
                 jax.ShapeDtypeStruct((ms, D_MODEL), F32), jax.ShapeDtypeStruct(conv_init_s.shape, F32))
    return pl.pallas_call(
        functools.partial(_mix_out_kernel, tiles_per_seq=tiles_per_seq),
        grid=(m // tm,),
        in_specs=[row(D_MODEL), row(ATTN_WIDTH), pl.BlockSpec((1, D_MODEL), lambda i: (0, 0))]
        + [once(w) for w in w_gates] + [pl.BlockSpec((3, CONV_WIDTH), lambda i: (0, 0)), once(w_out)]
        + [once(a) for a in (xs, os_attn, p2, p1)],
        out_specs=(row(D_MODEL), pl.BlockSpec((1, 2, CONV_WIDTH), lambda i: (i // tiles_per_seq, 0, 0)))
        + tuple(pl.BlockSpec(s.shape, lambda i, n=len(s.shape): (0,) * n) for s in out_shape[2:]),
        out_shape=out_shape,
        scratch_shapes=[pltpu.VMEM((2, CONV_WIDTH), F32)],
        compiler_params=_cparams("arbitrary"),
        name="mix_out",
    )(x, o_attn, g.reshape(1, -1), *w_gates, conv_w, w_out, xs, os_attn, p2, p1)


N_SCAN = 7
GROUPS_PER_TILE = LANES // SSM_GROUP


PROMPT_CHUNK = 32
ROW_C = PROMPT_CHUNK + 1
ROW_DBL = PROMPT_CHUNK + 2
TABLE_ROWS = ROW_DBL + N_SCAN


def _s5_disc_kernel(lr_ref, li_ref, ls_ref, tr_ref, ti_ref):
    lr = lr_ref[...]
    li = li_ref[...]
    step = jnp.exp(ls_ref[...])
    mag = jnp.exp(lr * step)
    ar = mag * jnp.cos(li * step)
    ai = mag * jnp.sin(li * step)
    den = lr * lr + li * li
    nr = ar - 1.0
    tr_ref[ROW_C] = (nr * lr + ai * li) / den
    ti_ref[ROW_C] = (ai * lr - nr * li) / den
    pr = jnp.ones_like(ar)
    pi = jnp.zeros_like(ar)
    for tau in range(PROMPT_CHUNK + 1):
        tr_ref[tau] = pr
        ti_ref[tau] = pi
        dr, di = pr, pi
        pr, pi = pr * ar - pi * ai, pr * ai + pi * ar
    for i in range(N_SCAN):
        tr_ref[ROW_DBL + i] = dr
        ti_ref[ROW_DBL + i] = di
        dr, di = dr * dr - di * di, 2.0 * dr * di


def _s5_tables(lam_re, lam_im, log_step):
    g = lam_re.shape[0]
    dup = lambda a: jnp.concatenate([a, a], axis=-1)
    tr, ti = pl.pallas_call(
        _s5_disc_kernel,
        out_shape=(jax.ShapeDtypeStruct((TABLE_ROWS, g, LANES), F32),) * 2,
        name="s5_disc",
    )(dup(lam_re), dup(lam_im), log_step[:, None])
    return tr, ti


def _split_bf16(x):
    hi = x.astype(BF16)
    return hi, (x - hi.astype(F32)).astype(BF16)


def _dot_split(a, b, b_is_bf16_exact=False):
    dot = lambda x, y: jnp.dot(x, y, preferred_element_type=F32)
    ah, al = _split_bf16(a)
    if b_is_bf16_exact:
        bh = b.astype(BF16)
        return dot(ah, bh) + dot(al, bh)
    bh, bl = _split_bf16(b)
    return dot(ah, bh) + (dot(ah, bl) + dot(al, bh))


def _pad_rows(x, rows):
    if x.shape[0] == rows:
        return x
    return jnp.concatenate([x, jnp.zeros((rows - x.shape[0], x.shape[1]), x.dtype)], axis=0)


def _s5_build_operators(tr_ref, ti_ref, br_ref, bi_ref, cr_ref, ci_ref, c2r_ref, c2i_ref,
                        wt_ref, qt_ref, kr_ref, ki_ref):
    chunk = PROMPT_CHUNK
    n = SSM_GROUP * chunk
    iota = lambda shape, dim: lax.broadcasted_iota(jnp.int32, shape, dim)
    lane = iota((SSM_GROUP, 2 * SSM_STATE), 1)
    e_sel = jnp.where(iota((chunk, n), 0) == chunk - 1 - iota((chunk, n), 1) // SSM_GROUP, 1.0, 0.0)
    tile = jnp.where(iota((SSM_GROUP, n), 0) == iota((SSM_GROUP, n), 1) % SSM_GROUP, 1.0, 0.0)

    for g in range(GROUPS_PER_TILE):
        tab_r, tab_i = tr_ref[:, g, :], ti_ref[:, g, :]
        col_r = _pad_rows(tab_r, LANES).T[:SSM_STATE]
        col_i = _pad_rows(tab_i, LANES).T[:SSM_STATE]
        kr_ref[g] = col_r
        ki_ref[g] = col_i
        ccr, cci = col_r[:, ROW_C:ROW_C + 1], col_i[:, ROW_C:ROW_C + 1]
        br, bi = br_ref[g], bi_ref[g]
        bbr = ccr * br - cci * bi
        bbi = ccr * bi + cci * br
        aer = _dot_split(col_r[:, :chunk], e_sel, True)
        aei = _dot_split(col_i[:, :chunk], e_sel, True)
        btr = _dot_split(bbr, tile, True)
        bti = _dot_split(bbi, tile, True)
        ptr = aer * btr - aei * bti
        pti = aer * bti + aei * btr
        wt_ref[g, n:n + SSM_STATE, :] = ptr.astype(BF16)
        wt_ref[g, n + SSM_STATE:n + 2 * SSM_STATE, :] = pti.astype(BF16)
        taps = _dot_split(cr_ref[g], ptr) - _dot_split(ci_ref[g], pti)
        padded = jnp.concatenate([taps, jnp.zeros_like(taps)], axis=1)
        for t in range(chunk):
            sh = SSM_GROUP * (chunk - 1 - t)
            blk = padded if sh == 0 else pltpu.roll(padded, 2 * n - sh, 1)
            wt_ref[g, SSM_GROUP * t:SSM_GROUP * (t + 1), :] = blk[:, :n].astype(BF16)
        c2r, c2i = c2r_ref[g], c2i_ref[g]
        for t in range(chunk):
            ar = tab_r[t + 1:t + 2, :]
            ai = tab_i[t + 1:t + 2, :]
            x1 = jnp.where(lane < SSM_STATE, ar, -ai)
            x2 = jnp.where(lane < SSM_STATE, ai, ar)
            qt_ref[g, SSM_GROUP * t:SSM_GROUP * (t + 1), :] = (c2r * x1 - c2i * x2).astype(BF16)


def _s5_kernel(*refs):
    build_in, (u_ref, us_ref, h0_ref, dsk_ref) = refs[:8], refs[8:12]
    y_ref, hf_ref, ys_ref, hfs_ref = refs[12:16]
    ops = refs[16:20], refs[20:24]
    (dall_ref, yall_ref), (dalls_ref, yalls_ref) = refs[24:26], refs[26:28]
    k = pl.program_id(0)
    chunk_s = us_ref.shape[0] * POS_TILE

    def step(p):
        _s5_build_operators(*build_in, *ops[p])
        for u, dall, yall, h0, hf, y, chunk in ((u_ref, dall_ref, yall_ref, None, hf_ref, y_ref, PROMPT_CHUNK),
                                                (us_ref, dalls_ref, yalls_ref, h0_ref, hfs_ref, ys_ref, chunk_s)):
            _s5_regroup_in(u, dall, chunk)
            _s5_scan_groups(dall, *ops[1 - p], h0, hf, yall, chunk=chunk)
            _s5_regroup_out(yall, u, dsk_ref, y, chunk)

    pl.when(k == 0)(lambda: _s5_build_operators(*build_in, *ops[0]))
    pl.when((k > 0) & (k % 2 == 0))(lambda: step(0))
    pl.when(k % 2 == 1)(lambda: step(1))


def _s5_rows(ref, s):
    return s // POS_TILE, pl.ds(s % POS_TILE, ref.shape[1] // POS_TILE, stride=POS_TILE), slice(None)


def _s5_regroup_in(u_ref, dall_ref, chunk):
    for s in range(chunk):
        dall_ref[s] = _pad_rows(u_ref[_s5_rows(u_ref, s)], dall_ref.shape[2]).T.astype(BF16)


def _s5_regroup_out(yall_ref, u_ref, dsk_ref, y_ref, chunk):
    nc = u_ref.shape[1] // POS_TILE
    for t in range(chunk):
        at = _s5_rows(u_ref, t)
        y_ref[at] = yall_ref[t].T[:nc] + dsk_ref[...] * u_ref[at]


def _s5_scan_groups(dall_ref, wt_ref, qt_ref, kr_ref, ki_ref, h0_ref, hf_ref, yall_ref, *, chunk):
    carry = h0_ref is None
    ncp = dall_ref.shape[2]
    nseq = ncp // LANES if carry else h0_ref.shape[0]
    n = SSM_GROUP * chunk
    n_full = SSM_GROUP * PROMPT_CHUNK
    iota = lambda shape, dim: lax.broadcasted_iota(jnp.int32, shape, dim)
    lane = iota((SSM_STATE, ncp), 1) & (LANES - 1)
    slot = iota((LANES, LANES), 1)
    if not carry:
        h0_all = _pad_rows(jnp.concatenate([h0_ref[:, gp, :] for gp in range(GROUPS_PER_TILE)], axis=0), LANES).T
    acc = jnp.zeros((LANES, LANES), F32)
    for gp in range(GROUPS_PER_TILE):
        rows = slice(SSM_GROUP * gp, SSM_GROUP * (gp + 1))
        d = dall_ref[:, rows, :].reshape(n, ncp)
        if n == n_full:
            ys = jnp.dot(wt_ref[gp], d, preferred_element_type=F32)
            st = ys[n:]
            ys = ys[:n]
        else:
            ys = jnp.dot(wt_ref[gp, 0:n, 0:n], d, preferred_element_type=F32)
            st = jnp.dot(wt_ref[gp, n_full:, n_full - n:], d, preferred_element_type=F32)
        sr = st[:SSM_STATE]
        si = st[SSM_STATE:]
        if carry:
            for i in range(N_SCAN):
                sh = 1 << i
                ar = kr_ref[gp, :, ROW_DBL + i:ROW_DBL + i + 1]
                ai = ki_ref[gp, :, ROW_DBL + i:ROW_DBL + i + 1]
                pr = jnp.where(lane >= sh, pltpu.roll(sr, sh, 1), 0.0)
                pi = jnp.where(lane >= sh, pltpu.roll(si, sh, 1), 0.0)
                sr, si = sr + ar * pr - ai * pi, si + ar * pi + ai * pr
            hp = jnp.concatenate([jnp.where(lane >= 1, pltpu.roll(sr, 1, 1), 0.0),
                                  jnp.where(lane >= 1, pltpu.roll(si, 1, 1), 0.0)], axis=0)
            for b in range(nseq):
                seq = slice(LANES * b, LANES * (b + 1))
                end = jnp.concatenate([sr[:, seq], si[:, seq]], axis=0)
                acc = jnp.where(slot == 8 * gp + b, pltpu.roll(end, (8 * gp + b + 1) % LANES, 1), acc)
        else:
            hp = h0_all if gp == 0 else pltpu.roll(h0_all, LANES - 8 * gp, 1)
            hp = jnp.where(slot < nseq, hp, 0.0)
            hpr, hpi = hp[:SSM_STATE], hp[SSM_STATE:]
            ar = kr_ref[gp, :, chunk:chunk + 1]
            ai = ki_ref[gp, :, chunk:chunk + 1]
            end = jnp.concatenate([sr + ar * hpr - ai * hpi, si + ar * hpi + ai * hpr], axis=0)
            if gp:
                end = pltpu.roll(end, 8 * gp, 1)
            acc = jnp.where((slot >= 8 * gp) & (slot < 8 * gp + nseq), end, acc)
        y = ys + jnp.dot(qt_ref[gp, 0:n, :], hp.astype(BF16), preferred_element_type=F32)
        yall_ref[:, rows, :] = y.reshape(chunk, SSM_GROUP, ncp)
    hf_ref[...] = acc.T[:8 * GROUPS_PER_TILE].reshape(GROUPS_PER_TILE, 8, LANES)


def _s5_layer(tables, b_re, b_im, c_re, c_im, d_skip, u, n_long, us, h0):
    ns = h0.shape[0]
    chunk_s = us.shape[0] * POS_TILE
    n, n_s = SSM_GROUP * PROMPT_CHUNK, SSM_GROUP * chunk_s
    assert u.shape[0] * POS_TILE == PROMPT_CHUNK and u.shape[1] == n_long * LANES * POS_TILE and n_long <= 8
    assert us.shape[1] == ns * POS_TILE and ns == 8 and n % n_s == 0
    gt = GROUPS_PER_TILE
    t = SSM_GROUPS // gt
    cur = lambda k: jnp.minimum(k, t - 1)
    prv = lambda k: jnp.maximum(k - 1, 0)
    built = lambda *s: pl.BlockSpec((gt,) + s, lambda k: (cur(k),) + (0,) * len(s))
    lanes = lambda a: pl.BlockSpec(a.shape[:-1] + (LANES,), lambda k: (0,) * (a.ndim - 1) + (prv(k),))
    states = pl.BlockSpec((gt, 8, LANES), lambda k: (prv(k), 0, 0))
    dup = lambda a: jnp.concatenate([a, a], axis=-1)
    op_set = [pltpu.VMEM((gt, n + 2 * SSM_STATE, n), BF16), pltpu.VMEM((gt, n, 2 * SSM_STATE), BF16),
              pltpu.VMEM((gt, SSM_STATE, LANES), F32), pltpu.VMEM((gt, SSM_STATE, LANES), F32)]
    regroup = lambda c, width: [pltpu.VMEM((c, LANES, width), BF16), pltpu.VMEM((c, LANES, width), F32)]
    return pl.pallas_call(
        _s5_kernel,
        grid=(t + 1,),
        in_specs=[pl.BlockSpec((TABLE_ROWS, gt, LANES), lambda k: (0, cur(k), 0))] * 2 + [
                  built(SSM_STATE, SSM_GROUP), built(SSM_STATE, SSM_GROUP),
                  built(SSM_GROUP, SSM_STATE), built(SSM_GROUP, SSM_STATE),
                  built(SSM_GROUP, LANES), built(SSM_GROUP, LANES),
                  lanes(u), lanes(us), pl.BlockSpec((ns, gt, LANES), lambda k: (0, prv(k), 0)),
                  pl.BlockSpec((1, LANES), lambda k: (0, prv(k)))],
        out_specs=(lanes(u), states, lanes(us), states),
        out_shape=(jax.ShapeDtypeStruct(u.shape, F32), jax.ShapeDtypeStruct((SSM_GROUPS, 8, LANES), F32),
                   jax.ShapeDtypeStruct(us.shape, F32), jax.ShapeDtypeStruct((SSM_GROUPS, 8, LANES), F32)),
        scratch_shapes=op_set + op_set + regroup(PROMPT_CHUNK, n_long * LANES) + regroup(chunk_s, LANES),
        compiler_params=_cparams("arbitrary"),
        name="s5",
    )(*tables, b_re, b_im, c_re, c_im, dup(c_re), dup(c_im), u, us, h0, d_skip.reshape(1, -1))


def _ssm_out(y, x, gn_ref, wz_ref, wg_ref, bg_ref, wo_ref, gf_ref):
    z = jnp.dot(_rms_norm(x, gn_ref[...]).astype(BF16), wz_ref[...], preferred_element_type=F32)
    g = jax.nn.gelu(y, approximate=True)
    gate = jnp.dot(g.astype(BF16), wg_ref[...], preferred_element_type=F32) + bg_ref[...]
    yy = (g * jax.nn.sigmoid(gate)) * _silu(z)
    return _rms_norm(x + jnp.dot(yy.astype(BF16), wo_ref[...], preferred_element_type=F32), gf_ref[...])


def _glu_out_kernel(y_ref, x_ref, ys_ref, xs_ref, *rest):
    params, (o_ref, os_ref) = rest[:-2], rest[-2:]
    o_ref[...] = _ssm_out(y_ref[...], x_ref[...].reshape(y_ref.shape), *params).reshape(o_ref.shape)

    @pl.when(pl.program_id(0) == pl.num_programs(0) - 1)
    def _():
        os_ref[...] = _ssm_out(ys_ref[...], xs_ref[...], *params)


def _glu_out(y, x, ys, xs, g_norm, w_z, w_glu, b_glu, w_out, g_final, *, ct):
    m, w = x.shape
    vec = pl.BlockSpec((1, w), lambda i: (0, 0))
    mat = pl.BlockSpec((w, w), lambda i: (0, 0), pipeline_mode=pl.Buffered(1))
    small = pl.BlockSpec(xs.shape, lambda i: (0, 0))
    sg, nc = y.shape[0], y.shape[1] // POS_TILE
    x_spec = pl.BlockSpec((ct, POS_TILE, w), lambda i: (i // sg, i % sg, 0))
    out, out_s = pl.pallas_call(
        _glu_out_kernel,
        grid=((nc // ct) * sg,),
        in_specs=[pl.BlockSpec((None, ct * POS_TILE, w), lambda i: (i % sg, i // sg, 0)), x_spec, small, small,
                  vec, mat, mat, vec, mat, vec],
        out_specs=(x_spec, small),
        out_shape=(jax.ShapeDtypeStruct((nc, sg * POS_TILE, w), F32), jax.ShapeDtypeStruct(xs.shape, F32)),
        compiler_params=_cparams("arbitrary"),
        name="glu_out",
    )(y, x.reshape(nc, sg * POS_TILE, w), ys, xs, g_norm.reshape(1, w), w_z, w_glu, b_glu.reshape(1, w), w_out,
      g_final.reshape(1, w))
    return out.reshape(m, w), out_s


PAST_LEN = 16384


def kernel(x_prompt, x_sample, cache_win_k, cache_win_v, state_conv, state_ssm_re, state_ssm_im, attn_norm, w_in_ab, conv_w, w_out_ab, ssm_norm, w_in_c, lam_re, lam_im, log_step, b_re, b_im, c_re, c_im, d_skip, w_glu, b_glu, w_out_c, final_norm):
    bp, tp, _ = x_prompt.shape
    bs, ts, _ = x_sample.shape
    n_keep = min(2048, tp)
    xp = x_prompt.reshape(bp * tp, D_MODEL)
    xs = x_sample.reshape(bs * ts, D_MODEL)

    qkv_cols = 3 * ATTN_WIDTH
    gate_casts = [(w_in_ab[0], CONV_WIDTH, qkv_cols // CONV_WIDTH + c) for c in range(5)]
    layer1_casts = [(w_in_c[0], D_MODEL, 0), (w_in_c[0], D_MODEL, 1), (w_glu[0], D_MODEL, 0), (w_out_c[0], D_MODEL, 0)]

    cos_p, sin_p, w_qkv = _rope_tables(tp, tp, 0, casts=[(w_in_ab[0], qkv_cols, 0)])
    proj_p, *w_gates, w_out0, proj_s = _norm_proj(
        xp, attn_norm[0], w_qkv, tm=512, rope=(cos_p, sin_p), rope_cols=2 * ATTN_WIDTH, tiles_per_seq=tp // 512,
        casts=gate_casts + [(w_out_ab[0], D_MODEL, 0)], tail=xs, tail_rope=_rope_tables(bs * ts, ts, PAST_LEN))
    o_p, k_p, v_p, w_u, w_z, w_glu1, w_out1 = _attn_prompt(proj_p, bp, tp, n_keep, casts=layer1_casts)
    o_p = o_p.reshape(bp * tp, ATTN_WIDTH)
    k_p = k_p.reshape(1, bp, n_keep, N_HEADS, HEAD_DIM)
    v_p = v_p.reshape(1, bp, n_keep, N_HEADS, HEAD_DIM)
    o_s = _attn_sample(proj_s, cache_win_k[0], cache_win_v[0], bs, ts)
    h1_p, conv_p, h1_s, conv_s = _mix_out(xp, o_p, xs, o_s, state_conv[0], attn_norm[0], w_gates, conv_w[0], w_out0,
                                          tm=512, tiles_per_seq=tp // 512)
    k_s = proj_s[:, ATTN_WIDTH:2 * ATTN_WIDTH].reshape(1, bs, ts, N_HEADS, HEAD_DIM)
    v_s = proj_s[:, 2 * ATTN_WIDTH:3 * ATTN_WIDTH].reshape(1, bs, ts, N_HEADS, HEAD_DIM)

    assert ts == POS_TILE
    tables = _s5_tables(lam_re[0], lam_im[0], log_step[0])
    u_p, u_s = _norm_proj(h1_p, ssm_norm[0], w_u, chunk=PROMPT_CHUNK, ct=64, tail=h1_s)
    h0 = jnp.concatenate([state_ssm_re[0], state_ssm_im[0]], axis=-1)
    y_p, hf_p, y_s, hf_s = _s5_layer(tables, b_re[0], b_im[0], c_re[0], c_im[0], d_skip[0], u_p, bp, u_s[None], h0)
    out_p, out_s = _glu_out(y_p, h1_p, y_s[0], h1_s, ssm_norm[0], w_z, w_glu1, b_glu[0], w_out1, final_norm, ct=32)
    hf_p = hf_p[:, :bp].transpose(1, 0, 2)[None]
    hf_s = hf_s[:, :bs].transpose(1, 0, 2)[None]
    return (out_p.reshape(bp, tp, D_MODEL), out_s.reshape(bs, ts, D_MODEL),
            k_p, v_p, conv_p[None], hf_p[..., :SSM_STATE], hf_p[..., SSM_STATE:],
            k_s, v_s, conv_s[None], hf_s[..., :SSM_STATE], hf_s[..., SSM_STATE:])
```
